```python
import jax, jax.numpy as jnp
from jax import lax
import numpy as np

D_MODEL = 2048
BATCH = 32
SEQ = 256
DEPTH = 2
DEC_BATCH = 8
DEC_SEQ = 2048
PAST_LEN = 256

GRID_W = 64
HEAD_DIM = 128
NA_HEADS = 12
NA_KR = 8
NA_KC = 16
NA_QC = 16
NA_KCB = 32
FNET_GROUPS = 4
FNET_CH = 128
POOL_WINDOWS = (2, 4, 8, 16)
POOL_CH = 128
GQA_Q_HEADS = 12
GQA_KV_HEADS = 4
D_FF = 5632
CONV_W = 3
ROPE_BASE = 10000.0
Q_BLOCK = 128
EPS = 1e-6
NEG_INF = -1e30
N_EVEN = (DEPTH + 1) // 2
N_ODD = DEPTH // 2
NA_WIDTH = NA_HEADS * HEAD_DIM
FNET_WIDTH = FNET_GROUPS * FNET_CH
POOL_WIDTH = len(POOL_WINDOWS) * POOL_CH
GQA_Q_WIDTH = GQA_Q_HEADS * HEAD_DIM
GQA_KV_WIDTH = GQA_KV_HEADS * HEAD_DIM
EVEN_IN = 3 * NA_WIDTH + FNET_WIDTH
ODD_IN = POOL_WIDTH + GQA_Q_WIDTH + 2 * GQA_KV_WIDTH
MIX_WIDTH = NA_WIDTH + FNET_WIDTH

kernel_name = 'hybrid_diffusion_prefix_step'

f32 = jnp.float32


def rms_norm(x, g):
    x32 = x.astype(f32)
    y = x32 * lax.rsqrt(jnp.mean(x32 * x32, axis=-1, keepdims=True) + EPS)
    return (y * g.astype(f32)).astype(x.dtype)


def ada_modulation(cond, w, b):
    m = jax.nn.silu(cond) @ w + b
    return m.reshape(cond.shape[0], 6, 1, D_MODEL)


def modulate(x, g, shift, scale):
    return rms_norm(x, g) * (1 + scale) + shift


def axial_rope(x):
    n = x.shape[1]
    t = jnp.arange(n)
    rows = (t // GRID_W).astype(f32)
    cols = (t % GRID_W).astype(f32)
    half = HEAD_DIM // 2
    inv_freq = jnp.power(ROPE_BASE, -jnp.arange(0, half, 2, dtype=f32) / half)

    def rot(xa, pos):
        ang = pos[:, None] * inv_freq[None, :]
        cos = jnp.cos(ang)[None, :, None, :]
        sin = jnp.sin(ang)[None, :, None, :]
        x1, x2 = xa[..., :half // 2], xa[..., half // 2:]
        return jnp.concatenate([x1 * cos - x2 * sin, x2 * cos + x1 * sin], axis=-1)

    x32 = x.astype(f32)
    out = jnp.concatenate([rot(x32[..., :half], rows), rot(x32[..., half:], cols)], axis=-1)
    return out.astype(x.dtype)


def block_attention(q, k, v):
    B, S, Hq, dh = q.shape
    Hkv = k.shape[2]
    G = Hq // Hkv
    nb = S // Q_BLOCK
    scale = dh ** -0.5
    qb = q.reshape(B, nb, Q_BLOCK, Hkv, G, dh).transpose(1, 0, 2, 3, 4, 5)

    def one(qi):
        s = jnp.einsum('bqhgd,bkhd->bhgqk', qi, k).astype(f32) * scale
        p = jax.nn.softmax(s, axis=-1).astype(v.dtype)
        return jnp.einsum('bhgqk,bkhd->bqhgd', p, v)

    o = lax.map(one, qb)
    return o.transpose(1, 0, 2, 3, 4, 5).reshape(B, S, Hq * dh)


def neighbourhood_attention(q, k, v, ctx_k, ctx_v, rel_bias):
    B, N, H, dh = q.shape
    R = N // GRID_W
    kr = min(NA_KR, R)
    n_cb = GRID_W // NA_QC
    qg = q.reshape(B, R, GRID_W, H, dh)
    kg = k.reshape(B, R, GRID_W, H, dh)
    vg = v.reshape(B, R, GRID_W, H, dh)
    cb_start = jnp.clip(jnp.arange(n_cb) * NA_QC - NA_KC // 2, 0, GRID_W - NA_KCB)
    col_idx = cb_start[:, None] + jnp.arange(NA_KCB)[None, :]
    q_col = jnp.arange(GRID_W).reshape(n_cb, NA_QC)
    q_start = jnp.clip(q_col - NA_KC // 2, 0, GRID_W - NA_KC)
    kc = col_idx[:, None, :]
    col_valid = (kc >= q_start[..., None]) & (kc < q_start[..., None] + NA_KC)
    dc_idx = jnp.clip(kc - q_col[..., None], -(NA_KC - 1), NA_KC - 1) + NA_KC - 1
    col_bias = rel_bias[:, :, dc_idx].astype(f32)
    scale = dh ** -0.5

    def one_row(r):
        r0 = jnp.clip(r - kr // 2, 0, R - kr)
        k_blk = lax.dynamic_slice_in_dim(kg, r0, kr, axis=1)[:, :, col_idx]
        v_blk = lax.dynamic_slice_in_dim(vg, r0, kr, axis=1)[:, :, col_idx]
        qr = lax.dynamic_index_in_dim(qg, r, axis=1, keepdims=False).reshape(B, n_cb, NA_QC, H, dh)
        dr_idx = r0 + jnp.arange(kr) - r + NA_KR - 1
        bias = col_bias[:, dr_idx].transpose(0, 2, 3, 1, 4)
        s_loc = jnp.einsum('bjqhd,bijchd->bhjqic', qr, k_blk).astype(f32) * scale + bias[None]
        s_loc = jnp.where(col_valid[:, :, None, :], s_loc, NEG_INF).reshape(B, H, n_cb, NA_QC, kr * NA_KCB)
        s_ctx = jnp.einsum('bjqhd,blhd->bhjql', qr, ctx_k).astype(f32) * scale
        p = jax.nn.softmax(jnp.concatenate([s_loc, s_ctx], axis=-1), axis=-1).astype(v.dtype)
        p_loc = p[..., :kr * NA_KCB].reshape(B, H, n_cb, NA_QC, kr, NA_KCB)
        p_ctx = p[..., kr * NA_KCB:]
        o = (jnp.einsum('bhjqic,bijchd->bjqhd', p_loc, v_blk)
             + jnp.einsum('bhjql,blhd->bjqhd', p_ctx, ctx_v))
        return o.reshape(B, GRID_W, H * dh)

    out = lax.map(one_row, jnp.arange(R))
    return out.transpose(1, 0, 2, 3).reshape(B, N, H * dh)


def fourier_mix(xb, fnet_w):
    B, N, _ = xb.shape
    xg = xb.astype(f32).reshape(B, N, FNET_GROUPS, FNET_CH)
    f = jnp.fft.fft2(xg, axes=(1, 3), norm='ortho').real.astype(xb.dtype)
    return jnp.einsum('bngc,gcd->bngd', f, fnet_w).reshape(B, N, FNET_WIDTH)


def pool_mix(xc, pool_w, pool_scale):
    B, N, _ = xc.shape
    xg = xc.reshape(B, N, len(POOL_WINDOWS), POOL_CH)
    cs = jnp.concatenate([jnp.zeros((B, 1, len(POOL_WINDOWS), POOL_CH), f32),
                          jnp.cumsum(xg.astype(f32), axis=1)], axis=1)
    t = jnp.arange(N)
    outs = []
    for g, w in enumerate(POOL_WINDOWS):
        lo = jnp.clip(t - w // 2, 0, N)
        hi = jnp.clip(t + w // 2, 0, N)
        mean = (cs[:, hi, g] - cs[:, lo, g]) / (hi - lo).astype(f32)[None, :, None]
        outs.append(mean - xg[:, :, g].astype(f32))
    pooled = jnp.stack(outs, axis=2).astype(xc.dtype)
    return jnp.einsum('bngc,gcd->bngd', pooled, pool_w).reshape(B, N, POOL_WIDTH) * pool_scale


def conv_ffn(h, w_up, conv_w, conv_b, w_down):
    n = h.shape[1]
    u = h @ w_up
    up = jnp.pad(u, ((0, 0), (CONV_W // 2, CONV_W // 2), (0, 0)))
    u = sum(up[:, i:i + n] * conv_w[i] for i in range(CONV_W)) + conv_b
    val, gate = jnp.split(u, 2, axis=-1)
    return (jax.nn.silu(gate) * val) @ w_down


def split_even(proj):
    B, N, _ = proj.shape
    qa, ka, va, xb = jnp.split(proj, [NA_WIDTH, 2 * NA_WIDTH, 3 * NA_WIDTH], axis=-1)
    heads = lambda a: a.reshape(B, N, NA_HEADS, HEAD_DIM)
    return heads(qa), heads(ka), heads(va), xb


def split_odd(proj):
    B, N, _ = proj.shape
    xc, q, k, v = jnp.split(proj, [POOL_WIDTH, POOL_WIDTH + GQA_Q_WIDTH,
                                   POOL_WIDTH + GQA_Q_WIDTH + GQA_KV_WIDTH], axis=-1)
    return (xc, q.reshape(B, N, GQA_Q_HEADS, HEAD_DIM),
            k.reshape(B, N, GQA_KV_HEADS, HEAD_DIM), v.reshape(B, N, GQA_KV_HEADS, HEAD_DIM))


def even_mixer_context(h, w_in, fnet_w, w_out):
    q, k, v, xb = split_even(h @ w_in)
    a = block_attention(q, k, v)
    out = jnp.concatenate([a, fourier_mix(xb, fnet_w)], axis=-1) @ w_out
    return out, k, v


def even_mixer_latent(h, ctx_k, ctx_v, w_in, na_bias, fnet_w, w_out):
    q, k, v, xb = split_even(h @ w_in)
    a = neighbourhood_attention(q, k, v, ctx_k, ctx_v, na_bias)
    return jnp.concatenate([a, fourier_mix(xb, fnet_w)], axis=-1) @ w_out


def odd_mixer_context(h, w_in, pool_w, pool_scale, q_g, k_g, w_out):
    xc, q, k, v = split_odd(h @ w_in)
    q = rms_norm(q, q_g)
    k = rms_norm(k, k_g)
    a = block_attention(q, k, v)
    out = jnp.concatenate([pool_mix(xc, pool_w, pool_scale), a], axis=-1) @ w_out
    return out, k, v


def odd_mixer_latent(h, ctx_k, ctx_v, w_in, pool_w, pool_scale, q_g, k_g, w_out):
    xc, q, k, v = split_odd(h @ w_in)
    q = axial_rope(rms_norm(q, q_g))
    k = axial_rope(rms_norm(k, k_g))
    a = block_attention(q, jnp.concatenate([ctx_k, k], axis=1), jnp.concatenate([ctx_v, v], axis=1))
    return jnp.concatenate([pool_mix(xc, pool_w, pool_scale), a], axis=-1) @ w_out


def setup_inputs(seed: int = 0) -> dict:
    key = jax.random.key(seed)
    ks = jax.random.split(key, 28)

    def nrm(k, shape, scale):
        return jax.random.normal(k, shape, f32) * scale

    return {
        'x_prompt': nrm(ks[0], (BATCH, SEQ, D_MODEL), 1.0),
        'x_sample': nrm(ks[1], (DEC_BATCH, DEC_SEQ, D_MODEL), 1.0),
        'c': nrm(ks[2], (DEC_BATCH, D_MODEL), 1.0),
        'cache_na_k': nrm(ks[3], (DEC_BATCH, N_EVEN, PAST_LEN, NA_HEADS, HEAD_DIM), 1.0),
        'cache_na_v': nrm(ks[4], (DEC_BATCH, N_EVEN, PAST_LEN, NA_HEADS, HEAD_DIM), 1.0),
        'cache_gqa_k': nrm(ks[5], (DEC_BATCH, N_ODD, PAST_LEN, GQA_KV_HEADS, HEAD_DIM), 1.0),
        'cache_gqa_v': nrm(ks[6], (DEC_BATCH, N_ODD, PAST_LEN, GQA_KV_HEADS, HEAD_DIM), 1.0),
        'c_ctx': nrm(ks[7], (D_MODEL,), 1.0),
        'norm1_g': 1.0 + nrm(ks[8], (DEPTH, D_MODEL), 0.05),
        'norm2_g': 1.0 + nrm(ks[9], (DEPTH, D_MODEL), 0.05),
        'ada_w': nrm(ks[10], (DEPTH, D_MODEL, 6 * D_MODEL), 0.5 * D_MODEL ** -0.5),
        'ada_b': nrm(ks[11], (DEPTH, 6 * D_MODEL), 0.02),
        'ev_w_in': nrm(ks[12], (N_EVEN, D_MODEL, EVEN_IN), D_MODEL ** -0.5),
        'ev_na_bias': nrm(ks[13], (N_EVEN, NA_HEADS, 2 * NA_KR - 1, 2 * NA_KC - 1), 0.1),
        'ev_fnet_w': nrm(ks[14], (N_EVEN, FNET_GROUPS, FNET_CH, FNET_CH), FNET_CH ** -0.5),
        'ev_w_out': nrm(ks[15], (N_EVEN, MIX_WIDTH, D_MODEL), MIX_WIDTH ** -0.5),
        'od_w_in': nrm(ks[16], (N_ODD, D_MODEL, ODD_IN), D_MODEL ** -0.5),
        'od_pool_w': nrm(ks[17], (N_ODD, len(POOL_WINDOWS), POOL_CH, POOL_CH), POOL_CH ** -0.5),
        'od_pool_scale': 1.0 + nrm(ks[18], (N_ODD, POOL_WIDTH), 0.1),
        'od_q_norm_g': 1.0 + nrm(ks[19], (N_ODD, HEAD_DIM), 0.05),
        'od_k_norm_g': 1.0 + nrm(ks[20], (N_ODD, HEAD_DIM), 0.05),
        'od_w_out': nrm(ks[21], (N_ODD, MIX_WIDTH, D_MODEL), MIX_WIDTH ** -0.5),
        'ffn_w_up': nrm(ks[22], (DEPTH, D_MODEL, 2 * D_FF), D_MODEL ** -0.5),
        'ffn_conv_w': nrm(ks[23], (DEPTH, CONV_W, 2 * D_FF), CONV_W ** -0.5),
        'ffn_conv_b': nrm(ks[24], (DEPTH, 2 * D_FF), 0.02),
        'ffn_w_down': nrm(ks[25], (DEPTH, D_FF, D_MODEL), D_FF ** -0.5),
        'final_norm_g': 1.0 + nrm(ks[26], (D_MODEL,), 0.05),
    }


def reference(x_prompt, x_sample, c, cache_na_k, cache_na_v, cache_gqa_k, cache_gqa_v, c_ctx,
              norm1_g, norm2_g, ada_w, ada_b,
              ev_w_in, ev_na_bias, ev_fnet_w, ev_w_out,
              od_w_in, od_pool_w, od_pool_scale, od_q_norm_g, od_k_norm_g, od_w_out,
              ffn_w_up, ffn_conv_w, ffn_conv_b, ffn_w_down, final_norm_g):
    xp = x_prompt
    xs = x_sample
    na_k_new, na_v_new, gqa_k_new, gqa_v_new = [], [], [], []
    for i in range(DEPTH):
        j = i // 2
        mc = ada_modulation(c_ctx[None, :], ada_w[i], ada_b[i])
        ms = ada_modulation(c, ada_w[i], ada_b[i])
        hp = modulate(xp, norm1_g[i], mc[:, 0], mc[:, 1])
        hs = modulate(xs, norm1_g[i], ms[:, 0], ms[:, 1])
        if i % 2 == 0:
            op, kp, vp = even_mixer_context(hp, ev_w_in[j], ev_fnet_w[j], ev_w_out[j])
            osm = even_mixer_latent(hs, cache_na_k[:, j], cache_na_v[:, j], ev_w_in[j],
                                    ev_na_bias[j], ev_fnet_w[j], ev_w_out[j])
            na_k_new.append(kp)
            na_v_new.append(vp)
        else:
            op, kp, vp = odd_mixer_context(hp, od_w_in[j], od_pool_w[j], od_pool_scale[j],
                                           od_q_norm_g[j], od_k_norm_g[j], od_w_out[j])
            osm = odd_mixer_latent(hs, cache_gqa_k[:, j], cache_gqa_v[:, j], od_w_in[j], od_pool_w[j],
                                   od_pool_scale[j], od_q_norm_g[j], od_k_norm_g[j], od_w_out[j])
            gqa_k_new.append(kp)
            gqa_v_new.append(vp)
        xp = xp + mc[:, 2] * op
        xs = xs + ms[:, 2] * osm
        xp = xp + mc[:, 5] * conv_ffn(modulate(xp, norm2_g[i], mc[:, 3], mc[:, 4]),
                                      ffn_w_up[i], ffn_conv_w[i], ffn_conv_b[i], ffn_w_down[i])
        xs = xs + ms[:, 5] * conv_ffn(modulate(xs, norm2_g[i], ms[:, 3], ms[:, 4]),
                                      ffn_w_up[i], ffn_conv_w[i], ffn_conv_b[i], ffn_w_down[i])
    y_prompt = rms_norm(xp, final_norm_g)
    y_sample = rms_norm(xs, final_norm_g)
    new_na_k = jnp.stack(na_k_new, axis=1)
    new_na_v = jnp.stack(na_v_new, axis=1)
    new_gqa_k = jnp.stack(gqa_k_new, axis=1)
    new_gqa_v = jnp.stack(gqa_v_new, axis=1)
    return (y_prompt, y_sample, new_na_k, new_na_v, new_gqa_k, new_gqa_v)
```

```python
import functools
import math

import jax
import jax.numpy as jnp
from jax import lax
from jax.experimental import pallas as pl
from jax.experimental.pallas import tpu as pltpu

f32 = jnp.float32
bf16 = jnp.bfloat16

D_MODEL = 2048
DEPTH = 2
GRID_W = 64
HEAD_DIM = 128
NA_HEADS = 12
NA_KR = 8
NA_KC = 16
FNET_GROUPS = 4
FNET_CH = 128
POOL_WINDOWS = (2, 4, 8, 16)
GQA_Q_HEADS = 12
GQA_KV_HEADS = 4
D_FF = 5632
ROPE_BASE = 10000.0
EPS = 1e-6
NEG_INF = -1e30
NA_WIDTH = NA_HEADS * HEAD_DIM
FNET_WIDTH = FNET_GROUPS * FNET_CH
POOL_WIDTH = len(POOL_WINDOWS) * FNET_CH
GQA_Q_WIDTH = GQA_Q_HEADS * HEAD_DIM
GQA_KV_WIDTH = GQA_KV_HEADS * HEAD_DIM
ATTN_SCALE = HEAD_DIM ** -0.5

MOD_ROWS = 16
VMEM_LIMIT = 56 * 1024 * 1024
HALO = 8
NA_QROWS = 4
NA_WROWS = 12
NA_INVALID = 2 * NA_KR - 1


def _params(*sem):
    return pltpu.CompilerParams(dimension_semantics=sem, vmem_limit_bytes=VMEM_LIMIT)


def _rms_modulate(x, g, shift, scale):
    ms = jnp.mean(x * x, axis=-1, keepdims=True)
    return (x * lax.rsqrt(ms + EPS) * g) * (1.0 + scale) + shift


def _mod_spec(k, tm, seq, base_row, per_seq):
    def index(m, n):
        return ((base_row + per_seq * ((m * tm) // seq)) * 6 + k, 0, 0)
    return pl.BlockSpec((None, 1, D_MODEL), index)


def _ada_kernel(c_ref, w_ref, b_ref, o_ref):
    c = c_ref[...]
    s = (c * jax.nn.sigmoid(c)).astype(bf16)
    o_ref[...] = jnp.dot(s, w_ref[...].astype(bf16), preferred_element_type=f32) + b_ref[...]


def ada_modulation(cond, ada_w, ada_b):
    tn = 1024
    n_out = 6 * D_MODEL
    return pl.pallas_call(
        _ada_kernel,
        grid=(DEPTH, n_out // tn),
        in_specs=[
            pl.BlockSpec((MOD_ROWS, D_MODEL), lambda i, n: (0, 0)),
            pl.BlockSpec((None, D_MODEL, tn), lambda i, n: (i, 0, n)),
            pl.BlockSpec((None, 1, tn), lambda i, n: (i, 0, n)),
        ],
        out_specs=pl.BlockSpec((None, MOD_ROWS, tn), lambda i, n: (i, 0, n)),
        out_shape=jax.ShapeDtypeStruct((DEPTH, MOD_ROWS, n_out), f32),
        compiler_params=_params("arbitrary", "arbitrary"),
        name="ada_modulation",
    )(cond, ada_w, ada_b.reshape(DEPTH, 1, n_out))


def _head_rms(a, g):
    ms = jnp.mean(a * a, axis=-1, keepdims=True)
    return a * lax.rsqrt(ms + EPS) * g


def _rope(a, cos, sin):
    lane = lax.broadcasted_iota(jnp.int32, a.shape, 1)
    quarter = HEAD_DIM // 4
    partner = jnp.where((lane & (2 * quarter - 1)) < quarter,
                        pltpu.roll(a, HEAD_DIM - quarter, 1),
                        pltpu.roll(a, quarter, 1))
    return a * cos + partner * sin


def _in_proj_kernel(segs, tn, has_norm, has_rope, *refs):
    it = iter(refs)
    x_ref, shift_ref, scale_ref, g_ref, w_ref = (next(it) for _ in range(5))
    qg_ref = kg_ref = cos_ref = sin_ref = None
    if has_norm:
        qg_ref, kg_ref = next(it), next(it)
    if has_rope:
        cos_ref, sin_ref = next(it), next(it)
    out_refs = [next(it) for _ in segs]
    h_ref = next(it)
    n = pl.program_id(1)

    @pl.when(n == 0)
    def _():
        h = _rms_modulate(x_ref[...], g_ref[...], shift_ref[...], scale_ref[...])
        h_ref[...] = h.astype(bf16)

    acc = jnp.dot(h_ref[...], w_ref[...], preferred_element_type=f32)

    def norm_heads(gn_ref, post_scale):
        cols = []
        for hh in range(tn // HEAD_DIM):
            a = _head_rms(acc[:, hh * HEAD_DIM:(hh + 1) * HEAD_DIM], gn_ref[...])
            if has_rope:
                a = _rope(a, cos_ref[...], sin_ref[...])
            if post_scale != 1.0:
                a = a * post_scale
            cols.append(a)
        return jnp.concatenate(cols, axis=-1)

    for (lo, hi, kind), o_ref in zip(segs, out_refs):
        @pl.when((n >= lo) & (n < hi))
        def _(kind=kind, o_ref=o_ref):
            if kind == "plain":
                r = acc
            elif kind == "scale":
                r = acc * ATTN_SCALE
            elif kind == "qnorm":
                r = norm_heads(qg_ref, ATTN_SCALE)
            else:
                r = norm_heads(kg_ref, 1.0)
            o_ref[...] = r.astype(o_ref.dtype)


def in_proj(x, mod, g, w, segs, seq, base_row, per_seq, norm_g=None, rope=None):
    t_tokens = x.shape[0]
    tm, tn = 512, 512
    n_total = w.shape[1]
    seg_blocks, out_shapes, out_specs = [], [], []
    lo = 0
    for width, dtype, kind in segs:
        nb = width // tn
        seg_blocks.append((lo, lo + nb, kind))
        out_shapes.append(jax.ShapeDtypeStruct((t_tokens, width), dtype))
        out_specs.append(pl.BlockSpec(
            (tm, tn), lambda m, n, lo=lo, nb=nb: (m, jnp.clip(n - lo, 0, nb - 1))))
        lo += nb
    assert lo * tn == n_total
    in_specs = [
        pl.BlockSpec((tm, D_MODEL), lambda m, n: (m, 0)),
        _mod_spec(0, tm, seq, base_row, per_seq),
        _mod_spec(1, tm, seq, base_row, per_seq),
        pl.BlockSpec((1, D_MODEL), lambda m, n: (0, 0)),
        pl.BlockSpec((D_MODEL, tn), lambda m, n: (0, n)),
    ]
    args = [x, mod, mod, g.reshape(1, D_MODEL), w]
    if norm_g is not None:
        in_specs += [pl.BlockSpec((1, HEAD_DIM), lambda m, n: (0, 0))] * 2
        args += [norm_g[0].reshape(1, HEAD_DIM), norm_g[1].reshape(1, HEAD_DIM)]
    if rope is not None:
        spb = seq // tm
        in_specs += [pl.BlockSpec((tm, HEAD_DIM), lambda m, n: (m % spb, 0))] * 2
        args += [rope[0], rope[1]]
    kern = functools.partial(_in_proj_kernel, tuple(seg_blocks), tn,
                             norm_g is not None, rope is not None)
    return pl.pallas_call(
        kern,
        grid=(t_tokens // tm, n_total // tn),
        in_specs=in_specs,
        out_specs=out_specs,
        out_shape=out_shapes,
        scratch_shapes=[pltpu.VMEM((tm, D_MODEL), bf16)],
        compiler_params=_params("arbitrary", "arbitrary"),
        name="in_proj",
    )(*args)


def _ctx_attn_kernel(n_q, n_kv, q_ref, k_ref, v_ref, o_ref):
    group = n_q // n_kv
    for h in range(n_q):
        kv = h // group
        q = q_ref[:, h * HEAD_DIM:(h + 1) * HEAD_DIM]
        k = k_ref[:, kv * HEAD_DIM:(kv + 1) * HEAD_DIM].astype(bf16)
        v = v_ref[:, kv * HEAD_DIM:(kv + 1) * HEAD_DIM].astype(bf16)
        s = lax.dot_general(q, k, (((1,), (1,)), ((), ())), preferred_element_type=f32)
        m = jnp.max(s, axis=-1, keepdims=True)
        e = jnp.exp(s - m)
        l = jnp.sum(e, axis=-1, keepdims=True)
        o = jnp.dot(e.astype(bf16), v, preferred_element_type=f32) / l
        o_ref[:, h * HEAD_DIM:(h + 1) * HEAD_DIM] = o.astype(o_ref.dtype)


def ctx_attention(q, k, v, seq, n_q, n_kv):
    t_tokens = q.shape[0]
    return pl.pallas_call(
        functools.partial(_ctx_attn_kernel, n_q, n_kv),
        grid=(t_tokens // seq,),
        in_specs=[
            pl.BlockSpec((seq, n_q * HEAD_DIM), lambda b: (b, 0)),
            pl.BlockSpec((seq, n_kv * HEAD_DIM), lambda b: (b, 0)),
            pl.BlockSpec((seq, n_kv * HEAD_DIM), lambda b: (b, 0)),
        ],
        out_specs=pl.BlockSpec((seq, n_q * HEAD_DIM), lambda b: (b, 0)),
        out_shape=jax.ShapeDtypeStruct((t_tokens, n_q * HEAD_DIM), bf16),
        compiler_params=_params("arbitrary"),
        name="ctx_attention",
    )(q, k, v)


def _gqa_attn_kernel(group, q_ref, ck_ref, cv_ref, k_ref, v_ref, o_ref):
    ck = ck_ref[...].astype(bf16)
    cv = cv_ref[...].astype(bf16)
    k = k_ref[...]
    v = v_ref[...]
    dn = (((1,), (1,)), ((), ()))
    for gi in range(group):
        q = q_ref[:, gi * HEAD_DIM:(gi + 1) * HEAD_DIM]
        s_c = lax.dot_general(q, ck, dn, preferred_element_type=f32)
        s_l = lax.dot_general(q, k, dn, preferred_element_type=f32)
        m = jnp.maximum(jnp.max(s_c, axis=-1, keepdims=True),
                        jnp.max(s_l, axis=-1, keepdims=True))
        e_c = jnp.exp(s_c - m)
        e_l = jnp.exp(s_l - m)
        l = jnp.sum(e_c, axis=-1, keepdims=True) + jnp.sum(e_l, axis=-1, keepdims=True)
        o = (jnp.dot(e_c.astype(bf16), cv, preferred_element_type=f32)
             + jnp.dot(e_l.astype(bf16), v, preferred_element_type=f32)) / l
        o_ref[:, gi * HEAD_DIM:(gi + 1) * HEAD_DIM] = o.astype(o_ref.dtype)


def gqa_attention(q, k, v, cache_k, cache_v, seq, past):
    t_tokens = q.shape[0]
    n_b = t_tokens // seq
    group = GQA_Q_HEADS // GQA_KV_HEADS
    tq = 256
    nq = seq // tq
    return pl.pallas_call(
        functools.partial(_gqa_attn_kernel, group),
        grid=(n_b, GQA_KV_HEADS, nq),
        in_specs=[
            pl.BlockSpec((tq, group * HEAD_DIM), lambda b, h, i: (b * nq + i, h)),
            pl.BlockSpec((past, HEAD_DIM), lambda b, h, i: (b, h)),
            pl.BlockSpec((past, HEAD_DIM), lambda b, h, i: (b, h)),
            pl.BlockSpec((seq, HEAD_DIM), lambda b, h, i: (b, h)),
            pl.BlockSpec((seq, HEAD_DIM), lambda b, h, i: (b, h)),
        ],
        out_specs=pl.BlockSpec((tq, group * HEAD_DIM), lambda b, h, i: (b * nq + i, h)),
        out_shape=jax.ShapeDtypeStruct((t_tokens, GQA_Q_WIDTH), bf16),
        compiler_params=_params("arbitrary", "arbitrary", "arbitrary"),
        name="gqa_attention",
    )(q, cache_k, cache_v, k, v)


def _na_bias_tiles(rel_bias):
    qc = jnp.arange(GRID_W)[:, None]
    kc = jnp.arange(GRID_W)[None, :]
    q_start = jnp.clip(qc - NA_KC // 2, 0, GRID_W - NA_KC)
    valid = (kc >= q_start) & (kc < q_start + NA_KC)
    dc = jnp.clip(kc - qc, -(NA_KC - 1), NA_KC - 1) + NA_KC - 1
    tiles = jnp.where(valid[None, None], rel_bias.astype(f32)[:, :, dc], NEG_INF)
    masked = jnp.full((NA_HEADS, 1, GRID_W, GRID_W), NEG_INF, f32)
    tiles = jnp.concatenate([tiles, masked], axis=1)
    zeros = jnp.zeros_like(tiles)
    left = jnp.concatenate([tiles, zeros], axis=-1)
    right = jnp.concatenate([zeros, tiles], axis=-1)
    return left, right


def _na_attn_kernel(n_rows, q_ref, k_ref, v_ref, ck_ref, cv_ref, bl_ref, br_ref, o_ref, s_ref):
    rb = pl.program_id(2)
    start = jnp.clip(rb * NA_QROWS - NA_KR // 2, 0, n_rows - NA_WROWS)
    tok0 = pl.multiple_of(start * GRID_W, GRID_W)
    kw = k_ref[pl.ds(tok0, NA_WROWS * GRID_W), :]
    vw = v_ref[pl.ds(tok0, NA_WROWS * GRID_W), :]
    q = q_ref[...]
    dn = (((1,), (1,)), ((), ()))
    s_ref[...] = lax.dot_general(q, kw, dn, preferred_element_type=f32)
    for i in range(NA_QROWS):
        qr = rb * NA_QROWS + i
        r0 = jnp.clip(qr - NA_KR // 2, 0, n_rows - NA_KR)
        for jp in range(NA_WROWS // 2):
            ka = start + 2 * jp
            kb = ka + 1
            ia = jnp.where((ka >= r0) & (ka < r0 + NA_KR), ka - qr + NA_KR - 1, NA_INVALID)
            ib = jnp.where((kb >= r0) & (kb < r0 + NA_KR), kb - qr + NA_KR - 1, NA_INVALID)
            rows = slice(i * GRID_W, (i + 1) * GRID_W)
            cols = slice(jp * 2 * GRID_W, (jp + 1) * 2 * GRID_W)
            s_ref[rows, cols] = s_ref[rows, cols] + bl_ref[ia] + br_ref[ib]
    s_l = s_ref[...]
    s_c = lax.dot_general(q, ck_ref[...].astype(bf16), dn, preferred_element_type=f32)
    m = jnp.maximum(jnp.max(s_c, axis=-1, keepdims=True), jnp.max(s_l, axis=-1, keepdims=True))
    e_c = jnp.exp(s_c - m)
    e_l = jnp.exp(s_l - m)
    l = jnp.sum(e_c, axis=-1, keepdims=True) + jnp.sum(e_l, axis=-1, keepdims=True)
    o = (jnp.dot(e_c.astype(bf16), cv_ref[...].astype(bf16), preferred_element_type=f32)
         + jnp.dot(e_l.astype(bf16), vw, preferred_element_type=f32)) / l
    o_ref[...] = o.astype(o_ref.dtype)


def na_attention(q, k, v, cache_k, cache_v, rel_bias, seq, past):
    t_tokens = q.shape[0]
    n_b = t_tokens // seq
    n_rows = seq // GRID_W
    n_rb = n_rows // NA_QROWS
    tq = NA_QROWS * GRID_W
    bias_l, bias_r = _na_bias_tiles(rel_bias)
    bias_spec = pl.BlockSpec((None, 2 * NA_KR, GRID_W, 2 * GRID_W), lambda b, h, r: (h, 0, 0, 0))
    return pl.pallas_call(
        functools.partial(_na_attn_kernel, n_rows),
        grid=(n_b, NA_HEADS, n_rb),
        in_specs=[
            pl.BlockSpec((tq, HEAD_DIM), lambda b, h, r: (b * n_rb + r, h)),
            pl.BlockSpec((seq, HEAD_DIM), lambda b, h, r: (b, h)),
            pl.BlockSpec((seq, HEAD_DIM), lambda b, h, r: (b, h)),
            pl.BlockSpec((past, HEAD_DIM), lambda b, h, r: (b, h)),
            pl.BlockSpec((past, HEAD_DIM), lambda b, h, r: (b, h)),
            bias_spec,
            bias_spec,
        ],
        out_specs=pl.BlockSpec((tq, HEAD_DIM), lambda b, h, r: (b * n_rb + r, h)),
        out_shape=jax.ShapeDtypeStruct((t_tokens, NA_WIDTH), bf16),
        scratch_shapes=[pltpu.VMEM((tq, NA_WROWS * GRID_W), f32)],
        compiler_params=_params("arbitrary", "arbitrary", "arbitrary"),
        name="na_attention",
    )(q, k, v, cache_k, cache_v, bias_l, bias_r)


def _dft_cols_kernel(x_ref, cs_ref, y_ref):
    for g in range(FNET_GROUPS):
        sl = slice(g * FNET_CH, (g + 1) * FNET_CH)
        y = jnp.dot(x_ref[:, sl].astype(bf16), cs_ref[...], preferred_element_type=f32)
        y_ref[0, :, sl] = y[:, :FNET_CH].astype(bf16)
        y_ref[1, :, sl] = y[:, FNET_CH:].astype(bf16)


def _dft_rows_kernel(d_ref, y_ref, w_ref, o_ref):
    f = jnp.dot(d_ref[...], y_ref[...], preferred_element_type=f32)
    for g in range(FNET_GROUPS):
        sl = slice(g * FNET_CH, (g + 1) * FNET_CH)
        o = jnp.dot(f[:, sl].astype(bf16), w_ref[g], preferred_element_type=f32)
        o_ref[:, sl] = o.astype(o_ref.dtype)


def _dft_tables(seq):
    c = jnp.arange(FNET_CH, dtype=jnp.int32)
    ang_c = (2.0 * math.pi / FNET_CH) * ((c[:, None] * c[None, :]) % FNET_CH).astype(f32)
    cs = jnp.concatenate([jnp.cos(ang_c), jnp.sin(ang_c)], axis=1).astype(bf16)
    n = jnp.arange(seq, dtype=jnp.int32)
    ang_n = (2.0 * math.pi / seq) * ((n[:, None] * n[None, :]) % seq).astype(f32)
    norm = 1.0 / math.sqrt(seq * FNET_CH)
    d = jnp.concatenate([jnp.cos(ang_n) * norm, -jnp.sin(ang_n) * norm], axis=1).astype(bf16)
    return cs, d


def fourier_mix(xb, fnet_w, seq):
    t_tokens = xb.shape[0]
    n_b = t_tokens // seq
    cs, d = _dft_tables(seq)
    tn = min(seq, 512)
    ns = seq // tn
    y = pl.pallas_call(
        _dft_cols_kernel,
        grid=(n_b, ns),
        in_specs=[
            pl.BlockSpec((tn, FNET_WIDTH), lambda b, i: (b * ns + i, 0)),
            pl.BlockSpec((FNET_CH, 2 * FNET_CH), lambda b, i: (0, 0)),
        ],
        out_specs=pl.BlockSpec((2, tn, FNET_WIDTH), lambda b, i: (0, i, b)),
        out_shape=jax.ShapeDtypeStruct((2, seq, n_b * FNET_WIDTH), bf16),
        compiler_params=_params("arbitrary", "arbitrary"),
        name="dft_cols",
    )(xb, cs)
    y2 = y.reshape(2 * seq, n_b * FNET_WIDTH)
    return pl.pallas_call(
        _dft_rows_kernel,
        grid=(n_b, ns),
        in_specs=[
            pl.BlockSpec((tn, 2 * seq), lambda b, i: (i, 0)),
            pl.BlockSpec((2 * seq, FNET_WIDTH), lambda b, i: (0, b)),
            pl.BlockSpec((FNET_GROUPS, FNET_CH, FNET_CH), lambda b, i: (0, 0, 0)),
        ],
        out_specs=pl.BlockSpec((tn, FNET_WIDTH), lambda b, i: (b * ns + i, 0)),
        out_shape=jax.ShapeDtypeStruct((t_tokens, FNET_WIDTH), bf16),
        compiler_params=_params("arbitrary", "arbitrary"),
        name="dft_rows",
    )(d, y2, fnet_w.astype(bf16))


def _pool_kernel(seq, x_ref, w_ref, sc_ref, o_ref):
    t = lax.broadcasted_iota(jnp.int32, (seq, 1), 0)

    def prev(a, k):
        return jnp.where(t >= k, pltpu.roll(a, k, 0), 0.0)

    def nxt(a, k):
        return jnp.where(t < seq - k, pltpu.roll(a, seq - k, 0), 0.0)

    for g, win in enumerate(POOL_WINDOWS):
        sl = slice(g * FNET_CH, (g + 1) * FNET_CH)
        x = x_ref[:, sl]
        half = win // 2
        back, fwd, k = x, x, 1
        while k < half:
            back = back + prev(back, k)
            fwd = fwd + nxt(fwd, k)
            k *= 2
        total = prev(back, 1) + fwd
        cnt = (jnp.minimum(t + half, seq) - jnp.maximum(t - half, 0)).astype(f32)
        pooled = (total / cnt - x).astype(bf16)
        o = jnp.dot(pooled, w_ref[g], preferred_element_type=f32) * sc_ref[:, sl]
        o_ref[:, sl] = o.astype(o_ref.dtype)


def pool_mix(xc, pool_w, pool_scale, seq):
    t_tokens = xc.shape[0]
    return pl.pallas_call(
        functools.partial(_pool_kernel, seq),
        grid=(t_tokens // seq,),
        in_specs=[
            pl.BlockSpec((seq, POOL_WIDTH), lambda b: (b, 0)),
            pl.BlockSpec((len(POOL_WINDOWS), FNET_CH, FNET_CH), lambda b: (0, 0, 0)),
            pl.BlockSpec((1, POOL_WIDTH), lambda b: (0, 0)),
        ],
        out_specs=pl.BlockSpec((seq, POOL_WIDTH), lambda b: (b, 0)),
        out_shape=jax.ShapeDtypeStruct((t_tokens, POOL_WIDTH), bf16),
        compiler_params=_params("arbitrary"),
        name="pool_mix",
    )(xc, pool_w.astype(bf16), pool_scale.reshape(1, POOL_WIDTH))


def _out_proj_kernel(x_ref, gate_ref, a_ref, b_ref, wa_ref, wb_ref, o_ref):
    y = (jnp.dot(a_ref[...], wa_ref[...], preferred_element_type=f32)
         + jnp.dot(b_ref[...], wb_ref[...], preferred_element_type=f32))
    o_ref[...] = x_ref[...] + gate_ref[...] * y


def out_proj(x, mod, a, b, wa, wb, seq, base_row, per_seq):
    t_tokens = x.shape[0]
    tm = 512
    return pl.pallas_call(
        _out_proj_kernel,
        grid=(t_tokens // tm, 1),
        in_specs=[
            pl.BlockSpec((tm, D_MODEL), lambda m, n: (m, 0)),
            _mod_spec(2, tm, seq, base_row, per_seq),
            pl.BlockSpec((tm, a.shape[1]), lambda m, n: (m, 0)),
            pl.BlockSpec((tm, b.shape[1]), lambda m, n: (m, 0)),
            pl.BlockSpec(wa.shape, lambda m, n: (0, 0)),
            pl.BlockSpec(wb.shape, lambda m, n: (0, 0)),
        ],
        out_specs=pl.BlockSpec((tm, D_MODEL), lambda m, n: (m, 0)),
        out_shape=jax.ShapeDtypeStruct((t_tokens, D_MODEL), f32),
        compiler_params=_params("arbitrary", "arbitrary"),
        name="out_proj",
    )(x, mod, a, b, wa, wb)


def _ffn_kernel(tm, seq, final, *refs):
    it = iter(refs)
    (x_ref, xp_ref, xn_ref, shift_ref, scale_ref, gate_ref, g_ref,
     wv_ref, wg_ref, cwv_ref, cwg_ref, cbv_ref, cbg_ref, wd_ref) = (next(it) for _ in range(14))
    gf_ref = next(it) if final else None
    o_ref, h_ref, acc_ref = next(it), next(it), next(it)
    i = pl.program_id(0)
    j = pl.program_id(1)
    rows = tm + 2 * HALO

    @pl.when(j == 0)
    def _():
        g, sh, sc = g_ref[...], shift_ref[...], scale_ref[...]
        h_ref[0:tm, :] = _rms_modulate(x_ref[...], g, sh, sc).astype(bf16)
        halo = jnp.concatenate([xn_ref[...], xp_ref[...]], axis=0)
        h_ref[tm:rows, :] = _rms_modulate(halo, g, sh, sc).astype(bf16)
        acc_ref[...] = jnp.zeros_like(acc_ref)

    pos = (i * tm + lax.broadcasted_iota(jnp.int32, (rows, 1), 0)) & (seq - 1)
    has_prev = (pos != 0).astype(f32)
    has_next = (pos != seq - 1).astype(f32)
    h = h_ref[...]

    def conv(w_ref, cw_ref, cb_ref):
        u = jnp.dot(h, w_ref[...], preferred_element_type=f32)
        up = pltpu.roll(u, 1, 0) * has_prev
        un = pltpu.roll(u, rows - 1, 0) * has_next
        c = up * cw_ref[0:1, :] + u * cw_ref[1:2, :] + un * cw_ref[2:3, :] + cb_ref[...]
        return c[0:tm, :]

    val = conv(wv_ref, cwv_ref, cbv_ref)
    gate = conv(wg_ref, cwg_ref, cbg_ref)
    act = (gate * jax.nn.sigmoid(gate) * val).astype(bf16)
    acc_ref[...] += jnp.dot(act, wd_ref[...], preferred_element_type=f32)

    @pl.when(j == pl.num_programs(1) - 1)
    def _():
        y = x_ref[...] + gate_ref[...] * acc_ref[...]
        if final:
            ms = jnp.mean(y * y, axis=-1, keepdims=True)
            y = y * lax.rsqrt(ms + EPS) * gf_ref[...]
        o_ref[...] = y


def conv_ffn(x, mod, g, w_up, conv_w, conv_b, w_down, seq, base_row, per_seq, final_g=None):
    t_tokens = x.shape[0]
    tm, tf = 512, 512
    assert seq & (seq - 1) == 0
    nj = D_FF // tf
    hb = tm // HALO
    n_hblk = t_tokens // HALO
    conv_b2 = conv_b.reshape(1, 2 * D_FF)
    in_specs = [
        pl.BlockSpec((tm, D_MODEL), lambda i, j: (i, 0)),
        pl.BlockSpec((HALO, D_MODEL), lambda i, j: (jnp.maximum(i * hb - 1, 0), 0)),
        pl.BlockSpec((HALO, D_MODEL), lambda i, j: (jnp.minimum((i + 1) * hb, n_hblk - 1), 0)),
        _mod_spec(3, tm, seq, base_row, per_seq),
        _mod_spec(4, tm, seq, base_row, per_seq),
        _mod_spec(5, tm, seq, base_row, per_seq),
        pl.BlockSpec((1, D_MODEL), lambda i, j: (0, 0)),
        pl.BlockSpec((D_MODEL, tf), lambda i, j: (0, j)),
        pl.BlockSpec((D_MODEL, tf), lambda i, j: (0, nj + j)),
        pl.BlockSpec((3, tf), lambda i, j: (0, j)),
        pl.BlockSpec((3, tf), lambda i, j: (0, nj + j)),
        pl.BlockSpec((1, tf), lambda i, j: (0, j)),
        pl.BlockSpec((1, tf), lambda i, j: (0, nj + j)),
        pl.BlockSpec((tf, D_MODEL), lambda i, j: (j, 0)),
    ]
    args = [x, x, x, mod, mod, mod, g.reshape(1, D_MODEL), w_up, w_up,
            conv_w, conv_w, conv_b2, conv_b2, w_down]
    if final_g is not None:
        in_specs.append(pl.BlockSpec((1, D_MODEL), lambda i, j: (0, 0)))
        args.append(final_g.reshape(1, D_MODEL))
    return pl.pallas_call(
        functools.partial(_ffn_kernel, tm, seq, final_g is not None),
        grid=(t_tokens // tm, nj),
        in_specs=in_specs,
        out_specs=pl.BlockSpec((tm, D_MODEL), lambda i, j: (i, 0)),
        out_shape=jax.ShapeDtypeStruct((t_tokens, D_MODEL), f32),
        scratch_shapes=[pltpu.VMEM((tm + 2 * HALO, D_MODEL), bf16),
                        pltpu.VMEM((tm, D_MODEL), f32)],
        compiler_params=_params("arbitrary", "arbitrary"),
        name="conv_ffn",
    )(*args)


def _rope_tables(seq):
    t = jnp.arange(seq)
    rows = (t // GRID_W).astype(f32)
    cols = (t % GRID_W).astype(f32)
    half = HEAD_DIM // 2
    inv_freq = jnp.power(ROPE_BASE, -jnp.arange(0, half, 2, dtype=f32) / half)
    ang_r = rows[:, None] * inv_freq[None, :]
    ang_c = cols[:, None] * inv_freq[None, :]
    cos = jnp.concatenate([jnp.cos(ang_r)] * 2 + [jnp.cos(ang_c)] * 2, axis=-1)
    sin = jnp.concatenate([-jnp.sin(ang_r), jnp.sin(ang_r), -jnp.sin(ang_c), jnp.sin(ang_c)], axis=-1)
    return cos, sin


def kernel(x_prompt, x_sample, c, cache_na_k, cache_na_v, cache_gqa_k, cache_gqa_v, c_ctx, norm1_g, norm2_g, ada_w, ada_b, ev_w_in, ev_na_bias, ev_fnet_w, ev_w_out, od_w_in, od_pool_w, od_pool_scale, od_q_norm_g, od_k_norm_g, od_w_out, ffn_w_up, ffn_conv_w, ffn_conv_b, ffn_w_down, final_norm_g):
    n_ctx, seq_p, _ = x_prompt.shape
    n_lat, seq_s, _ = x_sample.shape
    past = cache_na_k.shape[2]
    assert n_lat + 1 <= MOD_ROWS
    xp = x_prompt.reshape(n_ctx * seq_p, D_MODEL)
    xs = x_sample.reshape(n_lat * seq_s, D_MODEL)

    cond = jnp.zeros((MOD_ROWS, D_MODEL), f32).at[0].set(c_ctx).at[1:1 + n_lat].set(c)
    mods = ada_modulation(cond, ada_w, ada_b).reshape(DEPTH, MOD_ROWS * 6, 1, D_MODEL)
    rope = _rope_tables(seq_s)

    ctx = dict(seq=seq_p, base_row=0, per_seq=0)
    lat = dict(seq=seq_s, base_row=1, per_seq=1)
    new_k, new_v = {}, {}
    for i in range(DEPTH):
        j = i // 2
        mod = mods[i]
        w_up = ffn_w_up[i].astype(bf16)
        w_down = ffn_w_down[i].astype(bf16)
        if i % 2 == 0:
            w_in = ev_w_in[j].astype(bf16)
            w_attn = ev_w_out[j, :NA_WIDTH].astype(bf16)
            w_mix = ev_w_out[j, NA_WIDTH:].astype(bf16)
            wide = (NA_WIDTH, f32, "plain")
            segs_p = [(NA_WIDTH, bf16, "scale"), wide, wide, (FNET_WIDTH, f32, "plain")]
            q, k, v, xb = in_proj(xp, mod, norm1_g[i], w_in, segs_p, **ctx)
            new_k[i], new_v[i] = k, v
            attn_p = ctx_attention(q, k, v, seq_p, NA_HEADS, NA_HEADS)
            mix_p = fourier_mix(xb, ev_fnet_w[j], seq_p)
            segs_s = [(NA_WIDTH, bf16, "scale"), (NA_WIDTH, bf16, "plain"),
                      (NA_WIDTH, bf16, "plain"), (FNET_WIDTH, f32, "plain")]
            q, k, v, xb = in_proj(xs, mod, norm1_g[i], w_in, segs_s, **lat)
            attn_s = na_attention(q, k, v,
                                  cache_na_k[:, j].reshape(n_lat * past, NA_WIDTH),
                                  cache_na_v[:, j].reshape(n_lat * past, NA_WIDTH),
                                  ev_na_bias[j], seq_s, past)
            mix_s = fourier_mix(xb, ev_fnet_w[j], seq_s)
        else:
            w_in = od_w_in[j].astype(bf16)
            w_attn = od_w_out[j, POOL_WIDTH:].astype(bf16)
            w_mix = od_w_out[j, :POOL_WIDTH].astype(bf16)
            norm_g = (od_q_norm_g[j], od_k_norm_g[j])
            segs_p = [(POOL_WIDTH, f32, "plain"), (GQA_Q_WIDTH, bf16, "qnorm"),
                      (GQA_KV_WIDTH, f32, "knorm"), (GQA_KV_WIDTH, f32, "plain")]
            xc, q, k, v = in_proj(xp, mod, norm1_g[i], w_in, segs_p, norm_g=norm_g, **ctx)
            new_k[i], new_v[i] = k, v
            attn_p = ctx_attention(q, k, v, seq_p, GQA_Q_HEADS, GQA_KV_HEADS)
            mix_p = pool_mix(xc, od_pool_w[j], od_pool_scale[j], seq_p)
            segs_s = [(POOL_WIDTH, f32, "plain"), (GQA_Q_WIDTH, bf16, "qnorm"),
                      (GQA_KV_WIDTH, bf16, "knorm"), (GQA_KV_WIDTH, bf16, "plain")]
            xc, q, k, v = in_proj(xs, mod, norm1_g[i], w_in, segs_s, norm_g=norm_g, rope=rope, **lat)
            attn_s = gqa_attention(q, k, v,
                                   cache_gqa_k[:, j].reshape(n_lat * past, GQA_KV_WIDTH),
                                   cache_gqa_v[:, j].reshape(n_lat * past, GQA_KV_WIDTH),
                                   seq_s, past)
            mix_s = pool_mix(xc, od_pool_w[j], od_pool_scale[j], seq_s)
        xp = out_proj(xp, mod, attn_p, mix_p, w_attn, w_mix, **ctx)
        xs = out_proj(xs, mod, attn_s, mix_s, w_attn, w_mix, **lat)
        final_g = final_norm_g if i == DEPTH - 1 else None
        xp = conv_ffn(xp, mod, norm2_g[i], w_up, ffn_conv_w[i], ffn_conv_b[i], w_down,
                      final_g=final_g, **ctx)
        xs = conv_ffn(xs, mod, norm2_g[i], w_up, ffn_conv_w[i], ffn_conv_b[i], w_down,
                      final_g=final_g, **lat)

    y_prompt = xp.reshape(n_ctx, seq_p, D_MODEL)
    y_sample = xs.reshape(n_lat, seq_s, D_MODEL)
    even = [i for i in range(DEPTH) if i % 2 == 0]
    odd = [i for i in range(DEPTH) if i % 2 == 1]

    def stack(parts, layers, heads):
        return jnp.stack([parts[i].reshape(n_ctx, seq_p, heads, HEAD_DIM) for i in layers], axis=1)

    return (y_prompt, y_sample,
            stack(new_k, even, NA_HEADS), stack(new_v, even, NA_HEADS),
            stack(new_k, odd, GQA_KV_HEADS), stack(new_v, odd, GQA_KV_HEADS))
```

```python
import functools
import math

import jax
import jax.numpy as jnp
from jax import lax
from jax.experimental import pallas as pl
from jax.experimental.pallas import tpu as pltpu

f32 = jnp.float32
bf16 = jnp.bfloat16

D_MODEL = 2048
DEPTH = 2
GRID_W = 64
HEAD_DIM = 128
NA_HEADS = 12
NA_KR = 8
NA_KC = 16
FNET_GROUPS = 4
FNET_CH = 128
POOL_WINDOWS = (2, 4, 8, 16)
GQA_Q_HEADS = 12
GQA_KV_HEADS = 4
D_FF = 5632
ROPE_BASE = 10000.0
EPS = 1e-6
NEG_INF = -1e30
NA_WIDTH = NA_HEADS * HEAD_DIM
FNET_WIDTH = FNET_GROUPS * FNET_CH
POOL_WIDTH = len(POOL_WINDOWS) * FNET_CH
GQA_Q_WIDTH = GQA_Q_HEADS * HEAD_DIM
GQA_KV_WIDTH = GQA_KV_HEADS * HEAD_DIM
ATTN_SCALE = HEAD_DIM ** -0.5

MOD_ROWS = 16
VMEM_LIMIT = 56 * 1024 * 1024
HALO = 8
NA_QROWS = 4
NA_WROWS = 12
NA_INVALID = 2 * NA_KR - 1
FFN_ROWS = 128
FFN_SPLIT = 1


def _params(*sem, flags=None):
    return pltpu.CompilerParams(dimension_semantics=sem, vmem_limit_bytes=VMEM_LIMIT, flags=flags)


def _rms_modulate(x, g, shift, scale):
    ms = jnp.mean(x * x, axis=-1, keepdims=True)
    return (x * lax.rsqrt(ms + EPS) * g) * (1.0 + scale) + shift


def _mod_spec(k, tm, seq, base_row, per_seq):
    def index(m, n):
        return ((base_row + per_seq * ((m * tm) // seq)) * 6 + k, 0, 0)
    return pl.BlockSpec((None, 1, D_MODEL), index)


def _ada_kernel(c_ref, w_ref, b_ref, o_ref):
    c = c_ref[...]
    s = (c * jax.nn.sigmoid(c)).astype(bf16)
    o_ref[...] = jnp.dot(s, w_ref[...].astype(bf16), preferred_element_type=f32) + b_ref[...]


def ada_modulation(cond, ada_w, ada_b):
    tn = 1024
    n_out = 6 * D_MODEL
    return pl.pallas_call(
        _ada_kernel,
        grid=(DEPTH, n_out // tn),
        in_specs=[
            pl.BlockSpec((MOD_ROWS, D_MODEL), lambda i, n: (0, 0)),
            pl.BlockSpec((None, D_MODEL, tn), lambda i, n: (i, 0, n)),
            pl.BlockSpec((None, 1, tn), lambda i, n: (i, 0, n)),
        ],
        out_specs=pl.BlockSpec((None, MOD_ROWS, tn), lambda i, n: (i, 0, n)),
        out_shape=jax.ShapeDtypeStruct((DEPTH, MOD_ROWS, n_out), f32),
        compiler_params=_params("arbitrary", "arbitrary"),
        name="ada_modulation",
    )(cond, ada_w, ada_b.reshape(DEPTH, 1, n_out))


def _head_rms(a, g):
    ms = jnp.mean(a * a, axis=-1, keepdims=True)
    return a * lax.rsqrt(ms + EPS) * g


def _rope(a, cos, sin):
    lane = lax.broadcasted_iota(jnp.int32, a.shape, 1)
    quarter = HEAD_DIM // 4
    partner = jnp.where((lane & (2 * quarter - 1)) < quarter,
                        pltpu.roll(a, HEAD_DIM - quarter, 1),
                        pltpu.roll(a, quarter, 1))
    return a * cos + partner * sin


def _in_proj_kernel(segs, tn, has_norm, has_rope, *refs):
    it = iter(refs)
    x_ref, shift_ref, scale_ref, g_ref, w_ref = (next(it) for _ in range(5))
    qg_ref = kg_ref = cos_ref = sin_ref = None
    if has_norm:
        qg_ref, kg_ref = next(it), next(it)
    if has_rope:
        cos_ref, sin_ref = next(it), next(it)
    out_refs = [next(it) for _ in segs]
    h_ref = next(it)
    n = pl.program_id(1)

    @pl.when(n == 0)
    def _():
        h = _rms_modulate(x_ref[...], g_ref[...], shift_ref[...], scale_ref[...])
        h_ref[...] = h.astype(bf16)

    acc = jnp.dot(h_ref[...], w_ref[...], preferred_element_type=f32)

    def norm_heads(gn_ref, post_scale):
        cols = []
        for hh in range(tn // HEAD_DIM):
            a = _head_rms(acc[:, hh * HEAD_DIM:(hh + 1) * HEAD_DIM], gn_ref[...])
            if has_rope:
                a = _rope(a, cos_ref[...], sin_ref[...])
            if post_scale != 1.0:
                a = a * post_scale
            cols.append(a)
        return jnp.concatenate(cols, axis=-1)

    for (lo, hi, kind), o_ref in zip(segs, out_refs):
        @pl.when((n >= lo) & (n < hi))
        def _(kind=kind, o_ref=o_ref):
            if kind == "plain":
                r = acc
            elif kind == "scale":
                r = acc * ATTN_SCALE
            elif kind == "qnorm":
                r = norm_heads(qg_ref, ATTN_SCALE)
            else:
                r = norm_heads(kg_ref, 1.0)
            o_ref[...] = r.astype(o_ref.dtype)


def in_proj(x, mod, g, w, segs, seq, base_row, per_seq, norm_g=None, rope=None):
    t_tokens = x.shape[0]
    tm, tn = 512, 512
    n_total = w.shape[1]
    seg_blocks, out_shapes, out_specs = [], [], []
    lo = 0
    for width, dtype, kind in segs:
        nb = width // tn
        seg_blocks.append((lo, lo + nb, kind))
        out_shapes.append(jax.ShapeDtypeStruct((t_tokens, width), dtype))
        out_specs.append(pl.BlockSpec(
            (tm, tn), lambda m, n, lo=lo, nb=nb: (m, jnp.clip(n - lo, 0, nb - 1))))
        lo += nb
    assert lo * tn == n_total
    in_specs = [
        pl.BlockSpec((tm, D_MODEL), lambda m, n: (m, 0)),
        _mod_spec(0, tm, seq, base_row, per_seq),
        _mod_spec(1, tm, seq, base_row, per_seq),
        pl.BlockSpec((1, D_MODEL), lambda m, n: (0, 0)),
        pl.BlockSpec((D_MODEL, tn), lambda m, n: (0, n)),
    ]
    args = [x, mod, mod, g.reshape(1, D_MODEL), w]
    if norm_g is not None:
        in_specs += [pl.BlockSpec((1, HEAD_DIM), lambda m, n: (0, 0))] * 2
        args += [norm_g[0].reshape(1, HEAD_DIM), norm_g[1].reshape(1, HEAD_DIM)]
    if rope is not None:
        spb = seq // tm
        in_specs += [pl.BlockSpec((tm, HEAD_DIM), lambda m, n: (m % spb, 0))] * 2
        args += [rope[0], rope[1]]
    kern = functools.partial(_in_proj_kernel, tuple(seg_blocks), tn,
                             norm_g is not None, rope is not None)
    return pl.pallas_call(
        kern,
        grid=(t_tokens // tm, n_total // tn),
        in_specs=in_specs,
        out_specs=out_specs,
        out_shape=out_shapes,
        scratch_shapes=[pltpu.VMEM((tm, D_MODEL), bf16)],
        compiler_params=_params("arbitrary", "arbitrary"),
        name="in_proj",
    )(*args)


def _ctx_attn_kernel(n_q, n_kv, q_ref, k_ref, v_ref, o_ref):
    group = n_q // n_kv
    for h in range(n_q):
        kv = h // group
        q = q_ref[:, h * HEAD_DIM:(h + 1) * HEAD_DIM]
        k = k_ref[:, kv * HEAD_DIM:(kv + 1) * HEAD_DIM].astype(bf16)
        v = v_ref[:, kv * HEAD_DIM:(kv + 1) * HEAD_DIM].astype(bf16)
        s = lax.dot_general(q, k, (((1,), (1,)), ((), ())), preferred_element_type=f32)
        m = jnp.max(s, axis=-1, keepdims=True)
        e = jnp.exp(s - m)
        l = jnp.sum(e, axis=-1, keepdims=True)
        o = jnp.dot(e.astype(bf16), v, preferred_element_type=f32) / l
        o_ref[:, h * HEAD_DIM:(h + 1) * HEAD_DIM] = o.astype(o_ref.dtype)


def ctx_attention(q, k, v, seq, n_q, n_kv):
    t_tokens = q.shape[0]
    return pl.pallas_call(
        functools.partial(_ctx_attn_kernel, n_q, n_kv),
        grid=(t_tokens // seq,),
        in_specs=[
            pl.BlockSpec((seq, n_q * HEAD_DIM), lambda b: (b, 0)),
            pl.BlockSpec((seq, n_kv * HEAD_DIM), lambda b: (b, 0)),
            pl.BlockSpec((seq, n_kv * HEAD_DIM), lambda b: (b, 0)),
        ],
        out_specs=pl.BlockSpec((seq, n_q * HEAD_DIM), lambda b: (b, 0)),
        out_shape=jax.ShapeDtypeStruct((t_tokens, n_q * HEAD_DIM), bf16),
        compiler_params=_params("arbitrary"),
        name="ctx_attention",
    )(q, k, v)


def _gqa_attn_kernel(group, q_ref, ck_ref, cv_ref, k_ref, v_ref, o_ref):
    ck = ck_ref[...].astype(bf16)
    cv = cv_ref[...].astype(bf16)
    k = k_ref[...]
    v = v_ref[...]
    dn = (((1,), (1,)), ((), ()))
    for gi in range(group):
        q = q_ref[:, gi * HEAD_DIM:(gi + 1) * HEAD_DIM]
        s_c = lax.dot_general(q, ck, dn, preferred_element_type=f32)
        s_l = lax.dot_general(q, k, dn, preferred_element_type=f32)
        m = jnp.maximum(jnp.max(s_c, axis=-1, keepdims=True),
                        jnp.max(s_l, axis=-1, keepdims=True))
        e_c = jnp.exp(s_c - m)
        e_l = jnp.exp(s_l - m)
        l = jnp.sum(e_c, axis=-1, keepdims=True) + jnp.sum(e_l, axis=-1, keepdims=True)
        o = (jnp.dot(e_c.astype(bf16), cv, preferred_element_type=f32)
             + jnp.dot(e_l.astype(bf16), v, preferred_element_type=f32)) / l
        o_ref[:, gi * HEAD_DIM:(gi + 1) * HEAD_DIM] = o.astype(o_ref.dtype)


def gqa_attention(q, k, v, cache_k, cache_v, seq, past):
    t_tokens = q.shape[0]
    n_b = t_tokens // seq
    group = GQA_Q_HEADS // GQA_KV_HEADS
    tq = 256
    nq = seq // tq
    return pl.pallas_call(
        functools.partial(_gqa_attn_kernel, group),
        grid=(n_b, GQA_KV_HEADS, nq),
        in_specs=[
            pl.BlockSpec((tq, group * HEAD_DIM), lambda b, h, i: (b * nq + i, h)),
            pl.BlockSpec((past, HEAD_DIM), lambda b, h, i: (b, h)),
            pl.BlockSpec((past, HEAD_DIM), lambda b, h, i: (b, h)),
            pl.BlockSpec((seq, HEAD_DIM), lambda b, h, i: (b, h)),
            pl.BlockSpec((seq, HEAD_DIM), lambda b, h, i: (b, h)),
        ],
        out_specs=pl.BlockSpec((tq, group * HEAD_DIM), lambda b, h, i: (b * nq + i, h)),
        out_shape=jax.ShapeDtypeStruct((t_tokens, GQA_Q_WIDTH), bf16),
        compiler_params=_params("arbitrary", "arbitrary", "arbitrary"),
        name="gqa_attention",
    )(q, cache_k, cache_v, k, v)


def _na_bias_tiles(rel_bias):
    qc = jnp.arange(GRID_W)[:, None]
    kc = jnp.arange(GRID_W)[None, :]
    q_start = jnp.clip(qc - NA_KC // 2, 0, GRID_W - NA_KC)
    valid = (kc >= q_start) & (kc < q_start + NA_KC)
    dc = jnp.clip(kc - qc, -(NA_KC - 1), NA_KC - 1) + NA_KC - 1
    tiles = jnp.where(valid[None, None], rel_bias.astype(f32)[:, :, dc], NEG_INF)
    masked = jnp.full((NA_HEADS, 1, GRID_W, GRID_W), NEG_INF, f32)
    tiles = jnp.concatenate([tiles, masked], axis=1)
    zeros = jnp.zeros_like(tiles)
    left = jnp.concatenate([tiles, zeros], axis=-1)
    right = jnp.concatenate([zeros, tiles], axis=-1)
    return left, right


def _na_attn_kernel(n_rows, q_ref, k_ref, v_ref, ck_ref, cv_ref, bl_ref, br_ref, o_ref, s_ref):
    rb = pl.program_id(2)
    start = jnp.clip(rb * NA_QROWS - NA_KR // 2, 0, n_rows - NA_WROWS)
    tok0 = pl.multiple_of(start * GRID_W, GRID_W)
    kw = k_ref[pl.ds(tok0, NA_WROWS * GRID_W), :]
    vw = v_ref[pl.ds(tok0, NA_WROWS * GRID_W), :]
    q = q_ref[...]
    dn = (((1,), (1,)), ((), ()))
    s_ref[...] = lax.dot_general(q, kw, dn, preferred_element_type=f32)
    for i in range(NA_QROWS):
        qr = rb * NA_QROWS + i
        r0 = jnp.clip(qr - NA_KR // 2, 0, n_rows - NA_KR)
        for jp in range(NA_WROWS // 2):
            ka = start + 2 * jp
            kb = ka + 1
            ia = jnp.where((ka >= r0) & (ka < r0 + NA_KR), ka - qr + NA_KR - 1, NA_INVALID)
            ib = jnp.where((kb >= r0) & (kb < r0 + NA_KR), kb - qr + NA_KR - 1, NA_INVALID)
            rows = slice(i * GRID_W, (i + 1) * GRID_W)
            cols = slice(jp * 2 * GRID_W, (jp + 1) * 2 * GRID_W)
            s_ref[rows, cols] = s_ref[rows, cols] + bl_ref[ia] + br_ref[ib]
    s_l = s_ref[...]
    s_c = lax.dot_general(q, ck_ref[...].astype(bf16), dn, preferred_element_type=f32)
    m = jnp.maximum(jnp.max(s_c, axis=-1, keepdims=True), jnp.max(s_l, axis=-1, keepdims=True))
    e_c = jnp.exp(s_c - m)
    e_l = jnp.exp(s_l - m)
    l = jnp.sum(e_c, axis=-1, keepdims=True) + jnp.sum(e_l, axis=-1, keepdims=True)
    o = (jnp.dot(e_c.astype(bf16), cv_ref[...].astype(bf16), preferred_element_type=f32)
         + jnp.dot(e_l.astype(bf16), vw, preferred_element_type=f32)) / l
    o_ref[...] = o.astype(o_ref.dtype)


def na_attention(q, k, v, cache_k, cache_v, rel_bias, seq, past):
    t_tokens = q.shape[0]
    n_b = t_tokens // seq
    n_rows = seq // GRID_W
    n_rb = n_rows // NA_QROWS
    tq = NA_QROWS * GRID_W
    bias_l, bias_r = _na_bias_tiles(rel_bias)
    bias_spec = pl.BlockSpec((None, 2 * NA_KR, GRID_W, 2 * GRID_W), lambda b, h, r: (h, 0, 0, 0))
    return pl.pallas_call(
        functools.partial(_na_attn_kernel, n_rows),
        grid=(n_b, NA_HEADS, n_rb),
        in_specs=[
            pl.BlockSpec((tq, HEAD_DIM), lambda b, h, r: (b * n_rb + r, h)),
            pl.BlockSpec((seq, HEAD_DIM), lambda b, h, r: (b, h)),
            pl.BlockSpec((seq, HEAD_DIM), lambda b, h, r: (b, h)),
            pl.BlockSpec((past, HEAD_DIM), lambda b, h, r: (b, h)),
            pl.BlockSpec((past, HEAD_DIM), lambda b, h, r: (b, h)),
            bias_spec,
            bias_spec,
        ],
        out_specs=pl.BlockSpec((tq, HEAD_DIM), lambda b, h, r: (b * n_rb + r, h)),
        out_shape=jax.ShapeDtypeStruct((t_tokens, NA_WIDTH), bf16),
        scratch_shapes=[pltpu.VMEM((tq, NA_WROWS * GRID_W), f32)],
        compiler_params=_params("arbitrary", "arbitrary", "arbitrary"),
        name="na_attention",
    )(q, k, v, cache_k, cache_v, bias_l, bias_r)


def _dft_cols_kernel(x_ref, cs_ref, y_ref):
    for g in range(FNET_GROUPS):
        sl = slice(g * FNET_CH, (g + 1) * FNET_CH)
        y = jnp.dot(x_ref[:, sl].astype(bf16), cs_ref[...], preferred_element_type=f32)
        y_ref[0, :, sl] = y[:, :FNET_CH].astype(bf16)
        y_ref[1, :, sl] = y[:, FNET_CH:].astype(bf16)


def _dft_rows_kernel(d_ref, y_ref, w_ref, o_ref):
    f = jnp.dot(d_ref[...], y_ref[...], preferred_element_type=f32)
    for g in range(FNET_GROUPS):
        sl = slice(g * FNET_CH, (g + 1) * FNET_CH)
        o = jnp.dot(f[:, sl].astype(bf16), w_ref[g], preferred_element_type=f32)
        o_ref[:, sl] = o.astype(o_ref.dtype)


def _dft_tables(seq):
    c = jnp.arange(FNET_CH, dtype=jnp.int32)
    ang_c = (2.0 * math.pi / FNET_CH) * ((c[:, None] * c[None, :]) % FNET_CH).astype(f32)
    cs = jnp.concatenate([jnp.cos(ang_c), jnp.sin(ang_c)], axis=1).astype(bf16)
    n = jnp.arange(seq, dtype=jnp.int32)
    ang_n = (2.0 * math.pi / seq) * ((n[:, None] * n[None, :]) % seq).astype(f32)
    norm = 1.0 / math.sqrt(seq * FNET_CH)
    d = jnp.concatenate([jnp.cos(ang_n) * norm, -jnp.sin(ang_n) * norm], axis=1).astype(bf16)
    return cs, d


def fourier_mix(xb, fnet_w, seq):
    t_tokens = xb.shape[0]
    n_b = t_tokens // seq
    cs, d = _dft_tables(seq)
    tn = min(seq, 512)
    ns = seq // tn
    y = pl.pallas_call(
        _dft_cols_kernel,
        grid=(n_b, ns),
        in_specs=[
            pl.BlockSpec((tn, FNET_WIDTH), lambda b, i: (b * ns + i, 0)),
            pl.BlockSpec((FNET_CH, 2 * FNET_CH), lambda b, i: (0, 0)),
        ],
        out_specs=pl.BlockSpec((2, tn, FNET_WIDTH), lambda b, i: (0, i, b)),
        out_shape=jax.ShapeDtypeStruct((2, seq, n_b * FNET_WIDTH), bf16),
        compiler_params=_params("arbitrary", "arbitrary"),
        name="dft_cols",
    )(xb, cs)
    y2 = y.reshape(2 * seq, n_b * FNET_WIDTH)
    return pl.pallas_call(
        _dft_rows_kernel,
        grid=(n_b, ns),
        in_specs=[
            pl.BlockSpec((tn, 2 * seq), lambda b, i: (i, 0)),
            pl.BlockSpec((2 * seq, FNET_WIDTH), lambda b, i: (0, b)),
            pl.BlockSpec((FNET_GROUPS, FNET_CH, FNET_CH), lambda b, i: (0, 0, 0)),
        ],
        out_specs=pl.BlockSpec((tn, FNET_WIDTH), lambda b, i: (b * ns + i, 0)),
        out_shape=jax.ShapeDtypeStruct((t_tokens, FNET_WIDTH), bf16),
        compiler_params=_params("arbitrary", "arbitrary"),
        name="dft_rows",
    )(d, y2, fnet_w.astype(bf16))


def _pool_kernel(seq, x_ref, w_ref, sc_ref, o_ref):
    t = lax.broadcasted_iota(jnp.int32, (seq, 1), 0)

    def prev(a, k):
        return jnp.where(t >= k, pltpu.roll(a, k, 0), 0.0)

    def nxt(a, k):
        return jnp.where(t < seq - k, pltpu.roll(a, seq - k, 0), 0.0)

    for g, win in enumerate(POOL_WINDOWS):
        sl = slice(g * FNET_CH, (g + 1) * FNET_CH)
        x = x_ref[:, sl]
        half = win // 2
        back, fwd, k = x, x, 1
        while k < half:
            back = back + prev(back, k)
            fwd = fwd + nxt(fwd, k)
            k *= 2
        total = prev(back, 1) + fwd
        cnt = (jnp.minimum(t + half, seq) - jnp.maximum(t - half, 0)).astype(f32)
        pooled = (total / cnt - x).astype(bf16)
        o = jnp.dot(pooled, w_ref[g], preferred_element_type=f32) * sc_ref[:, sl]
        o_ref[:, sl] = o.astype(o_ref.dtype)


def pool_mix(xc, pool_w, pool_scale, seq):
    t_tokens = xc.shape[0]
    return pl.pallas_call(
        functools.partial(_pool_kernel, seq),
        grid=(t_tokens // seq,),
        in_specs=[
            pl.BlockSpec((seq, POOL_WIDTH), lambda b: (b, 0)),
            pl.BlockSpec((len(POOL_WINDOWS), FNET_CH, FNET_CH), lambda b: (0, 0, 0)),
            pl.BlockSpec((1, POOL_WIDTH), lambda b: (0, 0)),
        ],
        out_specs=pl.BlockSpec((seq, POOL_WIDTH), lambda b: (b, 0)),
        out_shape=jax.ShapeDtypeStruct((t_tokens, POOL_WIDTH), bf16),
        compiler_params=_params("arbitrary"),
        name="pool_mix",
    )(xc, pool_w.astype(bf16), pool_scale.reshape(1, POOL_WIDTH))


def _out_proj_kernel(x_ref, gate_ref, a_ref, b_ref, wa_ref, wb_ref, o_ref):
    y = (jnp.dot(a_ref[...], wa_ref[...], preferred_element_type=f32)
         + jnp.dot(b_ref[...], wb_ref[...], preferred_element_type=f32))
    o_ref[...] = x_ref[...] + gate_ref[...] * y


def out_proj(x, mod, a, b, wa, wb, seq, base_row, per_seq):
    t_tokens = x.shape[0]
    tm = 512
    return pl.pallas_call(
        _out_proj_kernel,
        grid=(t_tokens // tm, 1),
        in_specs=[
            pl.BlockSpec((tm, D_MODEL), lambda m, n: (m, 0)),
            _mod_spec(2, tm, seq, base_row, per_seq),
            pl.BlockSpec((tm, a.shape[1]), lambda m, n: (m, 0)),
            pl.BlockSpec((tm, b.shape[1]), lambda m, n: (m, 0)),
            pl.BlockSpec(wa.shape, lambda m, n: (0, 0)),
            pl.BlockSpec(wb.shape, lambda m, n: (0, 0)),
        ],
        out_specs=pl.BlockSpec((tm, D_MODEL), lambda m, n: (m, 0)),
        out_shape=jax.ShapeDtypeStruct((t_tokens, D_MODEL), f32),
        compiler_params=_params("arbitrary", "arbitrary"),
        name="out_proj",
    )(x, mod, a, b, wa, wb)


def _ffn_kernel(tm, seq, nj, final, *refs):
    it = iter(refs)
    (x_ref, xp_ref, xn_ref, shift_ref, scale_ref, gate_ref, g_ref,
     wv_ref, wg_ref, cwv_ref, cwg_ref, cbv_ref, cbg_ref, wd_ref) = (next(it) for _ in range(14))
    gf_ref = next(it) if final else None
    o_ref, h_ref, acc_ref = next(it), next(it), next(it)
    u_refs = [(next(it), next(it)), (next(it), next(it))]
    i = pl.program_id(0)
    j = pl.program_id(1)
    rows = tm + 2 * HALO

    def prologue():
        g, sh, sc = g_ref[...], shift_ref[...], scale_ref[...]
        h_ref[0:tm, :] = _rms_modulate(x_ref[...], g, sh, sc).astype(bf16)
        next_ok = jnp.where((((i + 1) * tm) & (seq - 1)) != 0, 1.0, 0.0)
        prev_ok = jnp.where(((i * tm) & (seq - 1)) != 0, 1.0, 0.0)
        h_next = _rms_modulate(xn_ref[...], g, sh, sc) * next_ok
        h_prev = _rms_modulate(xp_ref[...], g, sh, sc) * prev_ok
        h_ref[tm:rows, :] = jnp.concatenate([h_next, h_prev], axis=0).astype(bf16)
        acc_ref[...] = jnp.zeros_like(acc_ref)

    def zero_row(a, r):
        r0 = (r // HALO) * HALO
        row8 = lax.broadcasted_iota(jnp.int32, (HALO, 1), 0)
        fixed = jnp.where(row8 == r - r0, 0.0, a[r0:r0 + HALO])
        return jnp.concatenate([a[:r0], fixed, a[r0 + HALO:]], axis=0)

    def window(u_ref, r0):
        lo, hi = r0 - HALO, r0 + FFN_ROWS + HALO
        if lo < 0:
            return jnp.concatenate([u_ref[rows + lo:rows, :], u_ref[0:hi, :]], axis=0)
        return u_ref[lo:hi, :]

    def conv(u_ref, r0, cw_ref, cb_ref):
        u = window(u_ref, r0)
        n = FFN_ROWS + 2 * HALO
        up = pltpu.roll(u, 1, 0)
        un = pltpu.roll(u, n - 1, 0)
        for edge in range(seq, tm, seq):
            if r0 <= edge < r0 + FFN_ROWS:
                up = zero_row(up, edge - r0 + HALO)
            if r0 <= edge - 1 < r0 + FFN_ROWS:
                un = zero_row(un, edge - 1 - r0 + HALO)
        c = up * cw_ref[0:1, :] + u * cw_ref[1:2, :] + un * cw_ref[2:3, :] + cb_ref[...]
        return c[HALO:HALO + FFN_ROWS, :]

    def gated(slot):
        uv_ref, ug_ref = u_refs[slot]
        chunks = []
        for r0 in range(0, tm, FFN_ROWS):
            val = conv(uv_ref, r0, cwv_ref, cbv_ref)
            gate = conv(ug_ref, r0, cwg_ref, cbg_ref)
            chunks.append((gate * jax.nn.sigmoid(gate) * val).astype(bf16))
        return chunks

    def zero_after(chunks):
        parts = []
        for c in chunks:
            bits = pltpu.bitcast(c, jnp.uint32)
            parts += [bits[k:k + HALO] for k in range(0, bits.shape[0], HALO)]
        while len(parts) > 1:
            parts = [a | b for a, b in zip(parts[0::2], parts[1::2])] + parts[len(parts) & ~1:]
        return (parts[0] >> 16) >> 16

    def after(h, zero):
        tf = zero.shape[1]
        top = pltpu.bitcast(pltpu.bitcast(h[0:2 * HALO, 0:tf], jnp.uint32) | zero, bf16)
        if tf < h.shape[1]:
            top = jnp.concatenate([top, h[0:2 * HALO, tf:]], axis=1)
        return jnp.concatenate([top, h[2 * HALO:]], axis=0)

    def up_project(slot, chunks=None):
        h = h_ref[...]
        uv_ref, ug_ref = u_refs[slot]
        uv_ref[...] = jnp.dot(h, wv_ref[...], preferred_element_type=f32)
        if chunks is not None:
            h = after(h, zero_after(chunks[:FFN_SPLIT]))
        ug_ref[...] = jnp.dot(h, wg_ref[...], preferred_element_type=f32)

    def down_project(chunks, anchored):
        if anchored:
            chunks = [after(chunks[0], zero_after(chunks[FFN_SPLIT:]))] + chunks[1:]
        act = jnp.concatenate(chunks, axis=0)
        acc_ref[...] += jnp.dot(act, wd_ref[...], preferred_element_type=f32)

    @pl.when(j == 0)
    def _():
        prologue()
        up_project(0)

    for p in range(2):
        @pl.when((j >= 1) & (j < nj) & (j % 2 == p))
        def _(p=p):
            chunks = gated(1 - p)
            up_project(p, chunks)
            down_project(chunks, True)

    @pl.when(j == nj)
    def _():
        down_project(gated((nj - 1) % 2), False)
        y = x_ref[...] + gate_ref[...] * acc_ref[...]
        if final:
            ms = jnp.mean(y * y, axis=-1, keepdims=True)
            y = y * lax.rsqrt(ms + EPS) * gf_ref[...]
        o_ref[...] = y


def conv_ffn(x, mod, g, w_up, conv_w, conv_b, w_down, seq, base_row, per_seq, final_g=None):
    t_tokens = x.shape[0]
    tm, tf = 512, 512
    assert seq & (seq - 1) == 0
    nj = D_FF // tf
    hb = tm // HALO
    n_hblk = t_tokens // HALO
    conv_b2 = conv_b.reshape(1, 2 * D_FF)

    def blk(j):
        return jnp.clip(j, 0, nj - 1)

    in_specs = [
        pl.BlockSpec((tm, D_MODEL), lambda i, j: (i, 0)),
        pl.BlockSpec((HALO, D_MODEL), lambda i, j: (jnp.maximum(i * hb - 1, 0), 0)),
        pl.BlockSpec((HALO, D_MODEL), lambda i, j: (jnp.minimum((i + 1) * hb, n_hblk - 1), 0)),
        _mod_spec(3, tm, seq, base_row, per_seq),
        _mod_spec(4, tm, seq, base_row, per_seq),
        _mod_spec(5, tm, seq, base_row, per_seq),
        pl.BlockSpec((1, D_MODEL), lambda i, j: (0, 0)),
        pl.BlockSpec((D_MODEL, tf), lambda i, j: (0, blk(j))),
        pl.BlockSpec((D_MODEL, tf), lambda i, j: (0, nj + blk(j))),
        pl.BlockSpec((3, tf), lambda i, j: (0, blk(j - 1))),
        pl.BlockSpec((3, tf), lambda i, j: (0, nj + blk(j - 1))),
        pl.BlockSpec((1, tf), lambda i, j: (0, blk(j - 1))),
        pl.BlockSpec((1, tf), lambda i, j: (0, nj + blk(j - 1))),
        pl.BlockSpec((tf, D_MODEL), lambda i, j: (blk(j - 1), 0)),
    ]
    args = [x, x, x, mod, mod, mod, g.reshape(1, D_MODEL), w_up, w_up,
            conv_w, conv_w, conv_b2, conv_b2, w_down]
    if final_g is not None:
        in_specs.append(pl.BlockSpec((1, D_MODEL), lambda i, j: (0, 0)))
        args.append(final_g.reshape(1, D_MODEL))
    return pl.pallas_call(
        functools.partial(_ffn_kernel, tm, seq, nj, final_g is not None),
        grid=(t_tokens // tm, nj + 1),
        in_specs=in_specs,
        out_specs=pl.BlockSpec((tm, D_MODEL), lambda i, j: (i, 0)),
        out_shape=jax.ShapeDtypeStruct((t_tokens, D_MODEL), f32),
        scratch_shapes=[pltpu.VMEM((tm + 2 * HALO, D_MODEL), bf16),
                        pltpu.VMEM((tm, D_MODEL), f32)]
                       + [pltpu.VMEM((tm + 2 * HALO, tf), f32)] * 4,
        compiler_params=_params("arbitrary", "arbitrary"),
        name="conv_ffn",
    )(*args)


def _rope_tables(seq):
    t = jnp.arange(seq)
    rows = (t // GRID_W).astype(f32)
    cols = (t % GRID_W).astype(f32)
    half = HEAD_DIM // 2
    inv_freq = jnp.power(ROPE_BASE, -jnp.arange(0, half, 2, dtype=f32) / half)
    ang_r = rows[:, None] * inv_freq[None, :]
    ang_c = cols[:, None] * inv_freq[None, :]
    cos = jnp.concatenate([jnp.cos(ang_r)] * 2 + [jnp.cos(ang_c)] * 2, axis=-1)
    sin = jnp.concatenate([-jnp.sin(ang_r), jnp.sin(ang_r), -jnp.sin(ang_c), jnp.sin(ang_c)], axis=-1)
    return cos, sin


def kernel(x_prompt, x_sample, c, cache_na_k, cache_na_v, cache_gqa_k, cache_gqa_v, c_ctx, norm1_g, norm2_g, ada_w, ada_b, ev_w_in, ev_na_bias, ev_fnet_w, ev_w_out, od_w_in, od_pool_w, od_pool_scale, od_q_norm_g, od_k_norm_g, od_w_out, ffn_w_up, ffn_conv_w, ffn_conv_b, ffn_w_down, final_norm_g):
    n_ctx, seq_p, _ = x_prompt.shape
    n_lat, seq_s, _ = x_sample.shape
    past = cache_na_k.shape[2]
    assert n_lat + 1 <= MOD_ROWS
    xp = x_prompt.reshape(n_ctx * seq_p, D_MODEL)
    xs = x_sample.reshape(n_lat * seq_s, D_MODEL)

    cond = jnp.zeros((MOD_ROWS, D_MODEL), f32).at[0].set(c_ctx).at[1:1 + n_lat].set(c)
    mods = ada_modulation(cond, ada_w, ada_b).reshape(DEPTH, MOD_ROWS * 6, 1, D_MODEL)
    rope = _rope_tables(seq_s)

    ctx = dict(seq=seq_p, base_row=0, per_seq=0)
    lat = dict(seq=seq_s, base_row=1, per_seq=1)
    new_k, new_v = {}, {}
    for i in range(DEPTH):
        j = i // 2
        mod = mods[i]
        w_up = ffn_w_up[i].astype(bf16)
        w_down = ffn_w_down[i].astype(bf16)
        if i % 2 == 0:
            w_in = ev_w_in[j].astype(bf16)
            w_attn = ev_w_out[j, :NA_WIDTH].astype(bf16)
            w_mix = ev_w_out[j, NA_WIDTH:].astype(bf16)
            wide = (NA_WIDTH, f32, "plain")
            segs_p = [(NA_WIDTH, bf16, "scale"), wide, wide, (FNET_WIDTH, f32, "plain")]
            q, k, v, xb = in_proj(xp, mod, norm1_g[i], w_in, segs_p, **ctx)
            new_k[i], new_v[i] = k, v
            attn_p = ctx_attention(q, k, v, seq_p, NA_HEADS, NA_HEADS)
            mix_p = fourier_mix(xb, ev_fnet_w[j], seq_p)
            segs_s = [(NA_WIDTH, bf16, "scale"), (NA_WIDTH, bf16, "plain"),
                      (NA_WIDTH, bf16, "plain"), (FNET_WIDTH, f32, "plain")]
            q, k, v, xb = in_proj(xs, mod, norm1_g[i], w_in, segs_s, **lat)
            attn_s = na_attention(q, k, v,
                                  cache_na_k[:, j].reshape(n_lat * past, NA_WIDTH),
                                  cache_na_v[:, j].reshape(n_lat * past, NA_WIDTH),
                                  ev_na_bias[j], seq_s, past)
            mix_s = fourier_mix(xb, ev_fnet_w[j], seq_s)
        else:
            w_in = od_w_in[j].astype(bf16)
            w_attn = od_w_out[j, POOL_WIDTH:].astype(bf16)
            w_mix = od_w_out[j, :POOL_WIDTH].astype(bf16)
            norm_g = (od_q_norm_g[j], od_k_norm_g[j])
            segs_p = [(POOL_WIDTH, f32, "plain"), (GQA_Q_WIDTH, bf16, "qnorm"),
                      (GQA_KV_WIDTH, f32, "knorm"), (GQA_KV_WIDTH, f32, "plain")]
            xc, q, k, v = in_proj(xp, mod, norm1_g[i], w_in, segs_p, norm_g=norm_g, **ctx)
            new_k[i], new_v[i] = k, v
            attn_p = ctx_attention(q, k, v, seq_p, GQA_Q_HEADS, GQA_KV_HEADS)
            mix_p = pool_mix(xc, od_pool_w[j], od_pool_scale[j], seq_p)
            segs_s = [(POOL_WIDTH, f32, "plain"), (GQA_Q_WIDTH, bf16, "qnorm"),
                      (GQA_KV_WIDTH, bf16, "knorm"), (GQA_KV_WIDTH, bf16, "plain")]
            xc, q, k, v = in_proj(xs, mod, norm1_g[i], w_in, segs_s, norm_g=norm_g, rope=rope, **lat)
            attn_s = gqa_attention(q, k, v,
                                   cache_gqa_k[:, j].reshape(n_lat * past, GQA_KV_WIDTH),
                                   cache_gqa_v[:, j].reshape(n_lat * past, GQA_KV_WIDTH),
                                   seq_s, past)
            mix_s = pool_mix(xc, od_pool_w[j], od_pool_scale[j], seq_s)
        xp = out_proj(xp, mod, attn_p, mix_p, w_attn, w_mix, **ctx)
        xs = out_proj(xs, mod, attn_s, mix_s, w_attn, w_mix, **lat)
        final_g = final_norm_g if i == DEPTH - 1 else None
        xp = conv_ffn(xp, mod, norm2_g[i], w_up, ffn_conv_w[i], ffn_conv_b[i], w_down,
                      final_g=final_g, **ctx)
        xs = conv_ffn(xs, mod, norm2_g[i], w_up, ffn_conv_w[i], ffn_conv_b[i], w_down,
                      final_g=final_g, **lat)

    y_prompt = xp.reshape(n_ctx, seq_p, D_MODEL)
    y_sample = xs.reshape(n_lat, seq_s, D_MODEL)
    even = [i for i in range(DEPTH) if i % 2 == 0]
    odd = [i for i in range(DEPTH) if i % 2 == 1]

    def stack(parts, layers, heads):
        return jnp.stack([parts[i].reshape(n_ctx, seq_p, heads, HEAD_DIM) for i in layers], axis=1)

    return (y_prompt, y_sample,
            stack(new_k, even, NA_HEADS), stack(new_v, even, NA_HEADS),
            stack(new_k, odd, GQA_KV_HEADS), stack(new_v, odd, GQA_KV_HEADS))
```

```python
import functools
import math

import jax
import jax.numpy as jnp
import numpy as np
from jax import lax
from jax.experimental import pallas as pl
from jax.experimental.pallas import tpu as pltpu

f32 = jnp.float32
bf16 = jnp.bfloat16

D_MODEL = 2048
DEPTH = 2
GRID_W = 64
HEAD_DIM = 128
NA_HEADS = 12
NA_KR = 8
NA_KC = 16
FNET_GROUPS = 4
FNET_CH = 128
POOL_WINDOWS = (2, 4, 8, 16)
GQA_Q_HEADS = 12
GQA_KV_HEADS = 4
D_FF = 5632
ROPE_BASE = 10000.0
EPS = 1e-6
NEG_INF = -1e30
NA_WIDTH = NA_HEADS * HEAD_DIM
FNET_WIDTH = FNET_GROUPS * FNET_CH
POOL_WIDTH = len(POOL_WINDOWS) * FNET_CH
GQA_Q_WIDTH = GQA_Q_HEADS * HEAD_DIM
GQA_KV_WIDTH = GQA_KV_HEADS * HEAD_DIM
ATTN_SCALE = HEAD_DIM ** -0.5

MOD_ROWS = 16
VMEM_LIMIT = 56 * 1024 * 1024
HALO = 8
NA_QROWS = 4
NA_WROWS = 12
NA_INVALID = 2 * NA_KR - 1
FFN_ROWS = 128
FFN_SPLIT = 1


def _params(*sem, flags=None):
    return pltpu.CompilerParams(dimension_semantics=sem, vmem_limit_bytes=VMEM_LIMIT, flags=flags)


def _rms_modulate(x, g, shift, scale):
    ms = jnp.mean(x * x, axis=-1, keepdims=True)
    return (x * lax.rsqrt(ms + EPS)) * (g * (1.0 + scale)) + shift


def _mod_spec(k, tm, seq, base_row, per_seq):
    def index(m, n):
        return ((base_row + per_seq * ((m * tm) // seq)) * 6 + k, 0, 0)
    return pl.BlockSpec((None, 1, D_MODEL), index)


def _ada_kernel(c_ref, w_ref, b_ref, o_ref):
    c = c_ref[...]
    s = (c * jax.nn.sigmoid(c)).astype(bf16)
    o_ref[...] = jnp.dot(s, w_ref[...].astype(bf16), preferred_element_type=f32) + b_ref[...]


def ada_modulation(cond, ada_w, ada_b):
    tn = 1024
    n_out = 6 * D_MODEL
    return pl.pallas_call(
        _ada_kernel,
        grid=(DEPTH, n_out // tn),
        in_specs=[
            pl.BlockSpec((MOD_ROWS, D_MODEL), lambda i, n: (0, 0)),
            pl.BlockSpec((None, D_MODEL, tn), lambda i, n: (i, 0, n)),
            pl.BlockSpec((None, 1, tn), lambda i, n: (i, 0, n)),
        ],
        out_specs=pl.BlockSpec((None, MOD_ROWS, tn), lambda i, n: (i, 0, n)),
        out_shape=jax.ShapeDtypeStruct((DEPTH, MOD_ROWS, n_out), f32),
        compiler_params=_params("arbitrary", "arbitrary"),
        name="ada_modulation",
    )(cond, ada_w, ada_b.reshape(DEPTH, 1, n_out))


def _rope(a, cos, sin):
    lane = lax.broadcasted_iota(jnp.int32, a.shape, 1)
    quarter = HEAD_DIM // 4
    partner = jnp.where((lane & (2 * quarter - 1)) < quarter,
                        pltpu.roll(a, HEAD_DIM - quarter, 1),
                        pltpu.roll(a, quarter, 1))
    return a * cos + partner * sin


def _in_proj_kernel(tn, q_blocks, k_blocks, has_norm, has_rope, *refs):
    it = iter(refs)
    x_ref, shift_ref, scale_ref, g_ref, w_ref = (next(it) for _ in range(5))
    qg_ref = kg_ref = cos_ref = sin_ref = None
    if has_norm:
        qg_ref, kg_ref, avg_ref = next(it), next(it), next(it)
    if has_rope:
        cos_ref, sin_ref = next(it), next(it)
    o_ref, h_ref = next(it), next(it)
    n = pl.program_id(1)

    @pl.when(n == 0)
    def _():
        h = _rms_modulate(x_ref[...], g_ref[...], shift_ref[...], scale_ref[...])
        h_ref[...] = h.astype(bf16)

    is_q = n < q_blocks
    q_scale = jnp.where(is_q, ATTN_SCALE, 1.0)

    def project():
        return jnp.dot(h_ref[...], w_ref[...], preferred_element_type=f32)

    if not has_norm:
        o_ref[...] = (project() * q_scale).astype(o_ref.dtype)
        return

    is_qk = n < q_blocks + k_blocks

    @pl.when(is_qk)
    def _():
        acc = project()
        gn = jnp.where(is_q, qg_ref[...], kg_ref[...]) * q_scale
        ms = jnp.dot((acc * acc).astype(bf16), avg_ref[...], preferred_element_type=f32)
        inv = lax.rsqrt(ms + EPS)
        for hh in range(tn // HEAD_DIM):
            sl = slice(hh * HEAD_DIM, (hh + 1) * HEAD_DIM)
            a = acc[:, sl] * inv[:, sl] * gn
            if has_rope:
                a = _rope(a, cos_ref[...], sin_ref[...])
            o_ref[:, sl] = a.astype(o_ref.dtype)

    @pl.when(jnp.logical_not(is_qk))
    def _():
        o_ref[...] = project().astype(o_ref.dtype)


def in_proj(x, mod, g, w, out_dtype, q_width, seq, base_row, per_seq, norm=None, rope=None):
    t_tokens = x.shape[0]
    tm, tn = 512, 512
    n_total = w.shape[1]
    in_specs = [
        pl.BlockSpec((tm, D_MODEL), lambda m, n: (m, 0)),
        _mod_spec(0, tm, seq, base_row, per_seq),
        _mod_spec(1, tm, seq, base_row, per_seq),
        pl.BlockSpec((1, D_MODEL), lambda m, n: (0, 0)),
        pl.BlockSpec((D_MODEL, tn), lambda m, n: (0, n)),
    ]
    args = [x, mod, mod, g.reshape(1, D_MODEL), w]
    k_width = 0
    if norm is not None:
        k_width = norm[2]
        in_specs += [pl.BlockSpec((1, HEAD_DIM), lambda m, n: (0, 0))] * 2
        in_specs += [pl.BlockSpec((tn, tn), lambda m, n: (0, 0))]
        head_of = np.arange(tn) // HEAD_DIM
        avg = (head_of[:, None] == head_of[None, :]) / HEAD_DIM
        args += [norm[0].reshape(1, HEAD_DIM), norm[1].reshape(1, HEAD_DIM), jnp.asarray(avg, dtype=bf16)]
    if rope is not None:
        spb = seq // tm
        in_specs += [pl.BlockSpec((tm, HEAD_DIM), lambda m, n: (m % spb, 0))] * 2
        args += [rope[0], rope[1]]
    kern = functools.partial(_in_proj_kernel, tn, q_width // tn, k_width // tn,
                             norm is not None, rope is not None)
    return pl.pallas_call(
        kern,
        grid=(t_tokens // tm, n_total // tn),
        in_specs=in_specs,
        out_specs=pl.BlockSpec((tm, tn), lambda m, n: (m, n)),
        out_shape=jax.ShapeDtypeStruct((t_tokens, n_total), out_dtype),
        scratch_shapes=[pltpu.VMEM((tm, D_MODEL), bf16)],
        compiler_params=_params("arbitrary", "arbitrary"),
        name="in_proj",
    )(*args)


def _ctx_attn_kernel(n_q, n_kv, q_ref, k_ref, v_ref, o_ref):
    group = n_q // n_kv
    for h in range(n_q):
        kv = h // group
        q = q_ref[:, h * HEAD_DIM:(h + 1) * HEAD_DIM].astype(bf16)
        k = k_ref[:, kv * HEAD_DIM:(kv + 1) * HEAD_DIM].astype(bf16)
        v = v_ref[:, kv * HEAD_DIM:(kv + 1) * HEAD_DIM].astype(bf16)
        s = lax.dot_general(q, k, (((1,), (1,)), ((), ())), preferred_element_type=f32)
        m = jnp.max(s, axis=-1, keepdims=True)
        e = jnp.exp(s - m)
        l = jnp.sum(e, axis=-1, keepdims=True)
        o = jnp.dot(e.astype(bf16), v, preferred_element_type=f32) / l
        o_ref[:, h * HEAD_DIM:(h + 1) * HEAD_DIM] = o.astype(o_ref.dtype)


def ctx_attention(proj, seq, n_q, n_kv):
    t_tokens = proj.shape[0]
    qw, kw = n_q * HEAD_DIM, n_kv * HEAD_DIM
    assert qw % kw == 0
    k_blk = qw // kw
    return pl.pallas_call(
        functools.partial(_ctx_attn_kernel, n_q, n_kv),
        grid=(t_tokens // seq,),
        in_specs=[
            pl.BlockSpec((seq, qw), lambda b: (b, 0)),
            pl.BlockSpec((seq, kw), lambda b: (b, k_blk)),
            pl.BlockSpec((seq, kw), lambda b: (b, k_blk + 1)),
        ],
        out_specs=pl.BlockSpec((seq, qw), lambda b: (b, 0)),
        out_shape=jax.ShapeDtypeStruct((t_tokens, qw), bf16),
        compiler_params=_params("arbitrary"),
        name="ctx_attention",
    )(proj, proj, proj)


def _gqa_attn_kernel(group, q_ref, ck_ref, cv_ref, k_ref, v_ref, o_ref):
    ck = ck_ref[...].astype(bf16)
    cv = cv_ref[...].astype(bf16)
    k = k_ref[...]
    v = v_ref[...]
    dn = (((1,), (1,)), ((), ()))
    for gi in range(group):
        q = q_ref[:, gi * HEAD_DIM:(gi + 1) * HEAD_DIM]
        s_c = lax.dot_general(q, ck, dn, preferred_element_type=f32)
        s_l = lax.dot_general(q, k, dn, preferred_element_type=f32)
        m = jnp.maximum(jnp.max(s_c, axis=-1, keepdims=True),
                        jnp.max(s_l, axis=-1, keepdims=True))
        e_c = jnp.exp(s_c - m)
        e_l = jnp.exp(s_l - m)
        l = jnp.sum(e_c, axis=-1, keepdims=True) + jnp.sum(e_l, axis=-1, keepdims=True)
        o = (jnp.dot(e_c.astype(bf16), cv, preferred_element_type=f32)
             + jnp.dot(e_l.astype(bf16), v, preferred_element_type=f32)) / l
        o_ref[:, gi * HEAD_DIM:(gi + 1) * HEAD_DIM] = o.astype(o_ref.dtype)


def gqa_attention(proj, cache_k, cache_v, seq, past):
    t_tokens = proj.shape[0]
    n_b = t_tokens // seq
    group = GQA_Q_HEADS // GQA_KV_HEADS
    tq = 256
    nq = seq // tq
    return pl.pallas_call(
        functools.partial(_gqa_attn_kernel, group),
        grid=(n_b, GQA_KV_HEADS, nq),
        in_specs=[
            pl.BlockSpec((tq, group * HEAD_DIM), lambda b, h, i: (b * nq + i, h)),
            pl.BlockSpec((past, HEAD_DIM), lambda b, h, i: (b, h)),
            pl.BlockSpec((past, HEAD_DIM), lambda b, h, i: (b, h)),
            pl.BlockSpec((seq, HEAD_DIM), lambda b, h, i: (b, GQA_Q_HEADS + h)),
            pl.BlockSpec((seq, HEAD_DIM), lambda b, h, i: (b, GQA_Q_HEADS + GQA_KV_HEADS + h)),
        ],
        out_specs=pl.BlockSpec((tq, group * HEAD_DIM), lambda b, h, i: (b * nq + i, h)),
        out_shape=jax.ShapeDtypeStruct((t_tokens, GQA_Q_WIDTH), bf16),
        compiler_params=_params("arbitrary", "arbitrary", "arbitrary"),
        name="gqa_attention",
    )(proj, cache_k, cache_v, proj, proj)


def _na_bias_tiles(rel_bias):
    qc = jnp.arange(GRID_W)[:, None]
    kc = jnp.arange(GRID_W)[None, :]
    q_start = jnp.clip(qc - NA_KC // 2, 0, GRID_W - NA_KC)
    valid = (kc >= q_start) & (kc < q_start + NA_KC)
    dc = jnp.clip(kc - qc, -(NA_KC - 1), NA_KC - 1) + NA_KC - 1
    tiles = jnp.where(valid[None, None], rel_bias.astype(f32)[:, :, dc], NEG_INF)
    masked = jnp.full((NA_HEADS, 1, GRID_W, GRID_W), NEG_INF, f32)
    tiles = jnp.concatenate([tiles, masked], axis=1)
    zeros = jnp.zeros_like(tiles)
    left = jnp.concatenate([tiles, zeros], axis=-1)
    right = jnp.concatenate([zeros, tiles], axis=-1)
    return left, right


def _na_attn_kernel(n_rows, q_ref, k_ref, v_ref, ck_ref, cv_ref, bl_ref, br_ref, o_ref, s_ref):
    rb = pl.program_id(2)
    start = jnp.clip(rb * NA_QROWS - NA_KR // 2, 0, n_rows - NA_WROWS)
    tok0 = pl.multiple_of(start * GRID_W, GRID_W)
    kw = k_ref[pl.ds(tok0, NA_WROWS * GRID_W), :]
    vw = v_ref[pl.ds(tok0, NA_WROWS * GRID_W), :]
    q = q_ref[...]
    dn = (((1,), (1,)), ((), ()))
    s_ref[...] = lax.dot_general(q, kw, dn, preferred_element_type=f32)
    for i in range(NA_QROWS):
        qr = rb * NA_QROWS + i
        r0 = jnp.clip(qr - NA_KR // 2, 0, n_rows - NA_KR)
        for jp in range(NA_WROWS // 2):
            ka = start + 2 * jp
            kb = ka + 1
            ia = jnp.where((ka >= r0) & (ka < r0 + NA_KR), ka - qr + NA_KR - 1, NA_INVALID)
            ib = jnp.where((kb >= r0) & (kb < r0 + NA_KR), kb - qr + NA_KR - 1, NA_INVALID)
            rows = slice(i * GRID_W, (i + 1) * GRID_W)
            cols = slice(jp * 2 * GRID_W, (jp + 1) * 2 * GRID_W)
            s_ref[rows, cols] = s_ref[rows, cols] + bl_ref[ia] + br_ref[ib]
    s_l = s_ref[...]
    s_c = lax.dot_general(q, ck_ref[...].astype(bf16), dn, preferred_element_type=f32)
    m = jnp.maximum(jnp.max(s_c, axis=-1, keepdims=True), jnp.max(s_l, axis=-1, keepdims=True))
    e_c = jnp.exp(s_c - m)
    e_l = jnp.exp(s_l - m)
    l = jnp.sum(e_c, axis=-1, keepdims=True) + jnp.sum(e_l, axis=-1, keepdims=True)
    o = (jnp.dot(e_c.astype(bf16), cv_ref[...].astype(bf16), preferred_element_type=f32)
         + jnp.dot(e_l.astype(bf16), vw, preferred_element_type=f32)) / l
    o_ref[...] = o.astype(o_ref.dtype)


def na_attention(proj, cache_k, cache_v, rel_bias, seq, past):
    t_tokens = proj.shape[0]
    n_b = t_tokens // seq
    n_rows = seq // GRID_W
    n_rb = n_rows // NA_QROWS
    tq = NA_QROWS * GRID_W
    bias_l, bias_r = _na_bias_tiles(rel_bias)
    bias_spec = pl.BlockSpec((None, 2 * NA_KR, GRID_W, 2 * GRID_W), lambda b, h, r: (h, 0, 0, 0))
    return pl.pallas_call(
        functools.partial(_na_attn_kernel, n_rows),
        grid=(n_b, NA_HEADS, n_rb),
        in_specs=[
            pl.BlockSpec((tq, HEAD_DIM), lambda b, h, r: (b * n_rb + r, h)),
            pl.BlockSpec((seq, HEAD_DIM), lambda b, h, r: (b, NA_HEADS + h)),
            pl.BlockSpec((seq, HEAD_DIM), lambda b, h, r: (b, 2 * NA_HEADS + h)),
            pl.BlockSpec((past, HEAD_DIM), lambda b, h, r: (b, h)),
            pl.BlockSpec((past, HEAD_DIM), lambda b, h, r: (b, h)),
            bias_spec,
            bias_spec,
        ],
        out_specs=pl.BlockSpec((tq, HEAD_DIM), lambda b, h, r: (b * n_rb + r, h)),
        out_shape=jax.ShapeDtypeStruct((t_tokens, NA_WIDTH), bf16),
        scratch_shapes=[pltpu.VMEM((tq, NA_WROWS * GRID_W), f32)],
        compiler_params=_params("arbitrary", "arbitrary", "arbitrary"),
        name="na_attention",
    )(proj, proj, proj, cache_k, cache_v, bias_l, bias_r)


def _dft_cols_kernel(x_ref, cs_ref, y_ref):
    for g in range(FNET_GROUPS):
        sl = slice(g * FNET_CH, (g + 1) * FNET_CH)
        y = jnp.dot(x_ref[:, sl].astype(bf16), cs_ref[...], preferred_element_type=f32)
        y_ref[0, :, sl] = y[:, :FNET_CH].astype(bf16)
        y_ref[1, :, sl] = y[:, FNET_CH:].astype(bf16)


def _dft_rows_kernel(d_ref, y_ref, w_ref, o_ref):
    f = jnp.dot(d_ref[...], y_ref[...], preferred_element_type=f32)
    for g in range(FNET_GROUPS):
        sl = slice(g * FNET_CH, (g + 1) * FNET_CH)
        o = jnp.dot(f[:, sl].astype(bf16), w_ref[g], preferred_element_type=f32)
        o_ref[:, sl] = o.astype(o_ref.dtype)


def _dft_tables(seq):
    c = np.arange(FNET_CH, dtype=np.int64)
    ang_c = (2.0 * np.pi / FNET_CH) * ((c[:, None] * c[None, :]) % FNET_CH)
    cs = np.concatenate([np.cos(ang_c), np.sin(ang_c)], axis=1)
    n = np.arange(seq, dtype=np.int64)
    ang_n = (2.0 * np.pi / seq) * ((n[:, None] * n[None, :]) % seq)
    norm = 1.0 / math.sqrt(seq * FNET_CH)
    d = np.concatenate([np.cos(ang_n) * norm, -np.sin(ang_n) * norm], axis=1)
    return jnp.asarray(cs, dtype=bf16), jnp.asarray(d, dtype=bf16)


def fourier_mix(proj, col_blk, fnet_w, seq):
    t_tokens = proj.shape[0]
    n_b = t_tokens // seq
    cs, d = _dft_tables(seq)
    tn = min(seq, 512)
    ns = seq // tn
    y = pl.pallas_call(
        _dft_cols_kernel,
        grid=(n_b, ns),
        in_specs=[
            pl.BlockSpec((tn, FNET_WIDTH), lambda b, i: (b * ns + i, col_blk)),
            pl.BlockSpec((FNET_CH, 2 * FNET_CH), lambda b, i: (0, 0)),
        ],
        out_specs=pl.BlockSpec((2, tn, FNET_WIDTH), lambda b, i: (0, i, b)),
        out_shape=jax.ShapeDtypeStruct((2, seq, n_b * FNET_WIDTH), bf16),
        compiler_params=_params("arbitrary", "arbitrary"),
        name="dft_cols",
    )(proj, cs)
    y2 = y.reshape(2 * seq, n_b * FNET_WIDTH)
    return pl.pallas_call(
        _dft_rows_kernel,
        grid=(n_b, ns),
        in_specs=[
            pl.BlockSpec((tn, 2 * seq), lambda b, i: (i, 0)),
            pl.BlockSpec((2 * seq, FNET_WIDTH), lambda b, i: (0, b)),
            pl.BlockSpec((FNET_GROUPS, FNET_CH, FNET_CH), lambda b, i: (0, 0, 0)),
        ],
        out_specs=pl.BlockSpec((tn, FNET_WIDTH), lambda b, i: (b * ns + i, 0)),
        out_shape=jax.ShapeDtypeStruct((t_tokens, FNET_WIDTH), bf16),
        compiler_params=_params("arbitrary", "arbitrary"),
        name="dft_rows",
    )(d, y2, fnet_w.astype(bf16))


def _pool_kernel(seq, x_ref, w_ref, sc_ref, o_ref):
    t = lax.broadcasted_iota(jnp.int32, (seq, 1), 0)

    def prev(a, k):
        return jnp.where(t >= k, pltpu.roll(a, k, 0), 0.0)

    def nxt(a, k):
        return jnp.where(t < seq - k, pltpu.roll(a, seq - k, 0), 0.0)

    for g, win in enumerate(POOL_WINDOWS):
        sl = slice(g * FNET_CH, (g + 1) * FNET_CH)
        x = x_ref[:, sl].astype(f32)
        half = win // 2
        back, fwd, k = x, x, 1
        while k < half:
            back = back + prev(back, k)
            fwd = fwd + nxt(fwd, k)
            k *= 2
        total = prev(back, 1) + fwd
        cnt = (jnp.minimum(t + half, seq) - jnp.maximum(t - half, 0)).astype(f32)
        pooled = (total / cnt - x).astype(bf16)
        o = jnp.dot(pooled, w_ref[g], preferred_element_type=f32) * sc_ref[:, sl]
        o_ref[:, sl] = o.astype(o_ref.dtype)


def pool_mix(proj, col_blk, pool_w, pool_scale, seq):
    t_tokens = proj.shape[0]
    return pl.pallas_call(
        functools.partial(_pool_kernel, seq),
        grid=(t_tokens // seq,),
        in_specs=[
            pl.BlockSpec((seq, POOL_WIDTH), lambda b: (b, col_blk)),
            pl.BlockSpec((len(POOL_WINDOWS), FNET_CH, FNET_CH), lambda b: (0, 0, 0)),
            pl.BlockSpec((1, POOL_WIDTH), lambda b: (0, 0)),
        ],
        out_specs=pl.BlockSpec((seq, POOL_WIDTH), lambda b: (b, 0)),
        out_shape=jax.ShapeDtypeStruct((t_tokens, POOL_WIDTH), bf16),
        compiler_params=_params("arbitrary"),
        name="pool_mix",
    )(proj, pool_w.astype(bf16), pool_scale.reshape(1, POOL_WIDTH))


def _out_proj_kernel(x_ref, gate_ref, a_ref, b_ref, wa_ref, wb_ref, o_ref):
    y = (jnp.dot(a_ref[...], wa_ref[...], preferred_element_type=f32)
         + jnp.dot(b_ref[...], wb_ref[...], preferred_element_type=f32))
    o_ref[...] = x_ref[...] + gate_ref[...] * y


def out_proj(x, mod, a, b, wa, wb, seq, base_row, per_seq):
    t_tokens = x.shape[0]
    tm = 512
    return pl.pallas_call(
        _out_proj_kernel,
        grid=(t_tokens // tm, 1),
        in_specs=[
            pl.BlockSpec((tm, D_MODEL), lambda m, n: (m, 0)),
            _mod_spec(2, tm, seq, base_row, per_seq),
            pl.BlockSpec((tm, a.shape[1]), lambda m, n: (m, 0)),
            pl.BlockSpec((tm, b.shape[1]), lambda m, n: (m, 0)),
            pl.BlockSpec(wa.shape, lambda m, n: (0, 0)),
            pl.BlockSpec(wb.shape, lambda m, n: (0, 0)),
        ],
        out_specs=pl.BlockSpec((tm, D_MODEL), lambda m, n: (m, 0)),
        out_shape=jax.ShapeDtypeStruct((t_tokens, D_MODEL), f32),
        compiler_params=_params("arbitrary", "arbitrary"),
        name="out_proj",
    )(x, mod, a, b, wa, wb)


def _ffn_kernel(tm, seq, nj, final, *refs):
    it = iter(refs)
    (x_ref, xp_ref, xn_ref, shift_ref, scale_ref, gate_ref, g_ref,
     wv_ref, wg_ref, cwv_ref, cwg_ref, cbv_ref, cbg_ref, wd_ref) = (next(it) for _ in range(14))
    gf_ref = next(it) if final else None
    o_ref, h_ref, acc_ref = next(it), next(it), next(it)
    u_refs = [(next(it), next(it)), (next(it), next(it))]
    i = pl.program_id(0)
    j = pl.program_id(1)
    rows = tm + 2 * HALO

    def prologue():
        g, sh, sc = g_ref[...], shift_ref[...], scale_ref[...]
        h_ref[0:tm, :] = _rms_modulate(x_ref[...], g, sh, sc).astype(bf16)
        next_ok = jnp.where((((i + 1) * tm) & (seq - 1)) != 0, 1.0, 0.0)
        prev_ok = jnp.where(((i * tm) & (seq - 1)) != 0, 1.0, 0.0)
        h_next = _rms_modulate(xn_ref[...], g, sh, sc) * next_ok
        h_prev = _rms_modulate(xp_ref[...], g, sh, sc) * prev_ok
        h_ref[tm:rows, :] = jnp.concatenate([h_next, h_prev], axis=0).astype(bf16)
        acc_ref[...] = jnp.zeros_like(acc_ref)

    def zero_row(a, r):
        r0 = (r // HALO) * HALO
        row8 = lax.broadcasted_iota(jnp.int32, (HALO, 1), 0)
        fixed = jnp.where(row8 == r - r0, 0.0, a[r0:r0 + HALO])
        return jnp.concatenate([a[:r0], fixed, a[r0 + HALO:]], axis=0)

    def window(u_ref, r0):
        lo, hi = r0 - HALO, r0 + FFN_ROWS + HALO
        if lo < 0:
            return jnp.concatenate([u_ref[rows + lo:rows, :], u_ref[0:hi, :]], axis=0)
        return u_ref[lo:hi, :]

    def conv(u_ref, r0, cw_ref, cb_ref):
        u = window(u_ref, r0)
        n = FFN_ROWS + 2 * HALO
        up = pltpu.roll(u, 1, 0)
        un = pltpu.roll(u, n - 1, 0)
        for edge in range(seq, tm, seq):
            if r0 <= edge < r0 + FFN_ROWS:
                up = zero_row(up, edge - r0 + HALO)
            if r0 <= edge - 1 < r0 + FFN_ROWS:
                un = zero_row(un, edge - 1 - r0 + HALO)
        c = up * cw_ref[0:1, :] + u * cw_ref[1:2, :] + un * cw_ref[2:3, :] + cb_ref[...]
        return c[HALO:HALO + FFN_ROWS, :]

    def gated(slot):
        uv_ref, ug_ref = u_refs[slot]
        chunks = []
        for r0 in range(0, tm, FFN_ROWS):
            val = conv(uv_ref, r0, cwv_ref, cbv_ref)
            gate = conv(ug_ref, r0, cwg_ref, cbg_ref)
            chunks.append((gate * jax.nn.sigmoid(gate) * val).astype(bf16))
        return chunks

    def zero_after(chunks):
        parts = []
        for c in chunks:
            bits = pltpu.bitcast(c, jnp.uint32)
            parts += [bits[k:k + HALO] for k in range(0, bits.shape[0], HALO)]
        while len(parts) > 1:
            parts = [a | b for a, b in zip(parts[0::2], parts[1::2])] + parts[len(parts) & ~1:]
        return (parts[0] >> 16) >> 16

    def after(h, zero):
        tf = zero.shape[1]
        top = pltpu.bitcast(pltpu.bitcast(h[0:2 * HALO, 0:tf], jnp.uint32) | zero, bf16)
        if tf < h.shape[1]:
            top = jnp.concatenate([top, h[0:2 * HALO, tf:]], axis=1)
        return jnp.concatenate([top, h[2 * HALO:]], axis=0)

    def up_project(slot, chunks=None):
        h = h_ref[...]
        uv_ref, ug_ref = u_refs[slot]
        uv_ref[...] = jnp.dot(h, wv_ref[...], preferred_element_type=f32)
        if chunks is not None:
            h = after(h, zero_after(chunks[:FFN_SPLIT]))
        ug_ref[...] = jnp.dot(h, wg_ref[...], preferred_element_type=f32)

    def down_project(chunks, anchored):
        if anchored:
            chunks = [after(chunks[0], zero_after(chunks[FFN_SPLIT:]))] + chunks[1:]
        act = jnp.concatenate(chunks, axis=0)
        acc_ref[...] += jnp.dot(act, wd_ref[...], preferred_element_type=f32)

    @pl.when(j == 0)
    def _():
        prologue()
        up_project(0)

    for p in range(2):
        @pl.when((j >= 1) & (j < nj) & (j % 2 == p))
        def _(p=p):
            chunks = gated(1 - p)
            up_project(p, chunks)
            down_project(chunks, True)

    @pl.when(j == nj)
    def _():
        down_project(gated((nj - 1) % 2), False)
        y = x_ref[...] + gate_ref[...] * acc_ref[...]
        if final:
            ms = jnp.mean(y * y, axis=-1, keepdims=True)
            y = y * lax.rsqrt(ms + EPS) * gf_ref[...]
        o_ref[...] = y


def conv_ffn(x, mod, g, w_up, conv_w, conv_b, w_down, seq, base_row, per_seq, final_g=None):
    t_tokens = x.shape[0]
    tm, tf = 512, 512
    assert seq & (seq - 1) == 0
    nj = D_FF // tf
    hb = tm // HALO
    n_hblk = t_tokens // HALO
    conv_b2 = conv_b.reshape(1, 2 * D_FF)

    def blk(j):
        return jnp.clip(j, 0, nj - 1)

    in_specs = [
        pl.BlockSpec((tm, D_MODEL), lambda i, j: (i, 0)),
        pl.BlockSpec((HALO, D_MODEL), lambda i, j: (jnp.maximum(i * hb - 1, 0), 0)),
        pl.BlockSpec((HALO, D_MODEL), lambda i, j: (jnp.minimum((i + 1) * hb, n_hblk - 1), 0)),
        _mod_spec(3, tm, seq, base_row, per_seq),
        _mod_spec(4, tm, seq, base_row, per_seq),
        _mod_spec(5, tm, seq, base_row, per_seq),
        pl.BlockSpec((1, D_MODEL), lambda i, j: (0, 0)),
        pl.BlockSpec((D_MODEL, tf), lambda i, j: (0, blk(j))),
        pl.BlockSpec((D_MODEL, tf), lambda i, j: (0, nj + blk(j))),
        pl.BlockSpec((3, tf), lambda i, j: (0, blk(j - 1))),
        pl.BlockSpec((3, tf), lambda i, j: (0, nj + blk(j - 1))),
        pl.BlockSpec((1, tf), lambda i, j: (0, blk(j - 1))),
        pl.BlockSpec((1, tf), lambda i, j: (0, nj + blk(j - 1))),
        pl.BlockSpec((tf, D_MODEL), lambda i, j: (blk(j - 1), 0)),
    ]
    args = [x, x, x, mod, mod, mod, g.reshape(1, D_MODEL), w_up, w_up,
            conv_w, conv_w, conv_b2, conv_b2, w_down]
    if final_g is not None:
        in_specs.append(pl.BlockSpec((1, D_MODEL), lambda i, j: (0, 0)))
        args.append(final_g.reshape(1, D_MODEL))
    return pl.pallas_call(
        functools.partial(_ffn_kernel, tm, seq, nj, final_g is not None),
        grid=(t_tokens // tm, nj + 1),
        in_specs=in_specs,
        out_specs=pl.BlockSpec((tm, D_MODEL), lambda i, j: (i, 0)),
        out_shape=jax.ShapeDtypeStruct((t_tokens, D_MODEL), f32),
        scratch_shapes=[pltpu.VMEM((tm + 2 * HALO, D_MODEL), bf16),
                        pltpu.VMEM((tm, D_MODEL), f32)]
                       + [pltpu.VMEM((tm + 2 * HALO, tf), f32)] * 4,
        compiler_params=_params("arbitrary", "arbitrary"),
        name="conv_ffn",
    )(*args)


def _rope_tables(seq):
    t = np.arange(seq)
    half = HEAD_DIM // 2
    inv_freq = np.power(ROPE_BASE, -np.arange(0, half, 2, dtype=np.float64) / half)
    ang_r = (t // GRID_W)[:, None] * inv_freq[None, :]
    ang_c = (t % GRID_W)[:, None] * inv_freq[None, :]
    cos = np.concatenate([np.cos(ang_r)] * 2 + [np.cos(ang_c)] * 2, axis=-1)
    sin = np.concatenate([-np.sin(ang_r), np.sin(ang_r), -np.sin(ang_c), np.sin(ang_c)], axis=-1)
    return jnp.asarray(cos, dtype=f32), jnp.asarray(sin, dtype=f32)


def kernel(x_prompt, x_sample, c, cache_na_k, cache_na_v, cache_gqa_k, cache_gqa_v, c_ctx, norm1_g, norm2_g, ada_w, ada_b, ev_w_in, ev_na_bias, ev_fnet_w, ev_w_out, od_w_in, od_pool_w, od_pool_scale, od_q_norm_g, od_k_norm_g, od_w_out, ffn_w_up, ffn_conv_w, ffn_conv_b, ffn_w_down, final_norm_g):
    n_ctx, seq_p, _ = x_prompt.shape
    n_lat, seq_s, _ = x_sample.shape
    past = cache_na_k.shape[2]
    assert n_lat + 1 <= MOD_ROWS
    xp = x_prompt.reshape(n_ctx * seq_p, D_MODEL)
    xs = x_sample.reshape(n_lat * seq_s, D_MODEL)

    cond = jnp.zeros((MOD_ROWS, D_MODEL), f32).at[0].set(c_ctx).at[1:1 + n_lat].set(c)
    mods = ada_modulation(cond, ada_w, ada_b).reshape(DEPTH, MOD_ROWS * 6, 1, D_MODEL)
    rope = _rope_tables(seq_s)

    ctx = dict(seq=seq_p, base_row=0, per_seq=0)
    lat = dict(seq=seq_s, base_row=1, per_seq=1)
    new_k, new_v = {}, {}
    for i in range(DEPTH):
        j = i // 2
        mod = mods[i]
        w_up = ffn_w_up[i].astype(bf16)
        w_down = ffn_w_down[i].astype(bf16)
        if i % 2 == 0:
            w_in = ev_w_in[j].astype(bf16)
            w_attn = ev_w_out[j, :NA_WIDTH].astype(bf16)
            w_mix = ev_w_out[j, NA_WIDTH:].astype(bf16)
            mix_blk = 3 * NA_WIDTH // FNET_WIDTH
            proj_p = in_proj(xp, mod, norm1_g[i], w_in, f32, NA_WIDTH, **ctx)
            new_k[i] = proj_p[:, NA_WIDTH:2 * NA_WIDTH]
            new_v[i] = proj_p[:, 2 * NA_WIDTH:3 * NA_WIDTH]
            attn_p = ctx_attention(proj_p, seq_p, NA_HEADS, NA_HEADS)
            mix_p = fourier_mix(proj_p, mix_blk, ev_fnet_w[j], seq_p)
            proj_s = in_proj(xs, mod, norm1_g[i], w_in, bf16, NA_WIDTH, **lat)
            attn_s = na_attention(proj_s,
                                  cache_na_k[:, j].reshape(n_lat * past, NA_WIDTH),
                                  cache_na_v[:, j].reshape(n_lat * past, NA_WIDTH),
                                  ev_na_bias[j], seq_s, past)
            mix_s = fourier_mix(proj_s, mix_blk, ev_fnet_w[j], seq_s)
        else:
            w_in = jnp.concatenate([od_w_in[j, :, POOL_WIDTH:], od_w_in[j, :, :POOL_WIDTH]],
                                   axis=1).astype(bf16)
            w_attn = od_w_out[j, POOL_WIDTH:].astype(bf16)
            w_mix = od_w_out[j, :POOL_WIDTH].astype(bf16)
            norm = (od_q_norm_g[j], od_k_norm_g[j], GQA_KV_WIDTH)
            k_lo = GQA_Q_WIDTH
            v_lo = GQA_Q_WIDTH + GQA_KV_WIDTH
            mix_blk = (GQA_Q_WIDTH + 2 * GQA_KV_WIDTH) // POOL_WIDTH
            proj_p = in_proj(xp, mod, norm1_g[i], w_in, f32, GQA_Q_WIDTH, norm=norm, **ctx)
            new_k[i] = proj_p[:, k_lo:k_lo + GQA_KV_WIDTH]
            new_v[i] = proj_p[:, v_lo:v_lo + GQA_KV_WIDTH]
            attn_p = ctx_attention(proj_p, seq_p, GQA_Q_HEADS, GQA_KV_HEADS)
            mix_p = pool_mix(proj_p, mix_blk, od_pool_w[j], od_pool_scale[j], seq_p)
            proj_s = in_proj(xs, mod, norm1_g[i], w_in, bf16, GQA_Q_WIDTH, norm=norm, rope=rope, **lat)
            attn_s = gqa_attention(proj_s,
                                   cache_gqa_k[:, j].reshape(n_lat * past, GQA_KV_WIDTH),
                                   cache_gqa_v[:, j].reshape(n_lat * past, GQA_KV_WIDTH),
                                   seq_s, past)
            mix_s = pool_mix(proj_s, mix_blk, od_pool_w[j], od_pool_scale[j], seq_s)
        xp = out_proj(xp, mod, attn_p, mix_p, w_attn, w_mix, **ctx)
        xs = out_proj(xs, mod, attn_s, mix_s, w_attn, w_mix, **lat)
        final_g = final_norm_g if i == DEPTH - 1 else None
        xp = conv_ffn(xp, mod, norm2_g[i], w_up, ffn_conv_w[i], ffn_conv_b[i], w_down,
                      final_g=final_g, **ctx)
        xs = conv_ffn(xs, mod, norm2_g[i], w_up, ffn_conv_w[i], ffn_conv_b[i], w_down,
                      final_g=final_g, **lat)

    y_prompt = xp.reshape(n_ctx, seq_p, D_MODEL)
    y_sample = xs.reshape(n_lat, seq_s, D_MODEL)
    even = [i for i in range(DEPTH) if i % 2 == 0]
    odd = [i for i in range(DEPTH) if i % 2 == 1]

    def stack(parts, layers, heads):
        return jnp.stack([parts[i].reshape(n_ctx, seq_p, heads, HEAD_DIM) for i in layers], axis=1)

    return (y_prompt, y_sample,
            stack(new_k, even, NA_HEADS), stack(new_v, even, NA_HEADS),
            stack(new_k, odd, GQA_KV_HEADS), stack(new_v, odd, GQA_KV_HEADS))
```

```python
import functools
import math

import jax
import jax.numpy as jnp
import numpy as np
from jax import lax
from jax.experimental import pallas as pl
from jax.experimental.pallas import tpu as pltpu

f32 = jnp.float32
bf16 = jnp.bfloat16

D_MODEL = 2048
DEPTH = 2
GRID_W = 64
HEAD_DIM = 128
NA_HEADS = 12
NA_KR = 8
NA_KC = 16
FNET_GROUPS = 4
FNET_CH = 128
POOL_WINDOWS = (2, 4, 8, 16)
GQA_Q_HEADS = 12
GQA_KV_HEADS = 4
D_FF = 5632
ROPE_BASE = 10000.0
EPS = 1e-6
NEG_INF = -1e30
NA_WIDTH = NA_HEADS * HEAD_DIM
FNET_WIDTH = FNET_GROUPS * FNET_CH
POOL_WIDTH = len(POOL_WINDOWS) * FNET_CH
GQA_Q_WIDTH = GQA_Q_HEADS * HEAD_DIM
GQA_KV_WIDTH = GQA_KV_HEADS * HEAD_DIM
ATTN_SCALE = HEAD_DIM ** -0.5

MOD_ROWS = 16
VMEM_LIMIT = 56 * 1024 * 1024
HALO = 8
NA_QROWS = 4
NA_WROWS = 12
NA_INVALID = 2 * NA_KR - 1
FFN_ROWS = 128
FFN_SPLIT = 1
W_TILE = 512


def _params(*sem, flags=None):
    return pltpu.CompilerParams(dimension_semantics=sem, vmem_limit_bytes=VMEM_LIMIT, flags=flags)


def _col_blocks(w, tn):
    k, n = w.shape
    return w.astype(bf16).reshape(k, n // tn, tn).transpose(1, 0, 2)


def _rms_modulate(x, g, shift, scale):
    ms = jnp.mean(x * x, axis=-1, keepdims=True)
    return (x * lax.rsqrt(ms + EPS)) * (g * (1.0 + scale)) + shift


def _mod_spec(k, tm, seq, base_row, per_seq):
    def index(m, n):
        return ((base_row + per_seq * ((m * tm) // seq)) * 6 + k, 0, 0)
    return pl.BlockSpec((None, 1, D_MODEL), index)


def _ada_kernel(c_ref, w_ref, b_ref, o_ref):
    c = c_ref[...]
    s = (c * jax.nn.sigmoid(c)).astype(bf16)
    o_ref[...] = jnp.dot(s, w_ref[...].astype(bf16), preferred_element_type=f32) + b_ref[...]


def ada_modulation(cond, ada_w, ada_b):
    tn = 1024
    n_out = 6 * D_MODEL
    return pl.pallas_call(
        _ada_kernel,
        grid=(DEPTH, n_out // tn),
        in_specs=[
            pl.BlockSpec((MOD_ROWS, D_MODEL), lambda i, n: (0, 0)),
            pl.BlockSpec((None, D_MODEL, tn), lambda i, n: (i, 0, n)),
            pl.BlockSpec((None, 1, tn), lambda i, n: (i, 0, n)),
        ],
        out_specs=pl.BlockSpec((None, MOD_ROWS, tn), lambda i, n: (i, 0, n)),
        out_shape=jax.ShapeDtypeStruct((DEPTH, MOD_ROWS, n_out), f32),
        compiler_params=_params("arbitrary", "arbitrary"),
        name="ada_modulation",
    )(cond, ada_w, ada_b.reshape(DEPTH, 1, n_out))


def _rope(a, cos, sin):
    lane = lax.broadcasted_iota(jnp.int32, a.shape, 1)
    quarter = HEAD_DIM // 4
    partner = jnp.where((lane & (2 * quarter - 1)) < quarter,
                        pltpu.roll(a, HEAD_DIM - quarter, 1),
                        pltpu.roll(a, quarter, 1))
    return a * cos + partner * sin


def _in_proj_kernel(tn, q_blocks, k_blocks, has_norm, has_rope, *refs):
    it = iter(refs)
    x_ref, shift_ref, scale_ref, g_ref, w_ref = (next(it) for _ in range(5))
    qg_ref = kg_ref = cos_ref = sin_ref = None
    if has_norm:
        qg_ref, kg_ref, avg_ref = next(it), next(it), next(it)
    if has_rope:
        cos_ref, sin_ref = next(it), next(it)
    o_ref, h_ref = next(it), next(it)
    n = pl.program_id(1)

    @pl.when(n == 0)
    def _():
        h = _rms_modulate(x_ref[...], g_ref[...], shift_ref[...], scale_ref[...])
        h_ref[...] = h.astype(bf16)

    is_q = n < q_blocks
    q_scale = jnp.where(is_q, ATTN_SCALE, 1.0)

    def project():
        return jnp.dot(h_ref[...], w_ref[...], preferred_element_type=f32)

    if not has_norm:
        o_ref[...] = (project() * q_scale).astype(o_ref.dtype)
        return

    is_qk = n < q_blocks + k_blocks

    @pl.when(is_qk)
    def _():
        acc = project()
        gn = jnp.where(is_q, qg_ref[...], kg_ref[...]) * q_scale
        ms = jnp.dot((acc * acc).astype(bf16), avg_ref[...], preferred_element_type=f32)
        inv = lax.rsqrt(ms + EPS)
        for hh in range(tn // HEAD_DIM):
            sl = slice(hh * HEAD_DIM, (hh + 1) * HEAD_DIM)
            a = acc[:, sl] * inv[:, sl] * gn
            if has_rope:
                a = _rope(a, cos_ref[...], sin_ref[...])
            o_ref[:, sl] = a.astype(o_ref.dtype)

    @pl.when(jnp.logical_not(is_qk))
    def _():
        o_ref[...] = project().astype(o_ref.dtype)


def in_proj(x, mod, g, w, out_dtype, q_width, seq, base_row, per_seq, norm=None, rope=None):
    t_tokens = x.shape[0]
    tm = 512
    n_blocks, _, tn = w.shape
    n_total = n_blocks * tn
    in_specs = [
        pl.BlockSpec((tm, D_MODEL), lambda m, n: (m, 0)),
        _mod_spec(0, tm, seq, base_row, per_seq),
        _mod_spec(1, tm, seq, base_row, per_seq),
        pl.BlockSpec((1, D_MODEL), lambda m, n: (0, 0)),
        pl.BlockSpec((None, D_MODEL, tn), lambda m, n: (n, 0, 0)),
    ]
    args = [x, mod, mod, g.reshape(1, D_MODEL), w]
    k_width = 0
    if norm is not None:
        k_width = norm[2]
        in_specs += [pl.BlockSpec((1, HEAD_DIM), lambda m, n: (0, 0))] * 2
        in_specs += [pl.BlockSpec((tn, tn), lambda m, n: (0, 0))]
        head_of = np.arange(tn) // HEAD_DIM
        avg = (head_of[:, None] == head_of[None, :]) / HEAD_DIM
        args += [norm[0].reshape(1, HEAD_DIM), norm[1].reshape(1, HEAD_DIM), jnp.asarray(avg, dtype=bf16)]
    if rope is not None:
        spb = seq // tm
        in_specs += [pl.BlockSpec((tm, HEAD_DIM), lambda m, n: (m % spb, 0))] * 2
        args += [rope[0], rope[1]]
    kern = functools.partial(_in_proj_kernel, tn, q_width // tn, k_width // tn,
                             norm is not None, rope is not None)
    return pl.pallas_call(
        kern,
        grid=(t_tokens // tm, n_total // tn),
        in_specs=in_specs,
        out_specs=pl.BlockSpec((tm, tn), lambda m, n: (m, n)),
        out_shape=jax.ShapeDtypeStruct((t_tokens, n_total), out_dtype),
        scratch_shapes=[pltpu.VMEM((tm, D_MODEL), bf16)],
        compiler_params=_params("arbitrary", "arbitrary"),
        name="in_proj",
    )(*args)


def _ctx_attn_kernel(n_q, n_kv, q_ref, k_ref, v_ref, o_ref):
    group = n_q // n_kv
    for h in range(n_q):
        kv = h // group
        q = q_ref[:, h * HEAD_DIM:(h + 1) * HEAD_DIM].astype(bf16)
        k = k_ref[:, kv * HEAD_DIM:(kv + 1) * HEAD_DIM].astype(bf16)
        v = v_ref[:, kv * HEAD_DIM:(kv + 1) * HEAD_DIM].astype(bf16)
        s = lax.dot_general(q, k, (((1,), (1,)), ((), ())), preferred_element_type=f32)
        m = jnp.max(s, axis=-1, keepdims=True)
        e = jnp.exp(s - m)
        l = jnp.sum(e, axis=-1, keepdims=True)
        o = jnp.dot(e.astype(bf16), v, preferred_element_type=f32) / l
        o_ref[:, h * HEAD_DIM:(h + 1) * HEAD_DIM] = o.astype(o_ref.dtype)


def ctx_attention(proj, seq, n_q, n_kv):
    t_tokens = proj.shape[0]
    qw, kw = n_q * HEAD_DIM, n_kv * HEAD_DIM
    assert qw % kw == 0
    k_blk = qw // kw
    return pl.pallas_call(
        functools.partial(_ctx_attn_kernel, n_q, n_kv),
        grid=(t_tokens // seq,),
        in_specs=[
            pl.BlockSpec((seq, qw), lambda b: (b, 0)),
            pl.BlockSpec((seq, kw), lambda b: (b, k_blk)),
            pl.BlockSpec((seq, kw), lambda b: (b, k_blk + 1)),
        ],
        out_specs=pl.BlockSpec((seq, qw), lambda b: (b, 0)),
        out_shape=jax.ShapeDtypeStruct((t_tokens, qw), bf16),
        compiler_params=_params("arbitrary"),
        name="ctx_attention",
    )(proj, proj, proj)


def _gqa_attn_kernel(group, q_ref, ck_ref, cv_ref, k_ref, v_ref, o_ref):
    ck = ck_ref[...].astype(bf16)
    cv = cv_ref[...].astype(bf16)
    k = k_ref[...]
    v = v_ref[...]
    dn = (((1,), (1,)), ((), ()))
    for gi in range(group):
        q = q_ref[:, gi * HEAD_DIM:(gi + 1) * HEAD_DIM]
        s_c = lax.dot_general(q, ck, dn, preferred_element_type=f32)
        s_l = lax.dot_general(q, k, dn, preferred_element_type=f32)
        m = jnp.maximum(jnp.max(s_c, axis=-1, keepdims=True),
                        jnp.max(s_l, axis=-1, keepdims=True))
        e_c = jnp.exp(s_c - m)
        e_l = jnp.exp(s_l - m)
        l = jnp.sum(e_c, axis=-1, keepdims=True) + jnp.sum(e_l, axis=-1, keepdims=True)
        o = (jnp.dot(e_c.astype(bf16), cv, preferred_element_type=f32)
             + jnp.dot(e_l.astype(bf16), v, preferred_element_type=f32)) / l
        o_ref[:, gi * HEAD_DIM:(gi + 1) * HEAD_DIM] = o.astype(o_ref.dtype)


def gqa_attention(proj, cache_k, cache_v, seq, past):
    t_tokens = proj.shape[0]
    n_b = t_tokens // seq
    group = GQA_Q_HEADS // GQA_KV_HEADS
    tq = 256
    nq = seq // tq
    return pl.pallas_call(
        functools.partial(_gqa_attn_kernel, group),
        grid=(n_b, GQA_KV_HEADS, nq),
        in_specs=[
            pl.BlockSpec((tq, group * HEAD_DIM), lambda b, h, i: (b * nq + i, h)),
            pl.BlockSpec((past, HEAD_DIM), lambda b, h, i: (b, h)),
            pl.BlockSpec((past, HEAD_DIM), lambda b, h, i: (b, h)),
            pl.BlockSpec((seq, HEAD_DIM), lambda b, h, i: (b, GQA_Q_HEADS + h)),
            pl.BlockSpec((seq, HEAD_DIM), lambda b, h, i: (b, GQA_Q_HEADS + GQA_KV_HEADS + h)),
        ],
        out_specs=pl.BlockSpec((tq, group * HEAD_DIM), lambda b, h, i: (b * nq + i, h)),
        out_shape=jax.ShapeDtypeStruct((t_tokens, GQA_Q_WIDTH), bf16),
        compiler_params=_params("arbitrary", "arbitrary", "arbitrary"),
        name="gqa_attention",
    )(proj, cache_k, cache_v, proj, proj)


def _na_bias_tiles(rel_bias):
    qc = jnp.arange(GRID_W)[:, None]
    kc = jnp.arange(GRID_W)[None, :]
    q_start = jnp.clip(qc - NA_KC // 2, 0, GRID_W - NA_KC)
    valid = (kc >= q_start) & (kc < q_start + NA_KC)
    dc = jnp.clip(kc - qc, -(NA_KC - 1), NA_KC - 1) + NA_KC - 1
    tiles = jnp.where(valid[None, None], rel_bias.astype(f32)[:, :, dc], NEG_INF)
    masked = jnp.full((NA_HEADS, 1, GRID_W, GRID_W), NEG_INF, f32)
    tiles = jnp.concatenate([tiles, masked], axis=1)
    zeros = jnp.zeros_like(tiles)
    left = jnp.concatenate([tiles, zeros], axis=-1)
    right = jnp.concatenate([zeros, tiles], axis=-1)
    return left, right


def _na_attn_kernel(n_rows, q_ref, k_ref, v_ref, ck_ref, cv_ref, bl_ref, br_ref, o_ref, s_ref):
    rb = pl.program_id(2)
    start = jnp.clip(rb * NA_QROWS - NA_KR // 2, 0, n_rows - NA_WROWS)
    tok0 = pl.multiple_of(start * GRID_W, GRID_W)
    kw = k_ref[pl.ds(tok0, NA_WROWS * GRID_W), :]
    vw = v_ref[pl.ds(tok0, NA_WROWS * GRID_W), :]
    q = q_ref[...]
    dn = (((1,), (1,)), ((), ()))
    s_ref[...] = lax.dot_general(q, kw, dn, preferred_element_type=f32)
    for i in range(NA_QROWS):
        qr = rb * NA_QROWS + i
        r0 = jnp.clip(qr - NA_KR // 2, 0, n_rows - NA_KR)
        for jp in range(NA_WROWS // 2):
            ka = start + 2 * jp
            kb = ka + 1
            ia = jnp.where((ka >= r0) & (ka < r0 + NA_KR), ka - qr + NA_KR - 1, NA_INVALID)
            ib = jnp.where((kb >= r0) & (kb < r0 + NA_KR), kb - qr + NA_KR - 1, NA_INVALID)
            rows = slice(i * GRID_W, (i + 1) * GRID_W)
            cols = slice(jp * 2 * GRID_W, (jp + 1) * 2 * GRID_W)
            s_ref[rows, cols] = s_ref[rows, cols] + bl_ref[ia] + br_ref[ib]
    s_l = s_ref[...]
    s_c = lax.dot_general(q, ck_ref[...].astype(bf16), dn, preferred_element_type=f32)
    m = jnp.maximum(jnp.max(s_c, axis=-1, keepdims=True), jnp.max(s_l, axis=-1, keepdims=True))
    e_c = jnp.exp(s_c - m)
    e_l = jnp.exp(s_l - m)
    l = jnp.sum(e_c, axis=-1, keepdims=True) + jnp.sum(e_l, axis=-1, keepdims=True)
    o = (jnp.dot(e_c.astype(bf16), cv_ref[...].astype(bf16), preferred_element_type=f32)
         + jnp.dot(e_l.astype(bf16), vw, preferred_element_type=f32)) / l
    o_ref[...] = o.astype(o_ref.dtype)


def na_attention(proj, cache_k, cache_v, rel_bias, seq, past):
    t_tokens = proj.shape[0]
    n_b = t_tokens // seq
    n_rows = seq // GRID_W
    n_rb = n_rows // NA_QROWS
    tq = NA_QROWS * GRID_W
    bias_l, bias_r = _na_bias_tiles(rel_bias)
    bias_spec = pl.BlockSpec((None, 2 * NA_KR, GRID_W, 2 * GRID_W), lambda b, h, r: (h, 0, 0, 0))
    return pl.pallas_call(
        functools.partial(_na_attn_kernel, n_rows),
        grid=(n_b, NA_HEADS, n_rb),
        in_specs=[
            pl.BlockSpec((tq, HEAD_DIM), lambda b, h, r: (b * n_rb + r, h)),
            pl.BlockSpec((seq, HEAD_DIM), lambda b, h, r: (b, NA_HEADS + h)),
            pl.BlockSpec((seq, HEAD_DIM), lambda b, h, r: (b, 2 * NA_HEADS + h)),
            pl.BlockSpec((past, HEAD_DIM), lambda b, h, r: (b, h)),
            pl.BlockSpec((past, HEAD_DIM), lambda b, h, r: (b, h)),
            bias_spec,
            bias_spec,
        ],
        out_specs=pl.BlockSpec((tq, HEAD_DIM), lambda b, h, r: (b * n_rb + r, h)),
        out_shape=jax.ShapeDtypeStruct((t_tokens, NA_WIDTH), bf16),
        scratch_shapes=[pltpu.VMEM((tq, NA_WROWS * GRID_W), f32)],
        compiler_params=_params("arbitrary", "arbitrary", "arbitrary"),
        name="na_attention",
    )(proj, proj, proj, cache_k, cache_v, bias_l, bias_r)


def _dft_cols_kernel(x_ref, cs_ref, y_ref):
    for g in range(FNET_GROUPS):
        sl = slice(g * FNET_CH, (g + 1) * FNET_CH)
        y = jnp.dot(x_ref[:, sl].astype(bf16), cs_ref[...], preferred_element_type=f32)
        y_ref[0, :, sl] = y[:, :FNET_CH].astype(bf16)
        y_ref[1, :, sl] = y[:, FNET_CH:].astype(bf16)


def _dft_rows_kernel(d_ref, y_ref, w_ref, o_ref):
    f = jnp.dot(d_ref[...], y_ref[...], preferred_element_type=f32)
    for g in range(FNET_GROUPS):
        sl = slice(g * FNET_CH, (g + 1) * FNET_CH)
        o = jnp.dot(f[:, sl].astype(bf16), w_ref[g], preferred_element_type=f32)
        o_ref[:, sl] = o.astype(o_ref.dtype)


def _dft_tables(seq):
    c = np.arange(FNET_CH, dtype=np.int64)
    ang_c = (2.0 * np.pi / FNET_CH) * ((c[:, None] * c[None, :]) % FNET_CH)
    cs = np.concatenate([np.cos(ang_c), np.sin(ang_c)], axis=1)
    n = np.arange(seq, dtype=np.int64)
    ang_n = (2.0 * np.pi / seq) * ((n[:, None] * n[None, :]) % seq)
    norm = 1.0 / math.sqrt(seq * FNET_CH)
    d = np.concatenate([np.cos(ang_n) * norm, -np.sin(ang_n) * norm], axis=1)
    return jnp.asarray(cs, dtype=bf16), jnp.asarray(d, dtype=bf16)


def fourier_mix(proj, col_blk, fnet_w, seq):
    t_tokens = proj.shape[0]
    n_b = t_tokens // seq
    cs, d = _dft_tables(seq)
    tn = min(seq, 512)
    ns = seq // tn
    y = pl.pallas_call(
        _dft_cols_kernel,
        grid=(n_b, ns),
        in_specs=[
            pl.BlockSpec((tn, FNET_WIDTH), lambda b, i: (b * ns + i, col_blk)),
            pl.BlockSpec((FNET_CH, 2 * FNET_CH), lambda b, i: (0, 0)),
        ],
        out_specs=pl.BlockSpec((2, tn, FNET_WIDTH), lambda b, i: (0, i, b)),
        out_shape=jax.ShapeDtypeStruct((2, seq, n_b * FNET_WIDTH), bf16),
        compiler_params=_params("arbitrary", "arbitrary"),
        name="dft_cols",
    )(proj, cs)
    y2 = y.reshape(2 * seq, n_b * FNET_WIDTH)
    return pl.pallas_call(
        _dft_rows_kernel,
        grid=(n_b, ns),
        in_specs=[
            pl.BlockSpec((tn, 2 * seq), lambda b, i: (i, 0)),
            pl.BlockSpec((2 * seq, FNET_WIDTH), lambda b, i: (0, b)),
            pl.BlockSpec((FNET_GROUPS, FNET_CH, FNET_CH), lambda b, i: (0, 0, 0)),
        ],
        out_specs=pl.BlockSpec((tn, FNET_WIDTH), lambda b, i: (b * ns + i, 0)),
        out_shape=jax.ShapeDtypeStruct((t_tokens, FNET_WIDTH), bf16),
        compiler_params=_params("arbitrary", "arbitrary"),
        name="dft_rows",
    )(d, y2, fnet_w.astype(bf16))


def _pool_kernel(seq, x_ref, w_ref, sc_ref, o_ref):
    t = lax.broadcasted_iota(jnp.int32, (seq, 1), 0)

    def prev(a, k):
        return jnp.where(t >= k, pltpu.roll(a, k, 0), 0.0)

    def nxt(a, k):
        return jnp.where(t < seq - k, pltpu.roll(a, seq - k, 0), 0.0)

    for g, win in enumerate(POOL_WINDOWS):
        sl = slice(g * FNET_CH, (g + 1) * FNET_CH)
        x = x_ref[:, sl].astype(f32)
        half = win // 2
        back, fwd, k = x, x, 1
        while k < half:
            back = back + prev(back, k)
            fwd = fwd + nxt(fwd, k)
            k *= 2
        total = prev(back, 1) + fwd
        cnt = (jnp.minimum(t + half, seq) - jnp.maximum(t - half, 0)).astype(f32)
        pooled = (total / cnt - x).astype(bf16)
        o = jnp.dot(pooled, w_ref[g], preferred_element_type=f32) * sc_ref[:, sl]
        o_ref[:, sl] = o.astype(o_ref.dtype)


def pool_mix(proj, col_blk, pool_w, pool_scale, seq):
    t_tokens = proj.shape[0]
    return pl.pallas_call(
        functools.partial(_pool_kernel, seq),
        grid=(t_tokens // seq,),
        in_specs=[
            pl.BlockSpec((seq, POOL_WIDTH), lambda b: (b, col_blk)),
            pl.BlockSpec((len(POOL_WINDOWS), FNET_CH, FNET_CH), lambda b: (0, 0, 0)),
            pl.BlockSpec((1, POOL_WIDTH), lambda b: (0, 0)),
        ],
        out_specs=pl.BlockSpec((seq, POOL_WIDTH), lambda b: (b, 0)),
        out_shape=jax.ShapeDtypeStruct((t_tokens, POOL_WIDTH), bf16),
        compiler_params=_params("arbitrary"),
        name="pool_mix",
    )(proj, pool_w.astype(bf16), pool_scale.reshape(1, POOL_WIDTH))


def _out_proj_kernel(x_ref, gate_ref, a_ref, b_ref, wa_ref, wb_ref, o_ref):
    y = (jnp.dot(a_ref[...], wa_ref[...], preferred_element_type=f32)
         + jnp.dot(b_ref[...], wb_ref[...], preferred_element_type=f32))
    o_ref[...] = x_ref[...] + gate_ref[...] * y


def out_proj(x, mod, a, b, wa, wb, seq, base_row, per_seq):
    t_tokens = x.shape[0]
    tm = 512
    return pl.pallas_call(
        _out_proj_kernel,
        grid=(t_tokens // tm, 1),
        in_specs=[
            pl.BlockSpec((tm, D_MODEL), lambda m, n: (m, 0)),
            _mod_spec(2, tm, seq, base_row, per_seq),
            pl.BlockSpec((tm, a.shape[1]), lambda m, n: (m, 0)),
            pl.BlockSpec((tm, b.shape[1]), lambda m, n: (m, 0)),
            pl.BlockSpec(wa.shape, lambda m, n: (0, 0)),
            pl.BlockSpec(wb.shape, lambda m, n: (0, 0)),
        ],
        out_specs=pl.BlockSpec((tm, D_MODEL), lambda m, n: (m, 0)),
        out_shape=jax.ShapeDtypeStruct((t_tokens, D_MODEL), f32),
        compiler_params=_params("arbitrary", "arbitrary"),
        name="out_proj",
    )(x, mod, a, b, wa, wb)


def _ffn_kernel(tm, seq, nj, final, *refs):
    it = iter(refs)
    (x_ref, xp_ref, xn_ref, shift_ref, scale_ref, gate_ref, g_ref,
     wv_ref, wg_ref, cwv_ref, cwg_ref, cbv_ref, cbg_ref, wd_ref) = (next(it) for _ in range(14))
    gf_ref = next(it) if final else None
    o_ref, h_ref, acc_ref = next(it), next(it), next(it)
    u_refs = [(next(it), next(it)), (next(it), next(it))]
    i = pl.program_id(0)
    j = pl.program_id(1)
    rows = tm + 2 * HALO

    def prologue():
        g, sh, sc = g_ref[...], shift_ref[...], scale_ref[...]
        h_ref[0:tm, :] = _rms_modulate(x_ref[...], g, sh, sc).astype(bf16)
        next_ok = jnp.where((((i + 1) * tm) & (seq - 1)) != 0, 1.0, 0.0)
        prev_ok = jnp.where(((i * tm) & (seq - 1)) != 0, 1.0, 0.0)
        h_next = _rms_modulate(xn_ref[...], g, sh, sc) * next_ok
        h_prev = _rms_modulate(xp_ref[...], g, sh, sc) * prev_ok
        h_ref[tm:rows, :] = jnp.concatenate([h_next, h_prev], axis=0).astype(bf16)
        acc_ref[...] = jnp.zeros_like(acc_ref)

    def zero_row(a, r):
        r0 = (r // HALO) * HALO
        row8 = lax.broadcasted_iota(jnp.int32, (HALO, 1), 0)
        fixed = jnp.where(row8 == r - r0, 0.0, a[r0:r0 + HALO])
        return jnp.concatenate([a[:r0], fixed, a[r0 + HALO:]], axis=0)

    def window(u_ref, r0):
        lo, hi = r0 - HALO, r0 + FFN_ROWS + HALO
        if lo < 0:
            return jnp.concatenate([u_ref[rows + lo:rows, :], u_ref[0:hi, :]], axis=0)
        return u_ref[lo:hi, :]

    def conv(u_ref, r0, cw_ref, cb_ref):
        u = window(u_ref, r0)
        n = FFN_ROWS + 2 * HALO
        up = pltpu.roll(u, 1, 0)
        un = pltpu.roll(u, n - 1, 0)
        for edge in range(seq, tm, seq):
            if r0 <= edge < r0 + FFN_ROWS:
                up = zero_row(up, edge - r0 + HALO)
            if r0 <= edge - 1 < r0 + FFN_ROWS:
                un = zero_row(un, edge - 1 - r0 + HALO)
        c = up * cw_ref[0:1, :] + u * cw_ref[1:2, :] + un * cw_ref[2:3, :] + cb_ref[...]
        return c[HALO:HALO + FFN_ROWS, :]

    def gated(slot):
        uv_ref, ug_ref = u_refs[slot]
        chunks = []
        for r0 in range(0, tm, FFN_ROWS):
            val = conv(uv_ref, r0, cwv_ref, cbv_ref)
            gate = conv(ug_ref, r0, cwg_ref, cbg_ref)
            chunks.append((gate * jax.nn.sigmoid(gate) * val).astype(bf16))
        return chunks

    def zero_after(chunks):
        parts = []
        for c in chunks:
            bits = pltpu.bitcast(c, jnp.uint32)
            parts += [bits[k:k + HALO] for k in range(0, bits.shape[0], HALO)]
        while len(parts) > 1:
            parts = [a | b for a, b in zip(parts[0::2], parts[1::2])] + parts[len(parts) & ~1:]
        return (parts[0] >> 16) >> 16

    def after(h, zero):
        tf = zero.shape[1]
        top = pltpu.bitcast(pltpu.bitcast(h[0:2 * HALO, 0:tf], jnp.uint32) | zero, bf16)
        if tf < h.shape[1]:
            top = jnp.concatenate([top, h[0:2 * HALO, tf:]], axis=1)
        return jnp.concatenate([top, h[2 * HALO:]], axis=0)

    def up_project(slot, chunks=None):
        h = h_ref[...]
        uv_ref, ug_ref = u_refs[slot]
        uv_ref[...] = jnp.dot(h, wv_ref[...], preferred_element_type=f32)
        if chunks is not None:
            h = after(h, zero_after(chunks[:FFN_SPLIT]))
        ug_ref[...] = jnp.dot(h, wg_ref[...], preferred_element_type=f32)

    def down_project(chunks, anchored):
        if anchored:
            chunks = [after(chunks[0], zero_after(chunks[FFN_SPLIT:]))] + chunks[1:]
        act = jnp.concatenate(chunks, axis=0)
        acc_ref[...] += jnp.dot(act, wd_ref[...], preferred_element_type=f32)

    @pl.when(j == 0)
    def _():
        prologue()
        up_project(0)

    for p in range(2):
        @pl.when((j >= 1) & (j < nj) & (j % 2 == p))
        def _(p=p):
            chunks = gated(1 - p)
            up_project(p, chunks)
            down_project(chunks, True)

    @pl.when(j == nj)
    def _():
        down_project(gated((nj - 1) % 2), False)
        y = x_ref[...] + gate_ref[...] * acc_ref[...]
        if final:
            ms = jnp.mean(y * y, axis=-1, keepdims=True)
            y = y * lax.rsqrt(ms + EPS) * gf_ref[...]
        o_ref[...] = y


def conv_ffn(x, mod, g, w_up, conv_w, conv_b, w_down, seq, base_row, per_seq, final_g=None):
    t_tokens = x.shape[0]
    tm = 512
    tf = w_up.shape[2]
    assert seq & (seq - 1) == 0
    nj = D_FF // tf
    hb = tm // HALO
    n_hblk = t_tokens // HALO
    conv_b2 = conv_b.reshape(1, 2 * D_FF)

    def blk(j):
        return jnp.clip(j, 0, nj - 1)

    in_specs = [
        pl.BlockSpec((tm, D_MODEL), lambda i, j: (i, 0)),
        pl.BlockSpec((HALO, D_MODEL), lambda i, j: (jnp.maximum(i * hb - 1, 0), 0)),
        pl.BlockSpec((HALO, D_MODEL), lambda i, j: (jnp.minimum((i + 1) * hb, n_hblk - 1), 0)),
        _mod_spec(3, tm, seq, base_row, per_seq),
        _mod_spec(4, tm, seq, base_row, per_seq),
        _mod_spec(5, tm, seq, base_row, per_seq),
        pl.BlockSpec((1, D_MODEL), lambda i, j: (0, 0)),
        pl.BlockSpec((None, D_MODEL, tf), lambda i, j: (blk(j), 0, 0)),
        pl.BlockSpec((None, D_MODEL, tf), lambda i, j: (nj + blk(j), 0, 0)),
        pl.BlockSpec((3, tf), lambda i, j: (0, blk(j - 1))),
        pl.BlockSpec((3, tf), lambda i, j: (0, nj + blk(j - 1))),
        pl.BlockSpec((1, tf), lambda i, j: (0, blk(j - 1))),
        pl.BlockSpec((1, tf), lambda i, j: (0, nj + blk(j - 1))),
        pl.BlockSpec((tf, D_MODEL), lambda i, j: (blk(j - 1), 0)),
    ]
    args = [x, x, x, mod, mod, mod, g.reshape(1, D_MODEL), w_up, w_up,
            conv_w, conv_w, conv_b2, conv_b2, w_down]
    if final_g is not None:
        in_specs.append(pl.BlockSpec((1, D_MODEL), lambda i, j: (0, 0)))
        args.append(final_g.reshape(1, D_MODEL))
    return pl.pallas_call(
        functools.partial(_ffn_kernel, tm, seq, nj, final_g is not None),
        grid=(t_tokens // tm, nj + 1),
        in_specs=in_specs,
        out_specs=pl.BlockSpec((tm, D_MODEL), lambda i, j: (i, 0)),
        out_shape=jax.ShapeDtypeStruct((t_tokens, D_MODEL), f32),
        scratch_shapes=[pltpu.VMEM((tm + 2 * HALO, D_MODEL), bf16),
                        pltpu.VMEM((tm, D_MODEL), f32)]
                       + [pltpu.VMEM((tm + 2 * HALO, tf), f32)] * 4,
        compiler_params=_params("arbitrary", "arbitrary"),
        name="conv_ffn",
    )(*args)


def _rope_tables(seq):
    t = np.arange(seq)
    half = HEAD_DIM // 2
    inv_freq = np.power(ROPE_BASE, -np.arange(0, half, 2, dtype=np.float64) / half)
    ang_r = (t // GRID_W)[:, None] * inv_freq[None, :]
    ang_c = (t % GRID_W)[:, None] * inv_freq[None, :]
    cos = np.concatenate([np.cos(ang_r)] * 2 + [np.cos(ang_c)] * 2, axis=-1)
    sin = np.concatenate([-np.sin(ang_r), np.sin(ang_r), -np.sin(ang_c), np.sin(ang_c)], axis=-1)
    return jnp.asarray(cos, dtype=f32), jnp.asarray(sin, dtype=f32)


def kernel(x_prompt, x_sample, c, cache_na_k, cache_na_v, cache_gqa_k, cache_gqa_v, c_ctx, norm1_g, norm2_g, ada_w, ada_b, ev_w_in, ev_na_bias, ev_fnet_w, ev_w_out, od_w_in, od_pool_w, od_pool_scale, od_q_norm_g, od_k_norm_g, od_w_out, ffn_w_up, ffn_conv_w, ffn_conv_b, ffn_w_down, final_norm_g):
    n_ctx, seq_p, _ = x_prompt.shape
    n_lat, seq_s, _ = x_sample.shape
    past = cache_na_k.shape[2]
    assert n_lat + 1 <= MOD_ROWS
    xp = x_prompt.reshape(n_ctx * seq_p, D_MODEL)
    xs = x_sample.reshape(n_lat * seq_s, D_MODEL)

    cond = jnp.zeros((MOD_ROWS, D_MODEL), f32).at[0].set(c_ctx).at[1:1 + n_lat].set(c)
    mods = ada_modulation(cond, ada_w, ada_b).reshape(DEPTH, MOD_ROWS * 6, 1, D_MODEL)
    rope = _rope_tables(seq_s)

    ctx = dict(seq=seq_p, base_row=0, per_seq=0)
    lat = dict(seq=seq_s, base_row=1, per_seq=1)
    new_k, new_v = {}, {}
    for i in range(DEPTH):
        j = i // 2
        mod = mods[i]
        w_up = _col_blocks(ffn_w_up[i], W_TILE)
        w_down = ffn_w_down[i].astype(bf16)
        if i % 2 == 0:
            w_in = _col_blocks(ev_w_in[j], W_TILE)
            w_attn = ev_w_out[j, :NA_WIDTH].astype(bf16)
            w_mix = ev_w_out[j, NA_WIDTH:].astype(bf16)
            mix_blk = 3 * NA_WIDTH // FNET_WIDTH
            proj_p = in_proj(xp, mod, norm1_g[i], w_in, f32, NA_WIDTH, **ctx)
            new_k[i] = proj_p[:, NA_WIDTH:2 * NA_WIDTH]
            new_v[i] = proj_p[:, 2 * NA_WIDTH:3 * NA_WIDTH]
            attn_p = ctx_attention(proj_p, seq_p, NA_HEADS, NA_HEADS)
            mix_p = fourier_mix(proj_p, mix_blk, ev_fnet_w[j], seq_p)
            proj_s = in_proj(xs, mod, norm1_g[i], w_in, bf16, NA_WIDTH, **lat)
            attn_s = na_attention(proj_s,
                                  cache_na_k[:, j].reshape(n_lat * past, NA_WIDTH),
                                  cache_na_v[:, j].reshape(n_lat * past, NA_WIDTH),
                                  ev_na_bias[j], seq_s, past)
            mix_s = fourier_mix(proj_s, mix_blk, ev_fnet_w[j], seq_s)
        else:
            w_in = _col_blocks(jnp.concatenate([od_w_in[j, :, POOL_WIDTH:], od_w_in[j, :, :POOL_WIDTH]],
                                               axis=1), W_TILE)
            w_attn = od_w_out[j, POOL_WIDTH:].astype(bf16)
            w_mix = od_w_out[j, :POOL_WIDTH].astype(bf16)
            norm = (od_q_norm_g[j], od_k_norm_g[j], GQA_KV_WIDTH)
            k_lo = GQA_Q_WIDTH
            v_lo = GQA_Q_WIDTH + GQA_KV_WIDTH
            mix_blk = (GQA_Q_WIDTH + 2 * GQA_KV_WIDTH) // POOL_WIDTH
            proj_p = in_proj(xp, mod, norm1_g[i], w_in, f32, GQA_Q_WIDTH, norm=norm, **ctx)
            new_k[i] = proj_p[:, k_lo:k_lo + GQA_KV_WIDTH]
            new_v[i] = proj_p[:, v_lo:v_lo + GQA_KV_WIDTH]
            attn_p = ctx_attention(proj_p, seq_p, GQA_Q_HEADS, GQA_KV_HEADS)
            mix_p = pool_mix(proj_p, mix_blk, od_pool_w[j], od_pool_scale[j], seq_p)
            proj_s = in_proj(xs, mod, norm1_g[i], w_in, bf16, GQA_Q_WIDTH, norm=norm, rope=rope, **lat)
            attn_s = gqa_attention(proj_s,
                                   cache_gqa_k[:, j].reshape(n_lat * past, GQA_KV_WIDTH),
                                   cache_gqa_v[:, j].reshape(n_lat * past, GQA_KV_WIDTH),
                                   seq_s, past)
            mix_s = pool_mix(proj_s, mix_blk, od_pool_w[j], od_pool_scale[j], seq_s)
        xp = out_proj(xp, mod, attn_p, mix_p, w_attn, w_mix, **ctx)
        xs = out_proj(xs, mod, attn_s, mix_s, w_attn, w_mix, **lat)
        final_g = final_norm_g if i == DEPTH - 1 else None
        xp = conv_ffn(xp, mod, norm2_g[i], w_up, ffn_conv_w[i], ffn_conv_b[i], w_down,
                      final_g=final_g, **ctx)
        xs = conv_ffn(xs, mod, norm2_g[i], w_up, ffn_conv_w[i], ffn_conv_b[i], w_down,
                      final_g=final_g, **lat)

    y_prompt = xp.reshape(n_ctx, seq_p, D_MODEL)
    y_sample = xs.reshape(n_lat, seq_s, D_MODEL)
    even = [i for i in range(DEPTH) if i % 2 == 0]
    odd = [i for i in range(DEPTH) if i % 2 == 1]

    def stack(parts, layers, heads):
        return jnp.stack([parts[i].reshape(n_ctx, seq_p, heads, HEAD_DIM) for i in layers], axis=1)

    return (y_prompt, y_sample,
            stack(new_k, even, NA_HEADS), stack(new_v, even, NA_HEADS),
            stack(new_k, odd, GQA_KV_HEADS), stack(new_v, odd, GQA_KV_HEADS))
```

```python
import functools
import math

import jax
import jax.numpy as jnp
import numpy as np
from jax import lax
from jax.experimental import pallas as pl
from jax.experimental.pallas import tpu as pltpu

f32 = jnp.float32
bf16 = jnp.bfloat16

D_MODEL = 2048
DEPTH = 2
GRID_W = 64
HEAD_DIM = 128
NA_HEADS = 12
NA_KR = 8
NA_KC = 16
FNET_GROUPS = 4
FNET_CH = 128
POOL_WINDOWS = (2, 4, 8, 16)
GQA_Q_HEADS = 12
GQA_KV_HEADS = 4
D_FF = 5632
ROPE_BASE = 10000.0
EPS = 1e-6
NEG_INF = -1e30
NA_WIDTH = NA_HEADS * HEAD_DIM
FNET_WIDTH = FNET_GROUPS * FNET_CH
POOL_WIDTH = len(POOL_WINDOWS) * FNET_CH
GQA_Q_WIDTH = GQA_Q_HEADS * HEAD_DIM
GQA_KV_WIDTH = GQA_KV_HEADS * HEAD_DIM
ATTN_SCALE = HEAD_DIM ** -0.5

MOD_ROWS = 16
VMEM_LIMIT = 56 * 1024 * 1024
HALO = 8
NA_QROWS = 4
NA_WROWS = 12
NA_INVALID = 2 * NA_KR - 1
NA_HEADS_PER_STEP = 6
FFN_ROWS = 128
FFN_SPLIT = 1
W_TILE = 512


def _params(*sem, flags=None):
    return pltpu.CompilerParams(dimension_semantics=sem, vmem_limit_bytes=VMEM_LIMIT, flags=flags)


def _rms_modulate(x, g, shift, scale):
    ms = jnp.mean(x * x, axis=-1, keepdims=True)
    return (x * lax.rsqrt(ms + EPS)) * (g * (1.0 + scale)) + shift


def _mod_spec(k, tm, seq, base_row, per_seq):
    def index(m, n):
        return ((base_row + per_seq * ((m * tm) // seq)) * 6 + k, 0, 0)
    return pl.BlockSpec((None, 1, D_MODEL), index)


def _ada_kernel(c_ref, w_ref, b_ref, o_ref):
    c = c_ref[...]
    s = (c * jax.nn.sigmoid(c)).astype(bf16)
    o_ref[...] = jnp.dot(s, w_ref[...].astype(bf16), preferred_element_type=f32) + b_ref[...]


def ada_modulation(cond, ada_w, ada_b):
    tn = 1024
    n_out = 6 * D_MODEL
    return pl.pallas_call(
        _ada_kernel,
        grid=(DEPTH, n_out // tn),
        in_specs=[
            pl.BlockSpec((MOD_ROWS, D_MODEL), lambda i, n: (0, 0)),
            pl.BlockSpec((None, D_MODEL, tn), lambda i, n: (i, 0, n)),
            pl.BlockSpec((None, 1, tn), lambda i, n: (i, 0, n)),
        ],
        out_specs=pl.BlockSpec((None, MOD_ROWS, tn), lambda i, n: (i, 0, n)),
        out_shape=jax.ShapeDtypeStruct((DEPTH, MOD_ROWS, n_out), f32),
        compiler_params=_params("arbitrary", "arbitrary"),
        name="ada_modulation",
    )(cond, ada_w, ada_b.reshape(DEPTH, 1, n_out))


def _rope(a, cos, sin):
    lane = lax.broadcasted_iota(jnp.int32, a.shape, 1)
    quarter = HEAD_DIM // 4
    partner = jnp.where((lane & (2 * quarter - 1)) < quarter,
                        pltpu.roll(a, HEAD_DIM - quarter, 1),
                        pltpu.roll(a, quarter, 1))
    return a * cos + partner * sin


def _in_proj_kernel(tn, q_blocks, k_blocks, has_norm, has_rope, *refs):
    it = iter(refs)
    x_ref, shift_ref, scale_ref, g_ref, w_ref = (next(it) for _ in range(5))
    qg_ref = kg_ref = cos_ref = sin_ref = None
    if has_norm:
        qg_ref, kg_ref, avg_ref = next(it), next(it), next(it)
    if has_rope:
        cos_ref, sin_ref = next(it), next(it)
    o_ref, h_ref = next(it), next(it)
    n = pl.program_id(1)

    @pl.when(n == 0)
    def _():
        h = _rms_modulate(x_ref[...], g_ref[...], shift_ref[...], scale_ref[...])
        h_ref[...] = h.astype(bf16)

    is_q = n < q_blocks
    q_scale = jnp.where(is_q, ATTN_SCALE, 1.0)

    def project():
        return jnp.dot(h_ref[...], w_ref[...], preferred_element_type=f32)

    if not has_norm:
        o_ref[...] = (project() * q_scale).astype(o_ref.dtype)
        return

    is_qk = n < q_blocks + k_blocks

    @pl.when(is_qk)
    def _():
        acc = project()
        gn = jnp.where(is_q, qg_ref[...], kg_ref[...]) * q_scale
        ms = jnp.dot((acc * acc).astype(bf16), avg_ref[...], preferred_element_type=f32)
        inv = lax.rsqrt(ms + EPS)
        for hh in range(tn // HEAD_DIM):
            sl = slice(hh * HEAD_DIM, (hh + 1) * HEAD_DIM)
            a = acc[:, sl] * inv[:, sl] * gn
            if has_rope:
                a = _rope(a, cos_ref[...], sin_ref[...])
            o_ref[:, sl] = a.astype(o_ref.dtype)

    @pl.when(jnp.logical_not(is_qk))
    def _():
        o_ref[...] = project().astype(o_ref.dtype)


def in_proj(x, mod, g, w, out_dtype, q_width, seq, base_row, per_seq, norm=None, rope=None):
    t_tokens = x.shape[0]
    tm, tn = 512, W_TILE
    n_total = w.shape[1]
    in_specs = [
        pl.BlockSpec((tm, D_MODEL), lambda m, n: (m, 0)),
        _mod_spec(0, tm, seq, base_row, per_seq),
        _mod_spec(1, tm, seq, base_row, per_seq),
        pl.BlockSpec((1, D_MODEL), lambda m, n: (0, 0)),
        pl.BlockSpec((D_MODEL, tn), lambda m, n: (0, n)),
    ]
    args = [x, mod, mod, g.reshape(1, D_MODEL), w]
    k_width = 0
    if norm is not None:
        k_width = norm[2]
        in_specs += [pl.BlockSpec((1, HEAD_DIM), lambda m, n: (0, 0))] * 2
        in_specs += [pl.BlockSpec((tn, tn), lambda m, n: (0, 0))]
        head_of = np.arange(tn) // HEAD_DIM
        avg = (head_of[:, None] == head_of[None, :]) / HEAD_DIM
        args += [norm[0].reshape(1, HEAD_DIM), norm[1].reshape(1, HEAD_DIM), jnp.asarray(avg, dtype=bf16)]
    if rope is not None:
        spb = seq // tm
        in_specs += [pl.BlockSpec((tm, HEAD_DIM), lambda m, n: (m % spb, 0))] * 2
        args += [rope[0], rope[1]]
    kern = functools.partial(_in_proj_kernel, tn, q_width // tn, k_width // tn,
                             norm is not None, rope is not None)
    return pl.pallas_call(
        kern,
        grid=(t_tokens // tm, n_total // tn),
        in_specs=in_specs,
        out_specs=pl.BlockSpec((tm, tn), lambda m, n: (m, n)),
        out_shape=jax.ShapeDtypeStruct((t_tokens, n_total), out_dtype),
        scratch_shapes=[pltpu.VMEM((tm, D_MODEL), bf16)],
        compiler_params=_params("arbitrary", "arbitrary"),
        name="in_proj",
    )(*args)


def _ctx_attn_kernel(n_q, n_kv, q_ref, k_ref, v_ref, o_ref):
    group = n_q // n_kv
    for h in range(n_q):
        kv = h // group
        q = q_ref[:, h * HEAD_DIM:(h + 1) * HEAD_DIM].astype(bf16)
        k = k_ref[:, kv * HEAD_DIM:(kv + 1) * HEAD_DIM].astype(bf16)
        v = v_ref[:, kv * HEAD_DIM:(kv + 1) * HEAD_DIM].astype(bf16)
        s = lax.dot_general(q, k, (((1,), (1,)), ((), ())), preferred_element_type=f32)
        m = jnp.max(s, axis=-1, keepdims=True)
        e = jnp.exp(s - m)
        l = jnp.sum(e, axis=-1, keepdims=True)
        o = jnp.dot(e.astype(bf16), v, preferred_element_type=f32) / l
        o_ref[:, h * HEAD_DIM:(h + 1) * HEAD_DIM] = o.astype(o_ref.dtype)


def ctx_attention(proj, seq, n_q, n_kv):
    t_tokens = proj.shape[0]
    qw, kw = n_q * HEAD_DIM, n_kv * HEAD_DIM
    assert qw % kw == 0
    k_blk = qw // kw
    return pl.pallas_call(
        functools.partial(_ctx_attn_kernel, n_q, n_kv),
        grid=(t_tokens // seq,),
        in_specs=[
            pl.BlockSpec((seq, qw), lambda b: (b, 0)),
            pl.BlockSpec((seq, kw), lambda b: (b, k_blk)),
            pl.BlockSpec((seq, kw), lambda b: (b, k_blk + 1)),
        ],
        out_specs=pl.BlockSpec((seq, qw), lambda b: (b, 0)),
        out_shape=jax.ShapeDtypeStruct((t_tokens, qw), bf16),
        compiler_params=_params("arbitrary"),
        name="ctx_attention",
    )(proj, proj, proj)


def _gqa_attn_kernel(group, q_ref, ck_ref, cv_ref, k_ref, v_ref, o_ref):
    ck = ck_ref[...].astype(bf16)
    cv = cv_ref[...].astype(bf16)
    k = k_ref[...]
    v = v_ref[...]
    dn = (((1,), (1,)), ((), ()))
    for gi in range(group):
        q = q_ref[:, gi * HEAD_DIM:(gi + 1) * HEAD_DIM]
        s_c = lax.dot_general(q, ck, dn, preferred_element_type=f32)
        s_l = lax.dot_general(q, k, dn, preferred_element_type=f32)
        m = jnp.maximum(jnp.max(s_c, axis=-1, keepdims=True),
                        jnp.max(s_l, axis=-1, keepdims=True))
        e_c = jnp.exp(s_c - m)
        e_l = jnp.exp(s_l - m)
        l = jnp.sum(e_c, axis=-1, keepdims=True) + jnp.sum(e_l, axis=-1, keepdims=True)
        o = (jnp.dot(e_c.astype(bf16), cv, preferred_element_type=f32)
             + jnp.dot(e_l.astype(bf16), v, preferred_element_type=f32)) / l
        o_ref[:, gi * HEAD_DIM:(gi + 1) * HEAD_DIM] = o.astype(o_ref.dtype)


def gqa_attention(proj, cache_k, cache_v, seq, past):
    t_tokens = proj.shape[0]
    n_b = t_tokens // seq
    group = GQA_Q_HEADS // GQA_KV_HEADS
    tq = 256
    nq = seq // tq
    return pl.pallas_call(
        functools.partial(_gqa_attn_kernel, group),
        grid=(n_b, GQA_KV_HEADS, nq),
        in_specs=[
            pl.BlockSpec((tq, group * HEAD_DIM), lambda b, h, i: (b * nq + i, h)),
            pl.BlockSpec((past, HEAD_DIM), lambda b, h, i: (b, h)),
            pl.BlockSpec((past, HEAD_DIM), lambda b, h, i: (b, h)),
            pl.BlockSpec((seq, HEAD_DIM), lambda b, h, i: (b, GQA_Q_HEADS + h)),
            pl.BlockSpec((seq, HEAD_DIM), lambda b, h, i: (b, GQA_Q_HEADS + GQA_KV_HEADS + h)),
        ],
        out_specs=pl.BlockSpec((tq, group * HEAD_DIM), lambda b, h, i: (b * nq + i, h)),
        out_shape=jax.ShapeDtypeStruct((t_tokens, GQA_Q_WIDTH), bf16),
        compiler_params=_params("arbitrary", "arbitrary", "arbitrary"),
        name="gqa_attention",
    )(proj, cache_k, cache_v, proj, proj)


def _na_bias_tiles(rel_bias):
    qc = jnp.arange(GRID_W)[:, None]
    kc = jnp.arange(GRID_W)[None, :]
    q_start = jnp.clip(qc - NA_KC // 2, 0, GRID_W - NA_KC)
    valid = (kc >= q_start) & (kc < q_start + NA_KC)
    dc = jnp.clip(kc - qc, -(NA_KC - 1), NA_KC - 1) + NA_KC - 1
    tiles = jnp.where(valid[None, None], rel_bias.astype(f32)[:, :, dc], NEG_INF)
    masked = jnp.full((NA_HEADS, 1, GRID_W, GRID_W), NEG_INF, f32)
    tiles = jnp.concatenate([tiles, masked], axis=1)
    zeros = jnp.zeros_like(tiles)
    left = jnp.concatenate([tiles, zeros], axis=-1)
    right = jnp.concatenate([zeros, tiles], axis=-1)
    return left, right


def _na_attn_kernel(n_rows, q_ref, k_ref, v_ref, ck_ref, cv_ref, bl_ref, br_ref, o_ref, *s_refs):
    rb = pl.program_id(2)
    start = jnp.clip(rb * NA_QROWS - NA_KR // 2, 0, n_rows - NA_WROWS)
    tok0 = pl.multiple_of(start * GRID_W, GRID_W)
    dn = (((1,), (1,)), ((), ()))
    for hh, s_ref in enumerate(s_refs):
        hs = slice(hh * HEAD_DIM, (hh + 1) * HEAD_DIM)
        kw = k_ref[pl.ds(tok0, NA_WROWS * GRID_W), hs]
        vw = v_ref[pl.ds(tok0, NA_WROWS * GRID_W), hs]
        q = q_ref[:, hs]
        s_ref[...] = lax.dot_general(q, kw, dn, preferred_element_type=f32)
        for i in range(NA_QROWS):
            qr = rb * NA_QROWS + i
            r0 = jnp.clip(qr - NA_KR // 2, 0, n_rows - NA_KR)
            for jp in range(NA_WROWS // 2):
                ka = start + 2 * jp
                kb = ka + 1
                ia = jnp.where((ka >= r0) & (ka < r0 + NA_KR), ka - qr + NA_KR - 1, NA_INVALID)
                ib = jnp.where((kb >= r0) & (kb < r0 + NA_KR), kb - qr + NA_KR - 1, NA_INVALID)
                rows = slice(i * GRID_W, (i + 1) * GRID_W)
                cols = slice(jp * 2 * GRID_W, (jp + 1) * 2 * GRID_W)
                s_ref[rows, cols] = s_ref[rows, cols] + bl_ref[hh, ia] + br_ref[hh, ib]
        s_l = s_ref[...]
        s_c = lax.dot_general(q, ck_ref[:, hs].astype(bf16), dn, preferred_element_type=f32)
        m = jnp.maximum(jnp.max(s_c, axis=-1, keepdims=True), jnp.max(s_l, axis=-1, keepdims=True))
        e_c = jnp.exp(s_c - m)
        e_l = jnp.exp(s_l - m)
        l = jnp.sum(e_c, axis=-1, keepdims=True) + jnp.sum(e_l, axis=-1, keepdims=True)
        o = (jnp.dot(e_c.astype(bf16), cv_ref[:, hs].astype(bf16), preferred_element_type=f32)
             + jnp.dot(e_l.astype(bf16), vw, preferred_element_type=f32)) / l
        o_ref[:, hs] = o.astype(o_ref.dtype)


def na_attention(proj, cache_k, cache_v, rel_bias, seq, past):
    t_tokens = proj.shape[0]
    n_b = t_tokens // seq
    n_rows = seq // GRID_W
    n_rb = n_rows // NA_QROWS
    tq = NA_QROWS * GRID_W
    hb = NA_HEADS_PER_STEP
    n_hg = NA_HEADS // hb
    wide = hb * HEAD_DIM
    bias_l, bias_r = _na_bias_tiles(rel_bias)
    bias_spec = pl.BlockSpec((hb, 2 * NA_KR, GRID_W, 2 * GRID_W), lambda b, h, r: (h, 0, 0, 0))
    return pl.pallas_call(
        functools.partial(_na_attn_kernel, n_rows),
        grid=(n_b, n_hg, n_rb),
        in_specs=[
            pl.BlockSpec((tq, wide), lambda b, h, r: (b * n_rb + r, h)),
            pl.BlockSpec((seq, wide), lambda b, h, r: (b, n_hg + h)),
            pl.BlockSpec((seq, wide), lambda b, h, r: (b, 2 * n_hg + h)),
            pl.BlockSpec((past, wide), lambda b, h, r: (b, h)),
            pl.BlockSpec((past, wide), lambda b, h, r: (b, h)),
            bias_spec,
            bias_spec,
        ],
        out_specs=pl.BlockSpec((tq, wide), lambda b, h, r: (b * n_rb + r, h)),
        out_shape=jax.ShapeDtypeStruct((t_tokens, NA_WIDTH), bf16),
        scratch_shapes=[pltpu.VMEM((tq, NA_WROWS * GRID_W), f32)] * hb,
        compiler_params=_params("arbitrary", "arbitrary", "arbitrary"),
        name="na_attention",
    )(proj, proj, proj, cache_k, cache_v, bias_l, bias_r)


def _dft_cols_kernel(x_ref, cs_ref, y_ref):
    for g in range(FNET_GROUPS):
        sl = slice(g * FNET_CH, (g + 1) * FNET_CH)
        y = jnp.dot(x_ref[:, sl].astype(bf16), cs_ref[...], preferred_element_type=f32)
        y_ref[0, :, sl] = y[:, :FNET_CH].astype(bf16)
        y_ref[1, :, sl] = y[:, FNET_CH:].astype(bf16)


def _dft_rows_kernel(d_ref, y_ref, w_ref, o_ref):
    f = jnp.dot(d_ref[...], y_ref[...], preferred_element_type=f32)
    for g in range(FNET_GROUPS):
        sl = slice(g * FNET_CH, (g + 1) * FNET_CH)
        o = jnp.dot(f[:, sl].astype(bf16), w_ref[g], preferred_element_type=f32)
        o_ref[:, sl] = o.astype(o_ref.dtype)


def _dft_tables(seq):
    c = np.arange(FNET_CH, dtype=np.int64)
    ang_c = (2.0 * np.pi / FNET_CH) * ((c[:, None] * c[None, :]) % FNET_CH)
    cs = np.concatenate([np.cos(ang_c), np.sin(ang_c)], axis=1)
    n = np.arange(seq, dtype=np.int64)
    ang_n = (2.0 * np.pi / seq) * ((n[:, None] * n[None, :]) % seq)
    norm = 1.0 / math.sqrt(seq * FNET_CH)
    d = np.concatenate([np.cos(ang_n) * norm, -np.sin(ang_n) * norm], axis=1)
    return jnp.asarray(cs, dtype=bf16), jnp.asarray(d, dtype=bf16)


def fourier_mix(proj, col_blk, fnet_w, seq):
    t_tokens = proj.shape[0]
    n_b = t_tokens // seq
    cs, d = _dft_tables(seq)
    tn = min(seq, 512)
    ns = seq // tn
    y = pl.pallas_call(
        _dft_cols_kernel,
        grid=(n_b, ns),
        in_specs=[
            pl.BlockSpec((tn, FNET_WIDTH), lambda b, i: (b * ns + i, col_blk)),
            pl.BlockSpec((FNET_CH, 2 * FNET_CH), lambda b, i: (0, 0)),
        ],
        out_specs=pl.BlockSpec((2, tn, FNET_WIDTH), lambda b, i: (0, i, b)),
        out_shape=jax.ShapeDtypeStruct((2, seq, n_b * FNET_WIDTH), bf16),
        compiler_params=_params("arbitrary", "arbitrary"),
        name="dft_cols",
    )(proj, cs)
    y2 = y.reshape(2 * seq, n_b * FNET_WIDTH)
    return pl.pallas_call(
        _dft_rows_kernel,
        grid=(n_b, ns),
        in_specs=[
            pl.BlockSpec((tn, 2 * seq), lambda b, i: (i, 0)),
            pl.BlockSpec((2 * seq, FNET_WIDTH), lambda b, i: (0, b)),
            pl.BlockSpec((FNET_GROUPS, FNET_CH, FNET_CH), lambda b, i: (0, 0, 0)),
        ],
        out_specs=pl.BlockSpec((tn, FNET_WIDTH), lambda b, i: (b * ns + i, 0)),
        out_shape=jax.ShapeDtypeStruct((t_tokens, FNET_WIDTH), bf16),
        compiler_params=_params("arbitrary", "arbitrary"),
        name="dft_rows",
    )(d, y2, fnet_w.astype(bf16))


def _pool_kernel(seq, x_ref, w_ref, sc_ref, o_ref):
    t = lax.broadcasted_iota(jnp.int32, (seq, 1), 0)

    def prev(a, k):
        return jnp.where(t >= k, pltpu.roll(a, k, 0), 0.0)

    def nxt(a, k):
        return jnp.where(t < seq - k, pltpu.roll(a, seq - k, 0), 0.0)

    for g, win in enumerate(POOL_WINDOWS):
        sl = slice(g * FNET_CH, (g + 1) * FNET_CH)
        x = x_ref[:, sl].astype(f32)
        half = win // 2
        back, fwd, k = x, x, 1
        while k < half:
            back = back + prev(back, k)
            fwd = fwd + nxt(fwd, k)
            k *= 2
        total = prev(back, 1) + fwd
        cnt = (jnp.minimum(t + half, seq) - jnp.maximum(t - half, 0)).astype(f32)
        pooled = (total / cnt - x).astype(bf16)
        o = jnp.dot(pooled, w_ref[g], preferred_element_type=f32) * sc_ref[:, sl]
        o_ref[:, sl] = o.astype(o_ref.dtype)


def pool_mix(proj, col_blk, pool_w, pool_scale, seq):
    t_tokens = proj.shape[0]
    return pl.pallas_call(
        functools.partial(_pool_kernel, seq),
        grid=(t_tokens // seq,),
        in_specs=[
            pl.BlockSpec((seq, POOL_WIDTH), lambda b: (b, col_blk)),
            pl.BlockSpec((len(POOL_WINDOWS), FNET_CH, FNET_CH), lambda b: (0, 0, 0)),
            pl.BlockSpec((1, POOL_WIDTH), lambda b: (0, 0)),
        ],
        out_specs=pl.BlockSpec((seq, POOL_WIDTH), lambda b: (b, 0)),
        out_shape=jax.ShapeDtypeStruct((t_tokens, POOL_WIDTH), bf16),
        compiler_params=_params("arbitrary"),
        name="pool_mix",
    )(proj, pool_w.astype(bf16), pool_scale.reshape(1, POOL_WIDTH))


def _out_proj_kernel(x_ref, gate_ref, a_ref, b_ref, wa_ref, wb_ref, o_ref):
    y = (jnp.dot(a_ref[...], wa_ref[...], preferred_element_type=f32)
         + jnp.dot(b_ref[...], wb_ref[...], preferred_element_type=f32))
    o_ref[...] = x_ref[...] + gate_ref[...] * y


def out_proj(x, mod, a, b, wa, wb, seq, base_row, per_seq):
    t_tokens = x.shape[0]
    tm = 512
    return pl.pallas_call(
        _out_proj_kernel,
        grid=(t_tokens // tm, 1),
        in_specs=[
            pl.BlockSpec((tm, D_MODEL), lambda m, n: (m, 0)),
            _mod_spec(2, tm, seq, base_row, per_seq),
            pl.BlockSpec((tm, a.shape[1]), lambda m, n: (m, 0)),
            pl.BlockSpec((tm, b.shape[1]), lambda m, n: (m, 0)),
            pl.BlockSpec(wa.shape, lambda m, n: (0, 0)),
            pl.BlockSpec(wb.shape, lambda m, n: (0, 0)),
        ],
        out_specs=pl.BlockSpec((tm, D_MODEL), lambda m, n: (m, 0)),
        out_shape=jax.ShapeDtypeStruct((t_tokens, D_MODEL), f32),
        compiler_params=_params("arbitrary", "arbitrary"),
        name="out_proj",
    )(x, mod, a, b, wa, wb)


def _ffn_kernel(tm, seq, nj, final, *refs):
    it = iter(refs)
    (x_ref, xp_ref, xn_ref, shift_ref, scale_ref, gate_ref, g_ref,
     wv_ref, wg_ref, cwv_ref, cwg_ref, cbv_ref, cbg_ref, wd_ref) = (next(it) for _ in range(14))
    gf_ref = next(it) if final else None
    o_ref, h_ref, acc_ref = next(it), next(it), next(it)
    u_refs = [(next(it), next(it)), (next(it), next(it))]
    i = pl.program_id(0)
    j = pl.program_id(1)
    rows = tm + 2 * HALO

    def prologue():
        g, sh, sc = g_ref[...], shift_ref[...], scale_ref[...]
        h_ref[0:tm, :] = _rms_modulate(x_ref[...], g, sh, sc).astype(bf16)
        next_ok = jnp.where((((i + 1) * tm) & (seq - 1)) != 0, 1.0, 0.0)
        prev_ok = jnp.where(((i * tm) & (seq - 1)) != 0, 1.0, 0.0)
        h_next = _rms_modulate(xn_ref[...], g, sh, sc) * next_ok
        h_prev = _rms_modulate(xp_ref[...], g, sh, sc) * prev_ok
        h_ref[tm:rows, :] = jnp.concatenate([h_next, h_prev], axis=0).astype(bf16)
        acc_ref[...] = jnp.zeros_like(acc_ref)

    def zero_row(a, r):
        r0 = (r // HALO) * HALO
        row8 = lax.broadcasted_iota(jnp.int32, (HALO, 1), 0)
        fixed = jnp.where(row8 == r - r0, 0.0, a[r0:r0 + HALO])
        return jnp.concatenate([a[:r0], fixed, a[r0 + HALO:]], axis=0)

    def window(u_ref, r0):
        lo, hi = r0 - HALO, r0 + FFN_ROWS + HALO
        if lo < 0:
            return jnp.concatenate([u_ref[rows + lo:rows, :], u_ref[0:hi, :]], axis=0)
        return u_ref[lo:hi, :]

    def conv(u_ref, r0, cw_ref, cb_ref):
        u = window(u_ref, r0)
        n = FFN_ROWS + 2 * HALO
        up = pltpu.roll(u, 1, 0)
        un = pltpu.roll(u, n - 1, 0)
        for edge in range(seq, tm, seq):
            if r0 <= edge < r0 + FFN_ROWS:
                up = zero_row(up, edge - r0 + HALO)
            if r0 <= edge - 1 < r0 + FFN_ROWS:
                un = zero_row(un, edge - 1 - r0 + HALO)
        c = up * cw_ref[0:1, :] + u * cw_ref[1:2, :] + un * cw_ref[2:3, :] + cb_ref[...]
        return c[HALO:HALO + FFN_ROWS, :]

    def gated(slot):
        uv_ref, ug_ref = u_refs[slot]
        chunks = []
        for r0 in range(0, tm, FFN_ROWS):
            val = conv(uv_ref, r0, cwv_ref, cbv_ref)
            gate = conv(ug_ref, r0, cwg_ref, cbg_ref)
            chunks.append((gate * jax.nn.sigmoid(gate) * val).astype(bf16))
        return chunks

    def zero_after(chunks):
        parts = []
        for c in chunks:
            bits = pltpu.bitcast(c, jnp.uint32)
            parts += [bits[k:k + HALO] for k in range(0, bits.shape[0], HALO)]
        while len(parts) > 1:
            parts = [a | b for a, b in zip(parts[0::2], parts[1::2])] + parts[len(parts) & ~1:]
        return (parts[0] >> 16) >> 16

    def after(h, zero):
        tf = zero.shape[1]
        top = pltpu.bitcast(pltpu.bitcast(h[0:2 * HALO, 0:tf], jnp.uint32) | zero, bf16)
        if tf < h.shape[1]:
            top = jnp.concatenate([top, h[0:2 * HALO, tf:]], axis=1)
        return jnp.concatenate([top, h[2 * HALO:]], axis=0)

    def up_project(slot, chunks=None):
        h = h_ref[...]
        uv_ref, ug_ref = u_refs[slot]
        uv_ref[...] = jnp.dot(h, wv_ref[...], preferred_element_type=f32)
        if chunks is not None:
            h = after(h, zero_after(chunks[:FFN_SPLIT]))
        ug_ref[...] = jnp.dot(h, wg_ref[...], preferred_element_type=f32)

    def down_project(chunks, anchored):
        if anchored:
            chunks = [after(chunks[0], zero_after(chunks[FFN_SPLIT:]))] + chunks[1:]
        act = jnp.concatenate(chunks, axis=0)
        acc_ref[...] += jnp.dot(act, wd_ref[...], preferred_element_type=f32)

    @pl.when(j == 0)
    def _():
        prologue()
        up_project(0)

    for p in range(2):
        @pl.when((j >= 1) & (j < nj) & (j % 2 == p))
        def _(p=p):
            chunks = gated(1 - p)
            up_project(p, chunks)
            down_project(chunks, True)

    @pl.when(j == nj)
    def _():
        down_project(gated((nj - 1) % 2), False)
        y = x_ref[...] + gate_ref[...] * acc_ref[...]
        if final:
            ms = jnp.mean(y * y, axis=-1, keepdims=True)
            y = y * lax.rsqrt(ms + EPS) * gf_ref[...]
        o_ref[...] = y


def conv_ffn(x, mod, g, w_up, conv_w, conv_b, w_down, seq, base_row, per_seq, final_g=None):
    t_tokens = x.shape[0]
    tm, tf = 512, W_TILE
    assert seq & (seq - 1) == 0
    nj = D_FF // tf
    hb = tm // HALO
    n_hblk = t_tokens // HALO
    conv_b2 = conv_b.reshape(1, 2 * D_FF)

    def blk(j):
        return jnp.clip(j, 0, nj - 1)

    in_specs = [
        pl.BlockSpec((tm, D_MODEL), lambda i, j: (i, 0)),
        pl.BlockSpec((HALO, D_MODEL), lambda i, j: (jnp.maximum(i * hb - 1, 0), 0)),
        pl.BlockSpec((HALO, D_MODEL), lambda i, j: (jnp.minimum((i + 1) * hb, n_hblk - 1), 0)),
        _mod_spec(3, tm, seq, base_row, per_seq),
        _mod_spec(4, tm, seq, base_row, per_seq),
        _mod_spec(5, tm, seq, base_row, per_seq),
        pl.BlockSpec((1, D_MODEL), lambda i, j: (0, 0)),
        pl.BlockSpec((D_MODEL, tf), lambda i, j: (0, blk(j))),
        pl.BlockSpec((D_MODEL, tf), lambda i, j: (0, nj + blk(j))),
        pl.BlockSpec((3, tf), lambda i, j: (0, blk(j - 1))),
        pl.BlockSpec((3, tf), lambda i, j: (0, nj + blk(j - 1))),
        pl.BlockSpec((1, tf), lambda i, j: (0, blk(j - 1))),
        pl.BlockSpec((1, tf), lambda i, j: (0, nj + blk(j - 1))),
        pl.BlockSpec((tf, D_MODEL), lambda i, j: (blk(j - 1), 0)),
    ]
    args = [x, x, x, mod, mod, mod, g.reshape(1, D_MODEL), w_up, w_up,
            conv_w, conv_w, conv_b2, conv_b2, w_down]
    if final_g is not None:
        in_specs.append(pl.BlockSpec((1, D_MODEL), lambda i, j: (0, 0)))
        args.append(final_g.reshape(1, D_MODEL))
    return pl.pallas_call(
        functools.partial(_ffn_kernel, tm, seq, nj, final_g is not None),
        grid=(t_tokens // tm, nj + 1),
        in_specs=in_specs,
        out_specs=pl.BlockSpec((tm, D_MODEL), lambda i, j: (i, 0)),
        out_shape=jax.ShapeDtypeStruct((t_tokens, D_MODEL), f32),
        scratch_shapes=[pltpu.VMEM((tm + 2 * HALO, D_MODEL), bf16),
                        pltpu.VMEM((tm, D_MODEL), f32)]
                       + [pltpu.VMEM((tm + 2 * HALO, tf), f32)] * 4,
        compiler_params=_params("arbitrary", "arbitrary"),
        name="conv_ffn",
    )(*args)


def _rope_tables(seq):
    t = np.arange(seq)
    half = HEAD_DIM // 2
    inv_freq = np.power(ROPE_BASE, -np.arange(0, half, 2, dtype=np.float64) / half)
    ang_r = (t // GRID_W)[:, None] * inv_freq[None, :]
    ang_c = (t % GRID_W)[:, None] * inv_freq[None, :]
    cos = np.concatenate([np.cos(ang_r)] * 2 + [np.cos(ang_c)] * 2, axis=-1)
    sin = np.concatenate([-np.sin(ang_r), np.sin(ang_r), -np.sin(ang_c), np.sin(ang_c)], axis=-1)
    return jnp.asarray(cos, dtype=f32), jnp.asarray(sin, dtype=f32)


def kernel(x_prompt, x_sample, c, cache_na_k, cache_na_v, cache_gqa_k, cache_gqa_v, c_ctx, norm1_g, norm2_g, ada_w, ada_b, ev_w_in, ev_na_bias, ev_fnet_w, ev_w_out, od_w_in, od_pool_w, od_pool_scale, od_q_norm_g, od_k_norm_g, od_w_out, ffn_w_up, ffn_conv_w, ffn_conv_b, ffn_w_down, final_norm_g):
    n_ctx, seq_p, _ = x_prompt.shape
    n_lat, seq_s, _ = x_sample.shape
    past = cache_na_k.shape[2]
    assert n_lat + 1 <= MOD_ROWS
    xp = x_prompt.reshape(n_ctx * seq_p, D_MODEL)
    xs = x_sample.reshape(n_lat * seq_s, D_MODEL)

    cond = jnp.zeros((MOD_ROWS, D_MODEL), f32).at[0].set(c_ctx).at[1:1 + n_lat].set(c)
    mods = ada_modulation(cond, ada_w, ada_b).reshape(DEPTH, MOD_ROWS * 6, 1, D_MODEL)
    rope = _rope_tables(seq_s)

    ctx = dict(seq=seq_p, base_row=0, per_seq=0)
    lat = dict(seq=seq_s, base_row=1, per_seq=1)
    new_k, new_v = {}, {}
    for i in range(DEPTH):
        j = i // 2
        mod = mods[i]
        w_up = ffn_w_up[i].astype(bf16)
        w_down = ffn_w_down[i].astype(bf16)
        if i % 2 == 0:
            w_in = ev_w_in[j].astype(bf16)
            w_attn = ev_w_out[j, :NA_WIDTH].astype(bf16)
            w_mix = ev_w_out[j, NA_WIDTH:].astype(bf16)
            mix_blk = 3 * NA_WIDTH // FNET_WIDTH
            proj_p = in_proj(xp, mod, norm1_g[i], w_in, f32, NA_WIDTH, **ctx)
            new_k[i] = proj_p[:, NA_WIDTH:2 * NA_WIDTH]
            new_v[i] = proj_p[:, 2 * NA_WIDTH:3 * NA_WIDTH]
            attn_p = ctx_attention(proj_p, seq_p, NA_HEADS, NA_HEADS)
            mix_p = fourier_mix(proj_p, mix_blk, ev_fnet_w[j], seq_p)
            proj_s = in_proj(xs, mod, norm1_g[i], w_in, bf16, NA_WIDTH, **lat)
            attn_s = na_attention(proj_s,
                                  cache_na_k[:, j].reshape(n_lat * past, NA_WIDTH),
                                  cache_na_v[:, j].reshape(n_lat * past, NA_WIDTH),
                                  ev_na_bias[j], seq_s, past)
            mix_s = fourier_mix(proj_s, mix_blk, ev_fnet_w[j], seq_s)
        else:
            w_in = jnp.concatenate([od_w_in[j, :, POOL_WIDTH:], od_w_in[j, :, :POOL_WIDTH]],
                                   axis=1).astype(bf16)
            w_attn = od_w_out[j, POOL_WIDTH:].astype(bf16)
            w_mix = od_w_out[j, :POOL_WIDTH].astype(bf16)
            norm = (od_q_norm_g[j], od_k_norm_g[j], GQA_KV_WIDTH)
            k_lo = GQA_Q_WIDTH
            v_lo = GQA_Q_WIDTH + GQA_KV_WIDTH
            mix_blk = (GQA_Q_WIDTH + 2 * GQA_KV_WIDTH) // POOL_WIDTH
            proj_p = in_proj(xp, mod, norm1_g[i], w_in, f32, GQA_Q_WIDTH, norm=norm, **ctx)
            new_k[i] = proj_p[:, k_lo:k_lo + GQA_KV_WIDTH]
            new_v[i] = proj_p[:, v_lo:v_lo + GQA_KV_WIDTH]
            attn_p = ctx_attention(proj_p, seq_p, GQA_Q_HEADS, GQA_KV_HEADS)
            mix_p = pool_mix(proj_p, mix_blk, od_pool_w[j], od_pool_scale[j], seq_p)
            proj_s = in_proj(xs, mod, norm1_g[i], w_in, bf16, GQA_Q_WIDTH, norm=norm, rope=rope, **lat)
            attn_s = gqa_attention(proj_s,
                                   cache_gqa_k[:, j].reshape(n_lat * past, GQA_KV_WIDTH),
                                   cache_gqa_v[:, j].reshape(n_lat * past, GQA_KV_WIDTH),
                                   seq_s, past)
            mix_s = pool_mix(proj_s, mix_blk, od_pool_w[j], od_pool_scale[j], seq_s)
        xp = out_proj(xp, mod, attn_p, mix_p, w_attn, w_mix, **ctx)
        xs = out_proj(xs, mod, attn_s, mix_s, w_attn, w_mix, **lat)
        final_g = final_norm_g if i == DEPTH - 1 else None
        xp = conv_ffn(xp, mod, norm2_g[i], w_up, ffn_conv_w[i], ffn_conv_b[i], w_down,
                      final_g=final_g, **ctx)
        xs = conv_ffn(xs, mod, norm2_g[i], w_up, ffn_conv_w[i], ffn_conv_b[i], w_down,
                      final_g=final_g, **lat)

    y_prompt = xp.reshape(n_ctx, seq_p, D_MODEL)
    y_sample = xs.reshape(n_lat, seq_s, D_MODEL)
    even = [i for i in range(DEPTH) if i % 2 == 0]
    odd = [i for i in range(DEPTH) if i % 2 == 1]

    def stack(parts, layers, heads):
        return jnp.stack([parts[i].reshape(n_ctx, seq_p, heads, HEAD_DIM) for i in layers], axis=1)

    return (y_prompt, y_sample,
            stack(new_k, even, NA_HEADS), stack(new_v, even, NA_HEADS),
            stack(new_k, odd, GQA_KV_HEADS), stack(new_v, odd, GQA_KV_HEADS))
```

```python
import functools
import math

import jax
import jax.numpy as jnp
import numpy as np
from jax import lax
from jax.experimental import pallas as pl
from jax.experimental.pallas import tpu as pltpu

f32 = jnp.float32
bf16 = jnp.bfloat16

D_MODEL = 2048
DEPTH = 2
GRID_W = 64
HEAD_DIM = 128
NA_HEADS = 12
NA_KR = 8
NA_KC = 16
FNET_GROUPS = 4
FNET_CH = 128
POOL_WINDOWS = (2, 4, 8, 16)
GQA_Q_HEADS = 12
GQA_KV_HEADS = 4
D_FF = 5632
ROPE_BASE = 10000.0
EPS = 1e-6
NEG_INF = -1e30
NA_WIDTH = NA_HEADS * HEAD_DIM
FNET_WIDTH = FNET_GROUPS * FNET_CH
POOL_WIDTH = len(POOL_WINDOWS) * FNET_CH
GQA_Q_WIDTH = GQA_Q_HEADS * HEAD_DIM
GQA_KV_WIDTH = GQA_KV_HEADS * HEAD_DIM
ATTN_SCALE = HEAD_DIM ** -0.5

MOD_ROWS = 16
VMEM_LIMIT = 56 * 1024 * 1024
HALO = 8
NA_QROWS = 4
NA_WROWS = 12
NA_INVALID = 2 * NA_KR - 1
NA_HEADS_PER_STEP = 6
FFN_ROWS = 128
FFN_SPLIT = 1
W_TILE = 512
IN_ROWS = 256


def _params(*sem, flags=None):
    return pltpu.CompilerParams(dimension_semantics=sem, vmem_limit_bytes=VMEM_LIMIT, flags=flags)


def _rms_modulate(x, g, shift, scale):
    ms = jnp.mean(x * x, axis=-1, keepdims=True)
    return (x * lax.rsqrt(ms + EPS)) * (g * (1.0 + scale)) + shift


def _mod_spec(k, tm, seq, base_row, per_seq):
    def index(m, n):
        return ((base_row + per_seq * ((m * tm) // seq)) * 6 + k, 0, 0)
    return pl.BlockSpec((None, 1, D_MODEL), index)


def _ada_kernel(c_ref, w_ref, b_ref, o_ref):
    c = c_ref[...]
    s = (c * jax.nn.sigmoid(c)).astype(bf16)
    o_ref[...] = jnp.dot(s, w_ref[...].astype(bf16), preferred_element_type=f32) + b_ref[...]


def ada_modulation(cond, ada_w, ada_b):
    tn = 1024
    n_out = 6 * D_MODEL
    return pl.pallas_call(
        _ada_kernel,
        grid=(DEPTH, n_out // tn),
        in_specs=[
            pl.BlockSpec((MOD_ROWS, D_MODEL), lambda i, n: (0, 0)),
            pl.BlockSpec((None, D_MODEL, tn), lambda i, n: (i, 0, n)),
            pl.BlockSpec((None, 1, tn), lambda i, n: (i, 0, n)),
        ],
        out_specs=pl.BlockSpec((None, MOD_ROWS, tn), lambda i, n: (i, 0, n)),
        out_shape=jax.ShapeDtypeStruct((DEPTH, MOD_ROWS, n_out), f32),
        compiler_params=_params("arbitrary", "arbitrary"),
        name="ada_modulation",
    )(cond, ada_w, ada_b.reshape(DEPTH, 1, n_out))


def _rope(a, cos, sin):
    lane = lax.broadcasted_iota(jnp.int32, a.shape, 1)
    quarter = HEAD_DIM // 4
    partner = jnp.where((lane & (2 * quarter - 1)) < quarter,
                        pltpu.roll(a, HEAD_DIM - quarter, 1),
                        pltpu.roll(a, quarter, 1))
    return a * cos + partner * sin


def _in_proj_kernel(tn, q_blocks, k_blocks, has_norm, has_rope, *refs):
    it = iter(refs)
    x_ref, shift_ref, scale_ref, g_ref, w_ref = (next(it) for _ in range(5))
    qg_ref = kg_ref = cos_ref = sin_ref = None
    if has_norm:
        qg_ref, kg_ref, avg_ref = next(it), next(it), next(it)
    if has_rope:
        cos_ref, sin_ref = next(it), next(it)
    o_ref, h_ref = next(it), next(it)
    n = pl.program_id(1)
    tm = x_ref.shape[0]
    everything = slice(0, tm)

    def project(rs, h=None):
        h = h_ref[rs, :] if h is None else h
        return jnp.dot(h, w_ref[...], preferred_element_type=f32)

    def store_scaled(rs, acc, scale):
        o_ref[rs, :] = (acc if scale is None else acc * scale).astype(o_ref.dtype)

    def store_normed(rs, acc, gn):
        ms = jnp.dot((acc * acc).astype(bf16), avg_ref[...], preferred_element_type=f32)
        inv = lax.rsqrt(ms + EPS)
        for hh in range(tn // HEAD_DIM):
            sl = slice(hh * HEAD_DIM, (hh + 1) * HEAD_DIM)
            a = acc[:, sl] * inv[:, sl] * gn
            if has_rope:
                a = _rope(a, cos_ref[rs, :], sin_ref[rs, :])
            o_ref[rs, sl] = a.astype(o_ref.dtype)

    @pl.when(n == 0)
    def _():
        g, sh, sc = g_ref[...], shift_ref[...], scale_ref[...]
        for r0 in range(0, tm, IN_ROWS):
            rs = slice(r0, r0 + IN_ROWS)
            h = _rms_modulate(x_ref[rs, :], g, sh, sc).astype(bf16)
            h_ref[rs, :] = h
            if has_norm:
                store_normed(rs, project(rs, h), qg_ref[...] * ATTN_SCALE)
            else:
                store_scaled(rs, project(rs, h), ATTN_SCALE)

    is_q = n < q_blocks
    q_scale = jnp.where(is_q, ATTN_SCALE, 1.0)
    if not has_norm:
        @pl.when(n > 0)
        def _():
            store_scaled(everything, project(everything), q_scale)
        return

    is_qk = n < q_blocks + k_blocks

    @pl.when((n > 0) & is_qk)
    def _():
        gn = jnp.where(is_q, qg_ref[...], kg_ref[...]) * q_scale
        store_normed(everything, project(everything), gn)

    @pl.when(jnp.logical_not(is_qk))
    def _():
        store_scaled(everything, project(everything), None)


def in_proj(x, mod, g, w, out_dtype, q_width, seq, base_row, per_seq, norm=None, rope=None):
    t_tokens = x.shape[0]
    tm, tn = 512, W_TILE
    n_total = w.shape[1]
    in_specs = [
        pl.BlockSpec((tm, D_MODEL), lambda m, n: (m, 0)),
        _mod_spec(0, tm, seq, base_row, per_seq),
        _mod_spec(1, tm, seq, base_row, per_seq),
        pl.BlockSpec((1, D_MODEL), lambda m, n: (0, 0)),
        pl.BlockSpec((D_MODEL, tn), lambda m, n: (0, n)),
    ]
    args = [x, mod, mod, g.reshape(1, D_MODEL), w]
    k_width = 0
    if norm is not None:
        k_width = norm[2]
        in_specs += [pl.BlockSpec((1, HEAD_DIM), lambda m, n: (0, 0))] * 2
        in_specs += [pl.BlockSpec((tn, tn), lambda m, n: (0, 0))]
        head_of = np.arange(tn) // HEAD_DIM
        avg = (head_of[:, None] == head_of[None, :]) / HEAD_DIM
        args += [norm[0].reshape(1, HEAD_DIM), norm[1].reshape(1, HEAD_DIM), jnp.asarray(avg, dtype=bf16)]
    if rope is not None:
        spb = seq // tm
        in_specs += [pl.BlockSpec((tm, HEAD_DIM), lambda m, n: (m % spb, 0))] * 2
        args += [rope[0], rope[1]]
    kern = functools.partial(_in_proj_kernel, tn, q_width // tn, k_width // tn,
                             norm is not None, rope is not None)
    return pl.pallas_call(
        kern,
        grid=(t_tokens // tm, n_total // tn),
        in_specs=in_specs,
        out_specs=pl.BlockSpec((tm, tn), lambda m, n: (m, n)),
        out_shape=jax.ShapeDtypeStruct((t_tokens, n_total), out_dtype),
        scratch_shapes=[pltpu.VMEM((tm, D_MODEL), bf16)],
        compiler_params=_params("arbitrary", "arbitrary"),
        name="in_proj",
    )(*args)


def _ctx_attn_kernel(n_q, n_kv, q_ref, k_ref, v_ref, o_ref):
    group = n_q // n_kv
    for h in range(n_q):
        kv = h // group
        q = q_ref[:, h * HEAD_DIM:(h + 1) * HEAD_DIM].astype(bf16)
        k = k_ref[:, kv * HEAD_DIM:(kv + 1) * HEAD_DIM].astype(bf16)
        v = v_ref[:, kv * HEAD_DIM:(kv + 1) * HEAD_DIM].astype(bf16)
        s = lax.dot_general(q, k, (((1,), (1,)), ((), ())), preferred_element_type=f32)
        m = jnp.max(s, axis=-1, keepdims=True)
        e = jnp.exp(s - m)
        l = jnp.sum(e, axis=-1, keepdims=True)
        o = jnp.dot(e.astype(bf16), v, preferred_element_type=f32) / l
        o_ref[:, h * HEAD_DIM:(h + 1) * HEAD_DIM] = o.astype(o_ref.dtype)


def ctx_attention(proj, seq, n_q, n_kv):
    t_tokens = proj.shape[0]
    qw, kw = n_q * HEAD_DIM, n_kv * HEAD_DIM
    assert qw % kw == 0
    k_blk = qw // kw
    return pl.pallas_call(
        functools.partial(_ctx_attn_kernel, n_q, n_kv),
        grid=(t_tokens // seq,),
        in_specs=[
            pl.BlockSpec((seq, qw), lambda b: (b, 0)),
            pl.BlockSpec((seq, kw), lambda b: (b, k_blk)),
            pl.BlockSpec((seq, kw), lambda b: (b, k_blk + 1)),
        ],
        out_specs=pl.BlockSpec((seq, qw), lambda b: (b, 0)),
        out_shape=jax.ShapeDtypeStruct((t_tokens, qw), bf16),
        compiler_params=_params("arbitrary"),
        name="ctx_attention",
    )(proj, proj, proj)


def _gqa_attn_kernel(group, q_ref, ck_ref, cv_ref, k_ref, v_ref, o_ref):
    ck = ck_ref[...].astype(bf16)
    cv = cv_ref[...].astype(bf16)
    k = k_ref[...]
    v = v_ref[...]
    dn = (((1,), (1,)), ((), ()))
    for gi in range(group):
        q = q_ref[:, gi * HEAD_DIM:(gi + 1) * HEAD_DIM]
        s_c = lax.dot_general(q, ck, dn, preferred_element_type=f32)
        s_l = lax.dot_general(q, k, dn, preferred_element_type=f32)
        m = jnp.maximum(jnp.max(s_c, axis=-1, keepdims=True),
                        jnp.max(s_l, axis=-1, keepdims=True))
        e_c = jnp.exp(s_c - m)
        e_l = jnp.exp(s_l - m)
        l = jnp.sum(e_c, axis=-1, keepdims=True) + jnp.sum(e_l, axis=-1, keepdims=True)
        o = (jnp.dot(e_c.astype(bf16), cv, preferred_element_type=f32)
             + jnp.dot(e_l.astype(bf16), v, preferred_element_type=f32)) / l
        o_ref[:, gi * HEAD_DIM:(gi + 1) * HEAD_DIM] = o.astype(o_ref.dtype)


def gqa_attention(proj, cache_k, cache_v, seq, past):
    t_tokens = proj.shape[0]
    n_b = t_tokens // seq
    group = GQA_Q_HEADS // GQA_KV_HEADS
    tq = 256
    nq = seq // tq
    return pl.pallas_call(
        functools.partial(_gqa_attn_kernel, group),
        grid=(n_b, GQA_KV_HEADS, nq),
        in_specs=[
            pl.BlockSpec((tq, group * HEAD_DIM), lambda b, h, i: (b * nq + i, h)),
            pl.BlockSpec((past, HEAD_DIM), lambda b, h, i: (b, h)),
            pl.BlockSpec((past, HEAD_DIM), lambda b, h, i: (b, h)),
            pl.BlockSpec((seq, HEAD_DIM), lambda b, h, i: (b, GQA_Q_HEADS + h)),
            pl.BlockSpec((seq, HEAD_DIM), lambda b, h, i: (b, GQA_Q_HEADS + GQA_KV_HEADS + h)),
        ],
        out_specs=pl.BlockSpec((tq, group * HEAD_DIM), lambda b, h, i: (b * nq + i, h)),
        out_shape=jax.ShapeDtypeStruct((t_tokens, GQA_Q_WIDTH), bf16),
        compiler_params=_params("arbitrary", "arbitrary", "arbitrary"),
        name="gqa_attention",
    )(proj, cache_k, cache_v, proj, proj)


def _na_bias_tiles(rel_bias):
    qc = jnp.arange(GRID_W)[:, None]
    kc = jnp.arange(GRID_W)[None, :]
    q_start = jnp.clip(qc - NA_KC // 2, 0, GRID_W - NA_KC)
    valid = (kc >= q_start) & (kc < q_start + NA_KC)
    dc = jnp.clip(kc - qc, -(NA_KC - 1), NA_KC - 1) + NA_KC - 1
    tiles = jnp.where(valid[None, None], rel_bias.astype(f32)[:, :, dc], NEG_INF)
    masked = jnp.full((NA_HEADS, 1, GRID_W, GRID_W), NEG_INF, f32)
    tiles = jnp.concatenate([tiles, masked], axis=1)
    zeros = jnp.zeros_like(tiles)
    left = jnp.concatenate([tiles, zeros], axis=-1)
    right = jnp.concatenate([zeros, tiles], axis=-1)
    return left, right


def _na_attn_kernel(n_rows, q_ref, k_ref, v_ref, ck_ref, cv_ref, bl_ref, br_ref, o_ref, *s_refs):
    rb = pl.program_id(2)
    start = jnp.clip(rb * NA_QROWS - NA_KR // 2, 0, n_rows - NA_WROWS)
    tok0 = pl.multiple_of(start * GRID_W, GRID_W)
    dn = (((1,), (1,)), ((), ()))
    for hh, s_ref in enumerate(s_refs):
        hs = slice(hh * HEAD_DIM, (hh + 1) * HEAD_DIM)
        kw = k_ref[pl.ds(tok0, NA_WROWS * GRID_W), hs]
        vw = v_ref[pl.ds(tok0, NA_WROWS * GRID_W), hs]
        q = q_ref[:, hs]
        s_ref[...] = lax.dot_general(q, kw, dn, preferred_element_type=f32)
        for i in range(NA_QROWS):
            qr = rb * NA_QROWS + i
            r0 = jnp.clip(qr - NA_KR // 2, 0, n_rows - NA_KR)
            for jp in range(NA_WROWS // 2):
                ka = start + 2 * jp
                kb = ka + 1
                ia = jnp.where((ka >= r0) & (ka < r0 + NA_KR), ka - qr + NA_KR - 1, NA_INVALID)
                ib = jnp.where((kb >= r0) & (kb < r0 + NA_KR), kb - qr + NA_KR - 1, NA_INVALID)
                rows = slice(i * GRID_W, (i + 1) * GRID_W)
                cols = slice(jp * 2 * GRID_W, (jp + 1) * 2 * GRID_W)
                s_ref[rows, cols] = s_ref[rows, cols] + bl_ref[hh, ia] + br_ref[hh, ib]
        s_l = s_ref[...]
        s_c = lax.dot_general(q, ck_ref[:, hs].astype(bf16), dn, preferred_element_type=f32)
        m = jnp.maximum(jnp.max(s_c, axis=-1, keepdims=True), jnp.max(s_l, axis=-1, keepdims=True))
        e_c = jnp.exp(s_c - m)
        e_l = jnp.exp(s_l - m)
        l = jnp.sum(e_c, axis=-1, keepdims=True) + jnp.sum(e_l, axis=-1, keepdims=True)
        o = (jnp.dot(e_c.astype(bf16), cv_ref[:, hs].astype(bf16), preferred_element_type=f32)
             + jnp.dot(e_l.astype(bf16), vw, preferred_element_type=f32)) / l
        o_ref[:, hs] = o.astype(o_ref.dtype)


def na_attention(proj, cache_k, cache_v, rel_bias, seq, past):
    t_tokens = proj.shape[0]
    n_b = t_tokens // seq
    n_rows = seq // GRID_W
    n_rb = n_rows // NA_QROWS
    tq = NA_QROWS * GRID_W
    hb = NA_HEADS_PER_STEP
    n_hg = NA_HEADS // hb
    wide = hb * HEAD_DIM
    bias_l, bias_r = _na_bias_tiles(rel_bias)
    bias_spec = pl.BlockSpec((hb, 2 * NA_KR, GRID_W, 2 * GRID_W), lambda b, h, r: (h, 0, 0, 0))
    return pl.pallas_call(
        functools.partial(_na_attn_kernel, n_rows),
        grid=(n_b, n_hg, n_rb),
        in_specs=[
            pl.BlockSpec((tq, wide), lambda b, h, r: (b * n_rb + r, h)),
            pl.BlockSpec((seq, wide), lambda b, h, r: (b, n_hg + h)),
            pl.BlockSpec((seq, wide), lambda b, h, r: (b, 2 * n_hg + h)),
            pl.BlockSpec((past, wide), lambda b, h, r: (b, h)),
            pl.BlockSpec((past, wide), lambda b, h, r: (b, h)),
            bias_spec,
            bias_spec,
        ],
        out_specs=pl.BlockSpec((tq, wide), lambda b, h, r: (b * n_rb + r, h)),
        out_shape=jax.ShapeDtypeStruct((t_tokens, NA_WIDTH), bf16),
        scratch_shapes=[pltpu.VMEM((tq, NA_WROWS * GRID_W), f32)] * hb,
        compiler_params=_params("arbitrary", "arbitrary", "arbitrary"),
        name="na_attention",
    )(proj, proj, proj, cache_k, cache_v, bias_l, bias_r)


def _dft_cols_kernel(x_ref, cs_ref, y_ref):
    for g in range(FNET_GROUPS):
        sl = slice(g * FNET_CH, (g + 1) * FNET_CH)
        y = jnp.dot(x_ref[:, sl].astype(bf16), cs_ref[...], preferred_element_type=f32)
        y_ref[0, :, sl] = y[:, :FNET_CH].astype(bf16)
        y_ref[1, :, sl] = y[:, FNET_CH:].astype(bf16)


def _dft_rows_kernel(d_ref, y_ref, w_ref, o_ref):
    f = jnp.dot(d_ref[...], y_ref[...], preferred_element_type=f32)
    for g in range(FNET_GROUPS):
        sl = slice(g * FNET_CH, (g + 1) * FNET_CH)
        o = jnp.dot(f[:, sl].astype(bf16), w_ref[g], preferred_element_type=f32)
        o_ref[:, sl] = o.astype(o_ref.dtype)


def _dft_tables(seq):
    c = np.arange(FNET_CH, dtype=np.int64)
    ang_c = (2.0 * np.pi / FNET_CH) * ((c[:, None] * c[None, :]) % FNET_CH)
    cs = np.concatenate([np.cos(ang_c), np.sin(ang_c)], axis=1)
    n = np.arange(seq, dtype=np.int64)
    ang_n = (2.0 * np.pi / seq) * ((n[:, None] * n[None, :]) % seq)
    norm = 1.0 / math.sqrt(seq * FNET_CH)
    d = np.concatenate([np.cos(ang_n) * norm, -np.sin(ang_n) * norm], axis=1)
    return jnp.asarray(cs, dtype=bf16), jnp.asarray(d, dtype=bf16)


def fourier_mix(proj, col_blk, fnet_w, seq):
    t_tokens = proj.shape[0]
    n_b = t_tokens // seq
    cs, d = _dft_tables(seq)
    tn = min(seq, 512)
    ns = seq // tn
    y = pl.pallas_call(
        _dft_cols_kernel,
        grid=(n_b, ns),
        in_specs=[
            pl.BlockSpec((tn, FNET_WIDTH), lambda b, i: (b * ns + i, col_blk)),
            pl.BlockSpec((FNET_CH, 2 * FNET_CH), lambda b, i: (0, 0)),
        ],
        out_specs=pl.BlockSpec((2, tn, FNET_WIDTH), lambda b, i: (0, i, b)),
        out_shape=jax.ShapeDtypeStruct((2, seq, n_b * FNET_WIDTH), bf16),
        compiler_params=_params("arbitrary", "arbitrary"),
        name="dft_cols",
    )(proj, cs)
    y2 = y.reshape(2 * seq, n_b * FNET_WIDTH)
    return pl.pallas_call(
        _dft_rows_kernel,
        grid=(n_b, ns),
        in_specs=[
            pl.BlockSpec((tn, 2 * seq), lambda b, i: (i, 0)),
            pl.BlockSpec((2 * seq, FNET_WIDTH), lambda b, i: (0, b)),
            pl.BlockSpec((FNET_GROUPS, FNET_CH, FNET_CH), lambda b, i: (0, 0, 0)),
        ],
        out_specs=pl.BlockSpec((tn, FNET_WIDTH), lambda b, i: (b * ns + i, 0)),
        out_shape=jax.ShapeDtypeStruct((t_tokens, FNET_WIDTH), bf16),
        compiler_params=_params("arbitrary", "arbitrary"),
        name="dft_rows",
    )(d, y2, fnet_w.astype(bf16))


def _pool_kernel(seq, x_ref, w_ref, sc_ref, o_ref):
    t = lax.broadcasted_iota(jnp.int32, (seq, 1), 0)

    def prev(a, k):
        return jnp.where(t >= k, pltpu.roll(a, k, 0), 0.0)

    def nxt(a, k):
        return jnp.where(t < seq - k, pltpu.roll(a, seq - k, 0), 0.0)

    for g, win in enumerate(POOL_WINDOWS):
        sl = slice(g * FNET_CH, (g + 1) * FNET_CH)
        x = x_ref[:, sl].astype(f32)
        half = win // 2
        back, fwd, k = x, x, 1
        while k < half:
            back = back + prev(back, k)
            fwd = fwd + nxt(fwd, k)
            k *= 2
        total = prev(back, 1) + fwd
        cnt = (jnp.minimum(t + half, seq) - jnp.maximum(t - half, 0)).astype(f32)
        pooled = (total / cnt - x).astype(bf16)
        o = jnp.dot(pooled, w_ref[g], preferred_element_type=f32) * sc_ref[:, sl]
        o_ref[:, sl] = o.astype(o_ref.dtype)


def pool_mix(proj, col_blk, pool_w, pool_scale, seq):
    t_tokens = proj.shape[0]
    return pl.pallas_call(
        functools.partial(_pool_kernel, seq),
        grid=(t_tokens // seq,),
        in_specs=[
            pl.BlockSpec((seq, POOL_WIDTH), lambda b: (b, col_blk)),
            pl.BlockSpec((len(POOL_WINDOWS), FNET_CH, FNET_CH), lambda b: (0, 0, 0)),
            pl.BlockSpec((1, POOL_WIDTH), lambda b: (0, 0)),
        ],
        out_specs=pl.BlockSpec((seq, POOL_WIDTH), lambda b: (b, 0)),
        out_shape=jax.ShapeDtypeStruct((t_tokens, POOL_WIDTH), bf16),
        compiler_params=_params("arbitrary"),
        name="pool_mix",
    )(proj, pool_w.astype(bf16), pool_scale.reshape(1, POOL_WIDTH))


def _out_proj_kernel(x_ref, gate_ref, a_ref, b_ref, wa_ref, wb_ref, o_ref):
    y = (jnp.dot(a_ref[...], wa_ref[...], preferred_element_type=f32)
         + jnp.dot(b_ref[...], wb_ref[...], preferred_element_type=f32))
    o_ref[...] = x_ref[...] + gate_ref[...] * y


def out_proj(x, mod, a, b, wa, wb, seq, base_row, per_seq):
    t_tokens = x.shape[0]
    tm = 512
    return pl.pallas_call(
        _out_proj_kernel,
        grid=(t_tokens // tm, 1),
        in_specs=[
            pl.BlockSpec((tm, D_MODEL), lambda m, n: (m, 0)),
            _mod_spec(2, tm, seq, base_row, per_seq),
            pl.BlockSpec((tm, a.shape[1]), lambda m, n: (m, 0)),
            pl.BlockSpec((tm, b.shape[1]), lambda m, n: (m, 0)),
            pl.BlockSpec(wa.shape, lambda m, n: (0, 0)),
            pl.BlockSpec(wb.shape, lambda m, n: (0, 0)),
        ],
        out_specs=pl.BlockSpec((tm, D_MODEL), lambda m, n: (m, 0)),
        out_shape=jax.ShapeDtypeStruct((t_tokens, D_MODEL), f32),
        compiler_params=_params("arbitrary", "arbitrary"),
        name="out_proj",
    )(x, mod, a, b, wa, wb)


def _ffn_kernel(tm, seq, nj, final, *refs):
    it = iter(refs)
    (x_ref, xp_ref, xn_ref, shift_ref, scale_ref, gate_ref, g_ref,
     wv_ref, wg_ref, cwv_ref, cwg_ref, cbv_ref, cbg_ref, wd_ref) = (next(it) for _ in range(14))
    gf_ref = next(it) if final else None
    o_ref, h_ref, acc_ref = next(it), next(it), next(it)
    u_refs = [(next(it), next(it)), (next(it), next(it))]
    i = pl.program_id(0)
    j = pl.program_id(1)
    rows = tm + 2 * HALO

    def prologue_and_first_up():
        g, sh, sc = g_ref[...], shift_ref[...], scale_ref[...]
        uv_ref, ug_ref = u_refs[0]

        def emit(rs, h):
            h_ref[rs, :] = h
            uv_ref[rs, :] = jnp.dot(h, wv_ref[...], preferred_element_type=f32)
            ug_ref[rs, :] = jnp.dot(h, wg_ref[...], preferred_element_type=f32)

        for r0 in range(0, tm - IN_ROWS, IN_ROWS):
            rs = slice(r0, r0 + IN_ROWS)
            emit(rs, _rms_modulate(x_ref[rs, :], g, sh, sc).astype(bf16))
        next_ok = jnp.where((((i + 1) * tm) & (seq - 1)) != 0, 1.0, 0.0)
        prev_ok = jnp.where(((i * tm) & (seq - 1)) != 0, 1.0, 0.0)
        h_last = _rms_modulate(x_ref[tm - IN_ROWS:tm, :], g, sh, sc)
        h_next = _rms_modulate(xn_ref[...], g, sh, sc) * next_ok
        h_prev = _rms_modulate(xp_ref[...], g, sh, sc) * prev_ok
        emit(slice(tm - IN_ROWS, rows), jnp.concatenate([h_last, h_next, h_prev], axis=0).astype(bf16))
        acc_ref[...] = jnp.zeros_like(acc_ref)

    def zero_row(a, r):
        r0 = (r // HALO) * HALO
        row8 = lax.broadcasted_iota(jnp.int32, (HALO, 1), 0)
        fixed = jnp.where(row8 == r - r0, 0.0, a[r0:r0 + HALO])
        return jnp.concatenate([a[:r0], fixed, a[r0 + HALO:]], axis=0)

    def window(u_ref, r0):
        lo, hi = r0 - HALO, r0 + FFN_ROWS + HALO
        if lo < 0:
            return jnp.concatenate([u_ref[rows + lo:rows, :], u_ref[0:hi, :]], axis=0)
        return u_ref[lo:hi, :]

    def conv(u_ref, r0, cw_ref, cb_ref):
        u = window(u_ref, r0)
        n = FFN_ROWS + 2 * HALO
        up = pltpu.roll(u, 1, 0)
        un = pltpu.roll(u, n - 1, 0)
        for edge in range(seq, tm, seq):
            if r0 <= edge < r0 + FFN_ROWS:
                up = zero_row(up, edge - r0 + HALO)
            if r0 <= edge - 1 < r0 + FFN_ROWS:
                un = zero_row(un, edge - 1 - r0 + HALO)
        c = up * cw_ref[0:1, :] + u * cw_ref[1:2, :] + un * cw_ref[2:3, :] + cb_ref[...]
        return c[HALO:HALO + FFN_ROWS, :]

    def gated(slot):
        uv_ref, ug_ref = u_refs[slot]
        chunks = []
        for r0 in range(0, tm, FFN_ROWS):
            val = conv(uv_ref, r0, cwv_ref, cbv_ref)
            gate = conv(ug_ref, r0, cwg_ref, cbg_ref)
            chunks.append((gate * jax.nn.sigmoid(gate) * val).astype(bf16))
        return chunks

    def zero_after(chunks):
        parts = []
        for c in chunks:
            bits = pltpu.bitcast(c, jnp.uint32)
            parts += [bits[k:k + HALO] for k in range(0, bits.shape[0], HALO)]
        while len(parts) > 1:
            parts = [a | b for a, b in zip(parts[0::2], parts[1::2])] + parts[len(parts) & ~1:]
        return (parts[0] >> 16) >> 16

    def after(a, zero, row0=0):
        tf = zero.shape[1]
        mid = pltpu.bitcast(pltpu.bitcast(a[row0:row0 + 2 * HALO, 0:tf], jnp.uint32) | zero, bf16)
        if tf < a.shape[1]:
            mid = jnp.concatenate([mid, a[row0:row0 + 2 * HALO, tf:]], axis=1)
        parts = ([a[:row0]] if row0 else []) + [mid, a[row0 + 2 * HALO:]]
        return jnp.concatenate(parts, axis=0)

    def up_project(slot, chunks=None):
        h = h_ref[...]
        wv, wg = wv_ref[...], wg_ref[...]
        uv_ref, ug_ref = u_refs[slot]
        if chunks is not None:
            k_mid = wv.shape[0] // 2
            wv = after(wv, zero_after(chunks[0:1]), k_mid)
            wg = after(wg, zero_after(chunks[2:3]), k_mid)
        uv_ref[...] = jnp.dot(h, wv, preferred_element_type=f32)
        if chunks is not None:
            h = after(h, zero_after(chunks[1:2]))
        ug_ref[...] = jnp.dot(h, wg, preferred_element_type=f32)

    def down_project(chunks, anchored):
        if anchored:
            chunks = [after(chunks[0], zero_after(chunks[3:]))] + chunks[1:]
        act = jnp.concatenate(chunks, axis=0)
        acc_ref[...] += jnp.dot(act, wd_ref[...], preferred_element_type=f32)

    @pl.when(j == 0)
    def _():
        prologue_and_first_up()

    for p in range(2):
        @pl.when((j >= 1) & (j < nj) & (j % 2 == p))
        def _(p=p):
            chunks = gated(1 - p)
            up_project(p, chunks)
            down_project(chunks, True)

    @pl.when(j == nj)
    def _():
        down_project(gated((nj - 1) % 2), False)
        y = x_ref[...] + gate_ref[...] * acc_ref[...]
        if final:
            ms = jnp.mean(y * y, axis=-1, keepdims=True)
            y = y * lax.rsqrt(ms + EPS) * gf_ref[...]
        o_ref[...] = y


def conv_ffn(x, mod, g, w_up, conv_w, conv_b, w_down, seq, base_row, per_seq, final_g=None):
    t_tokens = x.shape[0]
    tm, tf = 512, W_TILE
    assert seq & (seq - 1) == 0
    nj = D_FF // tf
    hb = tm // HALO
    n_hblk = t_tokens // HALO
    conv_b2 = conv_b.reshape(1, 2 * D_FF)

    def blk(j):
        return jnp.clip(j, 0, nj - 1)

    in_specs = [
        pl.BlockSpec((tm, D_MODEL), lambda i, j: (i, 0)),
        pl.BlockSpec((HALO, D_MODEL), lambda i, j: (jnp.maximum(i * hb - 1, 0), 0)),
        pl.BlockSpec((HALO, D_MODEL), lambda i, j: (jnp.minimum((i + 1) * hb, n_hblk - 1), 0)),
        _mod_spec(3, tm, seq, base_row, per_seq),
        _mod_spec(4, tm, seq, base_row, per_seq),
        _mod_spec(5, tm, seq, base_row, per_seq),
        pl.BlockSpec((1, D_MODEL), lambda i, j: (0, 0)),
        pl.BlockSpec((D_MODEL, tf), lambda i, j: (0, blk(j))),
        pl.BlockSpec((D_MODEL, tf), lambda i, j: (0, nj + blk(j))),
        pl.BlockSpec((3, tf), lambda i, j: (0, blk(j - 1))),
        pl.BlockSpec((3, tf), lambda i, j: (0, nj + blk(j - 1))),
        pl.BlockSpec((1, tf), lambda i, j: (0, blk(j - 1))),
        pl.BlockSpec((1, tf), lambda i, j: (0, nj + blk(j - 1))),
        pl.BlockSpec((tf, D_MODEL), lambda i, j: (blk(j - 1), 0)),
    ]
    args = [x, x, x, mod, mod, mod, g.reshape(1, D_MODEL), w_up, w_up,
            conv_w, conv_w, conv_b2, conv_b2, w_down]
    if final_g is not None:
        in_specs.append(pl.BlockSpec((1, D_MODEL), lambda i, j: (0, 0)))
        args.append(final_g.reshape(1, D_MODEL))
    return pl.pallas_call(
        functools.partial(_ffn_kernel, tm, seq, nj, final_g is not None),
        grid=(t_tokens // tm, nj + 1),
        in_specs=in_specs,
        out_specs=pl.BlockSpec((tm, D_MODEL), lambda i, j: (i, 0)),
        out_shape=jax.ShapeDtypeStruct((t_tokens, D_MODEL), f32),
        scratch_shapes=[pltpu.VMEM((tm + 2 * HALO, D_MODEL), bf16),
                        pltpu.VMEM((tm, D_MODEL), f32)]
                       + [pltpu.VMEM((tm + 2 * HALO, tf), f32)] * 4,
        compiler_params=_params("arbitrary", "arbitrary"),
        name="conv_ffn",
    )(*args)


def _rope_tables(seq):
    t = np.arange(seq)
    half = HEAD_DIM // 2
    inv_freq = np.power(ROPE_BASE, -np.arange(0, half, 2, dtype=np.float64) / half)
    ang_r = (t // GRID_W)[:, None] * inv_freq[None, :]
    ang_c = (t % GRID_W)[:, None] * inv_freq[None, :]
    cos = np.concatenate([np.cos(ang_r)] * 2 + [np.cos(ang_c)] * 2, axis=-1)
    sin = np.concatenate([-np.sin(ang_r), np.sin(ang_r), -np.sin(ang_c), np.sin(ang_c)], axis=-1)
    return jnp.asarray(cos, dtype=f32), jnp.asarray(sin, dtype=f32)


def kernel(x_prompt, x_sample, c, cache_na_k, cache_na_v, cache_gqa_k, cache_gqa_v, c_ctx, norm1_g, norm2_g, ada_w, ada_b, ev_w_in, ev_na_bias, ev_fnet_w, ev_w_out, od_w_in, od_pool_w, od_pool_scale, od_q_norm_g, od_k_norm_g, od_w_out, ffn_w_up, ffn_conv_w, ffn_conv_b, ffn_w_down, final_norm_g):
    n_ctx, seq_p, _ = x_prompt.shape
    n_lat, seq_s, _ = x_sample.shape
    past = cache_na_k.shape[2]
    assert n_lat + 1 <= MOD_ROWS
    xp = x_prompt.reshape(n_ctx * seq_p, D_MODEL)
    xs = x_sample.reshape(n_lat * seq_s, D_MODEL)

    cond = jnp.zeros((MOD_ROWS, D_MODEL), f32).at[0].set(c_ctx).at[1:1 + n_lat].set(c)
    mods = ada_modulation(cond, ada_w, ada_b).reshape(DEPTH, MOD_ROWS * 6, 1, D_MODEL)
    rope = _rope_tables(seq_s)

    ctx = dict(seq=seq_p, base_row=0, per_seq=0)
    lat = dict(seq=seq_s, base_row=1, per_seq=1)
    new_k, new_v = {}, {}
    for i in range(DEPTH):
        j = i // 2
        mod = mods[i]
        w_up = ffn_w_up[i].astype(bf16)
        w_down = ffn_w_down[i].astype(bf16)
        if i % 2 == 0:
            w_in = ev_w_in[j].astype(bf16)
            w_attn = ev_w_out[j, :NA_WIDTH].astype(bf16)
            w_mix = ev_w_out[j, NA_WIDTH:].astype(bf16)
            mix_blk = 3 * NA_WIDTH // FNET_WIDTH
            proj_p = in_proj(xp, mod, norm1_g[i], w_in, f32, NA_WIDTH, **ctx)
            new_k[i] = proj_p[:, NA_WIDTH:2 * NA_WIDTH]
            new_v[i] = proj_p[:, 2 * NA_WIDTH:3 * NA_WIDTH]
            attn_p = ctx_attention(proj_p, seq_p, NA_HEADS, NA_HEADS)
            mix_p = fourier_mix(proj_p, mix_blk, ev_fnet_w[j], seq_p)
            proj_s = in_proj(xs, mod, norm1_g[i], w_in, bf16, NA_WIDTH, **lat)
            attn_s = na_attention(proj_s,
                                  cache_na_k[:, j].reshape(n_lat * past, NA_WIDTH),
                                  cache_na_v[:, j].reshape(n_lat * past, NA_WIDTH),
                                  ev_na_bias[j], seq_s, past)
            mix_s = fourier_mix(proj_s, mix_blk, ev_fnet_w[j], seq_s)
        else:
            w_in = jnp.concatenate([od_w_in[j, :, POOL_WIDTH:], od_w_in[j, :, :POOL_WIDTH]],
                                   axis=1).astype(bf16)
            w_attn = od_w_out[j, POOL_WIDTH:].astype(bf16)
            w_mix = od_w_out[j, :POOL_WIDTH].astype(bf16)
            norm = (od_q_norm_g[j], od_k_norm_g[j], GQA_KV_WIDTH)
            k_lo = GQA_Q_WIDTH
            v_lo = GQA_Q_WIDTH + GQA_KV_WIDTH
            mix_blk = (GQA_Q_WIDTH + 2 * GQA_KV_WIDTH) // POOL_WIDTH
            proj_p = in_proj(xp, mod, norm1_g[i], w_in, f32, GQA_Q_WIDTH, norm=norm, **ctx)
            new_k[i] = proj_p[:, k_lo:k_lo + GQA_KV_WIDTH]
            new_v[i] = proj_p[:, v_lo:v_lo + GQA_KV_WIDTH]
            attn_p = ctx_attention(proj_p, seq_p, GQA_Q_HEADS, GQA_KV_HEADS)
            mix_p = pool_mix(proj_p, mix_blk, od_pool_w[j], od_pool_scale[j], seq_p)
            proj_s = in_proj(xs, mod, norm1_g[i], w_in, bf16, GQA_Q_WIDTH, norm=norm, rope=rope, **lat)
            attn_s = gqa_attention(proj_s,
                                   cache_gqa_k[:, j].reshape(n_lat * past, GQA_KV_WIDTH),
                                   cache_gqa_v[:, j].reshape(n_lat * past, GQA_KV_WIDTH),
                                   seq_s, past)
            mix_s = pool_mix(proj_s, mix_blk, od_pool_w[j], od_pool_scale[j], seq_s)
        xp = out_proj(xp, mod, attn_p, mix_p, w_attn, w_mix, **ctx)
        xs = out_proj(xs, mod, attn_s, mix_s, w_attn, w_mix, **lat)
        final_g = final_norm_g if i == DEPTH - 1 else None
        xp = conv_ffn(xp, mod, norm2_g[i], w_up, ffn_conv_w[i], ffn_conv_b[i], w_down,
                      final_g=final_g, **ctx)
        xs = conv_ffn(xs, mod, norm2_g[i], w_up, ffn_conv_w[i], ffn_conv_b[i], w_down,
                      final_g=final_g, **lat)

    y_prompt = xp.reshape(n_ctx, seq_p, D_MODEL)
    y_sample = xs.reshape(n_lat, seq_s, D_MODEL)
    even = [i for i in range(DEPTH) if i % 2 == 0]
    odd = [i for i in range(DEPTH) if i % 2 == 1]

    def stack(parts, layers, heads):
        return jnp.stack([parts[i].reshape(n_ctx, seq_p, heads, HEAD_DIM) for i in layers], axis=1)

    return (y_prompt, y_sample,
            stack(new_k, even, NA_HEADS), stack(new_v, even, NA_HEADS),
            stack(new_k, odd, GQA_KV_HEADS), stack(new_v, odd, GQA_KV_HEADS))
```

```python
import functools
import math

import jax
import jax.numpy as jnp
import numpy as np
from jax import lax
from jax.experimental import pallas as pl
from jax.experimental.pallas import tpu as pltpu

f32 = jnp.float32
bf16 = jnp.bfloat16

D_MODEL = 2048
DEPTH = 2
GRID_W = 64
HEAD_DIM = 128
NA_HEADS = 12
NA_KR = 8
NA_KC = 16
FNET_GROUPS = 4
FNET_CH = 128
POOL_WINDOWS = (2, 4, 8, 16)
GQA_Q_HEADS = 12
GQA_KV_HEADS = 4
D_FF = 5632
ROPE_BASE = 10000.0
EPS = 1e-6
NEG_INF = -1e30
NA_WIDTH = NA_HEADS * HEAD_DIM
FNET_WIDTH = FNET_GROUPS * FNET_CH
POOL_WIDTH = len(POOL_WINDOWS) * FNET_CH
GQA_Q_WIDTH = GQA_Q_HEADS * HEAD_DIM
GQA_KV_WIDTH = GQA_KV_HEADS * HEAD_DIM
ATTN_SCALE = HEAD_DIM ** -0.5

MOD_ROWS = 16
VMEM_LIMIT = 56 * 1024 * 1024
HALO = 8
NA_QROWS = 4
NA_WROWS = 12
NA_INVALID = 2 * NA_KR - 1
NA_HEADS_PER_STEP = 6
FFN_ROWS = 128
FFN_SPLIT = 1
W_TILE = 512
IN_TOKENS = 1024
IN_ROWS = 256


def _params(*sem, flags=None):
    return pltpu.CompilerParams(dimension_semantics=sem, vmem_limit_bytes=VMEM_LIMIT, flags=flags)


def _rms_modulate(x, g, shift, scale):
    ms = jnp.mean(x * x, axis=-1, keepdims=True)
    return (x * lax.rsqrt(ms + EPS)) * (g * (1.0 + scale)) + shift


def _mod_spec(k, tm, seq, base_row, per_seq):
    def index(m, n):
        return ((base_row + per_seq * ((m * tm) // seq)) * 6 + k, 0, 0)
    return pl.BlockSpec((None, 1, D_MODEL), index)


def _ada_kernel(c_ref, w_ref, b_ref, o_ref):
    c = c_ref[...]
    s = (c * jax.nn.sigmoid(c)).astype(bf16)
    o_ref[...] = jnp.dot(s, w_ref[...].astype(bf16), preferred_element_type=f32) + b_ref[...]


def ada_modulation(cond, ada_w, ada_b):
    tn = 1024
    n_out = 6 * D_MODEL
    return pl.pallas_call(
        _ada_kernel,
        grid=(DEPTH, n_out // tn),
        in_specs=[
            pl.BlockSpec((MOD_ROWS, D_MODEL), lambda i, n: (0, 0)),
            pl.BlockSpec((None, D_MODEL, tn), lambda i, n: (i, 0, n)),
            pl.BlockSpec((None, 1, tn), lambda i, n: (i, 0, n)),
        ],
        out_specs=pl.BlockSpec((None, MOD_ROWS, tn), lambda i, n: (i, 0, n)),
        out_shape=jax.ShapeDtypeStruct((DEPTH, MOD_ROWS, n_out), f32),
        compiler_params=_params("arbitrary", "arbitrary"),
        name="ada_modulation",
    )(cond, ada_w, ada_b.reshape(DEPTH, 1, n_out))


def _rope(a, cos, sin):
    lane = lax.broadcasted_iota(jnp.int32, a.shape, 1)
    quarter = HEAD_DIM // 4
    partner = jnp.where((lane & (2 * quarter - 1)) < quarter,
                        pltpu.roll(a, HEAD_DIM - quarter, 1),
                        pltpu.roll(a, quarter, 1))
    return a * cos + partner * sin


def _in_proj_kernel(tn, q_blocks, k_blocks, has_norm, has_rope, *refs):
    it = iter(refs)
    x_ref, shift_ref, scale_ref, g_ref, w_ref = (next(it) for _ in range(5))
    qg_ref = kg_ref = cos_ref = sin_ref = None
    if has_norm:
        qg_ref, kg_ref, avg_ref = next(it), next(it), next(it)
    if has_rope:
        cos_ref, sin_ref = next(it), next(it)
    o_ref, h_ref = next(it), next(it)
    n = pl.program_id(1)
    tm = x_ref.shape[0]
    everything = slice(0, tm)

    def project(rs, h=None):
        h = h_ref[rs, :] if h is None else h
        return jnp.dot(h, w_ref[...], preferred_element_type=f32)

    def store_scaled(rs, acc, scale):
        o_ref[rs, :] = (acc if scale is None else acc * scale).astype(o_ref.dtype)

    def store_normed(rs, acc, gn):
        ms = jnp.dot((acc * acc).astype(bf16), avg_ref[...], preferred_element_type=f32)
        inv = lax.rsqrt(ms + EPS)
        for hh in range(tn // HEAD_DIM):
            sl = slice(hh * HEAD_DIM, (hh + 1) * HEAD_DIM)
            a = acc[:, sl] * inv[:, sl] * gn
            if has_rope:
                a = _rope(a, cos_ref[rs, :], sin_ref[rs, :])
            o_ref[rs, sl] = a.astype(o_ref.dtype)

    @pl.when(n == 0)
    def _():
        g, sh, sc = g_ref[...], shift_ref[...], scale_ref[...]
        for r0 in range(0, tm, IN_ROWS):
            rs = slice(r0, r0 + IN_ROWS)
            h = _rms_modulate(x_ref[rs, :], g, sh, sc).astype(bf16)
            h_ref[rs, :] = h
            if has_norm:
                store_normed(rs, project(rs, h), qg_ref[...] * ATTN_SCALE)
            else:
                store_scaled(rs, project(rs, h), ATTN_SCALE)

    is_q = n < q_blocks
    q_scale = jnp.where(is_q, ATTN_SCALE, 1.0)
    if not has_norm:
        @pl.when(n > 0)
        def _():
            store_scaled(everything, project(everything), q_scale)
        return

    is_qk = n < q_blocks + k_blocks

    @pl.when((n > 0) & is_qk)
    def _():
        gn = jnp.where(is_q, qg_ref[...], kg_ref[...]) * q_scale
        store_normed(everything, project(everything), gn)

    @pl.when(jnp.logical_not(is_qk))
    def _():
        store_scaled(everything, project(everything), None)


def in_proj(x, mod, g, w, out_dtype, q_width, seq, base_row, per_seq, norm=None, rope=None):
    t_tokens = x.shape[0]
    tm, tn = IN_TOKENS, W_TILE
    n_total = w.shape[1]
    in_specs = [
        pl.BlockSpec((tm, D_MODEL), lambda m, n: (m, 0)),
        _mod_spec(0, tm, seq, base_row, per_seq),
        _mod_spec(1, tm, seq, base_row, per_seq),
        pl.BlockSpec((1, D_MODEL), lambda m, n: (0, 0)),
        pl.BlockSpec((D_MODEL, tn), lambda m, n: (0, n)),
    ]
    args = [x, mod, mod, g.reshape(1, D_MODEL), w]
    k_width = 0
    if norm is not None:
        k_width = norm[2]
        in_specs += [pl.BlockSpec((1, HEAD_DIM), lambda m, n: (0, 0))] * 2
        in_specs += [pl.BlockSpec((tn, tn), lambda m, n: (0, 0))]
        head_of = np.arange(tn) // HEAD_DIM
        avg = (head_of[:, None] == head_of[None, :]) / HEAD_DIM
        args += [norm[0].reshape(1, HEAD_DIM), norm[1].reshape(1, HEAD_DIM), jnp.asarray(avg, dtype=bf16)]
    if rope is not None:
        spb = seq // tm
        in_specs += [pl.BlockSpec((tm, HEAD_DIM), lambda m, n: (m % spb, 0))] * 2
        args += [rope[0], rope[1]]
    kern = functools.partial(_in_proj_kernel, tn, q_width // tn, k_width // tn,
                             norm is not None, rope is not None)
    return pl.pallas_call(
        kern,
        grid=(t_tokens // tm, n_total // tn),
        in_specs=in_specs,
        out_specs=pl.BlockSpec((tm, tn), lambda m, n: (m, n)),
        out_shape=jax.ShapeDtypeStruct((t_tokens, n_total), out_dtype),
        scratch_shapes=[pltpu.VMEM((tm, D_MODEL), bf16)],
        compiler_params=_params("arbitrary", "arbitrary"),
        name="in_proj",
    )(*args)


def _ctx_attn_kernel(n_q, n_kv, q_ref, k_ref, v_ref, o_ref):
    group = n_q // n_kv
    for h in range(n_q):
        kv = h // group
        q = q_ref[:, h * HEAD_DIM:(h + 1) * HEAD_DIM].astype(bf16)
        k = k_ref[:, kv * HEAD_DIM:(kv + 1) * HEAD_DIM].astype(bf16)
        v = v_ref[:, kv * HEAD_DIM:(kv + 1) * HEAD_DIM].astype(bf16)
        s = lax.dot_general(q, k, (((1,), (1,)), ((), ())), preferred_element_type=f32)
        m = jnp.max(s, axis=-1, keepdims=True)
        e = jnp.exp(s - m)
        l = jnp.sum(e, axis=-1, keepdims=True)
        o = jnp.dot(e.astype(bf16), v, preferred_element_type=f32) / l
        o_ref[:, h * HEAD_DIM:(h + 1) * HEAD_DIM] = o.astype(o_ref.dtype)


def ctx_attention(proj, seq, n_q, n_kv):
    t_tokens = proj.shape[0]
    qw, kw = n_q * HEAD_DIM, n_kv * HEAD_DIM
    assert qw % kw == 0
    k_blk = qw // kw
    return pl.pallas_call(
        functools.partial(_ctx_attn_kernel, n_q, n_kv),
        grid=(t_tokens // seq,),
        in_specs=[
            pl.BlockSpec((seq, qw), lambda b: (b, 0)),
            pl.BlockSpec((seq, kw), lambda b: (b, k_blk)),
            pl.BlockSpec((seq, kw), lambda b: (b, k_blk + 1)),
        ],
        out_specs=pl.BlockSpec((seq, qw), lambda b: (b, 0)),
        out_shape=jax.ShapeDtypeStruct((t_tokens, qw), bf16),
        compiler_params=_params("arbitrary"),
        name="ctx_attention",
    )(proj, proj, proj)


def _gqa_attn_kernel(group, q_ref, ck_ref, cv_ref, k_ref, v_ref, o_ref):
    ck = ck_ref[...].astype(bf16)
    cv = cv_ref[...].astype(bf16)
    k = k_ref[...]
    v = v_ref[...]
    dn = (((1,), (1,)), ((), ()))
    for gi in range(group):
        q = q_ref[:, gi * HEAD_DIM:(gi + 1) * HEAD_DIM]
        s_c = lax.dot_general(q, ck, dn, preferred_element_type=f32)
        s_l = lax.dot_general(q, k, dn, preferred_element_type=f32)
        m = jnp.maximum(jnp.max(s_c, axis=-1, keepdims=True),
                        jnp.max(s_l, axis=-1, keepdims=True))
        e_c = jnp.exp(s_c - m)
        e_l = jnp.exp(s_l - m)
        l = jnp.sum(e_c, axis=-1, keepdims=True) + jnp.sum(e_l, axis=-1, keepdims=True)
        o = (jnp.dot(e_c.astype(bf16), cv, preferred_element_type=f32)
             + jnp.dot(e_l.astype(bf16), v, preferred_element_type=f32)) / l
        o_ref[:, gi * HEAD_DIM:(gi + 1) * HEAD_DIM] = o.astype(o_ref.dtype)


def gqa_attention(proj, cache_k, cache_v, seq, past):
    t_tokens = proj.shape[0]
    n_b = t_tokens // seq
    group = GQA_Q_HEADS // GQA_KV_HEADS
    tq = 256
    nq = seq // tq
    return pl.pallas_call(
        functools.partial(_gqa_attn_kernel, group),
        grid=(n_b, GQA_KV_HEADS, nq),
        in_specs=[
            pl.BlockSpec((tq, group * HEAD_DIM), lambda b, h, i: (b * nq + i, h)),
            pl.BlockSpec((past, HEAD_DIM), lambda b, h, i: (b, h)),
            pl.BlockSpec((past, HEAD_DIM), lambda b, h, i: (b, h)),
            pl.BlockSpec((seq, HEAD_DIM), lambda b, h, i: (b, GQA_Q_HEADS + h)),
            pl.BlockSpec((seq, HEAD_DIM), lambda b, h, i: (b, GQA_Q_HEADS + GQA_KV_HEADS + h)),
        ],
        out_specs=pl.BlockSpec((tq, group * HEAD_DIM), lambda b, h, i: (b * nq + i, h)),
        out_shape=jax.ShapeDtypeStruct((t_tokens, GQA_Q_WIDTH), bf16),
        compiler_params=_params("arbitrary", "arbitrary", "arbitrary"),
        name="gqa_attention",
    )(proj, cache_k, cache_v, proj, proj)


def _na_bias_tiles(rel_bias):
    qc = jnp.arange(GRID_W)[:, None]
    kc = jnp.arange(GRID_W)[None, :]
    q_start = jnp.clip(qc - NA_KC // 2, 0, GRID_W - NA_KC)
    valid = (kc >= q_start) & (kc < q_start + NA_KC)
    dc = jnp.clip(kc - qc, -(NA_KC - 1), NA_KC - 1) + NA_KC - 1
    tiles = jnp.where(valid[None, None], rel_bias.astype(f32)[:, :, dc], NEG_INF)
    masked = jnp.full((NA_HEADS, 1, GRID_W, GRID_W), NEG_INF, f32)
    tiles = jnp.concatenate([tiles, masked], axis=1)
    zeros = jnp.zeros_like(tiles)
    left = jnp.concatenate([tiles, zeros], axis=-1)
    right = jnp.concatenate([zeros, tiles], axis=-1)
    return left, right


def _na_attn_kernel(n_rows, q_ref, k_ref, v_ref, ck_ref, cv_ref, bl_ref, br_ref, o_ref, *s_refs):
    rb = pl.program_id(2)
    start = jnp.clip(rb * NA_QROWS - NA_KR // 2, 0, n_rows - NA_WROWS)
    tok0 = pl.multiple_of(start * GRID_W, GRID_W)
    dn = (((1,), (1,)), ((), ()))
    for hh, s_ref in enumerate(s_refs):
        hs = slice(hh * HEAD_DIM, (hh + 1) * HEAD_DIM)
        kw = k_ref[pl.ds(tok0, NA_WROWS * GRID_W), hs]
        vw = v_ref[pl.ds(tok0, NA_WROWS * GRID_W), hs]
        q = q_ref[:, hs]
        s_ref[...] = lax.dot_general(q, kw, dn, preferred_element_type=f32)
        for i in range(NA_QROWS):
            qr = rb * NA_QROWS + i
            r0 = jnp.clip(qr - NA_KR // 2, 0, n_rows - NA_KR)
            for jp in range(NA_WROWS // 2):
                ka = start + 2 * jp
                kb = ka + 1
                ia = jnp.where((ka >= r0) & (ka < r0 + NA_KR), ka - qr + NA_KR - 1, NA_INVALID)
                ib = jnp.where((kb >= r0) & (kb < r0 + NA_KR), kb - qr + NA_KR - 1, NA_INVALID)
                rows = slice(i * GRID_W, (i + 1) * GRID_W)
                cols = slice(jp * 2 * GRID_W, (jp + 1) * 2 * GRID_W)
                s_ref[rows, cols] = s_ref[rows, cols] + bl_ref[hh, ia] + br_ref[hh, ib]
        s_l = s_ref[...]
        s_c = lax.dot_general(q, ck_ref[:, hs].astype(bf16), dn, preferred_element_type=f32)
        m = jnp.maximum(jnp.max(s_c, axis=-1, keepdims=True), jnp.max(s_l, axis=-1, keepdims=True))
        e_c = jnp.exp(s_c - m)
        e_l = jnp.exp(s_l - m)
        l = jnp.sum(e_c, axis=-1, keepdims=True) + jnp.sum(e_l, axis=-1, keepdims=True)
        o = (jnp.dot(e_c.astype(bf16), cv_ref[:, hs].astype(bf16), preferred_element_type=f32)
             + jnp.dot(e_l.astype(bf16), vw, preferred_element_type=f32)) / l
        o_ref[:, hs] = o.astype(o_ref.dtype)


def na_attention(proj, cache_k, cache_v, rel_bias, seq, past):
    t_tokens = proj.shape[0]
    n_b = t_tokens // seq
    n_rows = seq // GRID_W
    n_rb = n_rows // NA_QROWS
    tq = NA_QROWS * GRID_W
    hb = NA_HEADS_PER_STEP
    n_hg = NA_HEADS // hb
    wide = hb * HEAD_DIM
    bias_l, bias_r = _na_bias_tiles(rel_bias)
    bias_spec = pl.BlockSpec((hb, 2 * NA_KR, GRID_W, 2 * GRID_W), lambda b, h, r: (h, 0, 0, 0))
    return pl.pallas_call(
        functools.partial(_na_attn_kernel, n_rows),
        grid=(n_b, n_hg, n_rb),
        in_specs=[
            pl.BlockSpec((tq, wide), lambda b, h, r: (b * n_rb + r, h)),
            pl.BlockSpec((seq, wide), lambda b, h, r: (b, n_hg + h)),
            pl.BlockSpec((seq, wide), lambda b, h, r: (b, 2 * n_hg + h)),
            pl.BlockSpec((past, wide), lambda b, h, r: (b, h)),
            pl.BlockSpec((past, wide), lambda b, h, r: (b, h)),
            bias_spec,
            bias_spec,
        ],
        out_specs=pl.BlockSpec((tq, wide), lambda b, h, r: (b * n_rb + r, h)),
        out_shape=jax.ShapeDtypeStruct((t_tokens, NA_WIDTH), bf16),
        scratch_shapes=[pltpu.VMEM((tq, NA_WROWS * GRID_W), f32)] * hb,
        compiler_params=_params("arbitrary", "arbitrary", "arbitrary"),
        name="na_attention",
    )(proj, proj, proj, cache_k, cache_v, bias_l, bias_r)


def _dft_cols_kernel(x_ref, cs_ref, y_ref):
    for g in range(FNET_GROUPS):
        sl = slice(g * FNET_CH, (g + 1) * FNET_CH)
        y = jnp.dot(x_ref[:, sl].astype(bf16), cs_ref[...], preferred_element_type=f32)
        y_ref[0, :, sl] = y[:, :FNET_CH].astype(bf16)
        y_ref[1, :, sl] = y[:, FNET_CH:].astype(bf16)


def _dft_rows_kernel(d_ref, y_ref, w_ref, o_ref):
    f = jnp.dot(d_ref[...], y_ref[...], preferred_element_type=f32)
    for g in range(FNET_GROUPS):
        sl = slice(g * FNET_CH, (g + 1) * FNET_CH)
        o = jnp.dot(f[:, sl].astype(bf16), w_ref[g], preferred_element_type=f32)
        o_ref[:, sl] = o.astype(o_ref.dtype)


def _dft_tables(seq):
    c = np.arange(FNET_CH, dtype=np.int64)
    ang_c = (2.0 * np.pi / FNET_CH) * ((c[:, None] * c[None, :]) % FNET_CH)
    cs = np.concatenate([np.cos(ang_c), np.sin(ang_c)], axis=1)
    n = np.arange(seq, dtype=np.int64)
    ang_n = (2.0 * np.pi / seq) * ((n[:, None] * n[None, :]) % seq)
    norm = 1.0 / math.sqrt(seq * FNET_CH)
    d = np.concatenate([np.cos(ang_n) * norm, -np.sin(ang_n) * norm], axis=1)
    return jnp.asarray(cs, dtype=bf16), jnp.asarray(d, dtype=bf16)


def fourier_mix(proj, col_blk, fnet_w, seq):
    t_tokens = proj.shape[0]
    n_b = t_tokens // seq
    cs, d = _dft_tables(seq)
    tn = min(seq, 512)
    ns = seq // tn
    y = pl.pallas_call(
        _dft_cols_kernel,
        grid=(n_b, ns),
        in_specs=[
            pl.BlockSpec((tn, FNET_WIDTH), lambda b, i: (b * ns + i, col_blk)),
            pl.BlockSpec((FNET_CH, 2 * FNET_CH), lambda b, i: (0, 0)),
        ],
        out_specs=pl.BlockSpec((2, tn, FNET_WIDTH), lambda b, i: (0, i, b)),
        out_shape=jax.ShapeDtypeStruct((2, seq, n_b * FNET_WIDTH), bf16),
        compiler_params=_params("arbitrary", "arbitrary"),
        name="dft_cols",
    )(proj, cs)
    y2 = y.reshape(2 * seq, n_b * FNET_WIDTH)
    return pl.pallas_call(
        _dft_rows_kernel,
        grid=(n_b, ns),
        in_specs=[
            pl.BlockSpec((tn, 2 * seq), lambda b, i: (i, 0)),
            pl.BlockSpec((2 * seq, FNET_WIDTH), lambda b, i: (0, b)),
            pl.BlockSpec((FNET_GROUPS, FNET_CH, FNET_CH), lambda b, i: (0, 0, 0)),
        ],
        out_specs=pl.BlockSpec((tn, FNET_WIDTH), lambda b, i: (b * ns + i, 0)),
        out_shape=jax.ShapeDtypeStruct((t_tokens, FNET_WIDTH), bf16),
        compiler_params=_params("arbitrary", "arbitrary"),
        name="dft_rows",
    )(d, y2, fnet_w.astype(bf16))


def _pool_kernel(seq, x_ref, w_ref, sc_ref, o_ref):
    t = lax.broadcasted_iota(jnp.int32, (seq, 1), 0)

    def prev(a, k):
        return jnp.where(t >= k, pltpu.roll(a, k, 0), 0.0)

    def nxt(a, k):
        return jnp.where(t < seq - k, pltpu.roll(a, seq - k, 0), 0.0)

    for g, win in enumerate(POOL_WINDOWS):
        sl = slice(g * FNET_CH, (g + 1) * FNET_CH)
        x = x_ref[:, sl].astype(f32)
        half = win // 2
        back, fwd, k = x, x, 1
        while k < half:
            back = back + prev(back, k)
            fwd = fwd + nxt(fwd, k)
            k *= 2
        total = prev(back, 1) + fwd
        cnt = (jnp.minimum(t + half, seq) - jnp.maximum(t - half, 0)).astype(f32)
        pooled = (total / cnt - x).astype(bf16)
        o = jnp.dot(pooled, w_ref[g], preferred_element_type=f32) * sc_ref[:, sl]
        o_ref[:, sl] = o.astype(o_ref.dtype)


def pool_mix(proj, col_blk, pool_w, pool_scale, seq):
    t_tokens = proj.shape[0]
    return pl.pallas_call(
        functools.partial(_pool_kernel, seq),
        grid=(t_tokens // seq,),
        in_specs=[
            pl.BlockSpec((seq, POOL_WIDTH), lambda b: (b, col_blk)),
            pl.BlockSpec((len(POOL_WINDOWS), FNET_CH, FNET_CH), lambda b: (0, 0, 0)),
            pl.BlockSpec((1, POOL_WIDTH), lambda b: (0, 0)),
        ],
        out_specs=pl.BlockSpec((seq, POOL_WIDTH), lambda b: (b, 0)),
        out_shape=jax.ShapeDtypeStruct((t_tokens, POOL_WIDTH), bf16),
        compiler_params=_params("arbitrary"),
        name="pool_mix",
    )(proj, pool_w.astype(bf16), pool_scale.reshape(1, POOL_WIDTH))


def _out_proj_kernel(x_ref, gate_ref, a_ref, b_ref, wa_ref, wb_ref, o_ref):
    y = (jnp.dot(a_ref[...], wa_ref[...], preferred_element_type=f32)
         + jnp.dot(b_ref[...], wb_ref[...], preferred_element_type=f32))
    o_ref[...] = x_ref[...] + gate_ref[...] * y


def out_proj(x, mod, a, b, wa, wb, seq, base_row, per_seq):
    t_tokens = x.shape[0]
    tm = 512
    return pl.pallas_call(
        _out_proj_kernel,
        grid=(t_tokens // tm, 1),
        in_specs=[
            pl.BlockSpec((tm, D_MODEL), lambda m, n: (m, 0)),
            _mod_spec(2, tm, seq, base_row, per_seq),
            pl.BlockSpec((tm, a.shape[1]), lambda m, n: (m, 0)),
            pl.BlockSpec((tm, b.shape[1]), lambda m, n: (m, 0)),
            pl.BlockSpec(wa.shape, lambda m, n: (0, 0)),
            pl.BlockSpec(wb.shape, lambda m, n: (0, 0)),
        ],
        out_specs=pl.BlockSpec((tm, D_MODEL), lambda m, n: (m, 0)),
        out_shape=jax.ShapeDtypeStruct((t_tokens, D_MODEL), f32),
        compiler_params=_params("arbitrary", "arbitrary"),
        name="out_proj",
    )(x, mod, a, b, wa, wb)


def _ffn_kernel(tm, seq, nj, final, *refs):
    it = iter(refs)
    (x_ref, xp_ref, xn_ref, shift_ref, scale_ref, gate_ref, g_ref,
     wv_ref, wg_ref, cwv_ref, cwg_ref, cbv_ref, cbg_ref, wd_ref) = (next(it) for _ in range(14))
    gf_ref = next(it) if final else None
    o_ref, h_ref, acc_ref = next(it), next(it), next(it)
    u_refs = [(next(it), next(it)), (next(it), next(it))]
    i = pl.program_id(0)
    j = pl.program_id(1)
    rows = tm + 2 * HALO

    def prologue_and_first_up():
        g, sh, sc = g_ref[...], shift_ref[...], scale_ref[...]
        uv_ref, ug_ref = u_refs[0]

        def emit(rs, h):
            h_ref[rs, :] = h
            uv_ref[rs, :] = jnp.dot(h, wv_ref[...], preferred_element_type=f32)
            ug_ref[rs, :] = jnp.dot(h, wg_ref[...], preferred_element_type=f32)

        for r0 in range(0, tm - IN_ROWS, IN_ROWS):
            rs = slice(r0, r0 + IN_ROWS)
            emit(rs, _rms_modulate(x_ref[rs, :], g, sh, sc).astype(bf16))
        next_ok = jnp.where((((i + 1) * tm) & (seq - 1)) != 0, 1.0, 0.0)
        prev_ok = jnp.where(((i * tm) & (seq - 1)) != 0, 1.0, 0.0)
        h_last = _rms_modulate(x_ref[tm - IN_ROWS:tm, :], g, sh, sc)
        h_next = _rms_modulate(xn_ref[...], g, sh, sc) * next_ok
        h_prev = _rms_modulate(xp_ref[...], g, sh, sc) * prev_ok
        emit(slice(tm - IN_ROWS, rows), jnp.concatenate([h_last, h_next, h_prev], axis=0).astype(bf16))
        acc_ref[...] = jnp.zeros_like(acc_ref)

    def zero_row(a, r):
        r0 = (r // HALO) * HALO
        row8 = lax.broadcasted_iota(jnp.int32, (HALO, 1), 0)
        fixed = jnp.where(row8 == r - r0, 0.0, a[r0:r0 + HALO])
        return jnp.concatenate([a[:r0], fixed, a[r0 + HALO:]], axis=0)

    def window(u_ref, r0):
        lo, hi = r0 - HALO, r0 + FFN_ROWS + HALO
        if lo < 0:
            return jnp.concatenate([u_ref[rows + lo:rows, :], u_ref[0:hi, :]], axis=0)
        return u_ref[lo:hi, :]

    def conv(u_ref, r0, cw_ref, cb_ref):
        u = window(u_ref, r0)
        n = FFN_ROWS + 2 * HALO
        up = pltpu.roll(u, 1, 0)
        un = pltpu.roll(u, n - 1, 0)
        for edge in range(seq, tm, seq):
            if r0 <= edge < r0 + FFN_ROWS:
                up = zero_row(up, edge - r0 + HALO)
            if r0 <= edge - 1 < r0 + FFN_ROWS:
                un = zero_row(un, edge - 1 - r0 + HALO)
        c = up * cw_ref[0:1, :] + u * cw_ref[1:2, :] + un * cw_ref[2:3, :] + cb_ref[...]
        return c[HALO:HALO + FFN_ROWS, :]

    def gated(slot):
        uv_ref, ug_ref = u_refs[slot]
        chunks = []
        for r0 in range(0, tm, FFN_ROWS):
            val = conv(uv_ref, r0, cwv_ref, cbv_ref)
            gate = conv(ug_ref, r0, cwg_ref, cbg_ref)
            chunks.append((gate * jax.nn.sigmoid(gate) * val).astype(bf16))
        return chunks

    def zero_after(chunks):
        parts = []
        for c in chunks:
            bits = pltpu.bitcast(c, jnp.uint32)
            parts += [bits[k:k + HALO] for k in range(0, bits.shape[0], HALO)]
        while len(parts) > 1:
            parts = [a | b for a, b in zip(parts[0::2], parts[1::2])] + parts[len(parts) & ~1:]
        return (parts[0] >> 16) >> 16

    def after(a, zero, row0=0):
        tf = zero.shape[1]
        mid = pltpu.bitcast(pltpu.bitcast(a[row0:row0 + 2 * HALO, 0:tf], jnp.uint32) | zero, bf16)
        if tf < a.shape[1]:
            mid = jnp.concatenate([mid, a[row0:row0 + 2 * HALO, tf:]], axis=1)
        parts = ([a[:row0]] if row0 else []) + [mid, a[row0 + 2 * HALO:]]
        return jnp.concatenate(parts, axis=0)

    def up_project(slot, chunks=None):
        h = h_ref[...]
        wv, wg = wv_ref[...], wg_ref[...]
        uv_ref, ug_ref = u_refs[slot]
        if chunks is not None:
            k_mid = wv.shape[0] // 2
            wv = after(wv, zero_after(chunks[0:1]), k_mid)
            wg = after(wg, zero_after(chunks[2:3]), k_mid)
        uv_ref[...] = jnp.dot(h, wv, preferred_element_type=f32)
        if chunks is not None:
            h = after(h, zero_after(chunks[1:2]))
        ug_ref[...] = jnp.dot(h, wg, preferred_element_type=f32)

    def down_project(chunks, anchored):
        if anchored:
            chunks = [after(chunks[0], zero_after(chunks[3:]))] + chunks[1:]
        act = jnp.concatenate(chunks, axis=0)
        acc_ref[...] += jnp.dot(act, wd_ref[...], preferred_element_type=f32)

    @pl.when(j == 0)
    def _():
        prologue_and_first_up()

    for p in range(2):
        @pl.when((j >= 1) & (j < nj) & (j % 2 == p))
        def _(p=p):
            chunks = gated(1 - p)
            up_project(p, chunks)
            down_project(chunks, True)

    @pl.when(j == nj)
    def _():
        down_project(gated((nj - 1) % 2), False)
        y = x_ref[...] + gate_ref[...] * acc_ref[...]
        if final:
            ms = jnp.mean(y * y, axis=-1, keepdims=True)
            y = y * lax.rsqrt(ms + EPS) * gf_ref[...]
        o_ref[...] = y


def conv_ffn(x, mod, g, w_up, conv_w, conv_b, w_down, seq, base_row, per_seq, final_g=None):
    t_tokens = x.shape[0]
    tm, tf = 512, W_TILE
    assert seq & (seq - 1) == 0
    nj = D_FF // tf
    hb = tm // HALO
    n_hblk = t_tokens // HALO
    conv_b2 = conv_b.reshape(1, 2 * D_FF)

    def blk(j):
        return jnp.clip(j, 0, nj - 1)

    in_specs = [
        pl.BlockSpec((tm, D_MODEL), lambda i, j: (i, 0)),
        pl.BlockSpec((HALO, D_MODEL), lambda i, j: (jnp.maximum(i * hb - 1, 0), 0)),
        pl.BlockSpec((HALO, D_MODEL), lambda i, j: (jnp.minimum((i + 1) * hb, n_hblk - 1), 0)),
        _mod_spec(3, tm, seq, base_row, per_seq),
        _mod_spec(4, tm, seq, base_row, per_seq),
        _mod_spec(5, tm, seq, base_row, per_seq),
        pl.BlockSpec((1, D_MODEL), lambda i, j: (0, 0)),
        pl.BlockSpec((D_MODEL, tf), lambda i, j: (0, blk(j))),
        pl.BlockSpec((D_MODEL, tf), lambda i, j: (0, nj + blk(j))),
        pl.BlockSpec((3, tf), lambda i, j: (0, blk(j - 1))),
        pl.BlockSpec((3, tf), lambda i, j: (0, nj + blk(j - 1))),
        pl.BlockSpec((1, tf), lambda i, j: (0, blk(j - 1))),
        pl.BlockSpec((1, tf), lambda i, j: (0, nj + blk(j - 1))),
        pl.BlockSpec((tf, D_MODEL), lambda i, j: (blk(j - 1), 0)),
    ]
    args = [x, x, x, mod, mod, mod, g.reshape(1, D_MODEL), w_up, w_up,
            conv_w, conv_w, conv_b2, conv_b2, w_down]
    if final_g is not None:
        in_specs.append(pl.BlockSpec((1, D_MODEL), lambda i, j: (0, 0)))
        args.append(final_g.reshape(1, D_MODEL))
    return pl.pallas_call(
        functools.partial(_ffn_kernel, tm, seq, nj, final_g is not None),
        grid=(t_tokens // tm, nj + 1),
        in_specs=in_specs,
        out_specs=pl.BlockSpec((tm, D_MODEL), lambda i, j: (i, 0)),
        out_shape=jax.ShapeDtypeStruct((t_tokens, D_MODEL), f32),
        scratch_shapes=[pltpu.VMEM((tm + 2 * HALO, D_MODEL), bf16),
                        pltpu.VMEM((tm, D_MODEL), f32)]
                       + [pltpu.VMEM((tm + 2 * HALO, tf), f32)] * 4,
        compiler_params=_params("arbitrary", "arbitrary"),
        name="conv_ffn",
    )(*args)


def _rope_tables(seq):
    t = np.arange(seq)
    half = HEAD_DIM // 2
    inv_freq = np.power(ROPE_BASE, -np.arange(0, half, 2, dtype=np.float64) / half)
    ang_r = (t // GRID_W)[:, None] * inv_freq[None, :]
    ang_c = (t % GRID_W)[:, None] * inv_freq[None, :]
    cos = np.concatenate([np.cos(ang_r)] * 2 + [np.cos(ang_c)] * 2, axis=-1)
    sin = np.concatenate([-np.sin(ang_r), np.sin(ang_r), -np.sin(ang_c), np.sin(ang_c)], axis=-1)
    return jnp.asarray(cos, dtype=f32), jnp.asarray(sin, dtype=f32)


def kernel(x_prompt, x_sample, c, cache_na_k, cache_na_v, cache_gqa_k, cache_gqa_v, c_ctx, norm1_g, norm2_g, ada_w, ada_b, ev_w_in, ev_na_bias, ev_fnet_w, ev_w_out, od_w_in, od_pool_w, od_pool_scale, od_q_norm_g, od_k_norm_g, od_w_out, ffn_w_up, ffn_conv_w, ffn_conv_b, ffn_w_down, final_norm_g):
    n_ctx, seq_p, _ = x_prompt.shape
    n_lat, seq_s, _ = x_sample.shape
    past = cache_na_k.shape[2]
    assert n_lat + 1 <= MOD_ROWS
    xp = x_prompt.reshape(n_ctx * seq_p, D_MODEL)
    xs = x_sample.reshape(n_lat * seq_s, D_MODEL)

    cond = jnp.zeros((MOD_ROWS, D_MODEL), f32).at[0].set(c_ctx).at[1:1 + n_lat].set(c)
    mods = ada_modulation(cond, ada_w, ada_b).reshape(DEPTH, MOD_ROWS * 6, 1, D_MODEL)
    rope = _rope_tables(seq_s)

    ctx = dict(seq=seq_p, base_row=0, per_seq=0)
    lat = dict(seq=seq_s, base_row=1, per_seq=1)
    new_k, new_v = {}, {}
    for i in range(DEPTH):
        j = i // 2
        mod = mods[i]
        w_up = ffn_w_up[i].astype(bf16)
        w_down = ffn_w_down[i].astype(bf16)
        if i % 2 == 0:
            w_in = ev_w_in[j].astype(bf16)
            w_attn = ev_w_out[j, :NA_WIDTH].astype(bf16)
            w_mix = ev_w_out[j, NA_WIDTH:].astype(bf16)
            mix_blk = 3 * NA_WIDTH // FNET_WIDTH
            proj_p = in_proj(xp, mod, norm1_g[i], w_in, f32, NA_WIDTH, **ctx)
            new_k[i] = proj_p[:, NA_WIDTH:2 * NA_WIDTH]
            new_v[i] = proj_p[:, 2 * NA_WIDTH:3 * NA_WIDTH]
            attn_p = ctx_attention(proj_p, seq_p, NA_HEADS, NA_HEADS)
            mix_p = fourier_mix(proj_p, mix_blk, ev_fnet_w[j], seq_p)
            proj_s = in_proj(xs, mod, norm1_g[i], w_in, bf16, NA_WIDTH, **lat)
            attn_s = na_attention(proj_s,
                                  cache_na_k[:, j].reshape(n_lat * past, NA_WIDTH),
                                  cache_na_v[:, j].reshape(n_lat * past, NA_WIDTH),
                                  ev_na_bias[j], seq_s, past)
            mix_s = fourier_mix(proj_s, mix_blk, ev_fnet_w[j], seq_s)
        else:
            w_in = jnp.concatenate([od_w_in[j, :, POOL_WIDTH:], od_w_in[j, :, :POOL_WIDTH]],
                                   axis=1).astype(bf16)
            w_attn = od_w_out[j, POOL_WIDTH:].astype(bf16)
            w_mix = od_w_out[j, :POOL_WIDTH].astype(bf16)
            norm = (od_q_norm_g[j], od_k_norm_g[j], GQA_KV_WIDTH)
            k_lo = GQA_Q_WIDTH
            v_lo = GQA_Q_WIDTH + GQA_KV_WIDTH
            mix_blk = (GQA_Q_WIDTH + 2 * GQA_KV_WIDTH) // POOL_WIDTH
            proj_p = in_proj(xp, mod, norm1_g[i], w_in, f32, GQA_Q_WIDTH, norm=norm, **ctx)
            new_k[i] = proj_p[:, k_lo:k_lo + GQA_KV_WIDTH]
            new_v[i] = proj_p[:, v_lo:v_lo + GQA_KV_WIDTH]
            attn_p = ctx_attention(proj_p, seq_p, GQA_Q_HEADS, GQA_KV_HEADS)
            mix_p = pool_mix(proj_p, mix_blk, od_pool_w[j], od_pool_scale[j], seq_p)
            proj_s = in_proj(xs, mod, norm1_g[i], w_in, bf16, GQA_Q_WIDTH, norm=norm, rope=rope, **lat)
            attn_s = gqa_attention(proj_s,
                                   cache_gqa_k[:, j].reshape(n_lat * past, GQA_KV_WIDTH),
                                   cache_gqa_v[:, j].reshape(n_lat * past, GQA_KV_WIDTH),
                                   seq_s, past)
            mix_s = pool_mix(proj_s, mix_blk, od_pool_w[j], od_pool_scale[j], seq_s)
        xp = out_proj(xp, mod, attn_p, mix_p, w_attn, w_mix, **ctx)
        xs = out_proj(xs, mod, attn_s, mix_s, w_attn, w_mix, **lat)
        final_g = final_norm_g if i == DEPTH - 1 else None
        xp = conv_ffn(xp, mod, norm2_g[i], w_up, ffn_conv_w[i], ffn_conv_b[i], w_down,
                      final_g=final_g, **ctx)
        xs = conv_ffn(xs, mod, norm2_g[i], w_up, ffn_conv_w[i], ffn_conv_b[i], w_down,
                      final_g=final_g, **lat)

    y_prompt = xp.reshape(n_ctx, seq_p, D_MODEL)
    y_sample = xs.reshape(n_lat, seq_s, D_MODEL)
    even = [i for i in range(DEPTH) if i % 2 == 0]
    odd = [i for i in range(DEPTH) if i % 2 == 1]

    def stack(parts, layers, heads):
        return jnp.stack([parts[i].reshape(n_ctx, seq_p, heads, HEAD_DIM) for i in layers], axis=1)

    return (y_prompt, y_sample,
            stack(new_k, even, NA_HEADS), stack(new_v, even, NA_HEADS),
            stack(new_k, odd, GQA_KV_HEADS), stack(new_v, odd, GQA_KV_HEADS))
```

```python
import functools
import math

import jax
import jax.numpy as jnp
import numpy as np
from jax import lax
from jax.experimental import pallas as pl
from jax.experimental.pallas import tpu as pltpu

f32 = jnp.float32
bf16 = jnp.bfloat16

D_MODEL = 2048
DEPTH = 2
GRID_W = 64
HEAD_DIM = 128
NA_HEADS = 12
NA_KR = 8
NA_KC = 16
FNET_GROUPS = 4
FNET_CH = 128
POOL_WINDOWS = (2, 4, 8, 16)
GQA_Q_HEADS = 12
GQA_KV_HEADS = 4
D_FF = 5632
ROPE_BASE = 10000.0
EPS = 1e-6
NEG_INF = -1e30
NA_WIDTH = NA_HEADS * HEAD_DIM
FNET_WIDTH = FNET_GROUPS * FNET_CH
POOL_WIDTH = len(POOL_WINDOWS) * FNET_CH
GQA_Q_WIDTH = GQA_Q_HEADS * HEAD_DIM
GQA_KV_WIDTH = GQA_KV_HEADS * HEAD_DIM
LOG2_E = math.log2(math.e)
ATTN_SCALE = HEAD_DIM ** -0.5 * LOG2_E

MOD_ROWS = 16
VMEM_LIMIT = 56 * 1024 * 1024
HALO = 8
NA_QROWS = 4
NA_WROWS = 12
NA_INVALID = 2 * NA_KR - 1
NA_HEADS_PER_STEP = 6
FFN_ROWS = 128
FFN_SPLIT = 1
W_TILE = 512
IN_TOKENS = 1024
IN_ROWS = 256


def _params(*sem, flags=None):
    return pltpu.CompilerParams(dimension_semantics=sem, vmem_limit_bytes=VMEM_LIMIT, flags=flags)


def _rms_modulate(x, g, shift, scale):
    ms = jnp.mean(x * x, axis=-1, keepdims=True)
    return (x * lax.rsqrt(ms + EPS)) * (g * (1.0 + scale)) + shift


def _mod_spec(k, tm, seq, base_row, per_seq):
    def index(m, n):
        return ((base_row + per_seq * ((m * tm) // seq)) * 6 + k, 0, 0)
    return pl.BlockSpec((None, 1, D_MODEL), index)


def _ada_kernel(c_ref, w_ref, b_ref, o_ref):
    c = c_ref[...]
    s = (c * jax.nn.sigmoid(c)).astype(bf16)
    o_ref[...] = jnp.dot(s, w_ref[...].astype(bf16), preferred_element_type=f32) + b_ref[...]


def ada_modulation(cond, ada_w, ada_b):
    tn = 1024
    n_out = 6 * D_MODEL
    return pl.pallas_call(
        _ada_kernel,
        grid=(DEPTH, n_out // tn),
        in_specs=[
            pl.BlockSpec((MOD_ROWS, D_MODEL), lambda i, n: (0, 0)),
            pl.BlockSpec((None, D_MODEL, tn), lambda i, n: (i, 0, n)),
            pl.BlockSpec((None, 1, tn), lambda i, n: (i, 0, n)),
        ],
        out_specs=pl.BlockSpec((None, MOD_ROWS, tn), lambda i, n: (i, 0, n)),
        out_shape=jax.ShapeDtypeStruct((DEPTH, MOD_ROWS, n_out), f32),
        compiler_params=_params("arbitrary", "arbitrary"),
        name="ada_modulation",
    )(cond, ada_w, ada_b.reshape(DEPTH, 1, n_out))


def _rope(a, cos, sin):
    lane = lax.broadcasted_iota(jnp.int32, a.shape, 1)
    quarter = HEAD_DIM // 4
    partner = jnp.where((lane & (2 * quarter - 1)) < quarter,
                        pltpu.roll(a, HEAD_DIM - quarter, 1),
                        pltpu.roll(a, quarter, 1))
    return a * cos + partner * sin


def _in_proj_kernel(tn, q_blocks, k_blocks, has_norm, has_rope, *refs):
    it = iter(refs)
    x_ref, shift_ref, scale_ref, g_ref, w_ref = (next(it) for _ in range(5))
    qg_ref = kg_ref = cos_ref = sin_ref = None
    if has_norm:
        qg_ref, kg_ref, avg_ref = next(it), next(it), next(it)
    if has_rope:
        cos_ref, sin_ref = next(it), next(it)
    o_ref, h_ref = next(it), next(it)
    n = pl.program_id(1)
    tm = x_ref.shape[0]
    everything = slice(0, tm)

    def project(rs, h=None):
        h = h_ref[rs, :] if h is None else h
        return jnp.dot(h, w_ref[...], preferred_element_type=f32)

    def store_scaled(rs, acc, scale):
        o_ref[rs, :] = (acc if scale is None else acc * scale).astype(o_ref.dtype)

    def store_normed(rs, acc, gn):
        ms = jnp.dot((acc * acc).astype(bf16), avg_ref[...], preferred_element_type=f32)
        inv = lax.rsqrt(ms + EPS)
        for hh in range(tn // HEAD_DIM):
            sl = slice(hh * HEAD_DIM, (hh + 1) * HEAD_DIM)
            a = acc[:, sl] * inv[:, sl] * gn
            if has_rope:
                a = _rope(a, cos_ref[rs, :], sin_ref[rs, :])
            o_ref[rs, sl] = a.astype(o_ref.dtype)

    @pl.when(n == 0)
    def _():
        g, sh, sc = g_ref[...], shift_ref[...], scale_ref[...]
        for r0 in range(0, tm, IN_ROWS):
            rs = slice(r0, r0 + IN_ROWS)
            h = _rms_modulate(x_ref[rs, :], g, sh, sc).astype(bf16)
            h_ref[rs, :] = h
            if has_norm:
                store_normed(rs, project(rs, h), qg_ref[...] * ATTN_SCALE)
            else:
                store_scaled(rs, project(rs, h), ATTN_SCALE)

    is_q = n < q_blocks
    q_scale = jnp.where(is_q, ATTN_SCALE, 1.0)
    if not has_norm:
        @pl.when(n > 0)
        def _():
            store_scaled(everything, project(everything), q_scale)
        return

    is_qk = n < q_blocks + k_blocks

    @pl.when((n > 0) & is_qk)
    def _():
        gn = jnp.where(is_q, qg_ref[...], kg_ref[...]) * q_scale
        store_normed(everything, project(everything), gn)

    @pl.when(jnp.logical_not(is_qk))
    def _():
        store_scaled(everything, project(everything), None)


def in_proj(x, mod, g, w, out_dtype, q_width, seq, base_row, per_seq, norm=None, rope=None):
    t_tokens = x.shape[0]
    tm, tn = IN_TOKENS, W_TILE
    n_total = w.shape[1]
    in_specs = [
        pl.BlockSpec((tm, D_MODEL), lambda m, n: (m, 0)),
        _mod_spec(0, tm, seq, base_row, per_seq),
        _mod_spec(1, tm, seq, base_row, per_seq),
        pl.BlockSpec((1, D_MODEL), lambda m, n: (0, 0)),
        pl.BlockSpec((D_MODEL, tn), lambda m, n: (0, n)),
    ]
    args = [x, mod, mod, g.reshape(1, D_MODEL), w]
    k_width = 0
    if norm is not None:
        k_width = norm[2]
        in_specs += [pl.BlockSpec((1, HEAD_DIM), lambda m, n: (0, 0))] * 2
        in_specs += [pl.BlockSpec((tn, tn), lambda m, n: (0, 0))]
        head_of = np.arange(tn) // HEAD_DIM
        avg = (head_of[:, None] == head_of[None, :]) / HEAD_DIM
        args += [norm[0].reshape(1, HEAD_DIM), norm[1].reshape(1, HEAD_DIM), jnp.asarray(avg, dtype=bf16)]
    if rope is not None:
        spb = seq // tm
        in_specs += [pl.BlockSpec((tm, HEAD_DIM), lambda m, n: (m % spb, 0))] * 2
        args += [rope[0], rope[1]]
    kern = functools.partial(_in_proj_kernel, tn, q_width // tn, k_width // tn,
                             norm is not None, rope is not None)
    return pl.pallas_call(
        kern,
        grid=(t_tokens // tm, n_total // tn),
        in_specs=in_specs,
        out_specs=pl.BlockSpec((tm, tn), lambda m, n: (m, n)),
        out_shape=jax.ShapeDtypeStruct((t_tokens, n_total), out_dtype),
        scratch_shapes=[pltpu.VMEM((tm, D_MODEL), bf16)],
        compiler_params=_params("arbitrary", "arbitrary"),
        name="in_proj",
    )(*args)


def _with_ones(v):
    return jnp.concatenate([v, jnp.ones_like(v)], axis=1)


def _softmax_pv(parts):
    m = functools.reduce(jnp.maximum, [jnp.max(s, axis=-1, keepdims=True) for s, _ in parts])
    acc = sum(jnp.dot(jnp.exp2(s - m).astype(bf16), v1, preferred_element_type=f32) for s, v1 in parts)
    return acc[:, :HEAD_DIM] / acc[:, HEAD_DIM:]


def _ctx_attn_kernel(n_q, n_kv, q_ref, k_ref, v_ref, o_ref):
    group = n_q // n_kv
    values = [_with_ones(v_ref[:, kv * HEAD_DIM:(kv + 1) * HEAD_DIM].astype(bf16)) for kv in range(n_kv)]
    for h in range(n_q):
        kv = h // group
        q = q_ref[:, h * HEAD_DIM:(h + 1) * HEAD_DIM].astype(bf16)
        k = k_ref[:, kv * HEAD_DIM:(kv + 1) * HEAD_DIM].astype(bf16)
        s = lax.dot_general(q, k, (((1,), (1,)), ((), ())), preferred_element_type=f32)
        o_ref[:, h * HEAD_DIM:(h + 1) * HEAD_DIM] = _softmax_pv([(s, values[kv])]).astype(o_ref.dtype)


def ctx_attention(proj, seq, n_q, n_kv):
    t_tokens = proj.shape[0]
    qw, kw = n_q * HEAD_DIM, n_kv * HEAD_DIM
    assert qw % kw == 0
    k_blk = qw // kw
    return pl.pallas_call(
        functools.partial(_ctx_attn_kernel, n_q, n_kv),
        grid=(t_tokens // seq,),
        in_specs=[
            pl.BlockSpec((seq, qw), lambda b: (b, 0)),
            pl.BlockSpec((seq, kw), lambda b: (b, k_blk)),
            pl.BlockSpec((seq, kw), lambda b: (b, k_blk + 1)),
        ],
        out_specs=pl.BlockSpec((seq, qw), lambda b: (b, 0)),
        out_shape=jax.ShapeDtypeStruct((t_tokens, qw), bf16),
        compiler_params=_params("arbitrary"),
        name="ctx_attention",
    )(proj, proj, proj)


def _gqa_attn_kernel(group, q_ref, ck_ref, cv_ref, k_ref, v_ref, o_ref):
    ck = ck_ref[...].astype(bf16)
    cv = _with_ones(cv_ref[...].astype(bf16))
    k = k_ref[...]
    v = _with_ones(v_ref[...])
    dn = (((1,), (1,)), ((), ()))
    scores = []
    for gi in range(group):
        q = q_ref[:, gi * HEAD_DIM:(gi + 1) * HEAD_DIM]
        scores.append((lax.dot_general(q, ck, dn, preferred_element_type=f32),
                       lax.dot_general(q, k, dn, preferred_element_type=f32)))
    for gi, (s_c, s_l) in enumerate(scores):
        o = _softmax_pv([(s_c, cv), (s_l, v)])
        o_ref[:, gi * HEAD_DIM:(gi + 1) * HEAD_DIM] = o.astype(o_ref.dtype)


def gqa_attention(proj, cache_k, cache_v, seq, past):
    t_tokens = proj.shape[0]
    n_b = t_tokens // seq
    group = GQA_Q_HEADS // GQA_KV_HEADS
    tq = 256
    nq = seq // tq
    return pl.pallas_call(
        functools.partial(_gqa_attn_kernel, group),
        grid=(n_b, GQA_KV_HEADS, nq),
        in_specs=[
            pl.BlockSpec((tq, group * HEAD_DIM), lambda b, h, i: (b * nq + i, h)),
            pl.BlockSpec((past, HEAD_DIM), lambda b, h, i: (b, h)),
            pl.BlockSpec((past, HEAD_DIM), lambda b, h, i: (b, h)),
            pl.BlockSpec((seq, HEAD_DIM), lambda b, h, i: (b, GQA_Q_HEADS + h)),
            pl.BlockSpec((seq, HEAD_DIM), lambda b, h, i: (b, GQA_Q_HEADS + GQA_KV_HEADS + h)),
        ],
        out_specs=pl.BlockSpec((tq, group * HEAD_DIM), lambda b, h, i: (b * nq + i, h)),
        out_shape=jax.ShapeDtypeStruct((t_tokens, GQA_Q_WIDTH), bf16),
        compiler_params=_params("arbitrary", "arbitrary", "arbitrary"),
        name="gqa_attention",
    )(proj, cache_k, cache_v, proj, proj)


def _na_bias_tiles(rel_bias):
    qc = jnp.arange(GRID_W)[:, None]
    kc = jnp.arange(GRID_W)[None, :]
    q_start = jnp.clip(qc - NA_KC // 2, 0, GRID_W - NA_KC)
    valid = (kc >= q_start) & (kc < q_start + NA_KC)
    dc = jnp.clip(kc - qc, -(NA_KC - 1), NA_KC - 1) + NA_KC - 1
    tiles = jnp.where(valid[None, None], rel_bias.astype(f32)[:, :, dc] * LOG2_E, NEG_INF)
    masked = jnp.full((NA_HEADS, 1, GRID_W, GRID_W), NEG_INF, f32)
    tiles = jnp.concatenate([tiles, masked], axis=1)
    zeros = jnp.zeros_like(tiles)
    left = jnp.concatenate([tiles, zeros], axis=-1)
    right = jnp.concatenate([zeros, tiles], axis=-1)
    return left, right


def _na_attn_kernel(n_rows, q_ref, k_ref, v_ref, ck_ref, cv_ref, bl_ref, br_ref, o_ref, *s_refs):
    rb = pl.program_id(2)
    start = jnp.clip(rb * NA_QROWS - NA_KR // 2, 0, n_rows - NA_WROWS)
    tok0 = pl.multiple_of(start * GRID_W, GRID_W)
    dn = (((1,), (1,)), ((), ()))
    for hh, s_ref in enumerate(s_refs):
        hs = slice(hh * HEAD_DIM, (hh + 1) * HEAD_DIM)
        kw = k_ref[pl.ds(tok0, NA_WROWS * GRID_W), hs]
        vw = v_ref[pl.ds(tok0, NA_WROWS * GRID_W), hs]
        q = q_ref[:, hs]
        s_ref[...] = lax.dot_general(q, kw, dn, preferred_element_type=f32)
        for i in range(NA_QROWS):
            qr = rb * NA_QROWS + i
            r0 = jnp.clip(qr - NA_KR // 2, 0, n_rows - NA_KR)
            for jp in range(NA_WROWS // 2):
                ka = start + 2 * jp
                kb = ka + 1
                ia = jnp.where((ka >= r0) & (ka < r0 + NA_KR), ka - qr + NA_KR - 1, NA_INVALID)
                ib = jnp.where((kb >= r0) & (kb < r0 + NA_KR), kb - qr + NA_KR - 1, NA_INVALID)
                rows = slice(i * GRID_W, (i + 1) * GRID_W)
                cols = slice(jp * 2 * GRID_W, (jp + 1) * 2 * GRID_W)
                s_ref[rows, cols] = s_ref[rows, cols] + bl_ref[hh, ia] + br_ref[hh, ib]
        s_l = s_ref[...]
        s_c = lax.dot_general(q, ck_ref[:, hs].astype(bf16), dn, preferred_element_type=f32)
        o = _softmax_pv([(s_c, _with_ones(cv_ref[:, hs].astype(bf16))), (s_l, _with_ones(vw))])
        o_ref[:, hs] = o.astype(o_ref.dtype)


def na_attention(proj, cache_k, cache_v, rel_bias, seq, past):
    t_tokens = proj.shape[0]
    n_b = t_tokens // seq
    n_rows = seq // GRID_W
    n_rb = n_rows // NA_QROWS
    tq = NA_QROWS * GRID_W
    hb = NA_HEADS_PER_STEP
    n_hg = NA_HEADS // hb
    wide = hb * HEAD_DIM
    bias_l, bias_r = _na_bias_tiles(rel_bias)
    bias_spec = pl.BlockSpec((hb, 2 * NA_KR, GRID_W, 2 * GRID_W), lambda b, h, r: (h, 0, 0, 0))
    return pl.pallas_call(
        functools.partial(_na_attn_kernel, n_rows),
        grid=(n_b, n_hg, n_rb),
        in_specs=[
            pl.BlockSpec((tq, wide), lambda b, h, r: (b * n_rb + r, h)),
            pl.BlockSpec((seq, wide), lambda b, h, r: (b, n_hg + h)),
            pl.BlockSpec((seq, wide), lambda b, h, r: (b, 2 * n_hg + h)),
            pl.BlockSpec((past, wide), lambda b, h, r: (b, h)),
            pl.BlockSpec((past, wide), lambda b, h, r: (b, h)),
            bias_spec,
            bias_spec,
        ],
        out_specs=pl.BlockSpec((tq, wide), lambda b, h, r: (b * n_rb + r, h)),
        out_shape=jax.ShapeDtypeStruct((t_tokens, NA_WIDTH), bf16),
        scratch_shapes=[pltpu.VMEM((tq, NA_WROWS * GRID_W), f32)] * hb,
        compiler_params=_params("arbitrary", "arbitrary", "arbitrary"),
        name="na_attention",
    )(proj, proj, proj, cache_k, cache_v, bias_l, bias_r)


def _dft_cols_kernel(x_ref, cs_ref, y_ref):
    for g in range(FNET_GROUPS):
        sl = slice(g * FNET_CH, (g + 1) * FNET_CH)
        y = jnp.dot(x_ref[:, sl].astype(bf16), cs_ref[...], preferred_element_type=f32)
        y_ref[0, :, sl] = y[:, :FNET_CH].astype(bf16)
        y_ref[1, :, sl] = y[:, FNET_CH:].astype(bf16)


def _dft_rows_kernel(d_ref, y_ref, w_ref, o_ref):
    f = jnp.dot(d_ref[...], y_ref[...], preferred_element_type=f32)
    for g in range(FNET_GROUPS):
        sl = slice(g * FNET_CH, (g + 1) * FNET_CH)
        o = jnp.dot(f[:, sl].astype(bf16), w_ref[g], preferred_element_type=f32)
        o_ref[:, sl] = o.astype(o_ref.dtype)


def _dft_tables(seq):
    c = np.arange(FNET_CH, dtype=np.int64)
    ang_c = (2.0 * np.pi / FNET_CH) * ((c[:, None] * c[None, :]) % FNET_CH)
    cs = np.concatenate([np.cos(ang_c), np.sin(ang_c)], axis=1)
    n = np.arange(seq, dtype=np.int64)
    ang_n = (2.0 * np.pi / seq) * ((n[:, None] * n[None, :]) % seq)
    norm = 1.0 / math.sqrt(seq * FNET_CH)
    d = np.concatenate([np.cos(ang_n) * norm, -np.sin(ang_n) * norm], axis=1)
    return jnp.asarray(cs, dtype=bf16), jnp.asarray(d, dtype=bf16)


def fourier_mix(proj, col_blk, fnet_w, seq):
    t_tokens = proj.shape[0]
    n_b = t_tokens // seq
    cs, d = _dft_tables(seq)
    tn = min(seq, 512)
    ns = seq // tn
    y = pl.pallas_call(
        _dft_cols_kernel,
        grid=(n_b, ns),
        in_specs=[
            pl.BlockSpec((tn, FNET_WIDTH), lambda b, i: (b * ns + i, col_blk)),
            pl.BlockSpec((FNET_CH, 2 * FNET_CH), lambda b, i: (0, 0)),
        ],
        out_specs=pl.BlockSpec((2, tn, FNET_WIDTH), lambda b, i: (0, i, b)),
        out_shape=jax.ShapeDtypeStruct((2, seq, n_b * FNET_WIDTH), bf16),
        compiler_params=_params("arbitrary", "arbitrary"),
        name="dft_cols",
    )(proj, cs)
    y2 = y.reshape(2 * seq, n_b * FNET_WIDTH)
    return pl.pallas_call(
        _dft_rows_kernel,
        grid=(n_b, ns),
        in_specs=[
            pl.BlockSpec((tn, 2 * seq), lambda b, i: (i, 0)),
            pl.BlockSpec((2 * seq, FNET_WIDTH), lambda b, i: (0, b)),
            pl.BlockSpec((FNET_GROUPS, FNET_CH, FNET_CH), lambda b, i: (0, 0, 0)),
        ],
        out_specs=pl.BlockSpec((tn, FNET_WIDTH), lambda b, i: (b * ns + i, 0)),
        out_shape=jax.ShapeDtypeStruct((t_tokens, FNET_WIDTH), bf16),
        compiler_params=_params("arbitrary", "arbitrary"),
        name="dft_rows",
    )(d, y2, fnet_w.astype(bf16))


def _pool_kernel(seq, x_ref, w_ref, sc_ref, o_ref):
    t = lax.broadcasted_iota(jnp.int32, (seq, 1), 0)

    def prev(a, k):
        return jnp.where(t >= k, pltpu.roll(a, k, 0), 0.0)

    def nxt(a, k):
        return jnp.where(t < seq - k, pltpu.roll(a, seq - k, 0), 0.0)

    for g, win in enumerate(POOL_WINDOWS):
        sl = slice(g * FNET_CH, (g + 1) * FNET_CH)
        x = x_ref[:, sl].astype(f32)
        half = win // 2
        back, fwd, k = x, x, 1
        while k < half:
            back = back + prev(back, k)
            fwd = fwd + nxt(fwd, k)
            k *= 2
        total = prev(back, 1) + fwd
        cnt = (jnp.minimum(t + half, seq) - jnp.maximum(t - half, 0)).astype(f32)
        pooled = (total / cnt - x).astype(bf16)
        o = jnp.dot(pooled, w_ref[g], preferred_element_type=f32) * sc_ref[:, sl]
        o_ref[:, sl] = o.astype(o_ref.dtype)


def pool_mix(proj, col_blk, pool_w, pool_scale, seq):
    t_tokens = proj.shape[0]
    return pl.pallas_call(
        functools.partial(_pool_kernel, seq),
        grid=(t_tokens // seq,),
        in_specs=[
            pl.BlockSpec((seq, POOL_WIDTH), lambda b: (b, col_blk)),
            pl.BlockSpec((len(POOL_WINDOWS), FNET_CH, FNET_CH), lambda b: (0, 0, 0)),
            pl.BlockSpec((1, POOL_WIDTH), lambda b: (0, 0)),
        ],
        out_specs=pl.BlockSpec((seq, POOL_WIDTH), lambda b: (b, 0)),
        out_shape=jax.ShapeDtypeStruct((t_tokens, POOL_WIDTH), bf16),
        compiler_params=_params("arbitrary"),
        name="pool_mix",
    )(proj, pool_w.astype(bf16), pool_scale.reshape(1, POOL_WIDTH))


def _out_proj_kernel(x_ref, gate_ref, a_ref, b_ref, wa_ref, wb_ref, o_ref):
    y = (jnp.dot(a_ref[...], wa_ref[...], preferred_element_type=f32)
         + jnp.dot(b_ref[...], wb_ref[...], preferred_element_type=f32))
    o_ref[...] = x_ref[...] + gate_ref[...] * y


def out_proj(x, mod, a, b, wa, wb, seq, base_row, per_seq):
    t_tokens = x.shape[0]
    tm = 512
    return pl.pallas_call(
        _out_proj_kernel,
        grid=(t_tokens // tm, 1),
        in_specs=[
            pl.BlockSpec((tm, D_MODEL), lambda m, n: (m, 0)),
            _mod_spec(2, tm, seq, base_row, per_seq),
            pl.BlockSpec((tm, a.shape[1]), lambda m, n: (m, 0)),
            pl.BlockSpec((tm, b.shape[1]), lambda m, n: (m, 0)),
            pl.BlockSpec(wa.shape, lambda m, n: (0, 0)),
            pl.BlockSpec(wb.shape, lambda m, n: (0, 0)),
        ],
        out_specs=pl.BlockSpec((tm, D_MODEL), lambda m, n: (m, 0)),
        out_shape=jax.ShapeDtypeStruct((t_tokens, D_MODEL), f32),
        compiler_params=_params("arbitrary", "arbitrary"),
        name="out_proj",
    )(x, mod, a, b, wa, wb)


def _ffn_kernel(tm, seq, nj, final, *refs):
    it = iter(refs)
    (x_ref, xp_ref, xn_ref, shift_ref, scale_ref, gate_ref, g_ref,
     wv_ref, wg_ref, cwv_ref, cwg_ref, cbv_ref, cbg_ref, wd_ref) = (next(it) for _ in range(14))
    gf_ref = next(it) if final else None
    o_ref, h_ref, acc_ref = next(it), next(it), next(it)
    u_refs = [(next(it), next(it)), (next(it), next(it))]
    i = pl.program_id(0)
    j = pl.program_id(1)
    rows = tm + 2 * HALO

    def prologue_and_first_up():
        g, sh, sc = g_ref[...], shift_ref[...], scale_ref[...]
        uv_ref, ug_ref = u_refs[0]

        def emit(rs, h):
            h_ref[rs, :] = h
            uv_ref[rs, :] = jnp.dot(h, wv_ref[...], preferred_element_type=f32)
            ug_ref[rs, :] = jnp.dot(h, wg_ref[...], preferred_element_type=f32)

        for r0 in range(0, tm - IN_ROWS, IN_ROWS):
            rs = slice(r0, r0 + IN_ROWS)
            emit(rs, _rms_modulate(x_ref[rs, :], g, sh, sc).astype(bf16))
        next_ok = jnp.where((((i + 1) * tm) & (seq - 1)) != 0, 1.0, 0.0)
        prev_ok = jnp.where(((i * tm) & (seq - 1)) != 0, 1.0, 0.0)
        h_last = _rms_modulate(x_ref[tm - IN_ROWS:tm, :], g, sh, sc)
        h_next = _rms_modulate(xn_ref[...], g, sh, sc) * next_ok
        h_prev = _rms_modulate(xp_ref[...], g, sh, sc) * prev_ok
        emit(slice(tm - IN_ROWS, rows), jnp.concatenate([h_last, h_next, h_prev], axis=0).astype(bf16))
        acc_ref[...] = jnp.zeros_like(acc_ref)

    def zero_row(a, r):
        r0 = (r // HALO) * HALO
        row8 = lax.broadcasted_iota(jnp.int32, (HALO, 1), 0)
        fixed = jnp.where(row8 == r - r0, 0.0, a[r0:r0 + HALO])
        return jnp.concatenate([a[:r0], fixed, a[r0 + HALO:]], axis=0)

    def window(u_ref, r0):
        lo, hi = r0 - HALO, r0 + FFN_ROWS + HALO
        if lo < 0:
            return jnp.concatenate([u_ref[rows + lo:rows, :], u_ref[0:hi, :]], axis=0)
        return u_ref[lo:hi, :]

    def conv(u_ref, r0, cw_ref, cb_ref):
        u = window(u_ref, r0)
        n = FFN_ROWS + 2 * HALO
        up = pltpu.roll(u, 1, 0)
        un = pltpu.roll(u, n - 1, 0)
        for edge in range(seq, tm, seq):
            if r0 <= edge < r0 + FFN_ROWS:
                up = zero_row(up, edge - r0 + HALO)
            if r0 <= edge - 1 < r0 + FFN_ROWS:
                un = zero_row(un, edge - 1 - r0 + HALO)
        c = up * cw_ref[0:1, :] + u * cw_ref[1:2, :] + un * cw_ref[2:3, :] + cb_ref[...]
        return c[HALO:HALO + FFN_ROWS, :]

    def gated(slot):
        uv_ref, ug_ref = u_refs[slot]
        chunks = []
        for r0 in range(0, tm, FFN_ROWS):
            val = conv(uv_ref, r0, cwv_ref, cbv_ref)
            gate = conv(ug_ref, r0, cwg_ref, cbg_ref)
            chunks.append((gate * jax.nn.sigmoid(gate) * val).astype(bf16))
        return chunks

    def zero_after(chunks):
        parts = []
        for c in chunks:
            bits = pltpu.bitcast(c, jnp.uint32)
            parts += [bits[k:k + HALO] for k in range(0, bits.shape[0], HALO)]
        while len(parts) > 1:
            parts = [a | b for a, b in zip(parts[0::2], parts[1::2])] + parts[len(parts) & ~1:]
        return (parts[0] >> 16) >> 16

    def after(a, zero, row0=0):
        tf = zero.shape[1]
        mid = pltpu.bitcast(pltpu.bitcast(a[row0:row0 + 2 * HALO, 0:tf], jnp.uint32) | zero, bf16)
        if tf < a.shape[1]:
            mid = jnp.concatenate([mid, a[row0:row0 + 2 * HALO, tf:]], axis=1)
        parts = ([a[:row0]] if row0 else []) + [mid, a[row0 + 2 * HALO:]]
        return jnp.concatenate(parts, axis=0)

    def up_project(slot, chunks=None):
        h = h_ref[...]
        wv, wg = wv_ref[...], wg_ref[...]
        uv_ref, ug_ref = u_refs[slot]
        if chunks is not None:
            k_mid = wv.shape[0] // 2
            wv = after(wv, zero_after(chunks[0:1]), k_mid)
            wg = after(wg, zero_after(chunks[2:3]), k_mid)
        uv_ref[...] = jnp.dot(h, wv, preferred_element_type=f32)
        if chunks is not None:
            h = after(h, zero_after(chunks[1:2]))
        ug_ref[...] = jnp.dot(h, wg, preferred_element_type=f32)

    def down_project(chunks, anchored):
        if anchored:
            chunks = [after(chunks[0], zero_after(chunks[3:]))] + chunks[1:]
        act = jnp.concatenate(chunks, axis=0)
        acc_ref[...] += jnp.dot(act, wd_ref[...], preferred_element_type=f32)

    @pl.when(j == 0)
    def _():
        prologue_and_first_up()

    for p in range(2):
        @pl.when((j >= 1) & (j < nj) & (j % 2 == p))
        def _(p=p):
            chunks = gated(1 - p)
            up_project(p, chunks)
            down_project(chunks, True)

    @pl.when(j == nj)
    def _():
        down_project(gated((nj - 1) % 2), False)
        y = x_ref[...] + gate_ref[...] * acc_ref[...]
        if final:
            ms = jnp.mean(y * y, axis=-1, keepdims=True)
            y = y * lax.rsqrt(ms + EPS) * gf_ref[...]
        o_ref[...] = y


def conv_ffn(x, mod, g, w_up, conv_w, conv_b, w_down, seq, base_row, per_seq, final_g=None):
    t_tokens = x.shape[0]
    tm, tf = 512, W_TILE
    assert seq & (seq - 1) == 0
    nj = D_FF // tf
    hb = tm // HALO
    n_hblk = t_tokens // HALO
    conv_b2 = conv_b.reshape(1, 2 * D_FF)

    def blk(j):
        return jnp.clip(j, 0, nj - 1)

    in_specs = [
        pl.BlockSpec((tm, D_MODEL), lambda i, j: (i, 0)),
        pl.BlockSpec((HALO, D_MODEL), lambda i, j: (jnp.maximum(i * hb - 1, 0), 0)),
        pl.BlockSpec((HALO, D_MODEL), lambda i, j: (jnp.minimum((i + 1) * hb, n_hblk - 1), 0)),
        _mod_spec(3, tm, seq, base_row, per_seq),
        _mod_spec(4, tm, seq, base_row, per_seq),
        _mod_spec(5, tm, seq, base_row, per_seq),
        pl.BlockSpec((1, D_MODEL), lambda i, j: (0, 0)),
        pl.BlockSpec((D_MODEL, tf), lambda i, j: (0, blk(j))),
        pl.BlockSpec((D_MODEL, tf), lambda i, j: (0, nj + blk(j))),
        pl.BlockSpec((3, tf), lambda i, j: (0, blk(j - 1))),
        pl.BlockSpec((3, tf), lambda i, j: (0, nj + blk(j - 1))),
        pl.BlockSpec((1, tf), lambda i, j: (0, blk(j - 1))),
        pl.BlockSpec((1, tf), lambda i, j: (0, nj + blk(j - 1))),
        pl.BlockSpec((tf, D_MODEL), lambda i, j: (blk(j - 1), 0)),
    ]
    args = [x, x, x, mod, mod, mod, g.reshape(1, D_MODEL), w_up, w_up,
            conv_w, conv_w, conv_b2, conv_b2, w_down]
    if final_g is not None:
        in_specs.append(pl.BlockSpec((1, D_MODEL), lambda i, j: (0, 0)))
        args.append(final_g.reshape(1, D_MODEL))
    return pl.pallas_call(
        functools.partial(_ffn_kernel, tm, seq, nj, final_g is not None),
        grid=(t_tokens // tm, nj + 1),
        in_specs=in_specs,
        out_specs=pl.BlockSpec((tm, D_MODEL), lambda i, j: (i, 0)),
        out_shape=jax.ShapeDtypeStruct((t_tokens, D_MODEL), f32),
        scratch_shapes=[pltpu.VMEM((tm + 2 * HALO, D_MODEL), bf16),
                        pltpu.VMEM((tm, D_MODEL), f32)]
                       + [pltpu.VMEM((tm + 2 * HALO, tf), f32)] * 4,
        compiler_params=_params("arbitrary", "arbitrary"),
        name="conv_ffn",
    )(*args)


def _rope_tables(seq):
    t = np.arange(seq)
    half = HEAD_DIM // 2
    inv_freq = np.power(ROPE_BASE, -np.arange(0, half, 2, dtype=np.float64) / half)
    ang_r = (t // GRID_W)[:, None] * inv_freq[None, :]
    ang_c = (t % GRID_W)[:, None] * inv_freq[None, :]
    cos = np.concatenate([np.cos(ang_r)] * 2 + [np.cos(ang_c)] * 2, axis=-1)
    sin = np.concatenate([-np.sin(ang_r), np.sin(ang_r), -np.sin(ang_c), np.sin(ang_c)], axis=-1)
    return jnp.asarray(cos, dtype=f32), jnp.asarray(sin, dtype=f32)


def kernel(x_prompt, x_sample, c, cache_na_k, cache_na_v, cache_gqa_k, cache_gqa_v, c_ctx, norm1_g, norm2_g, ada_w, ada_b, ev_w_in, ev_na_bias, ev_fnet_w, ev_w_out, od_w_in, od_pool_w, od_pool_scale, od_q_norm_g, od_k_norm_g, od_w_out, ffn_w_up, ffn_conv_w, ffn_conv_b, ffn_w_down, final_norm_g):
    n_ctx, seq_p, _ = x_prompt.shape
    n_lat, seq_s, _ = x_sample.shape
    past = cache_na_k.shape[2]
    assert n_lat + 1 <= MOD_ROWS
    xp = x_prompt.reshape(n_ctx * seq_p, D_MODEL)
    xs = x_sample.reshape(n_lat * seq_s, D_MODEL)

    cond = jnp.zeros((MOD_ROWS, D_MODEL), f32).at[0].set(c_ctx).at[1:1 + n_lat].set(c)
    mods = ada_modulation(cond, ada_w, ada_b).reshape(DEPTH, MOD_ROWS * 6, 1, D_MODEL)
    rope = _rope_tables(seq_s)

    ctx = dict(seq=seq_p, base_row=0, per_seq=0)
    lat = dict(seq=seq_s, base_row=1, per_seq=1)
    new_k, new_v = {}, {}
    for i in range(DEPTH):
        j = i // 2
        mod = mods[i]
        w_up = ffn_w_up[i].astype(bf16)
        w_down = ffn_w_down[i].astype(bf16)
        if i % 2 == 0:
            w_in = ev_w_in[j].astype(bf16)
            w_attn = ev_w_out[j, :NA_WIDTH].astype(bf16)
            w_mix = ev_w_out[j, NA_WIDTH:].astype(bf16)
            mix_blk = 3 * NA_WIDTH // FNET_WIDTH
            proj_p = in_proj(xp, mod, norm1_g[i], w_in, f32, NA_WIDTH, **ctx)
            new_k[i] = proj_p[:, NA_WIDTH:2 * NA_WIDTH]
            new_v[i] = proj_p[:, 2 * NA_WIDTH:3 * NA_WIDTH]
            attn_p = ctx_attention(proj_p, seq_p, NA_HEADS, NA_HEADS)
            mix_p = fourier_mix(proj_p, mix_blk, ev_fnet_w[j], seq_p)
            proj_s = in_proj(xs, mod, norm1_g[i], w_in, bf16, NA_WIDTH, **lat)
            attn_s = na_attention(proj_s,
                                  cache_na_k[:, j].reshape(n_lat * past, NA_WIDTH),
                                  cache_na_v[:, j].reshape(n_lat * past, NA_WIDTH),
                                  ev_na_bias[j], seq_s, past)
            mix_s = fourier_mix(proj_s, mix_blk, ev_fnet_w[j], seq_s)
        else:
            w_in = jnp.concatenate([od_w_in[j, :, POOL_WIDTH:], od_w_in[j, :, :POOL_WIDTH]],
                                   axis=1).astype(bf16)
            w_attn = od_w_out[j, POOL_WIDTH:].astype(bf16)
            w_mix = od_w_out[j, :POOL_WIDTH].astype(bf16)
            norm = (od_q_norm_g[j], od_k_norm_g[j], GQA_KV_WIDTH)
            k_lo = GQA_Q_WIDTH
            v_lo = GQA_Q_WIDTH + GQA_KV_WIDTH
            mix_blk = (GQA_Q_WIDTH + 2 * GQA_KV_WIDTH) // POOL_WIDTH
            proj_p = in_proj(xp, mod, norm1_g[i], w_in, f32, GQA_Q_WIDTH, norm=norm, **ctx)
            new_k[i] = proj_p[:, k_lo:k_lo + GQA_KV_WIDTH]
            new_v[i] = proj_p[:, v_lo:v_lo + GQA_KV_WIDTH]
            attn_p = ctx_attention(proj_p, seq_p, GQA_Q_HEADS, GQA_KV_HEADS)
            mix_p = pool_mix(proj_p, mix_blk, od_pool_w[j], od_pool_scale[j], seq_p)
            proj_s = in_proj(xs, mod, norm1_g[i], w_in, bf16, GQA_Q_WIDTH, norm=norm, rope=rope, **lat)
            attn_s = gqa_attention(proj_s,
                                   cache_gqa_k[:, j].reshape(n_lat * past, GQA_KV_WIDTH),
                                   cache_gqa_v[:, j].reshape(n_lat * past, GQA_KV_WIDTH),
                                   seq_s, past)
            mix_s = pool_mix(proj_s, mix_blk, od_pool_w[j], od_pool_scale[j], seq_s)
        xp = out_proj(xp, mod, attn_p, mix_p, w_attn, w_mix, **ctx)
        xs = out_proj(xs, mod, attn_s, mix_s, w_attn, w_mix, **lat)
        final_g = final_norm_g if i == DEPTH - 1 else None
        xp = conv_ffn(xp, mod, norm2_g[i], w_up, ffn_conv_w[i], ffn_conv_b[i], w_down,
                      final_g=final_g, **ctx)
        xs = conv_ffn(xs, mod, norm2_g[i], w_up, ffn_conv_w[i], ffn_conv_b[i], w_down,
                      final_g=final_g, **lat)

    y_prompt = xp.reshape(n_ctx, seq_p, D_MODEL)
    y_sample = xs.reshape(n_lat, seq_s, D_MODEL)
    even = [i for i in range(DEPTH) if i % 2 == 0]
    odd = [i for i in range(DEPTH) if i % 2 == 1]

    def stack(parts, layers, heads):
        return jnp.stack([parts[i].reshape(n_ctx, seq_p, heads, HEAD_DIM) for i in layers], axis=1)

    return (y_prompt, y_sample,
            stack(new_k, even, NA_HEADS), stack(new_v, even, NA_HEADS),
            stack(new_k, odd, GQA_KV_HEADS), stack(new_v, odd, GQA_KV_HEADS))
```

```python
import functools
import math

import jax
import jax.numpy as jnp
import numpy as np
from jax import lax
from jax.experimental import pallas as pl
from jax.experimental.pallas import tpu as pltpu

f32 = jnp.float32
bf16 = jnp.bfloat16

D_MODEL = 2048
DEPTH = 2
GRID_W = 64
HEAD_DIM = 128
NA_HEADS = 12
NA_KR = 8
NA_KC = 16
FNET_GROUPS = 4
FNET_CH = 128
POOL_WINDOWS = (2, 4, 8, 16)
GQA_Q_HEADS = 12
GQA_KV_HEADS = 4
D_FF = 5632
ROPE_BASE = 10000.0
EPS = 1e-6
NEG_INF = -1e30
NA_WIDTH = NA_HEADS * HEAD_DIM
FNET_WIDTH = FNET_GROUPS * FNET_CH
POOL_WIDTH = len(POOL_WINDOWS) * FNET_CH
GQA_Q_WIDTH = GQA_Q_HEADS * HEAD_DIM
GQA_KV_WIDTH = GQA_KV_HEADS * HEAD_DIM
LOG2_E = math.log2(math.e)
ATTN_SCALE = HEAD_DIM ** -0.5 * LOG2_E

MOD_ROWS = 16
VMEM_LIMIT = 56 * 1024 * 1024
HALO = 8
NA_QROWS = 4
NA_WROWS = 12
NA_INVALID = 2 * NA_KR - 1
GQA_QUERIES = 512
NA_HEADS_PER_STEP = 6
FFN_TOKENS = 512
FFN_ROWS = 128
W_TILE = 512
IN_TOKENS = 1024
IN_ROWS = 256


def _params(*sem, flags=None):
    return pltpu.CompilerParams(dimension_semantics=sem, vmem_limit_bytes=VMEM_LIMIT, flags=flags)


def _rms_modulate(x, g, shift, scale):
    ms = jnp.mean(x * x, axis=-1, keepdims=True)
    return (x * lax.rsqrt(ms + EPS)) * (g * (1.0 + scale)) + shift


def _mod_spec(k, tm, seq, base_row, per_seq):
    def index(m, n):
        return ((base_row + per_seq * ((m * tm) // seq)) * 6 + k, 0, 0)
    return pl.BlockSpec((None, 1, D_MODEL), index)


def _ada_kernel(c_ref, w_ref, b_ref, o_ref):
    c = c_ref[...]
    s = (c * jax.nn.sigmoid(c)).astype(bf16)
    o_ref[...] = jnp.dot(s, w_ref[...].astype(bf16), preferred_element_type=f32) + b_ref[...]


def ada_modulation(cond, ada_w, ada_b):
    tn = 1024
    n_out = 6 * D_MODEL
    return pl.pallas_call(
        _ada_kernel,
        grid=(DEPTH, n_out // tn),
        in_specs=[
            pl.BlockSpec((MOD_ROWS, D_MODEL), lambda i, n: (0, 0)),
            pl.BlockSpec((None, D_MODEL, tn), lambda i, n: (i, 0, n)),
            pl.BlockSpec((None, 1, tn), lambda i, n: (i, 0, n)),
        ],
        out_specs=pl.BlockSpec((None, MOD_ROWS, tn), lambda i, n: (i, 0, n)),
        out_shape=jax.ShapeDtypeStruct((DEPTH, MOD_ROWS, n_out), f32),
        compiler_params=_params("arbitrary", "arbitrary"),
        name="ada_modulation",
    )(cond, ada_w, ada_b.reshape(DEPTH, 1, n_out))


def _rope(a, cos, sin):
    lane = lax.broadcasted_iota(jnp.int32, a.shape, 1)
    quarter = HEAD_DIM // 4
    partner = jnp.where((lane & (2 * quarter - 1)) < quarter,
                        pltpu.roll(a, HEAD_DIM - quarter, 1),
                        pltpu.roll(a, quarter, 1))
    return a * cos + partner * sin


def _in_proj_kernel(tn, q_blocks, k_blocks, has_norm, has_rope, *refs):
    it = iter(refs)
    x_ref, shift_ref, scale_ref, g_ref, w_ref = (next(it) for _ in range(5))
    qg_ref = kg_ref = cos_ref = sin_ref = None
    if has_norm:
        qg_ref, kg_ref, avg_ref = next(it), next(it), next(it)
    if has_rope:
        cos_ref, sin_ref = next(it), next(it)
    o_ref, h_ref = next(it), next(it)
    n = pl.program_id(1)
    tm = x_ref.shape[0]
    everything = slice(0, tm)

    def project(rs, h=None):
        h = h_ref[rs, :] if h is None else h
        return jnp.dot(h, w_ref[...], preferred_element_type=f32)

    def store_scaled(rs, acc, scale):
        o_ref[rs, :] = (acc if scale is None else acc * scale).astype(o_ref.dtype)

    def store_normed(rs, acc, gn):
        ms = jnp.dot((acc * acc).astype(bf16), avg_ref[...], preferred_element_type=f32)
        inv = lax.rsqrt(ms + EPS)
        for hh in range(tn // HEAD_DIM):
            sl = slice(hh * HEAD_DIM, (hh + 1) * HEAD_DIM)
            a = acc[:, sl] * inv[:, sl] * gn
            if has_rope:
                a = _rope(a, cos_ref[rs, :], sin_ref[rs, :])
            o_ref[rs, sl] = a.astype(o_ref.dtype)

    @pl.when(n == 0)
    def _():
        g, sh, sc = g_ref[...], shift_ref[...], scale_ref[...]
        for r0 in range(0, tm, IN_ROWS):
            rs = slice(r0, r0 + IN_ROWS)
            h = _rms_modulate(x_ref[rs, :], g, sh, sc).astype(bf16)
            h_ref[rs, :] = h
            if has_norm:
                store_normed(rs, project(rs, h), qg_ref[...] * ATTN_SCALE)
            else:
                store_scaled(rs, project(rs, h), ATTN_SCALE)

    is_q = n < q_blocks
    q_scale = jnp.where(is_q, ATTN_SCALE, 1.0)
    if not has_norm:
        @pl.when(n > 0)
        def _():
            store_scaled(everything, project(everything), q_scale)
        return

    is_qk = n < q_blocks + k_blocks

    @pl.when((n > 0) & is_qk)
    def _():
        gn = jnp.where(is_q, qg_ref[...], kg_ref[...]) * q_scale
        store_normed(everything, project(everything), gn)

    @pl.when(jnp.logical_not(is_qk))
    def _():
        store_scaled(everything, project(everything), None)


def in_proj(x, mod, g, w, out_dtype, q_width, seq, base_row, per_seq, norm=None, rope=None):
    t_tokens = x.shape[0]
    tm, tn = IN_TOKENS, W_TILE
    n_total = w.shape[1]
    in_specs = [
        pl.BlockSpec((tm, D_MODEL), lambda m, n: (m, 0)),
        _mod_spec(0, tm, seq, base_row, per_seq),
        _mod_spec(1, tm, seq, base_row, per_seq),
        pl.BlockSpec((1, D_MODEL), lambda m, n: (0, 0)),
        pl.BlockSpec((D_MODEL, tn), lambda m, n: (0, n)),
    ]
    args = [x, mod, mod, g.reshape(1, D_MODEL), w]
    k_width = 0
    if norm is not None:
        k_width = norm[2]
        in_specs += [pl.BlockSpec((1, HEAD_DIM), lambda m, n: (0, 0))] * 2
        in_specs += [pl.BlockSpec((tn, tn), lambda m, n: (0, 0))]
        head_of = np.arange(tn) // HEAD_DIM
        avg = (head_of[:, None] == head_of[None, :]) / HEAD_DIM
        args += [norm[0].reshape(1, HEAD_DIM), norm[1].reshape(1, HEAD_DIM), jnp.asarray(avg, dtype=bf16)]
    if rope is not None:
        spb = seq // tm
        in_specs += [pl.BlockSpec((tm, HEAD_DIM), lambda m, n: (m % spb, 0))] * 2
        args += [rope[0], rope[1]]
    kern = functools.partial(_in_proj_kernel, tn, q_width // tn, k_width // tn,
                             norm is not None, rope is not None)
    return pl.pallas_call(
        kern,
        grid=(t_tokens // tm, n_total // tn),
        in_specs=in_specs,
        out_specs=pl.BlockSpec((tm, tn), lambda m, n: (m, n)),
        out_shape=jax.ShapeDtypeStruct((t_tokens, n_total), out_dtype),
        scratch_shapes=[pltpu.VMEM((tm, D_MODEL), bf16)],
        compiler_params=_params("arbitrary", "arbitrary"),
        name="in_proj",
    )(*args)


def _with_ones(v):
    return jnp.concatenate([v, jnp.ones_like(v)], axis=1)


def _softmax_pv(parts):
    m = functools.reduce(jnp.maximum, [jnp.max(s, axis=-1, keepdims=True) for s, _ in parts])
    acc = sum(jnp.dot(jnp.exp2(s - m).astype(bf16), v1, preferred_element_type=f32) for s, v1 in parts)
    return acc[:, :HEAD_DIM] / acc[:, HEAD_DIM:]


def _ctx_attn_kernel(n_q, n_kv, q_ref, k_ref, v_ref, o_ref):
    group = n_q // n_kv
    values = [_with_ones(v_ref[:, kv * HEAD_DIM:(kv + 1) * HEAD_DIM].astype(bf16)) for kv in range(n_kv)]
    keys = [k_ref[:, kv * HEAD_DIM:(kv + 1) * HEAD_DIM].astype(bf16) for kv in range(n_kv)]
    dn = (((1,), (1,)), ((), ()))
    scores = [lax.dot_general(q_ref[:, h * HEAD_DIM:(h + 1) * HEAD_DIM].astype(bf16), keys[h // group],
                              dn, preferred_element_type=f32) for h in range(n_q)]
    for h, s in enumerate(scores):
        o = _softmax_pv([(s, values[h // group])])
        o_ref[:, h * HEAD_DIM:(h + 1) * HEAD_DIM] = o.astype(o_ref.dtype)


def ctx_attention(proj, seq, n_q, n_kv):
    t_tokens = proj.shape[0]
    qw, kw = n_q * HEAD_DIM, n_kv * HEAD_DIM
    assert qw % kw == 0
    k_blk = qw // kw
    return pl.pallas_call(
        functools.partial(_ctx_attn_kernel, n_q, n_kv),
        grid=(t_tokens // seq,),
        in_specs=[
            pl.BlockSpec((seq, qw), lambda b: (b, 0)),
            pl.BlockSpec((seq, kw), lambda b: (b, k_blk)),
            pl.BlockSpec((seq, kw), lambda b: (b, k_blk + 1)),
        ],
        out_specs=pl.BlockSpec((seq, qw), lambda b: (b, 0)),
        out_shape=jax.ShapeDtypeStruct((t_tokens, qw), bf16),
        compiler_params=_params("arbitrary"),
        name="ctx_attention",
    )(proj, proj, proj)


def _gqa_attn_kernel(group, q_ref, ck_ref, cv_ref, k_ref, v_ref, o_ref):
    ck = ck_ref[...].astype(bf16)
    cv = _with_ones(cv_ref[...].astype(bf16))
    k = k_ref[...]
    v = _with_ones(v_ref[...])
    dn = (((1,), (1,)), ((), ()))
    scores = []
    for gi in range(group):
        q = q_ref[:, gi * HEAD_DIM:(gi + 1) * HEAD_DIM]
        scores.append((lax.dot_general(q, ck, dn, preferred_element_type=f32),
                       lax.dot_general(q, k, dn, preferred_element_type=f32)))
    for gi, (s_c, s_l) in enumerate(scores):
        o = _softmax_pv([(s_c, cv), (s_l, v)])
        o_ref[:, gi * HEAD_DIM:(gi + 1) * HEAD_DIM] = o.astype(o_ref.dtype)


def gqa_attention(proj, cache_k, cache_v, seq, past):
    t_tokens = proj.shape[0]
    n_b = t_tokens // seq
    group = GQA_Q_HEADS // GQA_KV_HEADS
    tq = GQA_QUERIES
    nq = seq // tq
    return pl.pallas_call(
        functools.partial(_gqa_attn_kernel, group),
        grid=(n_b, GQA_KV_HEADS, nq),
        in_specs=[
            pl.BlockSpec((tq, group * HEAD_DIM), lambda b, h, i: (b * nq + i, h)),
            pl.BlockSpec((past, HEAD_DIM), lambda b, h, i: (b, h)),
            pl.BlockSpec((past, HEAD_DIM), lambda b, h, i: (b, h)),
            pl.BlockSpec((seq, HEAD_DIM), lambda b, h, i: (b, GQA_Q_HEADS + h)),
            pl.BlockSpec((seq, HEAD_DIM), lambda b, h, i: (b, GQA_Q_HEADS + GQA_KV_HEADS + h)),
        ],
        out_specs=pl.BlockSpec((tq, group * HEAD_DIM), lambda b, h, i: (b * nq + i, h)),
        out_shape=jax.ShapeDtypeStruct((t_tokens, GQA_Q_WIDTH), bf16),
        compiler_params=_params("arbitrary", "arbitrary", "arbitrary"),
        name="gqa_attention",
    )(proj, cache_k, cache_v, proj, proj)


def _na_bias_tiles(rel_bias):
    qc = jnp.arange(GRID_W)[:, None]
    kc = jnp.arange(GRID_W)[None, :]
    q_start = jnp.clip(qc - NA_KC // 2, 0, GRID_W - NA_KC)
    valid = (kc >= q_start) & (kc < q_start + NA_KC)
    dc = jnp.clip(kc - qc, -(NA_KC - 1), NA_KC - 1) + NA_KC - 1
    tiles = jnp.where(valid[None, None], rel_bias.astype(f32)[:, :, dc] * LOG2_E, NEG_INF)
    masked = jnp.full((NA_HEADS, 1, GRID_W, GRID_W), NEG_INF, f32)
    tiles = jnp.concatenate([tiles, masked], axis=1)
    zeros = jnp.zeros_like(tiles)
    left = jnp.concatenate([tiles, zeros], axis=-1)
    right = jnp.concatenate([zeros, tiles], axis=-1)
    return left, right


def _na_attn_kernel(n_heads, n_rows, q_ref, k_ref, v_ref, ck_ref, cv_ref, bl_ref, br_ref, o_ref):
    rb = pl.program_id(2)
    start = jnp.clip(rb * NA_QROWS - NA_KR // 2, 0, n_rows - NA_WROWS)
    tok0 = pl.multiple_of(start * GRID_W, GRID_W)
    dn = (((1,), (1,)), ((), ()))

    def tile_index(i, kr):
        qr = rb * NA_QROWS + i
        r0 = jnp.clip(qr - NA_KR // 2, 0, n_rows - NA_KR)
        return jnp.where((kr >= r0) & (kr < r0 + NA_KR), kr - qr + NA_KR - 1, NA_INVALID)

    index = [[(tile_index(i, start + 2 * jp), tile_index(i, start + 2 * jp + 1))
              for jp in range(NA_WROWS // 2)] for i in range(NA_QROWS)]
    scores = []
    for hh in range(n_heads):
        hs = slice(hh * HEAD_DIM, (hh + 1) * HEAD_DIM)
        q = q_ref[:, hs]
        bias = jnp.concatenate(
            [jnp.concatenate([bl_ref[hh, ia] + br_ref[hh, ib] for ia, ib in row], axis=1)
             for row in index], axis=0)
        s_l = lax.dot_general(q, k_ref[pl.ds(tok0, NA_WROWS * GRID_W), hs], dn,
                              preferred_element_type=f32) + bias
        s_c = lax.dot_general(q, ck_ref[:, hs].astype(bf16), dn, preferred_element_type=f32)
        scores.append((s_c, s_l))
    for hh, (s_c, s_l) in enumerate(scores):
        hs = slice(hh * HEAD_DIM, (hh + 1) * HEAD_DIM)
        vw = v_ref[pl.ds(tok0, NA_WROWS * GRID_W), hs]
        o = _softmax_pv([(s_c, _with_ones(cv_ref[:, hs].astype(bf16))), (s_l, _with_ones(vw))])
        o_ref[:, hs] = o.astype(o_ref.dtype)


def na_attention(proj, cache_k, cache_v, rel_bias, seq, past):
    t_tokens = proj.shape[0]
    n_b = t_tokens // seq
    n_rows = seq // GRID_W
    n_rb = n_rows // NA_QROWS
    tq = NA_QROWS * GRID_W
    hb = NA_HEADS_PER_STEP
    n_hg = NA_HEADS // hb
    wide = hb * HEAD_DIM
    bias_l, bias_r = _na_bias_tiles(rel_bias)
    bias_spec = pl.BlockSpec((hb, 2 * NA_KR, GRID_W, 2 * GRID_W), lambda b, h, r: (h, 0, 0, 0))
    return pl.pallas_call(
        functools.partial(_na_attn_kernel, hb, n_rows),
        grid=(n_b, n_hg, n_rb),
        in_specs=[
            pl.BlockSpec((tq, wide), lambda b, h, r: (b * n_rb + r, h)),
            pl.BlockSpec((seq, wide), lambda b, h, r: (b, n_hg + h)),
            pl.BlockSpec((seq, wide), lambda b, h, r: (b, 2 * n_hg + h)),
            pl.BlockSpec((past, wide), lambda b, h, r: (b, h)),
            pl.BlockSpec((past, wide), lambda b, h, r: (b, h)),
            bias_spec,
            bias_spec,
        ],
        out_specs=pl.BlockSpec((tq, wide), lambda b, h, r: (b * n_rb + r, h)),
        out_shape=jax.ShapeDtypeStruct((t_tokens, NA_WIDTH), bf16),
        compiler_params=_params("arbitrary", "arbitrary", "arbitrary"),
        name="na_attention",
    )(proj, proj, proj, cache_k, cache_v, bias_l, bias_r)


def _dft_cols_kernel(x_ref, cs_ref, y_ref):
    for g in range(FNET_GROUPS):
        sl = slice(g * FNET_CH, (g + 1) * FNET_CH)
        y = jnp.dot(x_ref[:, sl].astype(bf16), cs_ref[...], preferred_element_type=f32)
        y_ref[0, :, sl] = y[:, :FNET_CH].astype(bf16)
        y_ref[1, :, sl] = y[:, FNET_CH:].astype(bf16)


def _dft_rows_kernel(d_ref, y_ref, w_ref, o_ref):
    f = jnp.dot(d_ref[...], y_ref[...], preferred_element_type=f32)
    for g in range(FNET_GROUPS):
        sl = slice(g * FNET_CH, (g + 1) * FNET_CH)
        o = jnp.dot(f[:, sl].astype(bf16), w_ref[g], preferred_element_type=f32)
        o_ref[:, sl] = o.astype(o_ref.dtype)


def _dft_tables(seq):
    c = np.arange(FNET_CH, dtype=np.int64)
    ang_c = (2.0 * np.pi / FNET_CH) * ((c[:, None] * c[None, :]) % FNET_CH)
    cs = np.concatenate([np.cos(ang_c), np.sin(ang_c)], axis=1)
    n = np.arange(seq, dtype=np.int64)
    ang_n = (2.0 * np.pi / seq) * ((n[:, None] * n[None, :]) % seq)
    norm = 1.0 / math.sqrt(seq * FNET_CH)
    d = np.concatenate([np.cos(ang_n) * norm, -np.sin(ang_n) * norm], axis=1)
    return jnp.asarray(cs, dtype=bf16), jnp.asarray(d, dtype=bf16)


def fourier_mix(proj, col_blk, fnet_w, seq):
    t_tokens = proj.shape[0]
    n_b = t_tokens // seq
    cs, d = _dft_tables(seq)
    tn = min(seq, 512)
    ns = seq // tn
    y = pl.pallas_call(
        _dft_cols_kernel,
        grid=(n_b, ns),
        in_specs=[
            pl.BlockSpec((tn, FNET_WIDTH), lambda b, i: (b * ns + i, col_blk)),
            pl.BlockSpec((FNET_CH, 2 * FNET_CH), lambda b, i: (0, 0)),
        ],
        out_specs=pl.BlockSpec((2, tn, FNET_WIDTH), lambda b, i: (0, i, b)),
        out_shape=jax.ShapeDtypeStruct((2, seq, n_b * FNET_WIDTH), bf16),
        compiler_params=_params("arbitrary", "arbitrary"),
        name="dft_cols",
    )(proj, cs)
    y2 = y.reshape(2 * seq, n_b * FNET_WIDTH)
    return pl.pallas_call(
        _dft_rows_kernel,
        grid=(n_b, ns),
        in_specs=[
            pl.BlockSpec((tn, 2 * seq), lambda b, i: (i, 0)),
            pl.BlockSpec((2 * seq, FNET_WIDTH), lambda b, i: (0, b)),
            pl.BlockSpec((FNET_GROUPS, FNET_CH, FNET_CH), lambda b, i: (0, 0, 0)),
        ],
        out_specs=pl.BlockSpec((tn, FNET_WIDTH), lambda b, i: (b * ns + i, 0)),
        out_shape=jax.ShapeDtypeStruct((t_tokens, FNET_WIDTH), bf16),
        compiler_params=_params("arbitrary", "arbitrary"),
        name="dft_rows",
    )(d, y2, fnet_w.astype(bf16))


def _pool_kernel(seq, x_ref, w_ref, sc_ref, o_ref):
    t = lax.broadcasted_iota(jnp.int32, (seq, 1), 0)

    def prev(a, k):
        return jnp.where(t >= k, pltpu.roll(a, k, 0), 0.0)

    def nxt(a, k):
        return jnp.where(t < seq - k, pltpu.roll(a, seq - k, 0), 0.0)

    for g, win in enumerate(POOL_WINDOWS):
        sl = slice(g * FNET_CH, (g + 1) * FNET_CH)
        x = x_ref[:, sl].astype(f32)
        half = win // 2
        back, fwd, k = x, x, 1
        while k < half:
            back = back + prev(back, k)
            fwd = fwd + nxt(fwd, k)
            k *= 2
        total = prev(back, 1) + fwd
        cnt = (jnp.minimum(t + half, seq) - jnp.maximum(t - half, 0)).astype(f32)
        pooled = (total / cnt - x).astype(bf16)
        o = jnp.dot(pooled, w_ref[g], preferred_element_type=f32) * sc_ref[:, sl]
        o_ref[:, sl] = o.astype(o_ref.dtype)


def pool_mix(proj, col_blk, pool_w, pool_scale, seq):
    t_tokens = proj.shape[0]
    return pl.pallas_call(
        functools.partial(_pool_kernel, seq),
        grid=(t_tokens // seq,),
        in_specs=[
            pl.BlockSpec((seq, POOL_WIDTH), lambda b: (b, col_blk)),
            pl.BlockSpec((len(POOL_WINDOWS), FNET_CH, FNET_CH), lambda b: (0, 0, 0)),
            pl.BlockSpec((1, POOL_WIDTH), lambda b: (0, 0)),
        ],
        out_specs=pl.BlockSpec((seq, POOL_WIDTH), lambda b: (b, 0)),
        out_shape=jax.ShapeDtypeStruct((t_tokens, POOL_WIDTH), bf16),
        compiler_params=_params("arbitrary"),
        name="pool_mix",
    )(proj, pool_w.astype(bf16), pool_scale.reshape(1, POOL_WIDTH))


def _out_proj_kernel(x_ref, gate_ref, a_ref, b_ref, wa_ref, wb_ref, o_ref):
    y = (jnp.dot(a_ref[...], wa_ref[...], preferred_element_type=f32)
         + jnp.dot(b_ref[...], wb_ref[...], preferred_element_type=f32))
    o_ref[...] = x_ref[...] + gate_ref[...] * y


def out_proj(x, mod, a, b, wa, wb, seq, base_row, per_seq):
    t_tokens = x.shape[0]
    tm = 512
    return pl.pallas_call(
        _out_proj_kernel,
        grid=(t_tokens // tm, 1),
        in_specs=[
            pl.BlockSpec((tm, D_MODEL), lambda m, n: (m, 0)),
            _mod_spec(2, tm, seq, base_row, per_seq),
            pl.BlockSpec((tm, a.shape[1]), lambda m, n: (m, 0)),
            pl.BlockSpec((tm, b.shape[1]), lambda m, n: (m, 0)),
            pl.BlockSpec(wa.shape, lambda m, n: (0, 0)),
            pl.BlockSpec(wb.shape, lambda m, n: (0, 0)),
        ],
        out_specs=pl.BlockSpec((tm, D_MODEL), lambda m, n: (m, 0)),
        out_shape=jax.ShapeDtypeStruct((t_tokens, D_MODEL), f32),
        compiler_params=_params("arbitrary", "arbitrary"),
        name="out_proj",
    )(x, mod, a, b, wa, wb)


def _ffn_kernel(tm, seq, nj, final, *refs):
    it = iter(refs)
    (x_ref, xp_ref, xn_ref, shift_ref, scale_ref, gate_ref, g_ref,
     wv_ref, wg_ref, cwv_ref, cwg_ref, cbv_ref, cbg_ref, wd_ref) = (next(it) for _ in range(14))
    gf_ref = next(it) if final else None
    o_ref, h_ref, acc_ref = next(it), next(it), next(it)
    u_refs = [(next(it), next(it)), (next(it), next(it))]
    i = pl.program_id(0)
    j = pl.program_id(1)
    rows = tm + 2 * HALO

    def prologue_and_first_up():
        g, sh, sc = g_ref[...], shift_ref[...], scale_ref[...]
        uv_ref, ug_ref = u_refs[0]

        def emit(rs, h):
            h_ref[rs, :] = h
            uv_ref[rs, :] = jnp.dot(h, wv_ref[...], preferred_element_type=f32)
            ug_ref[rs, :] = jnp.dot(h, wg_ref[...], preferred_element_type=f32)

        for r0 in range(0, tm - IN_ROWS, IN_ROWS):
            rs = slice(r0, r0 + IN_ROWS)
            emit(rs, _rms_modulate(x_ref[rs, :], g, sh, sc).astype(bf16))
        next_ok = jnp.where((((i + 1) * tm) & (seq - 1)) != 0, 1.0, 0.0)
        prev_ok = jnp.where(((i * tm) & (seq - 1)) != 0, 1.0, 0.0)
        h_last = _rms_modulate(x_ref[tm - IN_ROWS:tm, :], g, sh, sc)
        h_next = _rms_modulate(xn_ref[...], g, sh, sc) * next_ok
        h_prev = _rms_modulate(xp_ref[...], g, sh, sc) * prev_ok
        emit(slice(tm - IN_ROWS, rows), jnp.concatenate([h_last, h_next, h_prev], axis=0).astype(bf16))
        acc_ref[...] = jnp.zeros_like(acc_ref)

    def zero_row(a, r):
        r0 = (r // HALO) * HALO
        row8 = lax.broadcasted_iota(jnp.int32, (HALO, 1), 0)
        fixed = jnp.where(row8 == r - r0, 0.0, a[r0:r0 + HALO])
        return jnp.concatenate([a[:r0], fixed, a[r0 + HALO:]], axis=0)

    def window(u_ref, r0):
        lo, hi = r0 - HALO, r0 + FFN_ROWS + HALO
        if lo < 0:
            return jnp.concatenate([u_ref[rows + lo:rows, :], u_ref[0:hi, :]], axis=0)
        return u_ref[lo:hi, :]

    def conv(u_ref, r0, cw_ref, cb_ref):
        u = window(u_ref, r0)
        n = FFN_ROWS + 2 * HALO
        up = pltpu.roll(u, 1, 0)
        un = pltpu.roll(u, n - 1, 0)
        for edge in range(seq, tm, seq):
            if r0 <= edge < r0 + FFN_ROWS:
                up = zero_row(up, edge - r0 + HALO)
            if r0 <= edge - 1 < r0 + FFN_ROWS:
                un = zero_row(un, edge - 1 - r0 + HALO)
        keep = slice(HALO, HALO + FFN_ROWS)
        return (up[keep] * cw_ref[0:1, :] + u[keep] * cw_ref[1:2, :] + un[keep] * cw_ref[2:3, :]
                + cb_ref[...])

    def gated(slot):
        uv_ref, ug_ref = u_refs[slot]
        chunks = []
        for r0 in range(0, tm, FFN_ROWS):
            val = conv(uv_ref, r0, cwv_ref, cbv_ref)
            gate = conv(ug_ref, r0, cwg_ref, cbg_ref)
            chunks.append((gate * jax.nn.sigmoid(gate) * val).astype(bf16))
        return chunks

    def zero_after(chunks):
        parts = []
        for c in chunks:
            bits = pltpu.bitcast(c, jnp.uint32)
            parts += [bits[k:k + HALO] for k in range(0, bits.shape[0], HALO)]
        while len(parts) > 1:
            parts = [a | b for a, b in zip(parts[0::2], parts[1::2])] + parts[len(parts) & ~1:]
        return (parts[0] >> 16) >> 16

    def after(a, zero, row0=0):
        tf = zero.shape[1]
        mid = pltpu.bitcast(pltpu.bitcast(a[row0:row0 + 2 * HALO, 0:tf], jnp.uint32) | zero, bf16)
        if tf < a.shape[1]:
            mid = jnp.concatenate([mid, a[row0:row0 + 2 * HALO, tf:]], axis=1)
        parts = ([a[:row0]] if row0 else []) + [mid, a[row0 + 2 * HALO:]]
        return jnp.concatenate(parts, axis=0)

    def quarter(chunks, k):
        n = len(chunks) // 4
        return chunks[k * n:(k + 1) * n]

    def up_project(slot, chunks=None):
        h = h_ref[...]
        wv, wg = wv_ref[...], wg_ref[...]
        uv_ref, ug_ref = u_refs[slot]
        if chunks is not None:
            k_mid = wv.shape[0] // 2
            wv = after(wv, zero_after(quarter(chunks, 0)), k_mid)
            wg = after(wg, zero_after(quarter(chunks, 2)), k_mid)
        uv_ref[...] = jnp.dot(h, wv, preferred_element_type=f32)
        if chunks is not None:
            h = after(h, zero_after(quarter(chunks, 1)))
        ug_ref[...] = jnp.dot(h, wg, preferred_element_type=f32)

    def down_project(chunks, anchored):
        if anchored:
            chunks = [after(chunks[0], zero_after(quarter(chunks, 3)))] + chunks[1:]
        act = jnp.concatenate(chunks, axis=0)
        acc_ref[...] += jnp.dot(act, wd_ref[...], preferred_element_type=f32)

    @pl.when(j == 0)
    def _():
        prologue_and_first_up()

    for p in range(2):
        @pl.when((j >= 1) & (j < nj) & (j % 2 == p))
        def _(p=p):
            chunks = gated(1 - p)
            up_project(p, chunks)
            down_project(chunks, True)

    @pl.when(j == nj)
    def _():
        down_project(gated((nj - 1) % 2), False)
        y = x_ref[...] + gate_ref[...] * acc_ref[...]
        if final:
            ms = jnp.mean(y * y, axis=-1, keepdims=True)
            y = y * lax.rsqrt(ms + EPS) * gf_ref[...]
        o_ref[...] = y


def conv_ffn(x, mod, g, w_up, conv_w, conv_b, w_down, seq, base_row, per_seq, final_g=None):
    t_tokens = x.shape[0]
    tm, tf = FFN_TOKENS, W_TILE
    assert seq & (seq - 1) == 0 and (tm // FFN_ROWS) % 4 == 0
    nj = D_FF // tf
    hb = tm // HALO
    n_hblk = t_tokens // HALO
    conv_b2 = conv_b.reshape(1, 2 * D_FF)

    def blk(j):
        return jnp.clip(j, 0, nj - 1)

    in_specs = [
        pl.BlockSpec((tm, D_MODEL), lambda i, j: (i, 0)),
        pl.BlockSpec((HALO, D_MODEL), lambda i, j: (jnp.maximum(i * hb - 1, 0), 0)),
        pl.BlockSpec((HALO, D_MODEL), lambda i, j: (jnp.minimum((i + 1) * hb, n_hblk - 1), 0)),
        _mod_spec(3, tm, seq, base_row, per_seq),
        _mod_spec(4, tm, seq, base_row, per_seq),
        _mod_spec(5, tm, seq, base_row, per_seq),
        pl.BlockSpec((1, D_MODEL), lambda i, j: (0, 0)),
        pl.BlockSpec((D_MODEL, tf), lambda i, j: (0, blk(j))),
        pl.BlockSpec((D_MODEL, tf), lambda i, j: (0, nj + blk(j))),
        pl.BlockSpec((3, tf), lambda i, j: (0, blk(j - 1))),
        pl.BlockSpec((3, tf), lambda i, j: (0, nj + blk(j - 1))),
        pl.BlockSpec((1, tf), lambda i, j: (0, blk(j - 1))),
        pl.BlockSpec((1, tf), lambda i, j: (0, nj + blk(j - 1))),
        pl.BlockSpec((tf, D_MODEL), lambda i, j: (blk(j - 1), 0)),
    ]
    args = [x, x, x, mod, mod, mod, g.reshape(1, D_MODEL), w_up, w_up,
            conv_w, conv_w, conv_b2, conv_b2, w_down]
    if final_g is not None:
        in_specs.append(pl.BlockSpec((1, D_MODEL), lambda i, j: (0, 0)))
        args.append(final_g.reshape(1, D_MODEL))
    return pl.pallas_call(
        functools.partial(_ffn_kernel, tm, seq, nj, final_g is not None),
        grid=(t_tokens // tm, nj + 1),
        in_specs=in_specs,
        out_specs=pl.BlockSpec((tm, D_MODEL), lambda i, j: (i, 0)),
        out_shape=jax.ShapeDtypeStruct((t_tokens, D_MODEL), f32),
        scratch_shapes=[pltpu.VMEM((tm + 2 * HALO, D_MODEL), bf16),
                        pltpu.VMEM((tm, D_MODEL), f32)]
                       + [pltpu.VMEM((tm + 2 * HALO, tf), f32)] * 4,
        compiler_params=_params("arbitrary", "arbitrary"),
        name="conv_ffn",
    )(*args)


def _rope_tables(seq):
    t = np.arange(seq)
    half = HEAD_DIM // 2
    inv_freq = np.power(ROPE_BASE, -np.arange(0, half, 2, dtype=np.float64) / half)
    ang_r = (t // GRID_W)[:, None] * inv_freq[None, :]
    ang_c = (t % GRID_W)[:, None] * inv_freq[None, :]
    cos = np.concatenate([np.cos(ang_r)] * 2 + [np.cos(ang_c)] * 2, axis=-1)
    sin = np.concatenate([-np.sin(ang_r), np.sin(ang_r), -np.sin(ang_c), np.sin(ang_c)], axis=-1)
    return jnp.asarray(cos, dtype=f32), jnp.asarray(sin, dtype=f32)


def kernel(x_prompt, x_sample, c, cache_na_k, cache_na_v, cache_gqa_k, cache_gqa_v, c_ctx, norm1_g, norm2_g, ada_w, ada_b, ev_w_in, ev_na_bias, ev_fnet_w, ev_w_out, od_w_in, od_pool_w, od_pool_scale, od_q_norm_g, od_k_norm_g, od_w_out, ffn_w_up, ffn_conv_w, ffn_conv_b, ffn_w_down, final_norm_g):
    n_ctx, seq_p, _ = x_prompt.shape
    n_lat, seq_s, _ = x_sample.shape
    past = cache_na_k.shape[2]
    assert n_lat + 1 <= MOD_ROWS
    xp = x_prompt.reshape(n_ctx * seq_p, D_MODEL)
    xs = x_sample.reshape(n_lat * seq_s, D_MODEL)

    cond = jnp.zeros((MOD_ROWS, D_MODEL), f32).at[0].set(c_ctx).at[1:1 + n_lat].set(c)
    mods = ada_modulation(cond, ada_w, ada_b).reshape(DEPTH, MOD_ROWS * 6, 1, D_MODEL)
    rope = _rope_tables(seq_s)

    ctx = dict(seq=seq_p, base_row=0, per_seq=0)
    lat = dict(seq=seq_s, base_row=1, per_seq=1)
    new_k, new_v = {}, {}
    for i in range(DEPTH):
        j = i // 2
        mod = mods[i]
        w_up = ffn_w_up[i].astype(bf16)
        w_down = ffn_w_down[i].astype(bf16)
        if i % 2 == 0:
            w_in = ev_w_in[j].astype(bf16)
            w_attn = ev_w_out[j, :NA_WIDTH].astype(bf16)
            w_mix = ev_w_out[j, NA_WIDTH:].astype(bf16)
            mix_blk = 3 * NA_WIDTH // FNET_WIDTH
            proj_p = in_proj(xp, mod, norm1_g[i], w_in, f32, NA_WIDTH, **ctx)
            new_k[i] = proj_p[:, NA_WIDTH:2 * NA_WIDTH]
            new_v[i] = proj_p[:, 2 * NA_WIDTH:3 * NA_WIDTH]
            attn_p = ctx_attention(proj_p, seq_p, NA_HEADS, NA_HEADS)
            mix_p = fourier_mix(proj_p, mix_blk, ev_fnet_w[j], seq_p)
            proj_s = in_proj(xs, mod, norm1_g[i], w_in, bf16, NA_WIDTH, **lat)
            attn_s = na_attention(proj_s,
                                  cache_na_k[:, j].reshape(n_lat * past, NA_WIDTH),
                                  cache_na_v[:, j].reshape(n_lat * past, NA_WIDTH),
                                  ev_na_bias[j], seq_s, past)
            mix_s = fourier_mix(proj_s, mix_blk, ev_fnet_w[j], seq_s)
        else:
            w_in = jnp.concatenate([od_w_in[j, :, POOL_WIDTH:], od_w_in[j, :, :POOL_WIDTH]],
                                   axis=1).astype(bf16)
            w_attn = od_w_out[j, POOL_WIDTH:].astype(bf16)
            w_mix = od_w_out[j, :POOL_WIDTH].astype(bf16)
            norm = (od_q_norm_g[j], od_k_norm_g[j], GQA_KV_WIDTH)
            k_lo = GQA_Q_WIDTH
            v_lo = GQA_Q_WIDTH + GQA_KV_WIDTH
            mix_blk = (GQA_Q_WIDTH + 2 * GQA_KV_WIDTH) // POOL_WIDTH
            proj_p = in_proj(xp, mod, norm1_g[i], w_in, f32, GQA_Q_WIDTH, norm=norm, **ctx)
            new_k[i] = proj_p[:, k_lo:k_lo + GQA_KV_WIDTH]
            new_v[i] = proj_p[:, v_lo:v_lo + GQA_KV_WIDTH]
            attn_p = ctx_attention(proj_p, seq_p, GQA_Q_HEADS, GQA_KV_HEADS)
            mix_p = pool_mix(proj_p, mix_blk, od_pool_w[j], od_pool_scale[j], seq_p)
            proj_s = in_proj(xs, mod, norm1_g[i], w_in, bf16, GQA_Q_WIDTH, norm=norm, rope=rope, **lat)
            attn_s = gqa_attention(proj_s,
                                   cache_gqa_k[:, j].reshape(n_lat * past, GQA_KV_WIDTH),
                                   cache_gqa_v[:, j].reshape(n_lat * past, GQA_KV_WIDTH),
                                   seq_s, past)
            mix_s = pool_mix(proj_s, mix_blk, od_pool_w[j], od_pool_scale[j], seq_s)
        xp = out_proj(xp, mod, attn_p, mix_p, w_attn, w_mix, **ctx)
        xs = out_proj(xs, mod, attn_s, mix_s, w_attn, w_mix, **lat)
        final_g = final_norm_g if i == DEPTH - 1 else None
        xp = conv_ffn(xp, mod, norm2_g[i], w_up, ffn_conv_w[i], ffn_conv_b[i], w_down,
                      final_g=final_g, **ctx)
        xs = conv_ffn(xs, mod, norm2_g[i], w_up, ffn_conv_w[i], ffn_conv_b[i], w_down,
                      final_g=final_g, **lat)

    y_prompt = xp.reshape(n_ctx, seq_p, D_MODEL)
    y_sample = xs.reshape(n_lat, seq_s, D_MODEL)
    even = [i for i in range(DEPTH) if i % 2 == 0]
    odd = [i for i in range(DEPTH) if i % 2 == 1]

    def stack(parts, layers, heads):
        return jnp.stack([parts[i].reshape(n_ctx, seq_p, heads, HEAD_DIM) for i in layers], axis=1)

    return (y_prompt, y_sample,
            stack(new_k, even, NA_HEADS), stack(new_v, even, NA_HEADS),
            stack(new_k, odd, GQA_KV_HEADS), stack(new_v, odd, GQA_KV_HEADS))
```

```python
import functools
import math

import jax
import jax.numpy as jnp
import numpy as np
from jax import lax
from jax.experimental import pallas as pl
from jax.experimental.pallas import tpu as pltpu

f32 = jnp.float32
bf16 = jnp.bfloat16

D_MODEL = 2048
DEPTH = 2
GRID_W = 64
HEAD_DIM = 128
NA_HEADS = 12
NA_KR = 8
NA_KC = 16
FNET_GROUPS = 4
FNET_CH = 128
POOL_WINDOWS = (2, 4, 8, 16)
GQA_Q_HEADS = 12
GQA_KV_HEADS = 4
D_FF = 5632
ROPE_BASE = 10000.0
EPS = 1e-6
NEG_INF = -1e30
NA_WIDTH = NA_HEADS * HEAD_DIM
FNET_WIDTH = FNET_GROUPS * FNET_CH
POOL_WIDTH = len(POOL_WINDOWS) * FNET_CH
GQA_Q_WIDTH = GQA_Q_HEADS * HEAD_DIM
GQA_KV_WIDTH = GQA_KV_HEADS * HEAD_DIM
LOG2_E = math.log2(math.e)
ATTN_SCALE = HEAD_DIM ** -0.5 * LOG2_E

MOD_ROWS = 16
VMEM_LIMIT = 56 * 1024 * 1024
HALO = 8
NA_QROWS = 4
NA_WROWS = 12
NA_INVALID = 2 * NA_KR - 1
GQA_QUERIES = 512
NA_HEADS_PER_STEP = 6
FFN_TOKENS = 512
FFN_ROWS = 128
FFN_OUT_TILE = 512
W_TILE = 512
IN_TOKENS = 1024
IN_ROWS = 256


def _params(*sem, flags=None):
    return pltpu.CompilerParams(dimension_semantics=sem, vmem_limit_bytes=VMEM_LIMIT, flags=flags)


def _rms_modulate(x, g, shift, scale):
    ms = jnp.mean(x * x, axis=-1, keepdims=True)
    return (x * lax.rsqrt(ms + EPS)) * (g * (1.0 + scale)) + shift


def _mod_spec(k, tm, seq, base_row, per_seq):
    def index(m, n):
        return ((base_row + per_seq * ((m * tm) // seq)) * 6 + k, 0, 0)
    return pl.BlockSpec((None, 1, D_MODEL), index)


def _ada_kernel(c_ref, w_ref, b_ref, o_ref):
    c = c_ref[...]
    s = (c * jax.nn.sigmoid(c)).astype(bf16)
    o_ref[...] = jnp.dot(s, w_ref[...].astype(bf16), preferred_element_type=f32) + b_ref[...]


def ada_modulation(cond, ada_w, ada_b):
    tn = 1024
    n_out = 6 * D_MODEL
    return pl.pallas_call(
        _ada_kernel,
        grid=(DEPTH, n_out // tn),
        in_specs=[
            pl.BlockSpec((MOD_ROWS, D_MODEL), lambda i, n: (0, 0)),
            pl.BlockSpec((None, D_MODEL, tn), lambda i, n: (i, 0, n)),
            pl.BlockSpec((None, 1, tn), lambda i, n: (i, 0, n)),
        ],
        out_specs=pl.BlockSpec((None, MOD_ROWS, tn), lambda i, n: (i, 0, n)),
        out_shape=jax.ShapeDtypeStruct((DEPTH, MOD_ROWS, n_out), f32),
        compiler_params=_params("arbitrary", "arbitrary"),
        name="ada_modulation",
    )(cond, ada_w, ada_b.reshape(DEPTH, 1, n_out))


def _rope(a, cos, sin):
    lane = lax.broadcasted_iota(jnp.int32, a.shape, 1)
    quarter = HEAD_DIM // 4
    partner = jnp.where((lane & (2 * quarter - 1)) < quarter,
                        pltpu.roll(a, HEAD_DIM - quarter, 1),
                        pltpu.roll(a, quarter, 1))
    return a * cos + partner * sin


def _in_proj_kernel(tn, q_blocks, k_blocks, has_norm, has_rope, *refs):
    it = iter(refs)
    x_ref, shift_ref, scale_ref, g_ref, w_ref = (next(it) for _ in range(5))
    qg_ref = kg_ref = cos_ref = sin_ref = None
    if has_norm:
        qg_ref, kg_ref, avg_ref = next(it), next(it), next(it)
    if has_rope:
        cos_ref, sin_ref = next(it), next(it)
    o_ref, h_ref = next(it), next(it)
    n = pl.program_id(1)
    tm = x_ref.shape[0]
    everything = slice(0, tm)

    def project(rs, h=None):
        h = h_ref[rs, :] if h is None else h
        return jnp.dot(h, w_ref[...], preferred_element_type=f32)

    def store_scaled(rs, acc, scale):
        o_ref[rs, :] = (acc if scale is None else acc * scale).astype(o_ref.dtype)

    def store_normed(rs, acc, gn):
        ms = jnp.dot((acc * acc).astype(bf16), avg_ref[...], preferred_element_type=f32)
        inv = lax.rsqrt(ms + EPS)
        for hh in range(tn // HEAD_DIM):
            sl = slice(hh * HEAD_DIM, (hh + 1) * HEAD_DIM)
            a = acc[:, sl] * inv[:, sl] * gn
            if has_rope:
                a = _rope(a, cos_ref[rs, :], sin_ref[rs, :])
            o_ref[rs, sl] = a.astype(o_ref.dtype)

    @pl.when(n == 0)
    def _():
        g, sh, sc = g_ref[...], shift_ref[...], scale_ref[...]
        for r0 in range(0, tm, IN_ROWS):
            rs = slice(r0, r0 + IN_ROWS)
            h = _rms_modulate(x_ref[rs, :], g, sh, sc).astype(bf16)
            h_ref[rs, :] = h
            if has_norm:
                store_normed(rs, project(rs, h), qg_ref[...] * ATTN_SCALE)
            else:
                store_scaled(rs, project(rs, h), ATTN_SCALE)

    is_q = n < q_blocks
    q_scale = jnp.where(is_q, ATTN_SCALE, 1.0)
    if not has_norm:
        @pl.when(n > 0)
        def _():
            store_scaled(everything, project(everything), q_scale)
        return

    is_qk = n < q_blocks + k_blocks

    @pl.when((n > 0) & is_qk)
    def _():
        gn = jnp.where(is_q, qg_ref[...], kg_ref[...]) * q_scale
        store_normed(everything, project(everything), gn)

    @pl.when(jnp.logical_not(is_qk))
    def _():
        store_scaled(everything, project(everything), None)


def in_proj(x, mod, g, w, out_dtype, q_width, seq, base_row, per_seq, norm=None, rope=None):
    t_tokens = x.shape[0]
    tm, tn = IN_TOKENS, W_TILE
    n_total = w.shape[1]
    in_specs = [
        pl.BlockSpec((tm, D_MODEL), lambda m, n: (m, 0)),
        _mod_spec(0, tm, seq, base_row, per_seq),
        _mod_spec(1, tm, seq, base_row, per_seq),
        pl.BlockSpec((1, D_MODEL), lambda m, n: (0, 0)),
        pl.BlockSpec((D_MODEL, tn), lambda m, n: (0, n)),
    ]
    args = [x, mod, mod, g.reshape(1, D_MODEL), w]
    k_width = 0
    if norm is not None:
        k_width = norm[2]
        in_specs += [pl.BlockSpec((1, HEAD_DIM), lambda m, n: (0, 0))] * 2
        in_specs += [pl.BlockSpec((tn, tn), lambda m, n: (0, 0))]
        head_of = np.arange(tn) // HEAD_DIM
        avg = (head_of[:, None] == head_of[None, :]) / HEAD_DIM
        args += [norm[0].reshape(1, HEAD_DIM), norm[1].reshape(1, HEAD_DIM), jnp.asarray(avg, dtype=bf16)]
    if rope is not None:
        spb = seq // tm
        in_specs += [pl.BlockSpec((tm, HEAD_DIM), lambda m, n: (m % spb, 0))] * 2
        args += [rope[0], rope[1]]
    kern = functools.partial(_in_proj_kernel, tn, q_width // tn, k_width // tn,
                             norm is not None, rope is not None)
    return pl.pallas_call(
        kern,
        grid=(t_tokens // tm, n_total // tn),
        in_specs=in_specs,
        out_specs=pl.BlockSpec((tm, tn), lambda m, n: (m, n)),
        out_shape=jax.ShapeDtypeStruct((t_tokens, n_total), out_dtype),
        scratch_shapes=[pltpu.VMEM((tm, D_MODEL), bf16)],
        compiler_params=_params("arbitrary", "arbitrary"),
        name="in_proj",
    )(*args)


def _with_ones(v):
    return jnp.concatenate([v, jnp.ones_like(v)], axis=1)


def _softmax_pv(parts):
    m = functools.reduce(jnp.maximum, [jnp.max(s, axis=-1, keepdims=True) for s, _ in parts])
    acc = sum(jnp.dot(jnp.exp2(s - m).astype(bf16), v1, preferred_element_type=f32) for s, v1 in parts)
    return acc[:, :HEAD_DIM] / acc[:, HEAD_DIM:]


def _ctx_attn_kernel(n_q, n_kv, q_ref, k_ref, v_ref, o_ref):
    group = n_q // n_kv
    values = [_with_ones(v_ref[:, kv * HEAD_DIM:(kv + 1) * HEAD_DIM].astype(bf16)) for kv in range(n_kv)]
    keys = [k_ref[:, kv * HEAD_DIM:(kv + 1) * HEAD_DIM].astype(bf16) for kv in range(n_kv)]
    dn = (((1,), (1,)), ((), ()))
    scores = [lax.dot_general(q_ref[:, h * HEAD_DIM:(h + 1) * HEAD_DIM].astype(bf16), keys[h // group],
                              dn, preferred_element_type=f32) for h in range(n_q)]
    for h, s in enumerate(scores):
        o = _softmax_pv([(s, values[h // group])])
        o_ref[:, h * HEAD_DIM:(h + 1) * HEAD_DIM] = o.astype(o_ref.dtype)


def ctx_attention(proj, seq, n_q, n_kv):
    t_tokens = proj.shape[0]
    qw, kw = n_q * HEAD_DIM, n_kv * HEAD_DIM
    assert qw % kw == 0
    k_blk = qw // kw
    return pl.pallas_call(
        functools.partial(_ctx_attn_kernel, n_q, n_kv),
        grid=(t_tokens // seq,),
        in_specs=[
            pl.BlockSpec((seq, qw), lambda b: (b, 0)),
            pl.BlockSpec((seq, kw), lambda b: (b, k_blk)),
            pl.BlockSpec((seq, kw), lambda b: (b, k_blk + 1)),
        ],
        out_specs=pl.BlockSpec((seq, qw), lambda b: (b, 0)),
        out_shape=jax.ShapeDtypeStruct((t_tokens, qw), bf16),
        compiler_params=_params("arbitrary"),
        name="ctx_attention",
    )(proj, proj, proj)


def _gqa_attn_kernel(group, q_ref, ck_ref, cv_ref, k_ref, v_ref, o_ref):
    ck = ck_ref[...].astype(bf16)
    cv = _with_ones(cv_ref[...].astype(bf16))
    k = k_ref[...]
    v = _with_ones(v_ref[...])
    dn = (((1,), (1,)), ((), ()))
    scores = []
    for gi in range(group):
        q = q_ref[:, gi * HEAD_DIM:(gi + 1) * HEAD_DIM]
        scores.append((lax.dot_general(q, ck, dn, preferred_element_type=f32),
                       lax.dot_general(q, k, dn, preferred_element_type=f32)))
    for gi, (s_c, s_l) in enumerate(scores):
        o = _softmax_pv([(s_c, cv), (s_l, v)])
        o_ref[:, gi * HEAD_DIM:(gi + 1) * HEAD_DIM] = o.astype(o_ref.dtype)


def gqa_attention(proj, cache_k, cache_v, seq, past):
    t_tokens = proj.shape[0]
    n_b = t_tokens // seq
    group = GQA_Q_HEADS // GQA_KV_HEADS
    tq = GQA_QUERIES
    nq = seq // tq
    return pl.pallas_call(
        functools.partial(_gqa_attn_kernel, group),
        grid=(n_b, GQA_KV_HEADS, nq),
        in_specs=[
            pl.BlockSpec((tq, group * HEAD_DIM), lambda b, h, i: (b * nq + i, h)),
            pl.BlockSpec((past, HEAD_DIM), lambda b, h, i: (b, h)),
            pl.BlockSpec((past, HEAD_DIM), lambda b, h, i: (b, h)),
            pl.BlockSpec((seq, HEAD_DIM), lambda b, h, i: (b, GQA_Q_HEADS + h)),
            pl.BlockSpec((seq, HEAD_DIM), lambda b, h, i: (b, GQA_Q_HEADS + GQA_KV_HEADS + h)),
        ],
        out_specs=pl.BlockSpec((tq, group * HEAD_DIM), lambda b, h, i: (b * nq + i, h)),
        out_shape=jax.ShapeDtypeStruct((t_tokens, GQA_Q_WIDTH), bf16),
        compiler_params=_params("arbitrary", "arbitrary", "arbitrary"),
        name="gqa_attention",
    )(proj, cache_k, cache_v, proj, proj)


def _na_bias_tiles(rel_bias):
    qc = jnp.arange(GRID_W)[:, None]
    kc = jnp.arange(GRID_W)[None, :]
    q_start = jnp.clip(qc - NA_KC // 2, 0, GRID_W - NA_KC)
    valid = (kc >= q_start) & (kc < q_start + NA_KC)
    dc = jnp.clip(kc - qc, -(NA_KC - 1), NA_KC - 1) + NA_KC - 1
    tiles = jnp.where(valid[None, None], rel_bias.astype(f32)[:, :, dc] * LOG2_E, NEG_INF)
    masked = jnp.full((NA_HEADS, 1, GRID_W, GRID_W), NEG_INF, f32)
    tiles = jnp.concatenate([tiles, masked], axis=1)
    zeros = jnp.zeros_like(tiles)
    left = jnp.concatenate([tiles, zeros], axis=-1)
    right = jnp.concatenate([zeros, tiles], axis=-1)
    return left, right


def _na_attn_kernel(n_heads, n_rows, q_ref, k_ref, v_ref, ck_ref, cv_ref, bl_ref, br_ref, o_ref):
    rb = pl.program_id(2)
    start = jnp.clip(rb * NA_QROWS - NA_KR // 2, 0, n_rows - NA_WROWS)
    tok0 = pl.multiple_of(start * GRID_W, GRID_W)
    dn = (((1,), (1,)), ((), ()))

    def tile_index(i, kr):
        qr = rb * NA_QROWS + i
        r0 = jnp.clip(qr - NA_KR // 2, 0, n_rows - NA_KR)
        return jnp.where((kr >= r0) & (kr < r0 + NA_KR), kr - qr + NA_KR - 1, NA_INVALID)

    index = [[(tile_index(i, start + 2 * jp), tile_index(i, start + 2 * jp + 1))
              for jp in range(NA_WROWS // 2)] for i in range(NA_QROWS)]
    scores = []
    for hh in range(n_heads):
        hs = slice(hh * HEAD_DIM, (hh + 1) * HEAD_DIM)
        q = q_ref[:, hs]
        bias = jnp.concatenate(
            [jnp.concatenate([bl_ref[hh, ia] + br_ref[hh, ib] for ia, ib in row], axis=1)
             for row in index], axis=0)
        s_l = lax.dot_general(q, k_ref[pl.ds(tok0, NA_WROWS * GRID_W), hs], dn,
                              preferred_element_type=f32) + bias
        s_c = lax.dot_general(q, ck_ref[:, hs].astype(bf16), dn, preferred_element_type=f32)
        scores.append((s_c, s_l))
    for hh, (s_c, s_l) in enumerate(scores):
        hs = slice(hh * HEAD_DIM, (hh + 1) * HEAD_DIM)
        vw = v_ref[pl.ds(tok0, NA_WROWS * GRID_W), hs]
        o = _softmax_pv([(s_c, _with_ones(cv_ref[:, hs].astype(bf16))), (s_l, _with_ones(vw))])
        o_ref[:, hs] = o.astype(o_ref.dtype)


def na_attention(proj, cache_k, cache_v, rel_bias, seq, past):
    t_tokens = proj.shape[0]
    n_b = t_tokens // seq
    n_rows = seq // GRID_W
    n_rb = n_rows // NA_QROWS
    tq = NA_QROWS * GRID_W
    hb = NA_HEADS_PER_STEP
    n_hg = NA_HEADS // hb
    wide = hb * HEAD_DIM
    bias_l, bias_r = _na_bias_tiles(rel_bias)
    bias_spec = pl.BlockSpec((hb, 2 * NA_KR, GRID_W, 2 * GRID_W), lambda b, h, r: (h, 0, 0, 0))
    return pl.pallas_call(
        functools.partial(_na_attn_kernel, hb, n_rows),
        grid=(n_b, n_hg, n_rb),
        in_specs=[
            pl.BlockSpec((tq, wide), lambda b, h, r: (b * n_rb + r, h)),
            pl.BlockSpec((seq, wide), lambda b, h, r: (b, n_hg + h)),
            pl.BlockSpec((seq, wide), lambda b, h, r: (b, 2 * n_hg + h)),
            pl.BlockSpec((past, wide), lambda b, h, r: (b, h)),
            pl.BlockSpec((past, wide), lambda b, h, r: (b, h)),
            bias_spec,
            bias_spec,
        ],
        out_specs=pl.BlockSpec((tq, wide), lambda b, h, r: (b * n_rb + r, h)),
        out_shape=jax.ShapeDtypeStruct((t_tokens, NA_WIDTH), bf16),
        compiler_params=_params("arbitrary", "arbitrary", "arbitrary"),
        name="na_attention",
    )(proj, proj, proj, cache_k, cache_v, bias_l, bias_r)


def _dft_cols_kernel(x_ref, cs_ref, y_ref):
    for g in range(FNET_GROUPS):
        sl = slice(g * FNET_CH, (g + 1) * FNET_CH)
        y = jnp.dot(x_ref[:, sl].astype(bf16), cs_ref[...], preferred_element_type=f32)
        y_ref[0, :, sl] = y[:, :FNET_CH].astype(bf16)
        y_ref[1, :, sl] = y[:, FNET_CH:].astype(bf16)


def _dft_rows_kernel(d_ref, y_ref, w_ref, o_ref):
    f = jnp.dot(d_ref[...], y_ref[...], preferred_element_type=f32)
    for g in range(FNET_GROUPS):
        sl = slice(g * FNET_CH, (g + 1) * FNET_CH)
        o = jnp.dot(f[:, sl].astype(bf16), w_ref[g], preferred_element_type=f32)
        o_ref[:, sl] = o.astype(o_ref.dtype)


def _dft_tables(seq):
    c = np.arange(FNET_CH, dtype=np.int64)
    ang_c = (2.0 * np.pi / FNET_CH) * ((c[:, None] * c[None, :]) % FNET_CH)
    cs = np.concatenate([np.cos(ang_c), np.sin(ang_c)], axis=1)
    n = np.arange(seq, dtype=np.int64)
    ang_n = (2.0 * np.pi / seq) * ((n[:, None] * n[None, :]) % seq)
    norm = 1.0 / math.sqrt(seq * FNET_CH)
    d = np.concatenate([np.cos(ang_n) * norm, -np.sin(ang_n) * norm], axis=1)
    return jnp.asarray(cs, dtype=bf16), jnp.asarray(d, dtype=bf16)


def fourier_mix(proj, col_blk, fnet_w, seq):
    t_tokens = proj.shape[0]
    n_b = t_tokens // seq
    cs, d = _dft_tables(seq)
    tn = min(seq, 512)
    ns = seq // tn
    y = pl.pallas_call(
        _dft_cols_kernel,
        grid=(n_b, ns),
        in_specs=[
            pl.BlockSpec((tn, FNET_WIDTH), lambda b, i: (b * ns + i, col_blk)),
            pl.BlockSpec((FNET_CH, 2 * FNET_CH), lambda b, i: (0, 0)),
        ],
        out_specs=pl.BlockSpec((2, tn, FNET_WIDTH), lambda b, i: (0, i, b)),
        out_shape=jax.ShapeDtypeStruct((2, seq, n_b * FNET_WIDTH), bf16),
        compiler_params=_params("arbitrary", "arbitrary"),
        name="dft_cols",
    )(proj, cs)
    y2 = y.reshape(2 * seq, n_b * FNET_WIDTH)
    return pl.pallas_call(
        _dft_rows_kernel,
        grid=(n_b, ns),
        in_specs=[
            pl.BlockSpec((tn, 2 * seq), lambda b, i: (i, 0)),
            pl.BlockSpec((2 * seq, FNET_WIDTH), lambda b, i: (0, b)),
            pl.BlockSpec((FNET_GROUPS, FNET_CH, FNET_CH), lambda b, i: (0, 0, 0)),
        ],
        out_specs=pl.BlockSpec((tn, FNET_WIDTH), lambda b, i: (b * ns + i, 0)),
        out_shape=jax.ShapeDtypeStruct((t_tokens, FNET_WIDTH), bf16),
        compiler_params=_params("arbitrary", "arbitrary"),
        name="dft_rows",
    )(d, y2, fnet_w.astype(bf16))


def _pool_kernel(seq, x_ref, w_ref, sc_ref, o_ref):
    t = lax.broadcasted_iota(jnp.int32, (seq, 1), 0)

    def prev(a, k):
        return jnp.where(t >= k, pltpu.roll(a, k, 0), 0.0)

    def nxt(a, k):
        return jnp.where(t < seq - k, pltpu.roll(a, seq - k, 0), 0.0)

    for g, win in enumerate(POOL_WINDOWS):
        sl = slice(g * FNET_CH, (g + 1) * FNET_CH)
        x = x_ref[:, sl].astype(f32)
        half = win // 2
        back, fwd, k = x, x, 1
        while k < half:
            back = back + prev(back, k)
            fwd = fwd + nxt(fwd, k)
            k *= 2
        total = prev(back, 1) + fwd
        cnt = (jnp.minimum(t + half, seq) - jnp.maximum(t - half, 0)).astype(f32)
        pooled = (total / cnt - x).astype(bf16)
        o = jnp.dot(pooled, w_ref[g], preferred_element_type=f32) * sc_ref[:, sl]
        o_ref[:, sl] = o.astype(o_ref.dtype)


def pool_mix(proj, col_blk, pool_w, pool_scale, seq):
    t_tokens = proj.shape[0]
    return pl.pallas_call(
        functools.partial(_pool_kernel, seq),
        grid=(t_tokens // seq,),
        in_specs=[
            pl.BlockSpec((seq, POOL_WIDTH), lambda b: (b, col_blk)),
            pl.BlockSpec((len(POOL_WINDOWS), FNET_CH, FNET_CH), lambda b: (0, 0, 0)),
            pl.BlockSpec((1, POOL_WIDTH), lambda b: (0, 0)),
        ],
        out_specs=pl.BlockSpec((seq, POOL_WIDTH), lambda b: (b, 0)),
        out_shape=jax.ShapeDtypeStruct((t_tokens, POOL_WIDTH), bf16),
        compiler_params=_params("arbitrary"),
        name="pool_mix",
    )(proj, pool_w.astype(bf16), pool_scale.reshape(1, POOL_WIDTH))


def _out_proj_kernel(x_ref, gate_ref, a_ref, b_ref, wa_ref, wb_ref, o_ref):
    y = (jnp.dot(a_ref[...], wa_ref[...], preferred_element_type=f32)
         + jnp.dot(b_ref[...], wb_ref[...], preferred_element_type=f32))
    o_ref[...] = x_ref[...] + gate_ref[...] * y


def out_proj(x, mod, a, b, wa, wb, seq, base_row, per_seq):
    t_tokens = x.shape[0]
    tm = 512
    return pl.pallas_call(
        _out_proj_kernel,
        grid=(t_tokens // tm, 1),
        in_specs=[
            pl.BlockSpec((tm, D_MODEL), lambda m, n: (m, 0)),
            _mod_spec(2, tm, seq, base_row, per_seq),
            pl.BlockSpec((tm, a.shape[1]), lambda m, n: (m, 0)),
            pl.BlockSpec((tm, b.shape[1]), lambda m, n: (m, 0)),
            pl.BlockSpec(wa.shape, lambda m, n: (0, 0)),
            pl.BlockSpec(wb.shape, lambda m, n: (0, 0)),
        ],
        out_specs=pl.BlockSpec((tm, D_MODEL), lambda m, n: (m, 0)),
        out_shape=jax.ShapeDtypeStruct((t_tokens, D_MODEL), f32),
        compiler_params=_params("arbitrary", "arbitrary"),
        name="out_proj",
    )(x, mod, a, b, wa, wb)


def _ffn_kernel(tm, seq, nj, final, *refs):
    it = iter(refs)
    (x_ref, xp_ref, xn_ref, shift_ref, scale_ref, gate_ref, g_ref,
     wv_ref, wg_ref, cwv_ref, cwg_ref, cbv_ref, cbg_ref, wd_ref) = (next(it) for _ in range(14))
    gf_ref = next(it) if final else None
    o_ref, h_ref, act_ref = next(it), next(it), next(it)
    u_refs = [(next(it), next(it)), (next(it), next(it))]
    i = pl.program_id(0)
    j = pl.program_id(1)
    rows = tm + 2 * HALO

    def prologue_and_first_up():
        g, sh, sc = g_ref[...], shift_ref[...], scale_ref[...]
        uv_ref, ug_ref = u_refs[0]

        def emit(rs, h):
            h_ref[rs, :] = h
            uv_ref[rs, :] = jnp.dot(h, wv_ref[...], preferred_element_type=f32)
            ug_ref[rs, :] = jnp.dot(h, wg_ref[...], preferred_element_type=f32)

        for r0 in range(0, tm - IN_ROWS, IN_ROWS):
            rs = slice(r0, r0 + IN_ROWS)
            emit(rs, _rms_modulate(x_ref[rs, :], g, sh, sc).astype(bf16))
        next_ok = jnp.where((((i + 1) * tm) & (seq - 1)) != 0, 1.0, 0.0)
        prev_ok = jnp.where(((i * tm) & (seq - 1)) != 0, 1.0, 0.0)
        h_last = _rms_modulate(x_ref[tm - IN_ROWS:tm, :], g, sh, sc)
        h_next = _rms_modulate(xn_ref[...], g, sh, sc) * next_ok
        h_prev = _rms_modulate(xp_ref[...], g, sh, sc) * prev_ok
        emit(slice(tm - IN_ROWS, rows), jnp.concatenate([h_last, h_next, h_prev], axis=0).astype(bf16))

    def zero_row(a, r):
        r0 = (r // HALO) * HALO
        row8 = lax.broadcasted_iota(jnp.int32, (HALO, 1), 0)
        fixed = jnp.where(row8 == r - r0, 0.0, a[r0:r0 + HALO])
        return jnp.concatenate([a[:r0], fixed, a[r0 + HALO:]], axis=0)

    def window(u_ref, r0):
        lo, hi = r0 - HALO, r0 + FFN_ROWS + HALO
        if lo < 0:
            return jnp.concatenate([u_ref[rows + lo:rows, :], u_ref[0:hi, :]], axis=0)
        return u_ref[lo:hi, :]

    def conv(u_ref, r0, cw_ref, cb_ref):
        u = window(u_ref, r0)
        n = FFN_ROWS + 2 * HALO
        up = pltpu.roll(u, 1, 0)
        un = pltpu.roll(u, n - 1, 0)
        for edge in range(seq, tm, seq):
            if r0 <= edge < r0 + FFN_ROWS:
                up = zero_row(up, edge - r0 + HALO)
            if r0 <= edge - 1 < r0 + FFN_ROWS:
                un = zero_row(un, edge - 1 - r0 + HALO)
        keep = slice(HALO, HALO + FFN_ROWS)
        return (up[keep] * cw_ref[0:1, :] + u[keep] * cw_ref[1:2, :] + un[keep] * cw_ref[2:3, :]
                + cb_ref[...])

    def gated(slot):
        uv_ref, ug_ref = u_refs[slot]
        chunks = []
        for r0 in range(0, tm, FFN_ROWS):
            val = conv(uv_ref, r0, cwv_ref, cbv_ref)
            gate = conv(ug_ref, r0, cwg_ref, cbg_ref)
            chunks.append((gate * jax.nn.sigmoid(gate) * val).astype(bf16))
        return chunks

    def zero_after(chunks):
        parts = []
        for c in chunks:
            bits = pltpu.bitcast(c, jnp.uint32)
            parts += [bits[k:k + HALO] for k in range(0, bits.shape[0], HALO)]
        while len(parts) > 1:
            parts = [a | b for a, b in zip(parts[0::2], parts[1::2])] + parts[len(parts) & ~1:]
        return (parts[0] >> 16) >> 16

    def after(a, zero, row0=0):
        tf = zero.shape[1]
        mid = pltpu.bitcast(pltpu.bitcast(a[row0:row0 + 2 * HALO, 0:tf], jnp.uint32) | zero, bf16)
        if tf < a.shape[1]:
            mid = jnp.concatenate([mid, a[row0:row0 + 2 * HALO, tf:]], axis=1)
        parts = ([a[:row0]] if row0 else []) + [mid, a[row0 + 2 * HALO:]]
        return jnp.concatenate(parts, axis=0)

    def quarter(chunks, k):
        n = len(chunks) // 4
        return chunks[k * n:(k + 1) * n]

    def up_project(slot, chunks):
        h = h_ref[...]
        wv, wg = wv_ref[...], wg_ref[...]
        uv_ref, ug_ref = u_refs[slot]
        k_mid = wv.shape[0] // 2
        wv = after(wv, zero_after(quarter(chunks, 0)), k_mid)
        wg = after(wg, zero_after(quarter(chunks, 2)), k_mid)
        wg = after(wg, zero_after(quarter(chunks, 3)), k_mid + k_mid // 2)
        uv_ref[...] = jnp.dot(h, wv, preferred_element_type=f32)
        h = after(h, zero_after(quarter(chunks, 1)))
        ug_ref[...] = jnp.dot(h, wg, preferred_element_type=f32)

    @pl.when(j == 0)
    def _():
        prologue_and_first_up()

    for p in range(2):
        @pl.when((j >= 1) & (j < nj) & (j % 2 == p))
        def _(p=p):
            chunks = gated(1 - p)
            up_project(p, chunks)
            act_ref[j - 1] = jnp.concatenate(chunks, axis=0)

    dn = wd_ref.shape[1]
    tf = wv_ref.shape[1]
    n_out = o_ref.shape[1] // dn
    for n in range(n_out):
        @pl.when(j == nj + n)
        def _(n=n):
            cols = slice(n * dn, (n + 1) * dn)
            blocks = [act_ref[k] for k in range(nj - 1)]
            wd = wd_ref[...]
            if n == 0:
                chunks = gated((nj - 1) % 2)
                for k in range(4):
                    wd = after(wd, zero_after(quarter(chunks, k)), (2 * k + 2) * tf)
                last = jnp.concatenate(chunks, axis=0)
                act_ref[nj - 1] = last
                blocks.append(last)
            else:
                blocks.append(act_ref[nj - 1])
            y = jnp.dot(jnp.concatenate(blocks, axis=1), wd, preferred_element_type=f32)
            y = x_ref[:, cols] + gate_ref[:, cols] * y
            if final and n == n_out - 1:
                o_ref[:, cols] = y
                full = o_ref[...]
                ms = jnp.mean(full * full, axis=-1, keepdims=True)
                o_ref[...] = full * lax.rsqrt(ms + EPS) * gf_ref[...]
            else:
                o_ref[:, cols] = y


def conv_ffn(x, mod, g, w_up, conv_w, conv_b, w_down, seq, base_row, per_seq, final_g=None):
    t_tokens = x.shape[0]
    tm, tf = FFN_TOKENS, W_TILE
    assert seq & (seq - 1) == 0 and (tm // FFN_ROWS) % 4 == 0
    nj = D_FF // tf
    dn = FFN_OUT_TILE
    n_out = D_MODEL // dn
    hb = tm // HALO
    n_hblk = t_tokens // HALO
    conv_b2 = conv_b.reshape(1, 2 * D_FF)

    def blk(j):
        return jnp.clip(j, 0, nj - 1)

    in_specs = [
        pl.BlockSpec((tm, D_MODEL), lambda i, j: (i, 0)),
        pl.BlockSpec((HALO, D_MODEL), lambda i, j: (jnp.maximum(i * hb - 1, 0), 0)),
        pl.BlockSpec((HALO, D_MODEL), lambda i, j: (jnp.minimum((i + 1) * hb, n_hblk - 1), 0)),
        _mod_spec(3, tm, seq, base_row, per_seq),
        _mod_spec(4, tm, seq, base_row, per_seq),
        _mod_spec(5, tm, seq, base_row, per_seq),
        pl.BlockSpec((1, D_MODEL), lambda i, j: (0, 0)),
        pl.BlockSpec((D_MODEL, tf), lambda i, j: (0, blk(j))),
        pl.BlockSpec((D_MODEL, tf), lambda i, j: (0, nj + blk(j))),
        pl.BlockSpec((3, tf), lambda i, j: (0, blk(j - 1))),
        pl.BlockSpec((3, tf), lambda i, j: (0, nj + blk(j - 1))),
        pl.BlockSpec((1, tf), lambda i, j: (0, blk(j - 1))),
        pl.BlockSpec((1, tf), lambda i, j: (0, nj + blk(j - 1))),
        pl.BlockSpec((D_FF, dn), lambda i, j: (0, jnp.clip(j - nj, 0, n_out - 1))),
    ]
    args = [x, x, x, mod, mod, mod, g.reshape(1, D_MODEL), w_up, w_up,
            conv_w, conv_w, conv_b2, conv_b2, w_down]
    if final_g is not None:
        in_specs.append(pl.BlockSpec((1, D_MODEL), lambda i, j: (0, 0)))
        args.append(final_g.reshape(1, D_MODEL))
    return pl.pallas_call(
        functools.partial(_ffn_kernel, tm, seq, nj, final_g is not None),
        grid=(t_tokens // tm, nj + n_out),
        in_specs=in_specs,
        out_specs=pl.BlockSpec((tm, D_MODEL), lambda i, j: (i, 0)),
        out_shape=jax.ShapeDtypeStruct((t_tokens, D_MODEL), f32),
        scratch_shapes=[pltpu.VMEM((tm + 2 * HALO, D_MODEL), bf16),
                        pltpu.VMEM((nj, tm, tf), bf16)]
                       + [pltpu.VMEM((tm + 2 * HALO, tf), f32)] * 4,
        compiler_params=_params("arbitrary", "arbitrary"),
        name="conv_ffn",
    )(*args)


def _rope_tables(seq):
    t = np.arange(seq)
    half = HEAD_DIM // 2
    inv_freq = np.power(ROPE_BASE, -np.arange(0, half, 2, dtype=np.float64) / half)
    ang_r = (t // GRID_W)[:, None] * inv_freq[None, :]
    ang_c = (t % GRID_W)[:, None] * inv_freq[None, :]
    cos = np.concatenate([np.cos(ang_r)] * 2 + [np.cos(ang_c)] * 2, axis=-1)
    sin = np.concatenate([-np.sin(ang_r), np.sin(ang_r), -np.sin(ang_c), np.sin(ang_c)], axis=-1)
    return jnp.asarray(cos, dtype=f32), jnp.asarray(sin, dtype=f32)


def kernel(x_prompt, x_sample, c, cache_na_k, cache_na_v, cache_gqa_k, cache_gqa_v, c_ctx, norm1_g, norm2_g, ada_w, ada_b, ev_w_in, ev_na_bias, ev_fnet_w, ev_w_out, od_w_in, od_pool_w, od_pool_scale, od_q_norm_g, od_k_norm_g, od_w_out, ffn_w_up, ffn_conv_w, ffn_conv_b, ffn_w_down, final_norm_g):
    n_ctx, seq_p, _ = x_prompt.shape
    n_lat, seq_s, _ = x_sample.shape
    past = cache_na_k.shape[2]
    assert n_lat + 1 <= MOD_ROWS
    xp = x_prompt.reshape(n_ctx * seq_p, D_MODEL)
    xs = x_sample.reshape(n_lat * seq_s, D_MODEL)

    cond = jnp.zeros((MOD_ROWS, D_MODEL), f32).at[0].set(c_ctx).at[1:1 + n_lat].set(c)
    mods = ada_modulation(cond, ada_w, ada_b).reshape(DEPTH, MOD_ROWS * 6, 1, D_MODEL)
    rope = _rope_tables(seq_s)

    ctx = dict(seq=seq_p, base_row=0, per_seq=0)
    lat = dict(seq=seq_s, base_row=1, per_seq=1)
    new_k, new_v = {}, {}
    for i in range(DEPTH):
        j = i // 2
        mod = mods[i]
        w_up = ffn_w_up[i].astype(bf16)
        w_down = ffn_w_down[i].astype(bf16)
        if i % 2 == 0:
            w_in = ev_w_in[j].astype(bf16)
            w_attn = ev_w_out[j, :NA_WIDTH].astype(bf16)
            w_mix = ev_w_out[j, NA_WIDTH:].astype(bf16)
            mix_blk = 3 * NA_WIDTH // FNET_WIDTH
            proj_p = in_proj(xp, mod, norm1_g[i], w_in, f32, NA_WIDTH, **ctx)
            new_k[i] = proj_p[:, NA_WIDTH:2 * NA_WIDTH]
            new_v[i] = proj_p[:, 2 * NA_WIDTH:3 * NA_WIDTH]
            attn_p = ctx_attention(proj_p, seq_p, NA_HEADS, NA_HEADS)
            mix_p = fourier_mix(proj_p, mix_blk, ev_fnet_w[j], seq_p)
            proj_s = in_proj(xs, mod, norm1_g[i], w_in, bf16, NA_WIDTH, **lat)
            attn_s = na_attention(proj_s,
                                  cache_na_k[:, j].reshape(n_lat * past, NA_WIDTH),
                                  cache_na_v[:, j].reshape(n_lat * past, NA_WIDTH),
                                  ev_na_bias[j], seq_s, past)
            mix_s = fourier_mix(proj_s, mix_blk, ev_fnet_w[j], seq_s)
        else:
            w_in = jnp.concatenate([od_w_in[j, :, POOL_WIDTH:], od_w_in[j, :, :POOL_WIDTH]],
                                   axis=1).astype(bf16)
            w_attn = od_w_out[j, POOL_WIDTH:].astype(bf16)
            w_mix = od_w_out[j, :POOL_WIDTH].astype(bf16)
            norm = (od_q_norm_g[j], od_k_norm_g[j], GQA_KV_WIDTH)
            k_lo = GQA_Q_WIDTH
            v_lo = GQA_Q_WIDTH + GQA_KV_WIDTH
            mix_blk = (GQA_Q_WIDTH + 2 * GQA_KV_WIDTH) // POOL_WIDTH
            proj_p = in_proj(xp, mod, norm1_g[i], w_in, f32, GQA_Q_WIDTH, norm=norm, **ctx)
            new_k[i] = proj_p[:, k_lo:k_lo + GQA_KV_WIDTH]
            new_v[i] = proj_p[:, v_lo:v_lo + GQA_KV_WIDTH]
            attn_p = ctx_attention(proj_p, seq_p, GQA_Q_HEADS, GQA_KV_HEADS)
            mix_p = pool_mix(proj_p, mix_blk, od_pool_w[j], od_pool_scale[j], seq_p)
            proj_s = in_proj(xs, mod, norm1_g[i], w_in, bf16, GQA_Q_WIDTH, norm=norm, rope=rope, **lat)
            attn_s = gqa_attention(proj_s,
                                   cache_gqa_k[:, j].reshape(n_lat * past, GQA_KV_WIDTH),
                                   cache_gqa_v[:, j].reshape(n_lat * past, GQA_KV_WIDTH),
                                   seq_s, past)
            mix_s = pool_mix(proj_s, mix_blk, od_pool_w[j], od_pool_scale[j], seq_s)
        xp = out_proj(xp, mod, attn_p, mix_p, w_attn, w_mix, **ctx)
        xs = out_proj(xs, mod, attn_s, mix_s, w_attn, w_mix, **lat)
        final_g = final_norm_g if i == DEPTH - 1 else None
        xp = conv_ffn(xp, mod, norm2_g[i], w_up, ffn_conv_w[i], ffn_conv_b[i], w_down,
                      final_g=final_g, **ctx)
        xs = conv_ffn(xs, mod, norm2_g[i], w_up, ffn_conv_w[i], ffn_conv_b[i], w_down,
                      final_g=final_g, **lat)

    y_prompt = xp.reshape(n_ctx, seq_p, D_MODEL)
    y_sample = xs.reshape(n_lat, seq_s, D_MODEL)
    even = [i for i in range(DEPTH) if i % 2 == 0]
    odd = [i for i in range(DEPTH) if i % 2 == 1]

    def stack(parts, layers, heads):
        return jnp.stack([parts[i].reshape(n_ctx, seq_p, heads, HEAD_DIM) for i in layers], axis=1)

    return (y_prompt, y_sample,
            stack(new_k, even, NA_HEADS), stack(new_v, even, NA_HEADS),
            stack(new_k, odd, GQA_KV_HEADS), stack(new_v, odd, GQA_KV_HEADS))
```

```python
import functools
import math

import jax
import jax.numpy as jnp
import numpy as np
from jax import lax
from jax.experimental import pallas as pl
from jax.experimental.pallas import tpu as pltpu

f32 = jnp.float32
bf16 = jnp.bfloat16

D_MODEL = 2048
DEPTH = 2
GRID_W = 64
HEAD_DIM = 128
NA_HEADS = 12
NA_KR = 8
NA_KC = 16
FNET_GROUPS = 4
FNET_CH = 128
POOL_WINDOWS = (2, 4, 8, 16)
GQA_Q_HEADS = 12
GQA_KV_HEADS = 4
D_FF = 5632
ROPE_BASE = 10000.0
EPS = 1e-6
NEG_INF = -1e30
NA_WIDTH = NA_HEADS * HEAD_DIM
FNET_WIDTH = FNET_GROUPS * FNET_CH
POOL_WIDTH = len(POOL_WINDOWS) * FNET_CH
GQA_Q_WIDTH = GQA_Q_HEADS * HEAD_DIM
GQA_KV_WIDTH = GQA_KV_HEADS * HEAD_DIM
LOG2_E = math.log2(math.e)
ATTN_SCALE = HEAD_DIM ** -0.5 * LOG2_E

MOD_ROWS = 16
VMEM_LIMIT = 60 * 1024 * 1024
HALO = 8
NA_QROWS = 4
NA_WROWS = 12
NA_INVALID = 2 * NA_KR - 1
GQA_QUERIES = 512
NA_HEADS_PER_STEP = 6
FFN_TOKENS = 1024
FFN_ROWS = 128
W_TILE = 512
IN_TOKENS = 1024
IN_ROWS = 256


def _params(*sem, flags=None):
    return pltpu.CompilerParams(dimension_semantics=sem, vmem_limit_bytes=VMEM_LIMIT, flags=flags)


def _rms_modulate(x, g, shift, scale):
    ms = jnp.mean(x * x, axis=-1, keepdims=True)
    return (x * lax.rsqrt(ms + EPS)) * (g * (1.0 + scale)) + shift


def _mod_spec(k, tm, seq, base_row, per_seq):
    def index(m, n):
        return ((base_row + per_seq * ((m * tm) // seq)) * 6 + k, 0, 0)
    return pl.BlockSpec((None, 1, D_MODEL), index)


def _ada_kernel(c_ref, w_ref, b_ref, o_ref):
    c = c_ref[...]
    s = (c * jax.nn.sigmoid(c)).astype(bf16)
    o_ref[...] = jnp.dot(s, w_ref[...].astype(bf16), preferred_element_type=f32) + b_ref[...]


def ada_modulation(cond, ada_w, ada_b):
    tn = 1024
    n_out = 6 * D_MODEL
    return pl.pallas_call(
        _ada_kernel,
        grid=(DEPTH, n_out // tn),
        in_specs=[
            pl.BlockSpec((MOD_ROWS, D_MODEL), lambda i, n: (0, 0)),
            pl.BlockSpec((None, D_MODEL, tn), lambda i, n: (i, 0, n)),
            pl.BlockSpec((None, 1, tn), lambda i, n: (i, 0, n)),
        ],
        out_specs=pl.BlockSpec((None, MOD_ROWS, tn), lambda i, n: (i, 0, n)),
        out_shape=jax.ShapeDtypeStruct((DEPTH, MOD_ROWS, n_out), f32),
        compiler_params=_params("arbitrary", "arbitrary"),
        name="ada_modulation",
    )(cond, ada_w, ada_b.reshape(DEPTH, 1, n_out))


def _rope(a, cos, sin):
    lane = lax.broadcasted_iota(jnp.int32, a.shape, 1)
    quarter = HEAD_DIM // 4
    partner = jnp.where((lane & (2 * quarter - 1)) < quarter,
                        pltpu.roll(a, HEAD_DIM - quarter, 1),
                        pltpu.roll(a, quarter, 1))
    return a * cos + partner * sin


def _in_proj_kernel(tn, q_blocks, k_blocks, has_norm, has_rope, *refs):
    it = iter(refs)
    x_ref, shift_ref, scale_ref, g_ref, w_ref = (next(it) for _ in range(5))
    qg_ref = kg_ref = cos_ref = sin_ref = None
    if has_norm:
        qg_ref, kg_ref, avg_ref = next(it), next(it), next(it)
    if has_rope:
        cos_ref, sin_ref = next(it), next(it)
    o_ref, h_ref = next(it), next(it)
    n = pl.program_id(1)
    tm = x_ref.shape[0]
    everything = slice(0, tm)

    def project(rs, h=None):
        h = h_ref[rs, :] if h is None else h
        return jnp.dot(h, w_ref[...], preferred_element_type=f32)

    def store_scaled(rs, acc, scale):
        o_ref[rs, :] = (acc if scale is None else acc * scale).astype(o_ref.dtype)

    def store_normed(rs, acc, gn):
        ms = jnp.dot((acc * acc).astype(bf16), avg_ref[...], preferred_element_type=f32)
        inv = lax.rsqrt(ms + EPS)
        for hh in range(tn // HEAD_DIM):
            sl = slice(hh * HEAD_DIM, (hh + 1) * HEAD_DIM)
            a = acc[:, sl] * inv[:, sl] * gn
            if has_rope:
                a = _rope(a, cos_ref[rs, :], sin_ref[rs, :])
            o_ref[rs, sl] = a.astype(o_ref.dtype)

    @pl.when(n == 0)
    def _():
        g, sh, sc = g_ref[...], shift_ref[...], scale_ref[...]
        for r0 in range(0, tm, IN_ROWS):
            rs = slice(r0, r0 + IN_ROWS)
            h = _rms_modulate(x_ref[rs, :], g, sh, sc).astype(bf16)
            h_ref[rs, :] = h
            if has_norm:
                store_normed(rs, project(rs, h), qg_ref[...] * ATTN_SCALE)
            else:
                store_scaled(rs, project(rs, h), ATTN_SCALE)

    is_q = n < q_blocks
    q_scale = jnp.where(is_q, ATTN_SCALE, 1.0)
    if not has_norm:
        @pl.when(n > 0)
        def _():
            store_scaled(everything, project(everything), q_scale)
        return

    is_qk = n < q_blocks + k_blocks

    @pl.when((n > 0) & is_qk)
    def _():
        gn = jnp.where(is_q, qg_ref[...], kg_ref[...]) * q_scale
        store_normed(everything, project(everything), gn)

    @pl.when(jnp.logical_not(is_qk))
    def _():
        store_scaled(everything, project(everything), None)


def in_proj(x, mod, g, w, out_dtype, q_width, seq, base_row, per_seq, norm=None, rope=None):
    t_tokens = x.shape[0]
    tm, tn = IN_TOKENS, W_TILE
    n_total = w.shape[1]
    in_specs = [
        pl.BlockSpec((tm, D_MODEL), lambda m, n: (m, 0)),
        _mod_spec(0, tm, seq, base_row, per_seq),
        _mod_spec(1, tm, seq, base_row, per_seq),
        pl.BlockSpec((1, D_MODEL), lambda m, n: (0, 0)),
        pl.BlockSpec((D_MODEL, tn), lambda m, n: (0, n)),
    ]
    args = [x, mod, mod, g.reshape(1, D_MODEL), w]
    k_width = 0
    if norm is not None:
        k_width = norm[2]
        in_specs += [pl.BlockSpec((1, HEAD_DIM), lambda m, n: (0, 0))] * 2
        in_specs += [pl.BlockSpec((tn, tn), lambda m, n: (0, 0))]
        head_of = np.arange(tn) // HEAD_DIM
        avg = (head_of[:, None] == head_of[None, :]) / HEAD_DIM
        args += [norm[0].reshape(1, HEAD_DIM), norm[1].reshape(1, HEAD_DIM), jnp.asarray(avg, dtype=bf16)]
    if rope is not None:
        spb = seq // tm
        in_specs += [pl.BlockSpec((tm, HEAD_DIM), lambda m, n: (m % spb, 0))] * 2
        args += [rope[0], rope[1]]
    kern = functools.partial(_in_proj_kernel, tn, q_width // tn, k_width // tn,
                             norm is not None, rope is not None)
    return pl.pallas_call(
        kern,
        grid=(t_tokens // tm, n_total // tn),
        in_specs=in_specs,
        out_specs=pl.BlockSpec((tm, tn), lambda m, n: (m, n)),
        out_shape=jax.ShapeDtypeStruct((t_tokens, n_total), out_dtype),
        scratch_shapes=[pltpu.VMEM((tm, D_MODEL), bf16)],
        compiler_params=_params("arbitrary", "arbitrary"),
        name="in_proj",
    )(*args)


def _with_ones(v):
    return jnp.concatenate([v, jnp.ones_like(v)], axis=1)


def _softmax_pv(parts):
    m = functools.reduce(jnp.maximum, [jnp.max(s, axis=-1, keepdims=True) for s, _ in parts])
    acc = sum(jnp.dot(jnp.exp2(s - m).astype(bf16), v1, preferred_element_type=f32) for s, v1 in parts)
    return acc[:, :HEAD_DIM] / acc[:, HEAD_DIM:]


def _ctx_attn_kernel(n_q, n_kv, q_ref, k_ref, v_ref, o_ref):
    group = n_q // n_kv
    values = [_with_ones(v_ref[:, kv * HEAD_DIM:(kv + 1) * HEAD_DIM].astype(bf16)) for kv in range(n_kv)]
    keys = [k_ref[:, kv * HEAD_DIM:(kv + 1) * HEAD_DIM].astype(bf16) for kv in range(n_kv)]
    dn = (((1,), (1,)), ((), ()))
    scores = [lax.dot_general(q_ref[:, h * HEAD_DIM:(h + 1) * HEAD_DIM].astype(bf16), keys[h // group],
                              dn, preferred_element_type=f32) for h in range(n_q)]
    for h, s in enumerate(scores):
        o = _softmax_pv([(s, values[h // group])])
        o_ref[:, h * HEAD_DIM:(h + 1) * HEAD_DIM] = o.astype(o_ref.dtype)


def ctx_attention(proj, seq, n_q, n_kv):
    t_tokens = proj.shape[0]
    qw, kw = n_q * HEAD_DIM, n_kv * HEAD_DIM
    assert qw % kw == 0
    k_blk = qw // kw
    return pl.pallas_call(
        functools.partial(_ctx_attn_kernel, n_q, n_kv),
        grid=(t_tokens // seq,),
        in_specs=[
            pl.BlockSpec((seq, qw), lambda b: (b, 0)),
            pl.BlockSpec((seq, kw), lambda b: (b, k_blk)),
            pl.BlockSpec((seq, kw), lambda b: (b, k_blk + 1)),
        ],
        out_specs=pl.BlockSpec((seq, qw), lambda b: (b, 0)),
        out_shape=jax.ShapeDtypeStruct((t_tokens, qw), bf16),
        compiler_params=_params("arbitrary"),
        name="ctx_attention",
    )(proj, proj, proj)


def _gqa_attn_kernel(group, q_ref, ck_ref, cv_ref, k_ref, v_ref, o_ref):
    ck = ck_ref[...].astype(bf16)
    cv = _with_ones(cv_ref[...].astype(bf16))
    k = k_ref[...]
    v = _with_ones(v_ref[...])
    dn = (((1,), (1,)), ((), ()))
    scores = []
    for gi in range(group):
        q = q_ref[:, gi * HEAD_DIM:(gi + 1) * HEAD_DIM]
        scores.append((lax.dot_general(q, ck, dn, preferred_element_type=f32),
                       lax.dot_general(q, k, dn, preferred_element_type=f32)))
    for gi, (s_c, s_l) in enumerate(scores):
        o = _softmax_pv([(s_c, cv), (s_l, v)])
        o_ref[:, gi * HEAD_DIM:(gi + 1) * HEAD_DIM] = o.astype(o_ref.dtype)


def gqa_attention(proj, cache_k, cache_v, seq, past):
    t_tokens = proj.shape[0]
    n_b = t_tokens // seq
    group = GQA_Q_HEADS // GQA_KV_HEADS
    tq = GQA_QUERIES
    nq = seq // tq
    return pl.pallas_call(
        functools.partial(_gqa_attn_kernel, group),
        grid=(n_b, GQA_KV_HEADS, nq),
        in_specs=[
            pl.BlockSpec((tq, group * HEAD_DIM), lambda b, h, i: (b * nq + i, h)),
            pl.BlockSpec((past, HEAD_DIM), lambda b, h, i: (b, h)),
            pl.BlockSpec((past, HEAD_DIM), lambda b, h, i: (b, h)),
            pl.BlockSpec((seq, HEAD_DIM), lambda b, h, i: (b, GQA_Q_HEADS + h)),
            pl.BlockSpec((seq, HEAD_DIM), lambda b, h, i: (b, GQA_Q_HEADS + GQA_KV_HEADS + h)),
        ],
        out_specs=pl.BlockSpec((tq, group * HEAD_DIM), lambda b, h, i: (b * nq + i, h)),
        out_shape=jax.ShapeDtypeStruct((t_tokens, GQA_Q_WIDTH), bf16),
        compiler_params=_params("arbitrary", "arbitrary", "arbitrary"),
        name="gqa_attention",
    )(proj, cache_k, cache_v, proj, proj)


def _na_bias_tiles(rel_bias):
    qc = jnp.arange(GRID_W)[:, None]
    kc = jnp.arange(GRID_W)[None, :]
    q_start = jnp.clip(qc - NA_KC // 2, 0, GRID_W - NA_KC)
    valid = (kc >= q_start) & (kc < q_start + NA_KC)
    dc = jnp.clip(kc - qc, -(NA_KC - 1), NA_KC - 1) + NA_KC - 1
    tiles = jnp.where(valid[None, None], rel_bias.astype(f32)[:, :, dc] * LOG2_E, NEG_INF)
    masked = jnp.full((NA_HEADS, 1, GRID_W, GRID_W), NEG_INF, f32)
    tiles = jnp.concatenate([tiles, masked], axis=1)
    zeros = jnp.zeros_like(tiles)
    left = jnp.concatenate([tiles, zeros], axis=-1)
    right = jnp.concatenate([zeros, tiles], axis=-1)
    return left, right


def _na_attn_kernel(n_heads, n_rows, q_ref, k_ref, v_ref, ck_ref, cv_ref, bl_ref, br_ref, o_ref):
    rb = pl.program_id(2)
    start = jnp.clip(rb * NA_QROWS - NA_KR // 2, 0, n_rows - NA_WROWS)
    tok0 = pl.multiple_of(start * GRID_W, GRID_W)
    dn = (((1,), (1,)), ((), ()))

    def tile_index(i, kr):
        qr = rb * NA_QROWS + i
        r0 = jnp.clip(qr - NA_KR // 2, 0, n_rows - NA_KR)
        return jnp.where((kr >= r0) & (kr < r0 + NA_KR), kr - qr + NA_KR - 1, NA_INVALID)

    index = [[(tile_index(i, start + 2 * jp), tile_index(i, start + 2 * jp + 1))
              for jp in range(NA_WROWS // 2)] for i in range(NA_QROWS)]
    scores = []
    for hh in range(n_heads):
        hs = slice(hh * HEAD_DIM, (hh + 1) * HEAD_DIM)
        q = q_ref[:, hs]
        bias = jnp.concatenate(
            [jnp.concatenate([bl_ref[hh, ia] + br_ref[hh, ib] for ia, ib in row], axis=1)
             for row in index], axis=0)
        s_l = lax.dot_general(q, k_ref[pl.ds(tok0, NA_WROWS * GRID_W), hs], dn,
                              preferred_element_type=f32) + bias
        s_c = lax.dot_general(q, ck_ref[:, hs].astype(bf16), dn, preferred_element_type=f32)
        scores.append((s_c, s_l))
    for hh, (s_c, s_l) in enumerate(scores):
        hs = slice(hh * HEAD_DIM, (hh + 1) * HEAD_DIM)
        vw = v_ref[pl.ds(tok0, NA_WROWS * GRID_W), hs]
        o = _softmax_pv([(s_c, _with_ones(cv_ref[:, hs].astype(bf16))), (s_l, _with_ones(vw))])
        o_ref[:, hs] = o.astype(o_ref.dtype)


def na_attention(proj, cache_k, cache_v, rel_bias, seq, past):
    t_tokens = proj.shape[0]
    n_b = t_tokens // seq
    n_rows = seq // GRID_W
    n_rb = n_rows // NA_QROWS
    tq = NA_QROWS * GRID_W
    hb = NA_HEADS_PER_STEP
    n_hg = NA_HEADS // hb
    wide = hb * HEAD_DIM
    bias_l, bias_r = _na_bias_tiles(rel_bias)
    bias_spec = pl.BlockSpec((hb, 2 * NA_KR, GRID_W, 2 * GRID_W), lambda b, h, r: (h, 0, 0, 0))
    return pl.pallas_call(
        functools.partial(_na_attn_kernel, hb, n_rows),
        grid=(n_b, n_hg, n_rb),
        in_specs=[
            pl.BlockSpec((tq, wide), lambda b, h, r: (b * n_rb + r, h)),
            pl.BlockSpec((seq, wide), lambda b, h, r: (b, n_hg + h)),
            pl.BlockSpec((seq, wide), lambda b, h, r: (b, 2 * n_hg + h)),
            pl.BlockSpec((past, wide), lambda b, h, r: (b, h)),
            pl.BlockSpec((past, wide), lambda b, h, r: (b, h)),
            bias_spec,
            bias_spec,
        ],
        out_specs=pl.BlockSpec((tq, wide), lambda b, h, r: (b * n_rb + r, h)),
        out_shape=jax.ShapeDtypeStruct((t_tokens, NA_WIDTH), bf16),
        compiler_params=_params("arbitrary", "arbitrary", "arbitrary"),
        name="na_attention",
    )(proj, proj, proj, cache_k, cache_v, bias_l, bias_r)


def _dft_cols_kernel(x_ref, cs_ref, y_ref):
    for g in range(FNET_GROUPS):
        sl = slice(g * FNET_CH, (g + 1) * FNET_CH)
        y = jnp.dot(x_ref[:, sl].astype(bf16), cs_ref[...], preferred_element_type=f32)
        y_ref[0, :, sl] = y[:, :FNET_CH].astype(bf16)
        y_ref[1, :, sl] = y[:, FNET_CH:].astype(bf16)


def _dft_rows_kernel(d_ref, y_ref, w_ref, o_ref):
    f = jnp.dot(d_ref[...], y_ref[...], preferred_element_type=f32)
    for g in range(FNET_GROUPS):
        sl = slice(g * FNET_CH, (g + 1) * FNET_CH)
        o = jnp.dot(f[:, sl].astype(bf16), w_ref[g], preferred_element_type=f32)
        o_ref[:, sl] = o.astype(o_ref.dtype)


def _dft_tables(seq):
    c = np.arange(FNET_CH, dtype=np.int64)
    ang_c = (2.0 * np.pi / FNET_CH) * ((c[:, None] * c[None, :]) % FNET_CH)
    cs = np.concatenate([np.cos(ang_c), np.sin(ang_c)], axis=1)
    n = np.arange(seq, dtype=np.int64)
    ang_n = (2.0 * np.pi / seq) * ((n[:, None] * n[None, :]) % seq)
    norm = 1.0 / math.sqrt(seq * FNET_CH)
    d = np.concatenate([np.cos(ang_n) * norm, -np.sin(ang_n) * norm], axis=1)
    return jnp.asarray(cs, dtype=bf16), jnp.asarray(d, dtype=bf16)


def fourier_mix(proj, col_blk, fnet_w, seq):
    t_tokens = proj.shape[0]
    n_b = t_tokens // seq
    cs, d = _dft_tables(seq)
    tn = min(seq, 512)
    ns = seq // tn
    y = pl.pallas_call(
        _dft_cols_kernel,
        grid=(n_b, ns),
        in_specs=[
            pl.BlockSpec((tn, FNET_WIDTH), lambda b, i: (b * ns + i, col_blk)),
            pl.BlockSpec((FNET_CH, 2 * FNET_CH), lambda b, i: (0, 0)),
        ],
        out_specs=pl.BlockSpec((2, tn, FNET_WIDTH), lambda b, i: (0, i, b)),
        out_shape=jax.ShapeDtypeStruct((2, seq, n_b * FNET_WIDTH), bf16),
        compiler_params=_params("arbitrary", "arbitrary"),
        name="dft_cols",
    )(proj, cs)
    y2 = y.reshape(2 * seq, n_b * FNET_WIDTH)
    return pl.pallas_call(
        _dft_rows_kernel,
        grid=(n_b, ns),
        in_specs=[
            pl.BlockSpec((tn, 2 * seq), lambda b, i: (i, 0)),
            pl.BlockSpec((2 * seq, FNET_WIDTH), lambda b, i: (0, b)),
            pl.BlockSpec((FNET_GROUPS, FNET_CH, FNET_CH), lambda b, i: (0, 0, 0)),
        ],
        out_specs=pl.BlockSpec((tn, FNET_WIDTH), lambda b, i: (b * ns + i, 0)),
        out_shape=jax.ShapeDtypeStruct((t_tokens, FNET_WIDTH), bf16),
        compiler_params=_params("arbitrary", "arbitrary"),
        name="dft_rows",
    )(d, y2, fnet_w.astype(bf16))


def _pool_kernel(seq, x_ref, w_ref, sc_ref, o_ref):
    t = lax.broadcasted_iota(jnp.int32, (seq, 1), 0)

    def prev(a, k):
        return jnp.where(t >= k, pltpu.roll(a, k, 0), 0.0)

    def nxt(a, k):
        return jnp.where(t < seq - k, pltpu.roll(a, seq - k, 0), 0.0)

    for g, win in enumerate(POOL_WINDOWS):
        sl = slice(g * FNET_CH, (g + 1) * FNET_CH)
        x = x_ref[:, sl].astype(f32)
        half = win // 2
        back, fwd, k = x, x, 1
        while k < half:
            back = back + prev(back, k)
            fwd = fwd + nxt(fwd, k)
            k *= 2
        total = prev(back, 1) + fwd
        cnt = (jnp.minimum(t + half, seq) - jnp.maximum(t - half, 0)).astype(f32)
        pooled = (total / cnt - x).astype(bf16)
        o = jnp.dot(pooled, w_ref[g], preferred_element_type=f32) * sc_ref[:, sl]
        o_ref[:, sl] = o.astype(o_ref.dtype)


def pool_mix(proj, col_blk, pool_w, pool_scale, seq):
    t_tokens = proj.shape[0]
    return pl.pallas_call(
        functools.partial(_pool_kernel, seq),
        grid=(t_tokens // seq,),
        in_specs=[
            pl.BlockSpec((seq, POOL_WIDTH), lambda b: (b, col_blk)),
            pl.BlockSpec((len(POOL_WINDOWS), FNET_CH, FNET_CH), lambda b: (0, 0, 0)),
            pl.BlockSpec((1, POOL_WIDTH), lambda b: (0, 0)),
        ],
        out_specs=pl.BlockSpec((seq, POOL_WIDTH), lambda b: (b, 0)),
        out_shape=jax.ShapeDtypeStruct((t_tokens, POOL_WIDTH), bf16),
        compiler_params=_params("arbitrary"),
        name="pool_mix",
    )(proj, pool_w.astype(bf16), pool_scale.reshape(1, POOL_WIDTH))


def _out_proj_kernel(x_ref, gate_ref, a_ref, b_ref, wa_ref, wb_ref, o_ref):
    y = (jnp.dot(a_ref[...], wa_ref[...], preferred_element_type=f32)
         + jnp.dot(b_ref[...], wb_ref[...], preferred_element_type=f32))
    o_ref[...] = x_ref[...] + gate_ref[...] * y


def out_proj(x, mod, a, b, wa, wb, seq, base_row, per_seq):
    t_tokens = x.shape[0]
    tm = 512
    return pl.pallas_call(
        _out_proj_kernel,
        grid=(t_tokens // tm, 1),
        in_specs=[
            pl.BlockSpec((tm, D_MODEL), lambda m, n: (m, 0)),
            _mod_spec(2, tm, seq, base_row, per_seq),
            pl.BlockSpec((tm, a.shape[1]), lambda m, n: (m, 0)),
            pl.BlockSpec((tm, b.shape[1]), lambda m, n: (m, 0)),
            pl.BlockSpec(wa.shape, lambda m, n: (0, 0)),
            pl.BlockSpec(wb.shape, lambda m, n: (0, 0)),
        ],
        out_specs=pl.BlockSpec((tm, D_MODEL), lambda m, n: (m, 0)),
        out_shape=jax.ShapeDtypeStruct((t_tokens, D_MODEL), f32),
        compiler_params=_params("arbitrary", "arbitrary"),
        name="out_proj",
    )(x, mod, a, b, wa, wb)


def _ffn_kernel(tm, seq, nj, final, *refs):
    it = iter(refs)
    (x_ref, xp_ref, xn_ref, shift_ref, scale_ref, gate_ref, g_ref,
     wv_ref, wg_ref, cwv_ref, cwg_ref, cbv_ref, cbg_ref, wd_ref) = (next(it) for _ in range(14))
    gf_ref = next(it) if final else None
    o_ref, h_ref = next(it), next(it)
    acc_ref = o_ref
    u_refs = [(next(it), next(it)), (next(it), next(it))]
    i = pl.program_id(0)
    j = pl.program_id(1)
    rows = tm + 2 * HALO

    def prologue_and_first_up():
        g, sh, sc = g_ref[...], shift_ref[...], scale_ref[...]
        uv_ref, ug_ref = u_refs[0]

        def emit(rs, h):
            h_ref[rs, :] = h
            uv_ref[rs, :] = jnp.dot(h, wv_ref[...], preferred_element_type=f32)
            ug_ref[rs, :] = jnp.dot(h, wg_ref[...], preferred_element_type=f32)

        for r0 in range(0, tm - IN_ROWS, IN_ROWS):
            rs = slice(r0, r0 + IN_ROWS)
            emit(rs, _rms_modulate(x_ref[rs, :], g, sh, sc).astype(bf16))
        next_ok = jnp.where((((i + 1) * tm) & (seq - 1)) != 0, 1.0, 0.0)
        prev_ok = jnp.where(((i * tm) & (seq - 1)) != 0, 1.0, 0.0)
        h_last = _rms_modulate(x_ref[tm - IN_ROWS:tm, :], g, sh, sc)
        h_next = _rms_modulate(xn_ref[...], g, sh, sc) * next_ok
        h_prev = _rms_modulate(xp_ref[...], g, sh, sc) * prev_ok
        emit(slice(tm - IN_ROWS, rows), jnp.concatenate([h_last, h_next, h_prev], axis=0).astype(bf16))
        acc_ref[...] = jnp.zeros_like(acc_ref)

    def zero_row(a, r):
        r0 = (r // HALO) * HALO
        row8 = lax.broadcasted_iota(jnp.int32, (HALO, 1), 0)
        fixed = jnp.where(row8 == r - r0, 0.0, a[r0:r0 + HALO])
        return jnp.concatenate([a[:r0], fixed, a[r0 + HALO:]], axis=0)

    def window(u_ref, r0):
        lo, hi = r0 - HALO, r0 + FFN_ROWS + HALO
        if lo < 0:
            return jnp.concatenate([u_ref[rows + lo:rows, :], u_ref[0:hi, :]], axis=0)
        return u_ref[lo:hi, :]

    def conv(u_ref, r0, cw_ref, cb_ref):
        u = window(u_ref, r0)
        n = FFN_ROWS + 2 * HALO
        up = pltpu.roll(u, 1, 0)
        un = pltpu.roll(u, n - 1, 0)
        for edge in range(seq, tm, seq):
            if r0 <= edge < r0 + FFN_ROWS:
                up = zero_row(up, edge - r0 + HALO)
            if r0 <= edge - 1 < r0 + FFN_ROWS:
                un = zero_row(un, edge - 1 - r0 + HALO)
        keep = slice(HALO, HALO + FFN_ROWS)
        return (up[keep] * cw_ref[0:1, :] + u[keep] * cw_ref[1:2, :] + un[keep] * cw_ref[2:3, :]
                + cb_ref[...])

    def gated(slot):
        uv_ref, ug_ref = u_refs[slot]
        chunks = []
        for r0 in range(0, tm, FFN_ROWS):
            val = conv(uv_ref, r0, cwv_ref, cbv_ref)
            gate = conv(ug_ref, r0, cwg_ref, cbg_ref)
            chunks.append((gate * jax.nn.sigmoid(gate) * val).astype(bf16))
        return chunks

    def zero_after(chunks):
        parts = []
        for c in chunks:
            bits = pltpu.bitcast(c, jnp.uint32)
            parts += [bits[k:k + HALO] for k in range(0, bits.shape[0], HALO)]
        while len(parts) > 1:
            parts = [a | b for a, b in zip(parts[0::2], parts[1::2])] + parts[len(parts) & ~1:]
        return (parts[0] >> 16) >> 16

    def after(a, zero, row0=0):
        tf = zero.shape[1]
        mid = pltpu.bitcast(pltpu.bitcast(a[row0:row0 + 2 * HALO, 0:tf], jnp.uint32) | zero, bf16)
        if tf < a.shape[1]:
            mid = jnp.concatenate([mid, a[row0:row0 + 2 * HALO, tf:]], axis=1)
        parts = ([a[:row0]] if row0 else []) + [mid, a[row0 + 2 * HALO:]]
        return jnp.concatenate(parts, axis=0)

    def quarter(chunks, k):
        n = len(chunks) // 4
        return chunks[k * n:(k + 1) * n]

    def up_project(slot, chunks):
        h = h_ref[...]
        wv, wg = wv_ref[...], wg_ref[...]
        uv_ref, ug_ref = u_refs[slot]
        k_mid = wv.shape[0] // 2
        wv = after(wv, zero_after(quarter(chunks, 0)), k_mid)
        wg = after(wg, zero_after(quarter(chunks, 2)), k_mid)
        uv_ref[...] = jnp.dot(h, wv, preferred_element_type=f32)
        h = after(h, zero_after(quarter(chunks, 1)))
        ug_ref[...] = jnp.dot(h, wg, preferred_element_type=f32)

    def down_project(chunks, anchored):
        if anchored:
            chunks = [after(chunks[0], zero_after(quarter(chunks, 3)))] + chunks[1:]
        act = jnp.concatenate(chunks, axis=0)
        acc_ref[...] += jnp.dot(act, wd_ref[...], preferred_element_type=f32)

    @pl.when(j == 0)
    def _():
        prologue_and_first_up()

    for p in range(2):
        @pl.when((j >= 1) & (j < nj) & (j % 2 == p))
        def _(p=p):
            chunks = gated(1 - p)
            up_project(p, chunks)
            down_project(chunks, True)

    @pl.when(j == nj)
    def _():
        down_project(gated((nj - 1) % 2), False)
        y = x_ref[...] + gate_ref[...] * acc_ref[...]
        if final:
            ms = jnp.mean(y * y, axis=-1, keepdims=True)
            y = y * lax.rsqrt(ms + EPS) * gf_ref[...]
        o_ref[...] = y


def conv_ffn(x, mod, g, w_up, conv_w, conv_b, w_down, seq, base_row, per_seq, final_g=None):
    t_tokens = x.shape[0]
    tm, tf = FFN_TOKENS, W_TILE
    assert seq & (seq - 1) == 0 and (tm // FFN_ROWS) % 4 == 0
    nj = D_FF // tf
    hb = tm // HALO
    n_hblk = t_tokens // HALO
    conv_b2 = conv_b.reshape(1, 2 * D_FF)

    def blk(j):
        return jnp.clip(j, 0, nj - 1)

    in_specs = [
        pl.BlockSpec((tm, D_MODEL), lambda i, j: (i, 0)),
        pl.BlockSpec((HALO, D_MODEL), lambda i, j: (jnp.maximum(i * hb - 1, 0), 0)),
        pl.BlockSpec((HALO, D_MODEL), lambda i, j: (jnp.minimum((i + 1) * hb, n_hblk - 1), 0)),
        _mod_spec(3, tm, seq, base_row, per_seq),
        _mod_spec(4, tm, seq, base_row, per_seq),
        _mod_spec(5, tm, seq, base_row, per_seq),
        pl.BlockSpec((1, D_MODEL), lambda i, j: (0, 0)),
        pl.BlockSpec((D_MODEL, tf), lambda i, j: (0, blk(j))),
        pl.BlockSpec((D_MODEL, tf), lambda i, j: (0, nj + blk(j))),
        pl.BlockSpec((3, tf), lambda i, j: (0, blk(j - 1))),
        pl.BlockSpec((3, tf), lambda i, j: (0, nj + blk(j - 1))),
        pl.BlockSpec((1, tf), lambda i, j: (0, blk(j - 1))),
        pl.BlockSpec((1, tf), lambda i, j: (0, nj + blk(j - 1))),
        pl.BlockSpec((tf, D_MODEL), lambda i, j: (blk(j - 1), 0)),
    ]
    args = [x, x, x, mod, mod, mod, g.reshape(1, D_MODEL), w_up, w_up,
            conv_w, conv_w, conv_b2, conv_b2, w_down]
    if final_g is not None:
        in_specs.append(pl.BlockSpec((1, D_MODEL), lambda i, j: (0, 0)))
        args.append(final_g.reshape(1, D_MODEL))
    return pl.pallas_call(
        functools.partial(_ffn_kernel, tm, seq, nj, final_g is not None),
        grid=(t_tokens // tm, nj + 1),
        in_specs=in_specs,
        out_specs=pl.BlockSpec((tm, D_MODEL), lambda i, j: (i, 0), pipeline_mode=pl.Buffered(1)),
        out_shape=jax.ShapeDtypeStruct((t_tokens, D_MODEL), f32),
        scratch_shapes=[pltpu.VMEM((tm + 2 * HALO, D_MODEL), bf16)]
                       + [pltpu.VMEM((tm + 2 * HALO, tf), f32)] * 4,
        compiler_params=_params("arbitrary", "arbitrary"),
        name="conv_ffn",
    )(*args)


def _rope_tables(seq):
    t = np.arange(seq)
    half = HEAD_DIM // 2
    inv_freq = np.power(ROPE_BASE, -np.arange(0, half, 2, dtype=np.float64) / half)
    ang_r = (t // GRID_W)[:, None] * inv_freq[None, :]
    ang_c = (t % GRID_W)[:, None] * inv_freq[None, :]
    cos = np.concatenate([np.cos(ang_r)] * 2 + [np.cos(ang_c)] * 2, axis=-1)
    sin = np.concatenate([-np.sin(ang_r), np.sin(ang_r), -np.sin(ang_c), np.sin(ang_c)], axis=-1)
    return jnp.asarray(cos, dtype=f32), jnp.asarray(sin, dtype=f32)


def kernel(x_prompt, x_sample, c, cache_na_k, cache_na_v, cache_gqa_k, cache_gqa_v, c_ctx, norm1_g, norm2_g, ada_w, ada_b, ev_w_in, ev_na_bias, ev_fnet_w, ev_w_out, od_w_in, od_pool_w, od_pool_scale, od_q_norm_g, od_k_norm_g, od_w_out, ffn_w_up, ffn_conv_w, ffn_conv_b, ffn_w_down, final_norm_g):
    n_ctx, seq_p, _ = x_prompt.shape
    n_lat, seq_s, _ = x_sample.shape
    past = cache_na_k.shape[2]
    assert n_lat + 1 <= MOD_ROWS
    xp = x_prompt.reshape(n_ctx * seq_p, D_MODEL)
    xs = x_sample.reshape(n_lat * seq_s, D_MODEL)

    cond = jnp.zeros((MOD_ROWS, D_MODEL), f32).at[0].set(c_ctx).at[1:1 + n_lat].set(c)
    mods = ada_modulation(cond, ada_w, ada_b).reshape(DEPTH, MOD_ROWS * 6, 1, D_MODEL)
    rope = _rope_tables(seq_s)

    ctx = dict(seq=seq_p, base_row=0, per_seq=0)
    lat = dict(seq=seq_s, base_row=1, per_seq=1)
    new_k, new_v = {}, {}
    for i in range(DEPTH):
        j = i // 2
        mod = mods[i]
        w_up = ffn_w_up[i].astype(bf16)
        w_down = ffn_w_down[i].astype(bf16)
        if i % 2 == 0:
            w_in = ev_w_in[j].astype(bf16)
            w_attn = ev_w_out[j, :NA_WIDTH].astype(bf16)
            w_mix = ev_w_out[j, NA_WIDTH:].astype(bf16)
            mix_blk = 3 * NA_WIDTH // FNET_WIDTH
            proj_p = in_proj(xp, mod, norm1_g[i], w_in, f32, NA_WIDTH, **ctx)
            new_k[i] = proj_p[:, NA_WIDTH:2 * NA_WIDTH]
            new_v[i] = proj_p[:, 2 * NA_WIDTH:3 * NA_WIDTH]
            attn_p = ctx_attention(proj_p, seq_p, NA_HEADS, NA_HEADS)
            mix_p = fourier_mix(proj_p, mix_blk, ev_fnet_w[j], seq_p)
            proj_s = in_proj(xs, mod, norm1_g[i], w_in, bf16, NA_WIDTH, **lat)
            attn_s = na_attention(proj_s,
                                  cache_na_k[:, j].reshape(n_lat * past, NA_WIDTH),
                                  cache_na_v[:, j].reshape(n_lat * past, NA_WIDTH),
                                  ev_na_bias[j], seq_s, past)
            mix_s = fourier_mix(proj_s, mix_blk, ev_fnet_w[j], seq_s)
        else:
            w_in = jnp.concatenate([od_w_in[j, :, POOL_WIDTH:], od_w_in[j, :, :POOL_WIDTH]],
                                   axis=1).astype(bf16)
            w_attn = od_w_out[j, POOL_WIDTH:].astype(bf16)
            w_mix = od_w_out[j, :POOL_WIDTH].astype(bf16)
            norm = (od_q_norm_g[j], od_k_norm_g[j], GQA_KV_WIDTH)
            k_lo = GQA_Q_WIDTH
            v_lo = GQA_Q_WIDTH + GQA_KV_WIDTH
            mix_blk = (GQA_Q_WIDTH + 2 * GQA_KV_WIDTH) // POOL_WIDTH
            proj_p = in_proj(xp, mod, norm1_g[i], w_in, f32, GQA_Q_WIDTH, norm=norm, **ctx)
            new_k[i] = proj_p[:, k_lo:k_lo + GQA_KV_WIDTH]
            new_v[i] = proj_p[:, v_lo:v_lo + GQA_KV_WIDTH]
            attn_p = ctx_attention(proj_p, seq_p, GQA_Q_HEADS, GQA_KV_HEADS)
            mix_p = pool_mix(proj_p, mix_blk, od_pool_w[j], od_pool_scale[j], seq_p)
            proj_s = in_proj(xs, mod, norm1_g[i], w_in, bf16, GQA_Q_WIDTH, norm=norm, rope=rope, **lat)
            attn_s = gqa_attention(proj_s,
                                   cache_gqa_k[:, j].reshape(n_lat * past, GQA_KV_WIDTH),
                                   cache_gqa_v[:, j].reshape(n_lat * past, GQA_KV_WIDTH),
                                   seq_s, past)
            mix_s = pool_mix(proj_s, mix_blk, od_pool_w[j], od_pool_scale[j], seq_s)
        xp = out_proj(xp, mod, attn_p, mix_p, w_attn, w_mix, **ctx)
        xs = out_proj(xs, mod, attn_s, mix_s, w_attn, w_mix, **lat)
        final_g = final_norm_g if i == DEPTH - 1 else None
        xp = conv_ffn(xp, mod, norm2_g[i], w_up, ffn_conv_w[i], ffn_conv_b[i], w_down,
                      final_g=final_g, **ctx)
        xs = conv_ffn(xs, mod, norm2_g[i], w_up, ffn_conv_w[i], ffn_conv_b[i], w_down,
                      final_g=final_g, **lat)

    y_prompt = xp.reshape(n_ctx, seq_p, D_MODEL)
    y_sample = xs.reshape(n_lat, seq_s, D_MODEL)
    even = [i for i in range(DEPTH) if i % 2 == 0]
    odd = [i for i in range(DEPTH) if i % 2 == 1]

    def stack(parts, layers, heads):
        return jnp.stack([parts[i].reshape(n_ctx, seq_p, heads, HEAD_DIM) for i in layers], axis=1)

    return (y_prompt, y_sample,
            stack(new_k, even, NA_HEADS), stack(new_v, even, NA_HEADS),
            stack(new_k, odd, GQA_KV_HEADS), stack(new_v, odd, GQA_KV_HEADS))
```

```python
import functools
import math

import jax
import jax.numpy as jnp
import numpy as np
from jax import lax
from jax.experimental import pallas as pl
from jax.experimental.pallas import tpu as pltpu

f32 = jnp.float32
bf16 = jnp.bfloat16

D_MODEL = 2048
DEPTH = 2
GRID_W = 64
HEAD_DIM = 128
NA_HEADS = 12
NA_KR = 8
NA_KC = 16
FNET_GROUPS = 4
FNET_CH = 128
POOL_WINDOWS = (2, 4, 8, 16)
GQA_Q_HEADS = 12
GQA_KV_HEADS = 4
D_FF = 5632
ROPE_BASE = 10000.0
EPS = 1e-6
NEG_INF = -1e30
NA_WIDTH = NA_HEADS * HEAD_DIM
FNET_WIDTH = FNET_GROUPS * FNET_CH
POOL_WIDTH = len(POOL_WINDOWS) * FNET_CH
GQA_Q_WIDTH = GQA_Q_HEADS * HEAD_DIM
GQA_KV_WIDTH = GQA_KV_HEADS * HEAD_DIM
LOG2_E = math.log2(math.e)
ATTN_SCALE = HEAD_DIM ** -0.5 * LOG2_E

MOD_ROWS = 16
VMEM_LIMIT = 60 * 1024 * 1024
HALO = 8
NA_QROWS = 4
NA_WROWS = 12
NA_INVALID = 2 * NA_KR - 1
GQA_QUERIES = 512
NA_HEADS_PER_STEP = 6
FFN_TOKENS = 1024
FFN_ROWS = 128
W_TILE = 512
IN_TOKENS = 1024
IN_ROWS = 256


def _params(*sem, flags=None):
    return pltpu.CompilerParams(dimension_semantics=sem, vmem_limit_bytes=VMEM_LIMIT, flags=flags)


def _rms_modulate(x, g, shift, scale):
    ms = jnp.mean(x * x, axis=-1, keepdims=True)
    return (x * lax.rsqrt(ms + EPS)) * (g * (1.0 + scale)) + shift


def _mod_spec(k, tm, seq, base_row, per_seq):
    def index(m, n):
        return ((base_row + per_seq * ((m * tm) // seq)) * 6 + k, 0, 0)
    return pl.BlockSpec((None, 1, D_MODEL), index)


def _ada_kernel(c_ref, w_ref, b_ref, o_ref):
    c = c_ref[...]
    s = (c * jax.nn.sigmoid(c)).astype(bf16)
    o_ref[...] = jnp.dot(s, w_ref[...].astype(bf16), preferred_element_type=f32) + b_ref[...]


def ada_modulation(cond, ada_w, ada_b):
    tn = 1024
    n_out = 6 * D_MODEL
    return pl.pallas_call(
        _ada_kernel,
        grid=(DEPTH, n_out // tn),
        in_specs=[
            pl.BlockSpec((MOD_ROWS, D_MODEL), lambda i, n: (0, 0)),
            pl.BlockSpec((None, D_MODEL, tn), lambda i, n: (i, 0, n)),
            pl.BlockSpec((None, 1, tn), lambda i, n: (i, 0, n)),
        ],
        out_specs=pl.BlockSpec((None, MOD_ROWS, tn), lambda i, n: (i, 0, n)),
        out_shape=jax.ShapeDtypeStruct((DEPTH, MOD_ROWS, n_out), f32),
        compiler_params=_params("arbitrary", "arbitrary"),
        name="ada_modulation",
    )(cond, ada_w, ada_b.reshape(DEPTH, 1, n_out))


def _rope(a, cos, sin):
    lane = lax.broadcasted_iota(jnp.int32, a.shape, 1)
    quarter = HEAD_DIM // 4
    partner = jnp.where((lane & (2 * quarter - 1)) < quarter,
                        pltpu.roll(a, HEAD_DIM - quarter, 1),
                        pltpu.roll(a, quarter, 1))
    return a * cos + partner * sin


def _in_proj_kernel(tn, q_blocks, k_blocks, has_norm, has_rope, *refs):
    it = iter(refs)
    x_hbm, shift_ref, scale_ref, g_ref, w_ref = (next(it) for _ in range(5))
    qg_ref = kg_ref = cos_ref = sin_ref = None
    if has_norm:
        qg_ref, kg_ref, avg_ref = next(it), next(it), next(it)
    if has_rope:
        cos_ref, sin_ref = next(it), next(it)
    o_ref, h_ref, x_buf, x_sem = next(it), next(it), next(it), next(it)
    m = pl.program_id(0)
    n = pl.program_id(1)
    tm = h_ref.shape[0]
    everything = slice(0, tm)

    def x_copy(tile, slot):
        return pltpu.make_async_copy(x_hbm.at[pl.ds(tile * tm, tm), :], x_buf.at[slot], x_sem.at[slot])

    slot = m % 2
    x_ref = x_buf.at[slot]

    @pl.when((n == 0) & (m == 0))
    def _():
        x_copy(0, 0).start()

    @pl.when(n == 0)
    def _():
        x_copy(m, slot).wait()

    @pl.when((n == 1) & (m + 1 < pl.num_programs(0)))
    def _():
        x_copy(m + 1, 1 - slot).start()

    def project(rs, h=None):
        h = h_ref[rs, :] if h is None else h
        return jnp.dot(h, w_ref[...], preferred_element_type=f32)

    def store_scaled(rs, acc, scale):
        o_ref[rs, :] = (acc if scale is None else acc * scale).astype(o_ref.dtype)

    def store_normed(rs, acc, gn):
        ms = jnp.dot((acc * acc).astype(bf16), avg_ref[...], preferred_element_type=f32)
        inv = lax.rsqrt(ms + EPS)
        for hh in range(tn // HEAD_DIM):
            sl = slice(hh * HEAD_DIM, (hh + 1) * HEAD_DIM)
            a = acc[:, sl] * inv[:, sl] * gn
            if has_rope:
                a = _rope(a, cos_ref[rs, :], sin_ref[rs, :])
            o_ref[rs, sl] = a.astype(o_ref.dtype)

    @pl.when(n == 0)
    def _():
        g, sh, sc = g_ref[...], shift_ref[...], scale_ref[...]
        for r0 in range(0, tm, IN_ROWS):
            rs = slice(r0, r0 + IN_ROWS)
            h = _rms_modulate(x_ref[rs, :], g, sh, sc).astype(bf16)
            h_ref[rs, :] = h
            if has_norm:
                store_normed(rs, project(rs, h), qg_ref[...] * ATTN_SCALE)
            else:
                store_scaled(rs, project(rs, h), ATTN_SCALE)

    is_q = n < q_blocks
    q_scale = jnp.where(is_q, ATTN_SCALE, 1.0)
    if not has_norm:
        @pl.when(n > 0)
        def _():
            store_scaled(everything, project(everything), q_scale)
        return

    is_qk = n < q_blocks + k_blocks

    @pl.when((n > 0) & is_qk)
    def _():
        gn = jnp.where(is_q, qg_ref[...], kg_ref[...]) * q_scale
        store_normed(everything, project(everything), gn)

    @pl.when(jnp.logical_not(is_qk))
    def _():
        store_scaled(everything, project(everything), None)


def in_proj(x, mod, g, w, out_dtype, q_width, seq, base_row, per_seq, norm=None, rope=None):
    t_tokens = x.shape[0]
    tm, tn = IN_TOKENS, W_TILE
    n_total = w.shape[1]
    assert n_total // tn >= 2
    in_specs = [
        pl.BlockSpec(memory_space=pl.ANY),
        _mod_spec(0, tm, seq, base_row, per_seq),
        _mod_spec(1, tm, seq, base_row, per_seq),
        pl.BlockSpec((1, D_MODEL), lambda m, n: (0, 0)),
        pl.BlockSpec((D_MODEL, tn), lambda m, n: (0, n)),
    ]
    args = [x, mod, mod, g.reshape(1, D_MODEL), w]
    k_width = 0
    if norm is not None:
        k_width = norm[2]
        in_specs += [pl.BlockSpec((1, HEAD_DIM), lambda m, n: (0, 0))] * 2
        in_specs += [pl.BlockSpec((tn, tn), lambda m, n: (0, 0))]
        head_of = np.arange(tn) // HEAD_DIM
        avg = (head_of[:, None] == head_of[None, :]) / HEAD_DIM
        args += [norm[0].reshape(1, HEAD_DIM), norm[1].reshape(1, HEAD_DIM), jnp.asarray(avg, dtype=bf16)]
    if rope is not None:
        spb = seq // tm
        in_specs += [pl.BlockSpec((tm, HEAD_DIM), lambda m, n: (m % spb, 0))] * 2
        args += [rope[0], rope[1]]
    kern = functools.partial(_in_proj_kernel, tn, q_width // tn, k_width // tn,
                             norm is not None, rope is not None)
    return pl.pallas_call(
        kern,
        grid=(t_tokens // tm, n_total // tn),
        in_specs=in_specs,
        out_specs=pl.BlockSpec((tm, tn), lambda m, n: (m, n)),
        out_shape=jax.ShapeDtypeStruct((t_tokens, n_total), out_dtype),
        scratch_shapes=[pltpu.VMEM((tm, D_MODEL), bf16),
                        pltpu.VMEM((2, tm, D_MODEL), f32),
                        pltpu.SemaphoreType.DMA((2,))],
        compiler_params=_params("arbitrary", "arbitrary"),
        name="in_proj",
    )(*args)


def _with_ones(v):
    return jnp.concatenate([v, jnp.ones_like(v)], axis=1)


def _softmax_pv(parts):
    m = functools.reduce(jnp.maximum, [jnp.max(s, axis=-1, keepdims=True) for s, _ in parts])
    acc = sum(jnp.dot(jnp.exp2(s - m).astype(bf16), v1, preferred_element_type=f32) for s, v1 in parts)
    return acc[:, :HEAD_DIM] / acc[:, HEAD_DIM:]


def _ctx_attn_kernel(n_q, n_kv, q_ref, k_ref, v_ref, o_ref):
    group = n_q // n_kv
    values = [_with_ones(v_ref[:, kv * HEAD_DIM:(kv + 1) * HEAD_DIM].astype(bf16)) for kv in range(n_kv)]
    keys = [k_ref[:, kv * HEAD_DIM:(kv + 1) * HEAD_DIM].astype(bf16) for kv in range(n_kv)]
    dn = (((1,), (1,)), ((), ()))
    scores = [lax.dot_general(q_ref[:, h * HEAD_DIM:(h + 1) * HEAD_DIM].astype(bf16), keys[h // group],
                              dn, preferred_element_type=f32) for h in range(n_q)]
    for h, s in enumerate(scores):
        o = _softmax_pv([(s, values[h // group])])
        o_ref[:, h * HEAD_DIM:(h + 1) * HEAD_DIM] = o.astype(o_ref.dtype)


def ctx_attention(proj, seq, n_q, n_kv):
    t_tokens = proj.shape[0]
    qw, kw = n_q * HEAD_DIM, n_kv * HEAD_DIM
    assert qw % kw == 0
    k_blk = qw // kw
    return pl.pallas_call(
        functools.partial(_ctx_attn_kernel, n_q, n_kv),
        grid=(t_tokens // seq,),
        in_specs=[
            pl.BlockSpec((seq, qw), lambda b: (b, 0)),
            pl.BlockSpec((seq, kw), lambda b: (b, k_blk)),
            pl.BlockSpec((seq, kw), lambda b: (b, k_blk + 1)),
        ],
        out_specs=pl.BlockSpec((seq, qw), lambda b: (b, 0)),
        out_shape=jax.ShapeDtypeStruct((t_tokens, qw), bf16),
        compiler_params=_params("arbitrary"),
        name="ctx_attention",
    )(proj, proj, proj)


def _gqa_attn_kernel(group, q_ref, ck_ref, cv_ref, k_ref, v_ref, o_ref):
    ck = ck_ref[...].astype(bf16)
    cv = _with_ones(cv_ref[...].astype(bf16))
    k = k_ref[...]
    v = _with_ones(v_ref[...])
    dn = (((1,), (1,)), ((), ()))
    scores = []
    for gi in range(group):
        q = q_ref[:, gi * HEAD_DIM:(gi + 1) * HEAD_DIM]
        scores.append((lax.dot_general(q, ck, dn, preferred_element_type=f32),
                       lax.dot_general(q, k, dn, preferred_element_type=f32)))
    for gi, (s_c, s_l) in enumerate(scores):
        o = _softmax_pv([(s_c, cv), (s_l, v)])
        o_ref[:, gi * HEAD_DIM:(gi + 1) * HEAD_DIM] = o.astype(o_ref.dtype)


def gqa_attention(proj, cache_k, cache_v, seq, past):
    t_tokens = proj.shape[0]
    n_b = t_tokens // seq
    group = GQA_Q_HEADS // GQA_KV_HEADS
    tq = GQA_QUERIES
    nq = seq // tq
    return pl.pallas_call(
        functools.partial(_gqa_attn_kernel, group),
        grid=(n_b, GQA_KV_HEADS, nq),
        in_specs=[
            pl.BlockSpec((tq, group * HEAD_DIM), lambda b, h, i: (b * nq + i, h)),
            pl.BlockSpec((past, HEAD_DIM), lambda b, h, i: (b, h)),
            pl.BlockSpec((past, HEAD_DIM), lambda b, h, i: (b, h)),
            pl.BlockSpec((seq, HEAD_DIM), lambda b, h, i: (b, GQA_Q_HEADS + h)),
            pl.BlockSpec((seq, HEAD_DIM), lambda b, h, i: (b, GQA_Q_HEADS + GQA_KV_HEADS + h)),
        ],
        out_specs=pl.BlockSpec((tq, group * HEAD_DIM), lambda b, h, i: (b * nq + i, h)),
        out_shape=jax.ShapeDtypeStruct((t_tokens, GQA_Q_WIDTH), bf16),
        compiler_params=_params("arbitrary", "arbitrary", "arbitrary"),
        name="gqa_attention",
    )(proj, cache_k, cache_v, proj, proj)


def _na_bias_tiles(rel_bias):
    qc = jnp.arange(GRID_W)[:, None]
    kc = jnp.arange(GRID_W)[None, :]
    q_start = jnp.clip(qc - NA_KC // 2, 0, GRID_W - NA_KC)
    valid = (kc >= q_start) & (kc < q_start + NA_KC)
    dc = jnp.clip(kc - qc, -(NA_KC - 1), NA_KC - 1) + NA_KC - 1
    tiles = jnp.where(valid[None, None], rel_bias.astype(f32)[:, :, dc] * LOG2_E, NEG_INF)
    masked = jnp.full((NA_HEADS, 1, GRID_W, GRID_W), NEG_INF, f32)
    tiles = jnp.concatenate([tiles, masked], axis=1)
    zeros = jnp.zeros_like(tiles)
    left = jnp.concatenate([tiles, zeros], axis=-1)
    right = jnp.concatenate([zeros, tiles], axis=-1)
    return left, right


def _na_attn_kernel(n_heads, n_rows, q_ref, k_ref, v_ref, ck_ref, cv_ref, bl_ref, br_ref, o_ref):
    rb = pl.program_id(2)
    start = jnp.clip(rb * NA_QROWS - NA_KR // 2, 0, n_rows - NA_WROWS)
    tok0 = pl.multiple_of(start * GRID_W, GRID_W)
    dn = (((1,), (1,)), ((), ()))

    def tile_index(i, kr):
        qr = rb * NA_QROWS + i
        r0 = jnp.clip(qr - NA_KR // 2, 0, n_rows - NA_KR)
        return jnp.where((kr >= r0) & (kr < r0 + NA_KR), kr - qr + NA_KR - 1, NA_INVALID)

    index = [[(tile_index(i, start + 2 * jp), tile_index(i, start + 2 * jp + 1))
              for jp in range(NA_WROWS // 2)] for i in range(NA_QROWS)]
    scores = []
    for hh in range(n_heads):
        hs = slice(hh * HEAD_DIM, (hh + 1) * HEAD_DIM)
        q = q_ref[:, hs]
        bias = jnp.concatenate(
            [jnp.concatenate([bl_ref[hh, ia] + br_ref[hh, ib] for ia, ib in row], axis=1)
             for row in index], axis=0)
        s_l = lax.dot_general(q, k_ref[pl.ds(tok0, NA_WROWS * GRID_W), hs], dn,
                              preferred_element_type=f32) + bias
        s_c = lax.dot_general(q, ck_ref[:, hs].astype(bf16), dn, preferred_element_type=f32)
        scores.append((s_c, s_l))
    for hh, (s_c, s_l) in enumerate(scores):
        hs = slice(hh * HEAD_DIM, (hh + 1) * HEAD_DIM)
        vw = v_ref[pl.ds(tok0, NA_WROWS * GRID_W), hs]
        o = _softmax_pv([(s_c, _with_ones(cv_ref[:, hs].astype(bf16))), (s_l, _with_ones(vw))])
        o_ref[:, hs] = o.astype(o_ref.dtype)


def na_attention(proj, cache_k, cache_v, rel_bias, seq, past):
    t_tokens = proj.shape[0]
    n_b = t_tokens // seq
    n_rows = seq // GRID_W
    n_rb = n_rows // NA_QROWS
    tq = NA_QROWS * GRID_W
    hb = NA_HEADS_PER_STEP
    n_hg = NA_HEADS // hb
    wide = hb * HEAD_DIM
    bias_l, bias_r = _na_bias_tiles(rel_bias)
    bias_spec = pl.BlockSpec((hb, 2 * NA_KR, GRID_W, 2 * GRID_W), lambda b, h, r: (h, 0, 0, 0))
    return pl.pallas_call(
        functools.partial(_na_attn_kernel, hb, n_rows),
        grid=(n_b, n_hg, n_rb),
        in_specs=[
            pl.BlockSpec((tq, wide), lambda b, h, r: (b * n_rb + r, h)),
            pl.BlockSpec((seq, wide), lambda b, h, r: (b, n_hg + h)),
            pl.BlockSpec((seq, wide), lambda b, h, r: (b, 2 * n_hg + h)),
            pl.BlockSpec((past, wide), lambda b, h, r: (b, h)),
            pl.BlockSpec((past, wide), lambda b, h, r: (b, h)),
            bias_spec,
            bias_spec,
        ],
        out_specs=pl.BlockSpec((tq, wide), lambda b, h, r: (b * n_rb + r, h)),
        out_shape=jax.ShapeDtypeStruct((t_tokens, NA_WIDTH), bf16),
        compiler_params=_params("arbitrary", "arbitrary", "arbitrary"),
        name="na_attention",
    )(proj, proj, proj, cache_k, cache_v, bias_l, bias_r)


def _dft_cols_kernel(x_ref, cs_ref, y_ref):
    for g in range(FNET_GROUPS):
        sl = slice(g * FNET_CH, (g + 1) * FNET_CH)
        y = jnp.dot(x_ref[:, sl].astype(bf16), cs_ref[...], preferred_element_type=f32)
        y_ref[0, :, sl] = y[:, :FNET_CH].astype(bf16)
        y_ref[1, :, sl] = y[:, FNET_CH:].astype(bf16)


def _dft_rows_kernel(d_ref, y_ref, w_ref, o_ref):
    f = jnp.dot(d_ref[...], y_ref[...], preferred_element_type=f32)
    for g in range(FNET_GROUPS):
        sl = slice(g * FNET_CH, (g + 1) * FNET_CH)
        o = jnp.dot(f[:, sl].astype(bf16), w_ref[g], preferred_element_type=f32)
        o_ref[:, sl] = o.astype(o_ref.dtype)


def _dft_tables(seq):
    c = np.arange(FNET_CH, dtype=np.int64)
    ang_c = (2.0 * np.pi / FNET_CH) * ((c[:, None] * c[None, :]) % FNET_CH)
    cs = np.concatenate([np.cos(ang_c), np.sin(ang_c)], axis=1)
    n = np.arange(seq, dtype=np.int64)
    ang_n = (2.0 * np.pi / seq) * ((n[:, None] * n[None, :]) % seq)
    norm = 1.0 / math.sqrt(seq * FNET_CH)
    d = np.concatenate([np.cos(ang_n) * norm, -np.sin(ang_n) * norm], axis=1)
    return jnp.asarray(cs, dtype=bf16), jnp.asarray(d, dtype=bf16)


def fourier_mix(proj, col_blk, fnet_w, seq):
    t_tokens = proj.shape[0]
    n_b = t_tokens // seq
    cs, d = _dft_tables(seq)
    tn = min(seq, 512)
    ns = seq // tn
    y = pl.pallas_call(
        _dft_cols_kernel,
        grid=(n_b, ns),
        in_specs=[
            pl.BlockSpec((tn, FNET_WIDTH), lambda b, i: (b * ns + i, col_blk)),
            pl.BlockSpec((FNET_CH, 2 * FNET_CH), lambda b, i: (0, 0)),
        ],
        out_specs=pl.BlockSpec((2, tn, FNET_WIDTH), lambda b, i: (0, i, b)),
        out_shape=jax.ShapeDtypeStruct((2, seq, n_b * FNET_WIDTH), bf16),
        compiler_params=_params("arbitrary", "arbitrary"),
        name="dft_cols",
    )(proj, cs)
    y2 = y.reshape(2 * seq, n_b * FNET_WIDTH)
    return pl.pallas_call(
        _dft_rows_kernel,
        grid=(n_b, ns),
        in_specs=[
            pl.BlockSpec((tn, 2 * seq), lambda b, i: (i, 0)),
            pl.BlockSpec((2 * seq, FNET_WIDTH), lambda b, i: (0, b)),
            pl.BlockSpec((FNET_GROUPS, FNET_CH, FNET_CH), lambda b, i: (0, 0, 0)),
        ],
        out_specs=pl.BlockSpec((tn, FNET_WIDTH), lambda b, i: (b * ns + i, 0)),
        out_shape=jax.ShapeDtypeStruct((t_tokens, FNET_WIDTH), bf16),
        compiler_params=_params("arbitrary", "arbitrary"),
        name="dft_rows",
    )(d, y2, fnet_w.astype(bf16))


def _pool_kernel(seq, x_ref, w_ref, sc_ref, o_ref):
    t = lax.broadcasted_iota(jnp.int32, (seq, 1), 0)

    def prev(a, k):
        return jnp.where(t >= k, pltpu.roll(a, k, 0), 0.0)

    def nxt(a, k):
        return jnp.where(t < seq - k, pltpu.roll(a, seq - k, 0), 0.0)

    for g, win in enumerate(POOL_WINDOWS):
        sl = slice(g * FNET_CH, (g + 1) * FNET_CH)
        x = x_ref[:, sl].astype(f32)
        half = win // 2
        back, fwd, k = x, x, 1
        while k < half:
            back = back + prev(back, k)
            fwd = fwd + nxt(fwd, k)
            k *= 2
        total = prev(back, 1) + fwd
        cnt = (jnp.minimum(t + half, seq) - jnp.maximum(t - half, 0)).astype(f32)
        pooled = (total / cnt - x).astype(bf16)
        o = jnp.dot(pooled, w_ref[g], preferred_element_type=f32) * sc_ref[:, sl]
        o_ref[:, sl] = o.astype(o_ref.dtype)


def pool_mix(proj, col_blk, pool_w, pool_scale, seq):
    t_tokens = proj.shape[0]
    return pl.pallas_call(
        functools.partial(_pool_kernel, seq),
        grid=(t_tokens // seq,),
        in_specs=[
            pl.BlockSpec((seq, POOL_WIDTH), lambda b: (b, col_blk)),
            pl.BlockSpec((len(POOL_WINDOWS), FNET_CH, FNET_CH), lambda b: (0, 0, 0)),
            pl.BlockSpec((1, POOL_WIDTH), lambda b: (0, 0)),
        ],
        out_specs=pl.BlockSpec((seq, POOL_WIDTH), lambda b: (b, 0)),
        out_shape=jax.ShapeDtypeStruct((t_tokens, POOL_WIDTH), bf16),
        compiler_params=_params("arbitrary"),
        name="pool_mix",
    )(proj, pool_w.astype(bf16), pool_scale.reshape(1, POOL_WIDTH))


def _out_proj_kernel(x_ref, gate_ref, a_ref, b_ref, wa_ref, wb_ref, o_ref):
    y = (jnp.dot(a_ref[...], wa_ref[...], preferred_element_type=f32)
         + jnp.dot(b_ref[...], wb_ref[...], preferred_element_type=f32))
    o_ref[...] = x_ref[...] + gate_ref[...] * y


def out_proj(x, mod, a, b, wa, wb, seq, base_row, per_seq):
    t_tokens = x.shape[0]
    tm = 512
    return pl.pallas_call(
        _out_proj_kernel,
        grid=(t_tokens // tm, 1),
        in_specs=[
            pl.BlockSpec((tm, D_MODEL), lambda m, n: (m, 0)),
            _mod_spec(2, tm, seq, base_row, per_seq),
            pl.BlockSpec((tm, a.shape[1]), lambda m, n: (m, 0)),
            pl.BlockSpec((tm, b.shape[1]), lambda m, n: (m, 0)),
            pl.BlockSpec(wa.shape, lambda m, n: (0, 0)),
            pl.BlockSpec(wb.shape, lambda m, n: (0, 0)),
        ],
        out_specs=pl.BlockSpec((tm, D_MODEL), lambda m, n: (m, 0)),
        out_shape=jax.ShapeDtypeStruct((t_tokens, D_MODEL), f32),
        compiler_params=_params("arbitrary", "arbitrary"),
        name="out_proj",
    )(x, mod, a, b, wa, wb)


def _ffn_kernel(tm, seq, nj, final, *refs):
    it = iter(refs)
    (x_hbm, xp_ref, xn_ref, shift_ref, scale_ref, gate_ref, g_ref,
     wv_ref, wg_ref, cwv_ref, cwg_ref, cbv_ref, cbg_ref, wd_ref) = (next(it) for _ in range(14))
    gf_ref = next(it) if final else None
    o_hbm, h_ref, acc_ref = next(it), next(it), next(it)
    u_refs = [(next(it), next(it)), (next(it), next(it))]
    x_buf, x_sem, o_sem = next(it), next(it), next(it)
    i = pl.program_id(0)
    j = pl.program_id(1)
    n_tiles = pl.num_programs(0)
    rows = tm + 2 * HALO

    def x_copy(tile, slot):
        return pltpu.make_async_copy(x_hbm.at[pl.ds(tile * tm, tm), :], x_buf.at[slot], x_sem.at[slot])

    def o_copy(tile):
        return pltpu.make_async_copy(acc_ref, o_hbm.at[pl.ds(tile * tm, tm), :], o_sem.at[0])

    slot = i % 2
    x_ref = x_buf.at[slot]

    @pl.when((j == 0) & (i == 0))
    def _():
        x_copy(0, 0).start()

    @pl.when(j == 0)
    def _():
        x_copy(i, slot).wait()

    @pl.when((j == 1) & (i + 1 < n_tiles))
    def _():
        x_copy(i + 1, 1 - slot).start()

    @pl.when((j == 1) & (i > 0))
    def _():
        o_copy(i - 1).wait()

    @pl.when(j == 1)
    def _():
        acc_ref[...] = jnp.zeros_like(acc_ref)

    def prologue_and_first_up():
        g, sh, sc = g_ref[...], shift_ref[...], scale_ref[...]
        uv_ref, ug_ref = u_refs[0]

        def emit(rs, h):
            h_ref[rs, :] = h
            uv_ref[rs, :] = jnp.dot(h, wv_ref[...], preferred_element_type=f32)
            ug_ref[rs, :] = jnp.dot(h, wg_ref[...], preferred_element_type=f32)

        for r0 in range(0, tm - IN_ROWS, IN_ROWS):
            rs = slice(r0, r0 + IN_ROWS)
            emit(rs, _rms_modulate(x_ref[rs, :], g, sh, sc).astype(bf16))
        next_ok = jnp.where((((i + 1) * tm) & (seq - 1)) != 0, 1.0, 0.0)
        prev_ok = jnp.where(((i * tm) & (seq - 1)) != 0, 1.0, 0.0)
        h_last = _rms_modulate(x_ref[tm - IN_ROWS:tm, :], g, sh, sc)
        h_next = _rms_modulate(xn_ref[...], g, sh, sc) * next_ok
        h_prev = _rms_modulate(xp_ref[...], g, sh, sc) * prev_ok
        emit(slice(tm - IN_ROWS, rows), jnp.concatenate([h_last, h_next, h_prev], axis=0).astype(bf16))

    def zero_row(a, r):
        r0 = (r // HALO) * HALO
        row8 = lax.broadcasted_iota(jnp.int32, (HALO, 1), 0)
        fixed = jnp.where(row8 == r - r0, 0.0, a[r0:r0 + HALO])
        return jnp.concatenate([a[:r0], fixed, a[r0 + HALO:]], axis=0)

    def window(u_ref, r0):
        lo, hi = r0 - HALO, r0 + FFN_ROWS + HALO
        if lo < 0:
            return jnp.concatenate([u_ref[rows + lo:rows, :], u_ref[0:hi, :]], axis=0)
        return u_ref[lo:hi, :]

    def conv(u_ref, r0, cw_ref, cb_ref):
        u = window(u_ref, r0)
        n = FFN_ROWS + 2 * HALO
        up = pltpu.roll(u, 1, 0)
        un = pltpu.roll(u, n - 1, 0)
        for edge in range(seq, tm, seq):
            if r0 <= edge < r0 + FFN_ROWS:
                up = zero_row(up, edge - r0 + HALO)
            if r0 <= edge - 1 < r0 + FFN_ROWS:
                un = zero_row(un, edge - 1 - r0 + HALO)
        keep = slice(HALO, HALO + FFN_ROWS)
        return (up[keep] * cw_ref[0:1, :] + u[keep] * cw_ref[1:2, :] + un[keep] * cw_ref[2:3, :]
                + cb_ref[...])

    def gated(slot):
        uv_ref, ug_ref = u_refs[slot]
        chunks = []
        for r0 in range(0, tm, FFN_ROWS):
            val = conv(uv_ref, r0, cwv_ref, cbv_ref)
            gate = conv(ug_ref, r0, cwg_ref, cbg_ref)
            chunks.append((gate * jax.nn.sigmoid(gate) * val).astype(bf16))
        return chunks

    def zero_after(chunks):
        parts = []
        for c in chunks:
            bits = pltpu.bitcast(c, jnp.uint32)
            parts += [bits[k:k + HALO] for k in range(0, bits.shape[0], HALO)]
        while len(parts) > 1:
            parts = [a | b for a, b in zip(parts[0::2], parts[1::2])] + parts[len(parts) & ~1:]
        return (parts[0] >> 16) >> 16

    def after(a, zero, row0=0):
        tf = zero.shape[1]
        mid = pltpu.bitcast(pltpu.bitcast(a[row0:row0 + 2 * HALO, 0:tf], jnp.uint32) | zero, bf16)
        if tf < a.shape[1]:
            mid = jnp.concatenate([mid, a[row0:row0 + 2 * HALO, tf:]], axis=1)
        parts = ([a[:row0]] if row0 else []) + [mid, a[row0 + 2 * HALO:]]
        return jnp.concatenate(parts, axis=0)

    def quarter(chunks, k):
        n = len(chunks) // 4
        return chunks[k * n:(k + 1) * n]

    def up_project(slot, chunks):
        h = h_ref[...]
        wv, wg = wv_ref[...], wg_ref[...]
        uv_ref, ug_ref = u_refs[slot]
        k_mid = wv.shape[0] // 2
        wv = after(wv, zero_after(quarter(chunks, 0)), k_mid)
        wg = after(wg, zero_after(quarter(chunks, 2)), k_mid)
        uv_ref[...] = jnp.dot(h, wv, preferred_element_type=f32)
        h = after(h, zero_after(quarter(chunks, 1)))
        ug_ref[...] = jnp.dot(h, wg, preferred_element_type=f32)

    def down_project(chunks, anchored):
        if anchored:
            chunks = [after(chunks[0], zero_after(quarter(chunks, 3)))] + chunks[1:]
        act = jnp.concatenate(chunks, axis=0)
        acc_ref[...] += jnp.dot(act, wd_ref[...], preferred_element_type=f32)

    @pl.when(j == 0)
    def _():
        prologue_and_first_up()

    for p in range(2):
        @pl.when((j >= 1) & (j < nj) & (j % 2 == p))
        def _(p=p):
            chunks = gated(1 - p)
            up_project(p, chunks)
            down_project(chunks, True)

    @pl.when(j == nj)
    def _():
        down_project(gated((nj - 1) % 2), False)
        y = x_ref[...] + gate_ref[...] * acc_ref[...]
        if final:
            ms = jnp.mean(y * y, axis=-1, keepdims=True)
            y = y * lax.rsqrt(ms + EPS) * gf_ref[...]
        acc_ref[...] = y
        o_copy(i).start()

    @pl.when((j == nj) & (i == n_tiles - 1))
    def _():
        o_copy(i).wait()


def conv_ffn(x, mod, g, w_up, conv_w, conv_b, w_down, seq, base_row, per_seq, final_g=None):
    t_tokens = x.shape[0]
    tm, tf = FFN_TOKENS, W_TILE
    assert seq & (seq - 1) == 0 and (tm // FFN_ROWS) % 4 == 0
    nj = D_FF // tf
    hb = tm // HALO
    n_hblk = t_tokens // HALO
    conv_b2 = conv_b.reshape(1, 2 * D_FF)

    def blk(j):
        return jnp.clip(j, 0, nj - 1)

    assert nj >= 2
    in_specs = [
        pl.BlockSpec(memory_space=pl.ANY),
        pl.BlockSpec((HALO, D_MODEL), lambda i, j: (jnp.maximum(i * hb - 1, 0), 0)),
        pl.BlockSpec((HALO, D_MODEL), lambda i, j: (jnp.minimum((i + 1) * hb, n_hblk - 1), 0)),
        _mod_spec(3, tm, seq, base_row, per_seq),
        _mod_spec(4, tm, seq, base_row, per_seq),
        _mod_spec(5, tm, seq, base_row, per_seq),
        pl.BlockSpec((1, D_MODEL), lambda i, j: (0, 0)),
        pl.BlockSpec((D_MODEL, tf), lambda i, j: (0, blk(j))),
        pl.BlockSpec((D_MODEL, tf), lambda i, j: (0, nj + blk(j))),
        pl.BlockSpec((3, tf), lambda i, j: (0, blk(j - 1))),
        pl.BlockSpec((3, tf), lambda i, j: (0, nj + blk(j - 1))),
        pl.BlockSpec((1, tf), lambda i, j: (0, blk(j - 1))),
        pl.BlockSpec((1, tf), lambda i, j: (0, nj + blk(j - 1))),
        pl.BlockSpec((tf, D_MODEL), lambda i, j: (blk(j - 1), 0)),
    ]
    args = [x, x, x, mod, mod, mod, g.reshape(1, D_MODEL), w_up, w_up,
            conv_w, conv_w, conv_b2, conv_b2, w_down]
    if final_g is not None:
        in_specs.append(pl.BlockSpec((1, D_MODEL), lambda i, j: (0, 0)))
        args.append(final_g.reshape(1, D_MODEL))
    return pl.pallas_call(
        functools.partial(_ffn_kernel, tm, seq, nj, final_g is not None),
        grid=(t_tokens // tm, nj + 1),
        in_specs=in_specs,
        out_specs=pl.BlockSpec(memory_space=pl.ANY),
        out_shape=jax.ShapeDtypeStruct((t_tokens, D_MODEL), f32),
        scratch_shapes=[pltpu.VMEM((tm + 2 * HALO, D_MODEL), bf16),
                        pltpu.VMEM((tm, D_MODEL), f32)]
                       + [pltpu.VMEM((tm + 2 * HALO, tf), f32)] * 4
                       + [pltpu.VMEM((2, tm, D_MODEL), f32),
                          pltpu.SemaphoreType.DMA((2,)),
                          pltpu.SemaphoreType.DMA((1,))],
        compiler_params=_params("arbitrary", "arbitrary"),
        name="conv_ffn",
    )(*args)


def _rope_tables(seq):
    t = np.arange(seq)
    half = HEAD_DIM // 2
    inv_freq = np.power(ROPE_BASE, -np.arange(0, half, 2, dtype=np.float64) / half)
    ang_r = (t // GRID_W)[:, None] * inv_freq[None, :]
    ang_c = (t % GRID_W)[:, None] * inv_freq[None, :]
    cos = np.concatenate([np.cos(ang_r)] * 2 + [np.cos(ang_c)] * 2, axis=-1)
    sin = np.concatenate([-np.sin(ang_r), np.sin(ang_r), -np.sin(ang_c), np.sin(ang_c)], axis=-1)
    return jnp.asarray(cos, dtype=f32), jnp.asarray(sin, dtype=f32)


def kernel(x_prompt, x_sample, c, cache_na_k, cache_na_v, cache_gqa_k, cache_gqa_v, c_ctx, norm1_g, norm2_g, ada_w, ada_b, ev_w_in, ev_na_bias, ev_fnet_w, ev_w_out, od_w_in, od_pool_w, od_pool_scale, od_q_norm_g, od_k_norm_g, od_w_out, ffn_w_up, ffn_conv_w, ffn_conv_b, ffn_w_down, final_norm_g):
    n_ctx, seq_p, _ = x_prompt.shape
    n_lat, seq_s, _ = x_sample.shape
    past = cache_na_k.shape[2]
    assert n_lat + 1 <= MOD_ROWS
    xp = x_prompt.reshape(n_ctx * seq_p, D_MODEL)
    xs = x_sample.reshape(n_lat * seq_s, D_MODEL)

    cond = jnp.zeros((MOD_ROWS, D_MODEL), f32).at[0].set(c_ctx).at[1:1 + n_lat].set(c)
    mods = ada_modulation(cond, ada_w, ada_b).reshape(DEPTH, MOD_ROWS * 6, 1, D_MODEL)
    rope = _rope_tables(seq_s)

    ctx = dict(seq=seq_p, base_row=0, per_seq=0)
    lat = dict(seq=seq_s, base_row=1, per_seq=1)
    new_k, new_v = {}, {}
    for i in range(DEPTH):
        j = i // 2
        mod = mods[i]
        w_up = ffn_w_up[i].astype(bf16)
        w_down = ffn_w_down[i].astype(bf16)
        if i % 2 == 0:
            w_in = ev_w_in[j].astype(bf16)
            w_attn = ev_w_out[j, :NA_WIDTH].astype(bf16)
            w_mix = ev_w_out[j, NA_WIDTH:].astype(bf16)
            mix_blk = 3 * NA_WIDTH // FNET_WIDTH
            proj_p = in_proj(xp, mod, norm1_g[i], w_in, f32, NA_WIDTH, **ctx)
            new_k[i] = proj_p[:, NA_WIDTH:2 * NA_WIDTH]
            new_v[i] = proj_p[:, 2 * NA_WIDTH:3 * NA_WIDTH]
            attn_p = ctx_attention(proj_p, seq_p, NA_HEADS, NA_HEADS)
            mix_p = fourier_mix(proj_p, mix_blk, ev_fnet_w[j], seq_p)
            proj_s = in_proj(xs, mod, norm1_g[i], w_in, bf16, NA_WIDTH, **lat)
            attn_s = na_attention(proj_s,
                                  cache_na_k[:, j].reshape(n_lat * past, NA_WIDTH),
                                  cache_na_v[:, j].reshape(n_lat * past, NA_WIDTH),
                                  ev_na_bias[j], seq_s, past)
            mix_s = fourier_mix(proj_s, mix_blk, ev_fnet_w[j], seq_s)
        else:
            w_in = jnp.concatenate([od_w_in[j, :, POOL_WIDTH:], od_w_in[j, :, :POOL_WIDTH]],
                                   axis=1).astype(bf16)
            w_attn = od_w_out[j, POOL_WIDTH:].astype(bf16)
            w_mix = od_w_out[j, :POOL_WIDTH].astype(bf16)
            norm = (od_q_norm_g[j], od_k_norm_g[j], GQA_KV_WIDTH)
            k_lo = GQA_Q_WIDTH
            v_lo = GQA_Q_WIDTH + GQA_KV_WIDTH
            mix_blk = (GQA_Q_WIDTH + 2 * GQA_KV_WIDTH) // POOL_WIDTH
            proj_p = in_proj(xp, mod, norm1_g[i], w_in, f32, GQA_Q_WIDTH, norm=norm, **ctx)
            new_k[i] = proj_p[:, k_lo:k_lo + GQA_KV_WIDTH]
            new_v[i] = proj_p[:, v_lo:v_lo + GQA_KV_WIDTH]
            attn_p = ctx_attention(proj_p, seq_p, GQA_Q_HEADS, GQA_KV_HEADS)
            mix_p = pool_mix(proj_p, mix_blk, od_pool_w[j], od_pool_scale[j], seq_p)
            proj_s = in_proj(xs, mod, norm1_g[i], w_in, bf16, GQA_Q_WIDTH, norm=norm, rope=rope, **lat)
            attn_s = gqa_attention(proj_s,
                                   cache_gqa_k[:, j].reshape(n_lat * past, GQA_KV_WIDTH),
                                   cache_gqa_v[:, j].reshape(n_lat * past, GQA_KV_WIDTH),
                                   seq_s, past)
            mix_s = pool_mix(proj_s, mix_blk, od_pool_w[j], od_pool_scale[j], seq_s)
        xp = out_proj(xp, mod, attn_p, mix_p, w_attn, w_mix, **ctx)
        xs = out_proj(xs, mod, attn_s, mix_s, w_attn, w_mix, **lat)
        final_g = final_norm_g if i == DEPTH - 1 else None
        xp = conv_ffn(xp, mod, norm2_g[i], w_up, ffn_conv_w[i], ffn_conv_b[i], w_down,
                      final_g=final_g, **ctx)
        xs = conv_ffn(xs, mod, norm2_g[i], w_up, ffn_conv_w[i], ffn_conv_b[i], w_down,
                      final_g=final_g, **lat)

    y_prompt = xp.reshape(n_ctx, seq_p, D_MODEL)
    y_sample = xs.reshape(n_lat, seq_s, D_MODEL)
    even = [i for i in range(DEPTH) if i % 2 == 0]
    odd = [i for i in range(DEPTH) if i % 2 == 1]

    def stack(parts, layers, heads):
        return jnp.stack([parts[i].reshape(n_ctx, seq_p, heads, HEAD_DIM) for i in layers], axis=1)

    return (y_prompt, y_sample,
            stack(new_k, even, NA_HEADS), stack(new_v, even, NA_HEADS),
            stack(new_k, odd, GQA_KV_HEADS), stack(new_v, odd, GQA_KV_HEADS))
```

```python
import functools
import math

import jax
import jax.numpy as jnp
import numpy as np
from jax import lax
from jax.experimental import pallas as pl
from jax.experimental.pallas import tpu as pltpu

f32 = jnp.float32
bf16 = jnp.bfloat16

D_MODEL = 2048
DEPTH = 2
GRID_W = 64
HEAD_DIM = 128
NA_HEADS = 12
NA_KR = 8
NA_KC = 16
FNET_GROUPS = 4
FNET_CH = 128
POOL_WINDOWS = (2, 4, 8, 16)
GQA_Q_HEADS = 12
GQA_KV_HEADS = 4
D_FF = 5632
ROPE_BASE = 10000.0
EPS = 1e-6
NEG_INF = -1e30
NA_WIDTH = NA_HEADS * HEAD_DIM
FNET_WIDTH = FNET_GROUPS * FNET_CH
POOL_WIDTH = len(POOL_WINDOWS) * FNET_CH
GQA_Q_WIDTH = GQA_Q_HEADS * HEAD_DIM
GQA_KV_WIDTH = GQA_KV_HEADS * HEAD_DIM
LOG2_E = math.log2(math.e)
ATTN_SCALE = HEAD_DIM ** -0.5 * LOG2_E

MOD_ROWS = 16
VMEM_LIMIT = 60 * 1024 * 1024
HALO = 8
NA_QROWS = 4
NA_WROWS = 12
NA_INVALID = 2 * NA_KR - 1
GQA_QUERIES = 512
NA_HEADS_PER_STEP = 6
FFN_TOKENS = 1024
FFN_ROWS = 128
W_TILE = 512
IN_TOKENS = 1024
IN_ROWS = 256


def _params(*sem, flags=None):
    return pltpu.CompilerParams(dimension_semantics=sem, vmem_limit_bytes=VMEM_LIMIT, flags=flags)


def _rms_modulate(x, g, shift, scale):
    ms = jnp.mean(x * x, axis=-1, keepdims=True)
    return (x * lax.rsqrt(ms + EPS)) * (g * (1.0 + scale)) + shift


def _mod_spec(k, tm, seq, base_row, per_seq):
    def index(m, n):
        return ((base_row + per_seq * ((m * tm) // seq)) * 6 + k, 0, 0)
    return pl.BlockSpec((None, 1, D_MODEL), index)


def _ada_kernel(c_ref, w_ref, b_ref, o_ref):
    c = c_ref[...]
    s = (c * jax.nn.sigmoid(c)).astype(bf16)
    o_ref[...] = jnp.dot(s, w_ref[...].astype(bf16), preferred_element_type=f32) + b_ref[...]


def ada_modulation(cond, ada_w, ada_b):
    tn = 1024
    n_out = 6 * D_MODEL
    return pl.pallas_call(
        _ada_kernel,
        grid=(DEPTH, n_out // tn),
        in_specs=[
            pl.BlockSpec((MOD_ROWS, D_MODEL), lambda i, n: (0, 0)),
            pl.BlockSpec((None, D_MODEL, tn), lambda i, n: (i, 0, n)),
            pl.BlockSpec((None, 1, tn), lambda i, n: (i, 0, n)),
        ],
        out_specs=pl.BlockSpec((None, MOD_ROWS, tn), lambda i, n: (i, 0, n)),
        out_shape=jax.ShapeDtypeStruct((DEPTH, MOD_ROWS, n_out), f32),
        compiler_params=_params("arbitrary", "arbitrary"),
        name="ada_modulation",
    )(cond, ada_w, ada_b.reshape(DEPTH, 1, n_out))


def _rope(a, cos, sin):
    lane = lax.broadcasted_iota(jnp.int32, a.shape, 1)
    quarter = HEAD_DIM // 4
    partner = jnp.where((lane & (2 * quarter - 1)) < quarter,
                        pltpu.roll(a, HEAD_DIM - quarter, 1),
                        pltpu.roll(a, quarter, 1))
    return a * cos + partner * sin


def _in_proj_kernel(tn, q_blocks, k_blocks, has_norm, has_rope, *refs):
    it = iter(refs)
    x_hbm, shift_ref, scale_ref, g_ref, w_ref = (next(it) for _ in range(5))
    qg_ref = kg_ref = cos_ref = sin_ref = None
    if has_norm:
        qg_ref, kg_ref, avg_ref = next(it), next(it), next(it)
    if has_rope:
        cos_ref, sin_ref = next(it), next(it)
    o_ref, h_ref, x_buf, x_sem = next(it), next(it), next(it), next(it)
    m = pl.program_id(0)
    n = pl.program_id(1)
    tm = h_ref.shape[0]
    everything = slice(0, tm)

    def x_copy(tile, slot):
        return pltpu.make_async_copy(x_hbm.at[pl.ds(tile * tm, tm), :], x_buf.at[slot], x_sem.at[slot])

    slot = m % 2
    x_ref = x_buf.at[slot]

    @pl.when((n == 0) & (m == 0))
    def _():
        x_copy(0, 0).start()

    @pl.when(n == 0)
    def _():
        x_copy(m, slot).wait()

    @pl.when((n == 1) & (m + 1 < pl.num_programs(0)))
    def _():
        x_copy(m + 1, 1 - slot).start()

    def project(rs, h=None):
        h = h_ref[rs, :] if h is None else h
        return jnp.dot(h, w_ref[...], preferred_element_type=f32)

    def store_scaled(rs, acc, scale):
        o_ref[rs, :] = (acc if scale is None else acc * scale).astype(o_ref.dtype)

    def store_normed(rs, acc, gn):
        ms = jnp.dot((acc * acc).astype(bf16), avg_ref[...], preferred_element_type=f32)
        inv = lax.rsqrt(ms + EPS)
        for hh in range(tn // HEAD_DIM):
            sl = slice(hh * HEAD_DIM, (hh + 1) * HEAD_DIM)
            a = acc[:, sl] * inv[:, sl] * gn
            if has_rope:
                a = _rope(a, cos_ref[rs, :], sin_ref[rs, :])
            o_ref[rs, sl] = a.astype(o_ref.dtype)

    @pl.when(n == 0)
    def _():
        g, sh, sc = g_ref[...], shift_ref[...], scale_ref[...]
        for r0 in range(0, tm, IN_ROWS):
            rs = slice(r0, r0 + IN_ROWS)
            h = _rms_modulate(x_ref[rs, :], g, sh, sc).astype(bf16)
            h_ref[rs, :] = h
            if has_norm:
                store_normed(rs, project(rs, h), qg_ref[...] * ATTN_SCALE)
            else:
                store_scaled(rs, project(rs, h), ATTN_SCALE)

    is_q = n < q_blocks
    q_scale = jnp.where(is_q, ATTN_SCALE, 1.0)
    if not has_norm:
        @pl.when(n > 0)
        def _():
            store_scaled(everything, project(everything), q_scale)
        return

    is_qk = n < q_blocks + k_blocks

    @pl.when((n > 0) & is_qk)
    def _():
        gn = jnp.where(is_q, qg_ref[...], kg_ref[...]) * q_scale
        store_normed(everything, project(everything), gn)

    @pl.when(jnp.logical_not(is_qk))
    def _():
        store_scaled(everything, project(everything), None)


def in_proj(x, mod, g, w, out_dtype, q_width, seq, base_row, per_seq, norm=None, rope=None, w_rot=0):
    t_tokens = x.shape[0]
    tm, tn = IN_TOKENS, W_TILE
    n_total = w.shape[1]
    assert n_total // tn >= 2
    in_specs = [
        pl.BlockSpec(memory_space=pl.ANY),
        _mod_spec(0, tm, seq, base_row, per_seq),
        _mod_spec(1, tm, seq, base_row, per_seq),
        pl.BlockSpec((1, D_MODEL), lambda m, n: (0, 0)),
        pl.BlockSpec((D_MODEL, tn), lambda m, n: (0, (n + w_rot) % (n_total // tn))),
    ]
    args = [x, mod, mod, g.reshape(1, D_MODEL), w]
    k_width = 0
    if norm is not None:
        k_width = norm[2]
        in_specs += [pl.BlockSpec((1, HEAD_DIM), lambda m, n: (0, 0))] * 2
        in_specs += [pl.BlockSpec((tn, tn), lambda m, n: (0, 0))]
        head_of = np.arange(tn) // HEAD_DIM
        avg = (head_of[:, None] == head_of[None, :]) / HEAD_DIM
        args += [norm[0].reshape(1, HEAD_DIM), norm[1].reshape(1, HEAD_DIM), jnp.asarray(avg, dtype=bf16)]
    if rope is not None:
        spb = seq // tm
        in_specs += [pl.BlockSpec((tm, HEAD_DIM), lambda m, n: (m % spb, 0))] * 2
        args += [rope[0], rope[1]]
    kern = functools.partial(_in_proj_kernel, tn, q_width // tn, k_width // tn,
                             norm is not None, rope is not None)
    return pl.pallas_call(
        kern,
        grid=(t_tokens // tm, n_total // tn),
        in_specs=in_specs,
        out_specs=pl.BlockSpec((tm, tn), lambda m, n: (m, n)),
        out_shape=jax.ShapeDtypeStruct((t_tokens, n_total), out_dtype),
        scratch_shapes=[pltpu.VMEM((tm, D_MODEL), bf16),
                        pltpu.VMEM((2, tm, D_MODEL), f32),
                        pltpu.SemaphoreType.DMA((2,))],
        compiler_params=_params("arbitrary", "arbitrary"),
        name="in_proj",
    )(*args)


def _with_ones(v):
    return jnp.concatenate([v, jnp.ones_like(v)], axis=1)


def _softmax_pv(parts):
    m = functools.reduce(jnp.maximum, [jnp.max(s, axis=-1, keepdims=True) for s, _ in parts])
    acc = sum(jnp.dot(jnp.exp2(s - m).astype(bf16), v1, preferred_element_type=f32) for s, v1 in parts)
    return acc[:, :HEAD_DIM] / acc[:, HEAD_DIM:]


def _ctx_attn_kernel(n_q, n_kv, q_ref, k_ref, v_ref, o_ref):
    group = n_q // n_kv
    values = [_with_ones(v_ref[:, kv * HEAD_DIM:(kv + 1) * HEAD_DIM].astype(bf16)) for kv in range(n_kv)]
    keys = [k_ref[:, kv * HEAD_DIM:(kv + 1) * HEAD_DIM].astype(bf16) for kv in range(n_kv)]
    dn = (((1,), (1,)), ((), ()))
    scores = [lax.dot_general(q_ref[:, h * HEAD_DIM:(h + 1) * HEAD_DIM].astype(bf16), keys[h // group],
                              dn, preferred_element_type=f32) for h in range(n_q)]
    for h, s in enumerate(scores):
        o = _softmax_pv([(s, values[h // group])])
        o_ref[:, h * HEAD_DIM:(h + 1) * HEAD_DIM] = o.astype(o_ref.dtype)


def ctx_attention(proj, seq, n_q, n_kv):
    t_tokens = proj.shape[0]
    qw, kw = n_q * HEAD_DIM, n_kv * HEAD_DIM
    assert qw % kw == 0
    k_blk = qw // kw
    return pl.pallas_call(
        functools.partial(_ctx_attn_kernel, n_q, n_kv),
        grid=(t_tokens // seq,),
        in_specs=[
            pl.BlockSpec((seq, qw), lambda b: (b, 0)),
            pl.BlockSpec((seq, kw), lambda b: (b, k_blk)),
            pl.BlockSpec((seq, kw), lambda b: (b, k_blk + 1)),
        ],
        out_specs=pl.BlockSpec((seq, qw), lambda b: (b, 0)),
        out_shape=jax.ShapeDtypeStruct((t_tokens, qw), bf16),
        compiler_params=_params("arbitrary"),
        name="ctx_attention",
    )(proj, proj, proj)


def _gqa_attn_kernel(group, q_ref, ck_ref, cv_ref, k_ref, v_ref, o_ref):
    ck = ck_ref[...].astype(bf16)
    cv = _with_ones(cv_ref[...].astype(bf16))
    k = k_ref[...]
    v = _with_ones(v_ref[...])
    dn = (((1,), (1,)), ((), ()))
    scores = []
    for gi in range(group):
        q = q_ref[:, gi * HEAD_DIM:(gi + 1) * HEAD_DIM]
        scores.append((lax.dot_general(q, ck, dn, preferred_element_type=f32),
                       lax.dot_general(q, k, dn, preferred_element_type=f32)))
    for gi, (s_c, s_l) in enumerate(scores):
        o = _softmax_pv([(s_c, cv), (s_l, v)])
        o_ref[:, gi * HEAD_DIM:(gi + 1) * HEAD_DIM] = o.astype(o_ref.dtype)


def gqa_attention(proj, cache_k, cache_v, seq, past):
    t_tokens = proj.shape[0]
    n_b = t_tokens // seq
    group = GQA_Q_HEADS // GQA_KV_HEADS
    tq = GQA_QUERIES
    nq = seq // tq
    return pl.pallas_call(
        functools.partial(_gqa_attn_kernel, group),
        grid=(n_b, GQA_KV_HEADS, nq),
        in_specs=[
            pl.BlockSpec((tq, group * HEAD_DIM), lambda b, h, i: (b * nq + i, h)),
            pl.BlockSpec((past, HEAD_DIM), lambda b, h, i: (b, h)),
            pl.BlockSpec((past, HEAD_DIM), lambda b, h, i: (b, h)),
            pl.BlockSpec((seq, HEAD_DIM), lambda b, h, i: (b, GQA_Q_HEADS + h)),
            pl.BlockSpec((seq, HEAD_DIM), lambda b, h, i: (b, GQA_Q_HEADS + GQA_KV_HEADS + h)),
        ],
        out_specs=pl.BlockSpec((tq, group * HEAD_DIM), lambda b, h, i: (b * nq + i, h)),
        out_shape=jax.ShapeDtypeStruct((t_tokens, GQA_Q_WIDTH), bf16),
        compiler_params=_params("arbitrary", "arbitrary", "arbitrary"),
        name="gqa_attention",
    )(proj, cache_k, cache_v, proj, proj)


def _na_bias_tiles(rel_bias):
    qc = np.arange(GRID_W)[:, None]
    kc = np.arange(GRID_W)[None, :]
    q_start = np.clip(qc - NA_KC // 2, 0, GRID_W - NA_KC)
    valid = (kc >= q_start) & (kc < q_start + NA_KC)
    dc = np.clip(kc - qc, -(NA_KC - 1), NA_KC - 1) + NA_KC - 1
    pick = (np.arange(2 * NA_KC - 1)[:, None, None] == dc[None]) & valid[None]
    tiles = jnp.einsum("hdc,cqk->hdqk", rel_bias.astype(f32) * LOG2_E, jnp.asarray(pick, dtype=f32),
                       precision=lax.Precision.HIGHEST)
    tiles = jnp.where(jnp.asarray(valid)[None, None], tiles, NEG_INF)
    masked = jnp.full((NA_HEADS, 1, GRID_W, GRID_W), NEG_INF, f32)
    tiles = jnp.concatenate([tiles, masked], axis=1)
    zeros = jnp.zeros_like(tiles)
    left = jnp.concatenate([tiles, zeros], axis=-1)
    right = jnp.concatenate([zeros, tiles], axis=-1)
    return left, right


def _na_attn_kernel(n_heads, n_rows, q_ref, k_ref, v_ref, ck_ref, cv_ref, bl_ref, br_ref, o_ref):
    rb = pl.program_id(2)
    start = jnp.clip(rb * NA_QROWS - NA_KR // 2, 0, n_rows - NA_WROWS)
    tok0 = pl.multiple_of(start * GRID_W, GRID_W)
    dn = (((1,), (1,)), ((), ()))

    def tile_index(i, kr):
        qr = rb * NA_QROWS + i
        r0 = jnp.clip(qr - NA_KR // 2, 0, n_rows - NA_KR)
        return jnp.where((kr >= r0) & (kr < r0 + NA_KR), kr - qr + NA_KR - 1, NA_INVALID)

    index = [[(tile_index(i, start + 2 * jp), tile_index(i, start + 2 * jp + 1))
              for jp in range(NA_WROWS // 2)] for i in range(NA_QROWS)]
    scores = []
    for hh in range(n_heads):
        hs = slice(hh * HEAD_DIM, (hh + 1) * HEAD_DIM)
        q = q_ref[:, hs]
        bias = jnp.concatenate(
            [jnp.concatenate([bl_ref[hh, ia] + br_ref[hh, ib] for ia, ib in row], axis=1)
             for row in index], axis=0)
        s_l = lax.dot_general(q, k_ref[pl.ds(tok0, NA_WROWS * GRID_W), hs], dn,
                              preferred_element_type=f32) + bias
        s_c = lax.dot_general(q, ck_ref[:, hs].astype(bf16), dn, preferred_element_type=f32)
        scores.append((s_c, s_l))
    for hh, (s_c, s_l) in enumerate(scores):
        hs = slice(hh * HEAD_DIM, (hh + 1) * HEAD_DIM)
        vw = v_ref[pl.ds(tok0, NA_WROWS * GRID_W), hs]
        o = _softmax_pv([(s_c, _with_ones(cv_ref[:, hs].astype(bf16))), (s_l, _with_ones(vw))])
        o_ref[:, hs] = o.astype(o_ref.dtype)


def na_attention(proj, cache_k, cache_v, rel_bias, seq, past):
    t_tokens = proj.shape[0]
    n_b = t_tokens // seq
    n_rows = seq // GRID_W
    n_rb = n_rows // NA_QROWS
    tq = NA_QROWS * GRID_W
    hb = NA_HEADS_PER_STEP
    n_hg = NA_HEADS // hb
    wide = hb * HEAD_DIM
    bias_l, bias_r = _na_bias_tiles(rel_bias)
    bias_spec = pl.BlockSpec((hb, 2 * NA_KR, GRID_W, 2 * GRID_W), lambda b, h, r: (h, 0, 0, 0))
    return pl.pallas_call(
        functools.partial(_na_attn_kernel, hb, n_rows),
        grid=(n_b, n_hg, n_rb),
        in_specs=[
            pl.BlockSpec((tq, wide), lambda b, h, r: (b * n_rb + r, h)),
            pl.BlockSpec((seq, wide), lambda b, h, r: (b, n_hg + h)),
            pl.BlockSpec((seq, wide), lambda b, h, r: (b, 2 * n_hg + h)),
            pl.BlockSpec((past, wide), lambda b, h, r: (b, h)),
            pl.BlockSpec((past, wide), lambda b, h, r: (b, h)),
            bias_spec,
            bias_spec,
        ],
        out_specs=pl.BlockSpec((tq, wide), lambda b, h, r: (b * n_rb + r, h)),
        out_shape=jax.ShapeDtypeStruct((t_tokens, NA_WIDTH), bf16),
        compiler_params=_params("arbitrary", "arbitrary", "arbitrary"),
        name="na_attention",
    )(proj, proj, proj, cache_k, cache_v, bias_l, bias_r)


def _dft_cols_kernel(x_ref, cs_ref, y_ref):
    for g in range(FNET_GROUPS):
        sl = slice(g * FNET_CH, (g + 1) * FNET_CH)
        y = jnp.dot(x_ref[:, sl].astype(bf16), cs_ref[...], preferred_element_type=f32)
        y_ref[0, :, sl] = y[:, :FNET_CH].astype(bf16)
        y_ref[1, :, sl] = y[:, FNET_CH:].astype(bf16)


def _dft_rows_kernel(d_ref, y_ref, w_ref, o_ref):
    f = jnp.dot(d_ref[...], y_ref[...], preferred_element_type=f32)
    for g in range(FNET_GROUPS):
        sl = slice(g * FNET_CH, (g + 1) * FNET_CH)
        o = jnp.dot(f[:, sl].astype(bf16), w_ref[g], preferred_element_type=f32)
        o_ref[:, sl] = o.astype(o_ref.dtype)


def _dft_tables(seq):
    c = np.arange(FNET_CH, dtype=np.int64)
    ang_c = (2.0 * np.pi / FNET_CH) * ((c[:, None] * c[None, :]) % FNET_CH)
    cs = np.concatenate([np.cos(ang_c), np.sin(ang_c)], axis=1)
    n = np.arange(seq, dtype=np.int64)
    ang_n = (2.0 * np.pi / seq) * ((n[:, None] * n[None, :]) % seq)
    norm = 1.0 / math.sqrt(seq * FNET_CH)
    d = np.concatenate([np.cos(ang_n) * norm, -np.sin(ang_n) * norm], axis=1)
    return jnp.asarray(cs, dtype=bf16), jnp.asarray(d, dtype=bf16)


def fourier_mix(proj, col_blk, fnet_w, seq):
    t_tokens = proj.shape[0]
    n_b = t_tokens // seq
    cs, d = _dft_tables(seq)
    tn = min(seq, 512)
    ns = seq // tn
    y = pl.pallas_call(
        _dft_cols_kernel,
        grid=(n_b, ns),
        in_specs=[
            pl.BlockSpec((tn, FNET_WIDTH), lambda b, i: (b * ns + i, col_blk)),
            pl.BlockSpec((FNET_CH, 2 * FNET_CH), lambda b, i: (0, 0)),
        ],
        out_specs=pl.BlockSpec((2, tn, FNET_WIDTH), lambda b, i: (0, i, b)),
        out_shape=jax.ShapeDtypeStruct((2, seq, n_b * FNET_WIDTH), bf16),
        compiler_params=_params("arbitrary", "arbitrary"),
        name="dft_cols",
    )(proj, cs)
    y2 = y.reshape(2 * seq, n_b * FNET_WIDTH)
    return pl.pallas_call(
        _dft_rows_kernel,
        grid=(n_b, ns),
        in_specs=[
            pl.BlockSpec((tn, 2 * seq), lambda b, i: (i, 0)),
            pl.BlockSpec((2 * seq, FNET_WIDTH), lambda b, i: (0, b)),
            pl.BlockSpec((FNET_GROUPS, FNET_CH, FNET_CH), lambda b, i: (0, 0, 0)),
        ],
        out_specs=pl.BlockSpec((tn, FNET_WIDTH), lambda b, i: (b * ns + i, 0)),
        out_shape=jax.ShapeDtypeStruct((t_tokens, FNET_WIDTH), bf16),
        compiler_params=_params("arbitrary", "arbitrary"),
        name="dft_rows",
    )(d, y2, fnet_w.astype(bf16))


def _pool_kernel(seq, x_ref, w_ref, sc_ref, o_ref):
    t = lax.broadcasted_iota(jnp.int32, (seq, 1), 0)

    def prev(a, k):
        return jnp.where(t >= k, pltpu.roll(a, k, 0), 0.0)

    def nxt(a, k):
        return jnp.where(t < seq - k, pltpu.roll(a, seq - k, 0), 0.0)

    for g, win in enumerate(POOL_WINDOWS):
        sl = slice(g * FNET_CH, (g + 1) * FNET_CH)
        x = x_ref[:, sl].astype(f32)
        half = win // 2
        back, fwd, k = x, x, 1
        while k < half:
            back = back + prev(back, k)
            fwd = fwd + nxt(fwd, k)
            k *= 2
        total = prev(back, 1) + fwd
        cnt = (jnp.minimum(t + half, seq) - jnp.maximum(t - half, 0)).astype(f32)
        pooled = (total / cnt - x).astype(bf16)
        o = jnp.dot(pooled, w_ref[g], preferred_element_type=f32) * sc_ref[:, sl]
        o_ref[:, sl] = o.astype(o_ref.dtype)


def pool_mix(proj, col_blk, pool_w, pool_scale, seq):
    t_tokens = proj.shape[0]
    return pl.pallas_call(
        functools.partial(_pool_kernel, seq),
        grid=(t_tokens // seq,),
        in_specs=[
            pl.BlockSpec((seq, POOL_WIDTH), lambda b: (b, col_blk)),
            pl.BlockSpec((len(POOL_WINDOWS), FNET_CH, FNET_CH), lambda b: (0, 0, 0)),
            pl.BlockSpec((1, POOL_WIDTH), lambda b: (0, 0)),
        ],
        out_specs=pl.BlockSpec((seq, POOL_WIDTH), lambda b: (b, 0)),
        out_shape=jax.ShapeDtypeStruct((t_tokens, POOL_WIDTH), bf16),
        compiler_params=_params("arbitrary"),
        name="pool_mix",
    )(proj, pool_w.astype(bf16), pool_scale.reshape(1, POOL_WIDTH))


def _out_proj_kernel(x_ref, gate_ref, a_ref, b_ref, wa_ref, wb_ref, o_ref):
    y = (jnp.dot(a_ref[...], wa_ref[...], preferred_element_type=f32)
         + jnp.dot(b_ref[...], wb_ref[...], preferred_element_type=f32))
    o_ref[...] = x_ref[...] + gate_ref[...] * y


def out_proj(x, mod, a, b, wa, wb, seq, base_row, per_seq):
    t_tokens = x.shape[0]
    tm = 512
    return pl.pallas_call(
        _out_proj_kernel,
        grid=(t_tokens // tm, 1),
        in_specs=[
            pl.BlockSpec((tm, D_MODEL), lambda m, n: (m, 0)),
            _mod_spec(2, tm, seq, base_row, per_seq),
            pl.BlockSpec((tm, a.shape[1]), lambda m, n: (m, 0)),
            pl.BlockSpec((tm, b.shape[1]), lambda m, n: (m, 0)),
            pl.BlockSpec(wa.shape, lambda m, n: (0, 0)),
            pl.BlockSpec(wb.shape, lambda m, n: (0, 0)),
        ],
        out_specs=pl.BlockSpec((tm, D_MODEL), lambda m, n: (m, 0)),
        out_shape=jax.ShapeDtypeStruct((t_tokens, D_MODEL), f32),
        compiler_params=_params("arbitrary", "arbitrary"),
        name="out_proj",
    )(x, mod, a, b, wa, wb)


def _ffn_kernel(tm, seq, nj, final, *refs):
    it = iter(refs)
    (x_hbm, xp_ref, xn_ref, shift_ref, scale_ref, gate_ref, g_ref,
     wv_ref, wg_ref, cwv_ref, cwg_ref, cbv_ref, cbg_ref, wd_ref) = (next(it) for _ in range(14))
    gf_ref = next(it) if final else None
    o_hbm, h_ref, acc_ref = next(it), next(it), next(it)
    u_refs = [(next(it), next(it)), (next(it), next(it))]
    x_buf, x_sem, o_sem = next(it), next(it), next(it)
    i = pl.program_id(0)
    j = pl.program_id(1)
    n_tiles = pl.num_programs(0)
    rows = tm + 2 * HALO

    def x_copy(tile, slot):
        return pltpu.make_async_copy(x_hbm.at[pl.ds(tile * tm, tm), :], x_buf.at[slot], x_sem.at[slot])

    def o_copy(tile):
        return pltpu.make_async_copy(acc_ref, o_hbm.at[pl.ds(tile * tm, tm), :], o_sem.at[0])

    slot = i % 2
    x_ref = x_buf.at[slot]

    @pl.when((j == 0) & (i == 0))
    def _():
        x_copy(0, 0).start()

    @pl.when(j == 0)
    def _():
        x_copy(i, slot).wait()

    @pl.when((j == 1) & (i + 1 < n_tiles))
    def _():
        x_copy(i + 1, 1 - slot).start()

    @pl.when((j == 1) & (i > 0))
    def _():
        o_copy(i - 1).wait()

    @pl.when(j == 1)
    def _():
        acc_ref[...] = jnp.zeros_like(acc_ref)

    def prologue_and_first_up():
        g, sh, sc = g_ref[...], shift_ref[...], scale_ref[...]
        uv_ref, ug_ref = u_refs[0]

        def emit(rs, h):
            h_ref[rs, :] = h
            uv_ref[rs, :] = jnp.dot(h, wv_ref[...], preferred_element_type=f32)
            ug_ref[rs, :] = jnp.dot(h, wg_ref[...], preferred_element_type=f32)

        for r0 in range(0, tm - IN_ROWS, IN_ROWS):
            rs = slice(r0, r0 + IN_ROWS)
            emit(rs, _rms_modulate(x_ref[rs, :], g, sh, sc).astype(bf16))
        next_ok = jnp.where((((i + 1) * tm) & (seq - 1)) != 0, 1.0, 0.0)
        prev_ok = jnp.where(((i * tm) & (seq - 1)) != 0, 1.0, 0.0)
        h_last = _rms_modulate(x_ref[tm - IN_ROWS:tm, :], g, sh, sc)
        h_next = _rms_modulate(xn_ref[...], g, sh, sc) * next_ok
        h_prev = _rms_modulate(xp_ref[...], g, sh, sc) * prev_ok
        emit(slice(tm - IN_ROWS, rows), jnp.concatenate([h_last, h_next, h_prev], axis=0).astype(bf16))

    def zero_row(a, r):
        r0 = (r // HALO) * HALO
        row8 = lax.broadcasted_iota(jnp.int32, (HALO, 1), 0)
        fixed = jnp.where(row8 == r - r0, 0.0, a[r0:r0 + HALO])
        return jnp.concatenate([a[:r0], fixed, a[r0 + HALO:]], axis=0)

    def window(u_ref, r0):
        lo, hi = r0 - HALO, r0 + FFN_ROWS + HALO
        if lo < 0:
            return jnp.concatenate([u_ref[rows + lo:rows, :], u_ref[0:hi, :]], axis=0)
        return u_ref[lo:hi, :]

    def conv(u_ref, r0, cw_ref, cb_ref):
        u = window(u_ref, r0)
        n = FFN_ROWS + 2 * HALO
        up = pltpu.roll(u, 1, 0)
        un = pltpu.roll(u, n - 1, 0)
        for edge in range(seq, tm, seq):
            if r0 <= edge < r0 + FFN_ROWS:
                up = zero_row(up, edge - r0 + HALO)
            if r0 <= edge - 1 < r0 + FFN_ROWS:
                un = zero_row(un, edge - 1 - r0 + HALO)
        keep = slice(HALO, HALO + FFN_ROWS)
        return (up[keep] * cw_ref[0:1, :] + u[keep] * cw_ref[1:2, :] + un[keep] * cw_ref[2:3, :]
                + cb_ref[...])

    def gated(slot):
        uv_ref, ug_ref = u_refs[slot]
        chunks = []
        for r0 in range(0, tm, FFN_ROWS):
            val = conv(uv_ref, r0, cwv_ref, cbv_ref)
            gate = conv(ug_ref, r0, cwg_ref, cbg_ref)
            chunks.append((gate * jax.nn.sigmoid(gate) * val).astype(bf16))
        return chunks

    def zero_after(chunks):
        parts = []
        for c in chunks:
            bits = pltpu.bitcast(c, jnp.uint32)
            parts += [bits[k:k + HALO] for k in range(0, bits.shape[0], HALO)]
        while len(parts) > 1:
            parts = [a | b for a, b in zip(parts[0::2], parts[1::2])] + parts[len(parts) & ~1:]
        return (parts[0] >> 16) >> 16

    def after(a, zero, row0=0):
        tf = zero.shape[1]
        mid = pltpu.bitcast(pltpu.bitcast(a[row0:row0 + 2 * HALO, 0:tf], jnp.uint32) | zero, bf16)
        if tf < a.shape[1]:
            mid = jnp.concatenate([mid, a[row0:row0 + 2 * HALO, tf:]], axis=1)
        parts = ([a[:row0]] if row0 else []) + [mid, a[row0 + 2 * HALO:]]
        return jnp.concatenate(parts, axis=0)

    def quarter(chunks, k):
        n = len(chunks) // 4
        return chunks[k * n:(k + 1) * n]

    def up_project(slot, chunks):
        h = h_ref[...]
        wv, wg = wv_ref[...], wg_ref[...]
        uv_ref, ug_ref = u_refs[slot]
        k_mid = wv.shape[0] // 2
        wv = after(wv, zero_after(quarter(chunks, 0)), k_mid)
        wg = after(wg, zero_after(quarter(chunks, 2)), k_mid)
        uv_ref[...] = jnp.dot(h, wv, preferred_element_type=f32)
        h = after(h, zero_after(quarter(chunks, 1)))
        ug_ref[...] = jnp.dot(h, wg, preferred_element_type=f32)

    def down_project(chunks, anchored):
        if anchored:
            chunks = [after(chunks[0], zero_after(quarter(chunks, 3)))] + chunks[1:]
        act = jnp.concatenate(chunks, axis=0)
        acc_ref[...] += jnp.dot(act, wd_ref[...], preferred_element_type=f32)

    @pl.when(j == 0)
    def _():
        prologue_and_first_up()

    for p in range(2):
        @pl.when((j >= 1) & (j < nj) & (j % 2 == p))
        def _(p=p):
            chunks = gated(1 - p)
            up_project(p, chunks)
            down_project(chunks, True)

    @pl.when(j == nj)
    def _():
        down_project(gated((nj - 1) % 2), False)
        y = x_ref[...] + gate_ref[...] * acc_ref[...]
        if final:
            ms = jnp.mean(y * y, axis=-1, keepdims=True)
            y = y * lax.rsqrt(ms + EPS) * gf_ref[...]
        acc_ref[...] = y
        o_copy(i).start()

    @pl.when((j == nj) & (i == n_tiles - 1))
    def _():
        o_copy(i).wait()


def conv_ffn(x, mod, g, w_up, conv_w, conv_b, w_down, seq, base_row, per_seq, final_g=None):
    t_tokens = x.shape[0]
    tm, tf = FFN_TOKENS, W_TILE
    assert seq & (seq - 1) == 0 and (tm // FFN_ROWS) % 4 == 0
    nj = D_FF // tf
    hb = tm // HALO
    n_hblk = t_tokens // HALO
    conv_b2 = conv_b.reshape(1, 2 * D_FF)

    def blk(j):
        return jnp.clip(j, 0, nj - 1)

    assert nj >= 2
    in_specs = [
        pl.BlockSpec(memory_space=pl.ANY),
        pl.BlockSpec((HALO, D_MODEL), lambda i, j: (jnp.maximum(i * hb - 1, 0), 0)),
        pl.BlockSpec((HALO, D_MODEL), lambda i, j: (jnp.minimum((i + 1) * hb, n_hblk - 1), 0)),
        _mod_spec(3, tm, seq, base_row, per_seq),
        _mod_spec(4, tm, seq, base_row, per_seq),
        _mod_spec(5, tm, seq, base_row, per_seq),
        pl.BlockSpec((1, D_MODEL), lambda i, j: (0, 0)),
        pl.BlockSpec((D_MODEL, tf), lambda i, j: (0, blk(j))),
        pl.BlockSpec((D_MODEL, tf), lambda i, j: (0, nj + blk(j))),
        pl.BlockSpec((3, tf), lambda i, j: (0, blk(j - 1))),
        pl.BlockSpec((3, tf), lambda i, j: (0, nj + blk(j - 1))),
        pl.BlockSpec((1, tf), lambda i, j: (0, blk(j - 1))),
        pl.BlockSpec((1, tf), lambda i, j: (0, nj + blk(j - 1))),
        pl.BlockSpec((tf, D_MODEL), lambda i, j: (blk(j - 1), 0)),
    ]
    args = [x, x, x, mod, mod, mod, g.reshape(1, D_MODEL), w_up, w_up,
            conv_w, conv_w, conv_b2, conv_b2, w_down]
    if final_g is not None:
        in_specs.append(pl.BlockSpec((1, D_MODEL), lambda i, j: (0, 0)))
        args.append(final_g.reshape(1, D_MODEL))
    return pl.pallas_call(
        functools.partial(_ffn_kernel, tm, seq, nj, final_g is not None),
        grid=(t_tokens // tm, nj + 1),
        in_specs=in_specs,
        out_specs=pl.BlockSpec(memory_space=pl.ANY),
        out_shape=jax.ShapeDtypeStruct((t_tokens, D_MODEL), f32),
        scratch_shapes=[pltpu.VMEM((tm + 2 * HALO, D_MODEL), bf16),
                        pltpu.VMEM((tm, D_MODEL), f32)]
                       + [pltpu.VMEM((tm + 2 * HALO, tf), f32)] * 4
                       + [pltpu.VMEM((2, tm, D_MODEL), f32),
                          pltpu.SemaphoreType.DMA((2,)),
                          pltpu.SemaphoreType.DMA((1,))],
        compiler_params=_params("arbitrary", "arbitrary"),
        name="conv_ffn",
    )(*args)


def _rope_tables(seq):
    t = np.arange(seq)
    half = HEAD_DIM // 2
    inv_freq = np.power(ROPE_BASE, -np.arange(0, half, 2, dtype=np.float64) / half)
    ang_r = (t // GRID_W)[:, None] * inv_freq[None, :]
    ang_c = (t % GRID_W)[:, None] * inv_freq[None, :]
    cos = np.concatenate([np.cos(ang_r)] * 2 + [np.cos(ang_c)] * 2, axis=-1)
    sin = np.concatenate([-np.sin(ang_r), np.sin(ang_r), -np.sin(ang_c), np.sin(ang_c)], axis=-1)
    return jnp.asarray(cos, dtype=f32), jnp.asarray(sin, dtype=f32)


def kernel(x_prompt, x_sample, c, cache_na_k, cache_na_v, cache_gqa_k, cache_gqa_v, c_ctx, norm1_g, norm2_g, ada_w, ada_b, ev_w_in, ev_na_bias, ev_fnet_w, ev_w_out, od_w_in, od_pool_w, od_pool_scale, od_q_norm_g, od_k_norm_g, od_w_out, ffn_w_up, ffn_conv_w, ffn_conv_b, ffn_w_down, final_norm_g):
    n_ctx, seq_p, _ = x_prompt.shape
    n_lat, seq_s, _ = x_sample.shape
    past = cache_na_k.shape[2]
    assert n_lat + 1 <= MOD_ROWS
    xp = x_prompt.reshape(n_ctx * seq_p, D_MODEL)
    xs = x_sample.reshape(n_lat * seq_s, D_MODEL)

    cond = jnp.zeros((MOD_ROWS, D_MODEL), f32).at[0].set(c_ctx).at[1:1 + n_lat].set(c)
    mods = ada_modulation(cond, ada_w, ada_b).reshape(DEPTH, MOD_ROWS * 6, 1, D_MODEL)
    rope = _rope_tables(seq_s)

    ctx = dict(seq=seq_p, base_row=0, per_seq=0)
    lat = dict(seq=seq_s, base_row=1, per_seq=1)
    new_k, new_v = {}, {}
    for i in range(DEPTH):
        j = i // 2
        mod = mods[i]
        w_up = ffn_w_up[i].astype(bf16)
        w_down = ffn_w_down[i].astype(bf16)
        if i % 2 == 0:
            w_in = ev_w_in[j].astype(bf16)
            w_attn = ev_w_out[j, :NA_WIDTH].astype(bf16)
            w_mix = ev_w_out[j, NA_WIDTH:].astype(bf16)
            mix_blk = 3 * NA_WIDTH // FNET_WIDTH
            proj_p = in_proj(xp, mod, norm1_g[i], w_in, f32, NA_WIDTH, **ctx)
            new_k[i] = proj_p[:, NA_WIDTH:2 * NA_WIDTH]
            new_v[i] = proj_p[:, 2 * NA_WIDTH:3 * NA_WIDTH]
            attn_p = ctx_attention(proj_p, seq_p, NA_HEADS, NA_HEADS)
            mix_p = fourier_mix(proj_p, mix_blk, ev_fnet_w[j], seq_p)
            proj_s = in_proj(xs, mod, norm1_g[i], w_in, bf16, NA_WIDTH, **lat)
            attn_s = na_attention(proj_s,
                                  cache_na_k[:, j].reshape(n_lat * past, NA_WIDTH),
                                  cache_na_v[:, j].reshape(n_lat * past, NA_WIDTH),
                                  ev_na_bias[j], seq_s, past)
            mix_s = fourier_mix(proj_s, mix_blk, ev_fnet_w[j], seq_s)
        else:
            w_in = od_w_in[j].astype(bf16)
            rot = dict(w_rot=POOL_WIDTH // W_TILE)
            w_attn = od_w_out[j, POOL_WIDTH:].astype(bf16)
            w_mix = od_w_out[j, :POOL_WIDTH].astype(bf16)
            norm = (od_q_norm_g[j], od_k_norm_g[j], GQA_KV_WIDTH)
            k_lo = GQA_Q_WIDTH
            v_lo = GQA_Q_WIDTH + GQA_KV_WIDTH
            mix_blk = (GQA_Q_WIDTH + 2 * GQA_KV_WIDTH) // POOL_WIDTH
            proj_p = in_proj(xp, mod, norm1_g[i], w_in, f32, GQA_Q_WIDTH, norm=norm, **rot, **ctx)
            new_k[i] = proj_p[:, k_lo:k_lo + GQA_KV_WIDTH]
            new_v[i] = proj_p[:, v_lo:v_lo + GQA_KV_WIDTH]
            attn_p = ctx_attention(proj_p, seq_p, GQA_Q_HEADS, GQA_KV_HEADS)
            mix_p = pool_mix(proj_p, mix_blk, od_pool_w[j], od_pool_scale[j], seq_p)
            proj_s = in_proj(xs, mod, norm1_g[i], w_in, bf16, GQA_Q_WIDTH, norm=norm, rope=rope, **rot, **lat)
            attn_s = gqa_attention(proj_s,
                                   cache_gqa_k[:, j].reshape(n_lat * past, GQA_KV_WIDTH),
                                   cache_gqa_v[:, j].reshape(n_lat * past, GQA_KV_WIDTH),
                                   seq_s, past)
            mix_s = pool_mix(proj_s, mix_blk, od_pool_w[j], od_pool_scale[j], seq_s)
        xp = out_proj(xp, mod, attn_p, mix_p, w_attn, w_mix, **ctx)
        xs = out_proj(xs, mod, attn_s, mix_s, w_attn, w_mix, **lat)
        final_g = final_norm_g if i == DEPTH - 1 else None
        xp = conv_ffn(xp, mod, norm2_g[i], w_up, ffn_conv_w[i], ffn_conv_b[i], w_down,
                      final_g=final_g, **ctx)
        xs = conv_ffn(xs, mod, norm2_g[i], w_up, ffn_conv_w[i], ffn_conv_b[i], w_down,
                      final_g=final_g, **lat)

    y_prompt = xp.reshape(n_ctx, seq_p, D_MODEL)
    y_sample = xs.reshape(n_lat, seq_s, D_MODEL)
    even = [i for i in range(DEPTH) if i % 2 == 0]
    odd = [i for i in range(DEPTH) if i % 2 == 1]

    def stack(parts, layers, heads):
        return jnp.stack([parts[i].reshape(n_ctx, seq_p, heads, HEAD_DIM) for i in layers], axis=1)

    return (y_prompt, y_sample,
            stack(new_k, even, NA_HEADS), stack(new_v, even, NA_HEADS),
            stack(new_k, odd, GQA_KV_HEADS), stack(new_v, odd, GQA_KV_HEADS))
```

```python
import functools
import math

import jax
import jax.numpy as jnp
import numpy as np
from jax import lax
from jax.experimental import pallas as pl
from jax.experimental.pallas import tpu as pltpu

f32 = jnp.float32
bf16 = jnp.bfloat16

D_MODEL = 2048
DEPTH = 2
GRID_W = 64
HEAD_DIM = 128
NA_HEADS = 12
NA_KR = 8
NA_KC = 16
FNET_GROUPS = 4
FNET_CH = 128
POOL_WINDOWS = (2, 4, 8, 16)
GQA_Q_HEADS = 12
GQA_KV_HEADS = 4
D_FF = 5632
ROPE_BASE = 10000.0
EPS = 1e-6
NEG_INF = -1e30
NA_WIDTH = NA_HEADS * HEAD_DIM
FNET_WIDTH = FNET_GROUPS * FNET_CH
POOL_WIDTH = len(POOL_WINDOWS) * FNET_CH
GQA_Q_WIDTH = GQA_Q_HEADS * HEAD_DIM
GQA_KV_WIDTH = GQA_KV_HEADS * HEAD_DIM
LOG2_E = math.log2(math.e)
ATTN_SCALE = HEAD_DIM ** -0.5 * LOG2_E

MOD_ROWS = 16
VMEM_LIMIT = 60 * 1024 * 1024
HALO = 8
NA_QROWS = 4
NA_WROWS = 12
NA_INVALID = 2 * NA_KR - 1
GQA_QUERIES = 512
NA_HEADS_PER_STEP = 6
FFN_TOKENS = 1024
FFN_ROWS = 128
W_TILE = 512
IN_TOKENS = 1024
IN_WIDE_TILE = 1024
IN_ROWS = 256


def _params(*sem, flags=None):
    return pltpu.CompilerParams(dimension_semantics=sem, vmem_limit_bytes=VMEM_LIMIT, flags=flags)


def _rms_modulate(x, g, shift, scale):
    ms = jnp.mean(x * x, axis=-1, keepdims=True)
    return (x * lax.rsqrt(ms + EPS)) * (g * (1.0 + scale)) + shift


def _mod_spec(k, tm, seq, base_row, per_seq):
    def index(m, n):
        return ((base_row + per_seq * ((m * tm) // seq)) * 6 + k, 0, 0)
    return pl.BlockSpec((None, 1, D_MODEL), index)


def _ada_kernel(c_ref, w_ref, b_ref, o_ref):
    c = c_ref[...]
    s = (c * jax.nn.sigmoid(c)).astype(bf16)
    o_ref[...] = jnp.dot(s, w_ref[...].astype(bf16), preferred_element_type=f32) + b_ref[...]


def ada_modulation(cond, ada_w, ada_b):
    tn = 1024
    n_out = 6 * D_MODEL
    return pl.pallas_call(
        _ada_kernel,
        grid=(DEPTH, n_out // tn),
        in_specs=[
            pl.BlockSpec((MOD_ROWS, D_MODEL), lambda i, n: (0, 0)),
            pl.BlockSpec((None, D_MODEL, tn), lambda i, n: (i, 0, n)),
            pl.BlockSpec((None, 1, tn), lambda i, n: (i, 0, n)),
        ],
        out_specs=pl.BlockSpec((None, MOD_ROWS, tn), lambda i, n: (i, 0, n)),
        out_shape=jax.ShapeDtypeStruct((DEPTH, MOD_ROWS, n_out), f32),
        compiler_params=_params("arbitrary", "arbitrary"),
        name="ada_modulation",
    )(cond, ada_w, ada_b.reshape(DEPTH, 1, n_out))


def _rope(a, cos, sin):
    lane = lax.broadcasted_iota(jnp.int32, a.shape, 1)
    quarter = HEAD_DIM // 4
    partner = jnp.where((lane & (2 * quarter - 1)) < quarter,
                        pltpu.roll(a, HEAD_DIM - quarter, 1),
                        pltpu.roll(a, quarter, 1))
    return a * cos + partner * sin


def _in_proj_kernel(tn, q_blocks, k_blocks, has_norm, has_rope, *refs):
    it = iter(refs)
    x_hbm, shift_ref, scale_ref, g_ref, w_ref = (next(it) for _ in range(5))
    qg_ref = kg_ref = cos_ref = sin_ref = qs_ref = None
    if has_norm:
        qg_ref, kg_ref, avg_ref = next(it), next(it), next(it)
    else:
        qs_ref = next(it)
    if has_rope:
        cos_ref, sin_ref = next(it), next(it)
    o_ref, h_ref, x_buf, x_sem = next(it), next(it), next(it), next(it)
    m = pl.program_id(0)
    n = pl.program_id(1)
    tm = h_ref.shape[0]
    everything = slice(0, tm)

    def x_copy(tile, slot):
        return pltpu.make_async_copy(x_hbm.at[pl.ds(tile * tm, tm), :], x_buf.at[slot], x_sem.at[slot])

    slot = m % 2
    x_ref = x_buf.at[slot]

    @pl.when((n == 0) & (m == 0))
    def _():
        x_copy(0, 0).start()

    @pl.when(n == 0)
    def _():
        x_copy(m, slot).wait()

    @pl.when((n == 1) & (m + 1 < pl.num_programs(0)))
    def _():
        x_copy(m + 1, 1 - slot).start()

    def project(rs, h=None):
        h = h_ref[rs, :] if h is None else h
        return jnp.dot(h, w_ref[...], preferred_element_type=f32)

    def store_scaled(rs, acc, scale):
        o_ref[rs, :] = (acc if scale is None else acc * scale).astype(o_ref.dtype)

    def store_normed(rs, acc, gn):
        ms = jnp.dot((acc * acc).astype(bf16), avg_ref[...], preferred_element_type=f32)
        inv = lax.rsqrt(ms + EPS)
        for hh in range(tn // HEAD_DIM):
            sl = slice(hh * HEAD_DIM, (hh + 1) * HEAD_DIM)
            a = acc[:, sl] * inv[:, sl] * gn
            if has_rope:
                a = _rope(a, cos_ref[rs, :], sin_ref[rs, :])
            o_ref[rs, sl] = a.astype(o_ref.dtype)

    @pl.when(n == 0)
    def _():
        g, sh, sc = g_ref[...], shift_ref[...], scale_ref[...]
        for r0 in range(0, tm, IN_ROWS):
            rs = slice(r0, r0 + IN_ROWS)
            h = _rms_modulate(x_ref[rs, :], g, sh, sc).astype(bf16)
            h_ref[rs, :] = h
            if has_norm:
                store_normed(rs, project(rs, h), qg_ref[...] * ATTN_SCALE)
            else:
                store_scaled(rs, project(rs, h), qs_ref[...])

    if not has_norm:
        @pl.when(n > 0)
        def _():
            store_scaled(everything, project(everything), qs_ref[...])
        return

    is_q = n < q_blocks
    q_scale = jnp.where(is_q, ATTN_SCALE, 1.0)

    is_qk = n < q_blocks + k_blocks

    @pl.when((n > 0) & is_qk)
    def _():
        gn = jnp.where(is_q, qg_ref[...], kg_ref[...]) * q_scale
        store_normed(everything, project(everything), gn)

    @pl.when(jnp.logical_not(is_qk))
    def _():
        store_scaled(everything, project(everything), None)


def in_proj(x, mod, g, w, out_dtype, q_width, seq, base_row, per_seq, norm=None, rope=None, w_rot=0):
    t_tokens = x.shape[0]
    tm, tn = IN_TOKENS, (W_TILE if norm is not None else IN_WIDE_TILE)
    n_total = w.shape[1]
    assert n_total % tn == 0 and n_total // tn >= 2
    in_specs = [
        pl.BlockSpec(memory_space=pl.ANY),
        _mod_spec(0, tm, seq, base_row, per_seq),
        _mod_spec(1, tm, seq, base_row, per_seq),
        pl.BlockSpec((1, D_MODEL), lambda m, n: (0, 0)),
        pl.BlockSpec((D_MODEL, tn), lambda m, n: (0, (n + w_rot) % (n_total // tn))),
    ]
    args = [x, mod, mod, g.reshape(1, D_MODEL), w]
    k_width = 0
    if norm is not None:
        k_width = norm[2]
        in_specs += [pl.BlockSpec((1, HEAD_DIM), lambda m, n: (0, 0))] * 2
        in_specs += [pl.BlockSpec((tn, tn), lambda m, n: (0, 0))]
        head_of = np.arange(tn) // HEAD_DIM
        avg = (head_of[:, None] == head_of[None, :]) / HEAD_DIM
        args += [norm[0].reshape(1, HEAD_DIM), norm[1].reshape(1, HEAD_DIM), jnp.asarray(avg, dtype=bf16)]
    else:
        in_specs += [pl.BlockSpec((1, tn), lambda m, n: (0, n))]
        args += [jnp.asarray(np.where(np.arange(n_total) < q_width, ATTN_SCALE, 1.0)[None, :], dtype=f32)]
    if rope is not None:
        spb = seq // tm
        in_specs += [pl.BlockSpec((tm, HEAD_DIM), lambda m, n: (m % spb, 0))] * 2
        args += [rope[0], rope[1]]
    kern = functools.partial(_in_proj_kernel, tn, q_width // tn, k_width // tn,
                             norm is not None, rope is not None)
    return pl.pallas_call(
        kern,
        grid=(t_tokens // tm, n_total // tn),
        in_specs=in_specs,
        out_specs=pl.BlockSpec((tm, tn), lambda m, n: (m, n)),
        out_shape=jax.ShapeDtypeStruct((t_tokens, n_total), out_dtype),
        scratch_shapes=[pltpu.VMEM((tm, D_MODEL), bf16),
                        pltpu.VMEM((2, tm, D_MODEL), f32),
                        pltpu.SemaphoreType.DMA((2,))],
        compiler_params=_params("arbitrary", "arbitrary"),
        name="in_proj",
    )(*args)


def _with_ones(v):
    return jnp.concatenate([v, jnp.ones_like(v)], axis=1)


def _softmax_pv(parts):
    m = functools.reduce(jnp.maximum, [jnp.max(s, axis=-1, keepdims=True) for s, _ in parts])
    acc = sum(jnp.dot(jnp.exp2(s - m).astype(bf16), v1, preferred_element_type=f32) for s, v1 in parts)
    return acc[:, :HEAD_DIM] / acc[:, HEAD_DIM:]


def _ctx_attn_kernel(n_q, n_kv, q_ref, k_ref, v_ref, o_ref):
    group = n_q // n_kv
    values = [_with_ones(v_ref[:, kv * HEAD_DIM:(kv + 1) * HEAD_DIM].astype(bf16)) for kv in range(n_kv)]
    keys = [k_ref[:, kv * HEAD_DIM:(kv + 1) * HEAD_DIM].astype(bf16) for kv in range(n_kv)]
    dn = (((1,), (1,)), ((), ()))
    scores = [lax.dot_general(q_ref[:, h * HEAD_DIM:(h + 1) * HEAD_DIM].astype(bf16), keys[h // group],
                              dn, preferred_element_type=f32) for h in range(n_q)]
    for h, s in enumerate(scores):
        o = _softmax_pv([(s, values[h // group])])
        o_ref[:, h * HEAD_DIM:(h + 1) * HEAD_DIM] = o.astype(o_ref.dtype)


def ctx_attention(proj, seq, n_q, n_kv):
    t_tokens = proj.shape[0]
    qw, kw = n_q * HEAD_DIM, n_kv * HEAD_DIM
    assert qw % kw == 0
    k_blk = qw // kw
    return pl.pallas_call(
        functools.partial(_ctx_attn_kernel, n_q, n_kv),
        grid=(t_tokens // seq,),
        in_specs=[
            pl.BlockSpec((seq, qw), lambda b: (b, 0)),
            pl.BlockSpec((seq, kw), lambda b: (b, k_blk)),
            pl.BlockSpec((seq, kw), lambda b: (b, k_blk + 1)),
        ],
        out_specs=pl.BlockSpec((seq, qw), lambda b: (b, 0)),
        out_shape=jax.ShapeDtypeStruct((t_tokens, qw), bf16),
        compiler_params=_params("arbitrary"),
        name="ctx_attention",
    )(proj, proj, proj)


def _gqa_attn_kernel(group, q_ref, ck_ref, cv_ref, k_ref, v_ref, o_ref):
    ck = ck_ref[...].astype(bf16)
    cv = _with_ones(cv_ref[...].astype(bf16))
    k = k_ref[...]
    v = _with_ones(v_ref[...])
    dn = (((1,), (1,)), ((), ()))
    scores = []
    for gi in range(group):
        q = q_ref[:, gi * HEAD_DIM:(gi + 1) * HEAD_DIM]
        scores.append((lax.dot_general(q, ck, dn, preferred_element_type=f32),
                       lax.dot_general(q, k, dn, preferred_element_type=f32)))
    for gi, (s_c, s_l) in enumerate(scores):
        o = _softmax_pv([(s_c, cv), (s_l, v)])
        o_ref[:, gi * HEAD_DIM:(gi + 1) * HEAD_DIM] = o.astype(o_ref.dtype)


def gqa_attention(proj, cache_k, cache_v, seq, past):
    t_tokens = proj.shape[0]
    n_b = t_tokens // seq
    group = GQA_Q_HEADS // GQA_KV_HEADS
    tq = GQA_QUERIES
    nq = seq // tq
    return pl.pallas_call(
        functools.partial(_gqa_attn_kernel, group),
        grid=(n_b, GQA_KV_HEADS, nq),
        in_specs=[
            pl.BlockSpec((tq, group * HEAD_DIM), lambda b, h, i: (b * nq + i, h)),
            pl.BlockSpec((past, HEAD_DIM), lambda b, h, i: (b, h)),
            pl.BlockSpec((past, HEAD_DIM), lambda b, h, i: (b, h)),
            pl.BlockSpec((seq, HEAD_DIM), lambda b, h, i: (b, GQA_Q_HEADS + h)),
            pl.BlockSpec((seq, HEAD_DIM), lambda b, h, i: (b, GQA_Q_HEADS + GQA_KV_HEADS + h)),
        ],
        out_specs=pl.BlockSpec((tq, group * HEAD_DIM), lambda b, h, i: (b * nq + i, h)),
        out_shape=jax.ShapeDtypeStruct((t_tokens, GQA_Q_WIDTH), bf16),
        compiler_params=_params("arbitrary", "arbitrary", "arbitrary"),
        name="gqa_attention",
    )(proj, cache_k, cache_v, proj, proj)


def _na_bias_tiles(rel_bias):
    qc = np.arange(GRID_W)[:, None]
    kc = np.arange(GRID_W)[None, :]
    q_start = np.clip(qc - NA_KC // 2, 0, GRID_W - NA_KC)
    valid = (kc >= q_start) & (kc < q_start + NA_KC)
    dc = np.clip(kc - qc, -(NA_KC - 1), NA_KC - 1) + NA_KC - 1
    pick = (np.arange(2 * NA_KC - 1)[:, None, None] == dc[None]) & valid[None]
    tiles = jnp.einsum("hdc,cqk->hdqk", rel_bias.astype(f32) * LOG2_E, jnp.asarray(pick, dtype=f32),
                       precision=lax.Precision.HIGHEST)
    tiles = jnp.where(jnp.asarray(valid)[None, None], tiles, NEG_INF)
    masked = jnp.full((NA_HEADS, 1, GRID_W, GRID_W), NEG_INF, f32)
    tiles = jnp.concatenate([tiles, masked], axis=1)
    zeros = jnp.zeros_like(tiles)
    left = jnp.concatenate([tiles, zeros], axis=-1)
    right = jnp.concatenate([zeros, tiles], axis=-1)
    return left, right


def _na_attn_kernel(n_heads, n_rows, q_ref, k_ref, v_ref, ck_ref, cv_ref, bl_ref, br_ref, o_ref):
    rb = pl.program_id(2)
    start = jnp.clip(rb * NA_QROWS - NA_KR // 2, 0, n_rows - NA_WROWS)
    tok0 = pl.multiple_of(start * GRID_W, GRID_W)
    dn = (((1,), (1,)), ((), ()))

    def tile_index(i, kr):
        qr = rb * NA_QROWS + i
        r0 = jnp.clip(qr - NA_KR // 2, 0, n_rows - NA_KR)
        return jnp.where((kr >= r0) & (kr < r0 + NA_KR), kr - qr + NA_KR - 1, NA_INVALID)

    index = [[(tile_index(i, start + 2 * jp), tile_index(i, start + 2 * jp + 1))
              for jp in range(NA_WROWS // 2)] for i in range(NA_QROWS)]
    scores = []
    for hh in range(n_heads):
        hs = slice(hh * HEAD_DIM, (hh + 1) * HEAD_DIM)
        q = q_ref[:, hs]
        bias = jnp.concatenate(
            [jnp.concatenate([bl_ref[hh, ia] + br_ref[hh, ib] for ia, ib in row], axis=1)
             for row in index], axis=0)
        s_l = lax.dot_general(q, k_ref[pl.ds(tok0, NA_WROWS * GRID_W), hs], dn,
                              preferred_element_type=f32) + bias
        s_c = lax.dot_general(q, ck_ref[:, hs].astype(bf16), dn, preferred_element_type=f32)
        scores.append((s_c, s_l))
    for hh, (s_c, s_l) in enumerate(scores):
        hs = slice(hh * HEAD_DIM, (hh + 1) * HEAD_DIM)
        vw = v_ref[pl.ds(tok0, NA_WROWS * GRID_W), hs]
        o = _softmax_pv([(s_c, _with_ones(cv_ref[:, hs].astype(bf16))), (s_l, _with_ones(vw))])
        o_ref[:, hs] = o.astype(o_ref.dtype)


def na_attention(proj, cache_k, cache_v, rel_bias, seq, past):
    t_tokens = proj.shape[0]
    n_b = t_tokens // seq
    n_rows = seq // GRID_W
    n_rb = n_rows // NA_QROWS
    tq = NA_QROWS * GRID_W
    hb = NA_HEADS_PER_STEP
    n_hg = NA_HEADS // hb
    wide = hb * HEAD_DIM
    bias_l, bias_r = _na_bias_tiles(rel_bias)
    bias_spec = pl.BlockSpec((hb, 2 * NA_KR, GRID_W, 2 * GRID_W), lambda b, h, r: (h, 0, 0, 0))
    return pl.pallas_call(
        functools.partial(_na_attn_kernel, hb, n_rows),
        grid=(n_b, n_hg, n_rb),
        in_specs=[
            pl.BlockSpec((tq, wide), lambda b, h, r: (b * n_rb + r, h)),
            pl.BlockSpec((seq, wide), lambda b, h, r: (b, n_hg + h)),
            pl.BlockSpec((seq, wide), lambda b, h, r: (b, 2 * n_hg + h)),
            pl.BlockSpec((past, wide), lambda b, h, r: (b, h)),
            pl.BlockSpec((past, wide), lambda b, h, r: (b, h)),
            bias_spec,
            bias_spec,
        ],
        out_specs=pl.BlockSpec((tq, wide), lambda b, h, r: (b * n_rb + r, h)),
        out_shape=jax.ShapeDtypeStruct((t_tokens, NA_WIDTH), bf16),
        compiler_params=_params("arbitrary", "arbitrary", "arbitrary"),
        name="na_attention",
    )(proj, proj, proj, cache_k, cache_v, bias_l, bias_r)


def _dft_cols_kernel(x_ref, cs_ref, y_ref):
    for g in range(FNET_GROUPS):
        sl = slice(g * FNET_CH, (g + 1) * FNET_CH)
        y = jnp.dot(x_ref[:, sl].astype(bf16), cs_ref[...], preferred_element_type=f32)
        y_ref[0, :, sl] = y[:, :FNET_CH].astype(bf16)
        y_ref[1, :, sl] = y[:, FNET_CH:].astype(bf16)


def _dft_rows_kernel(d_ref, y_ref, w_ref, o_ref):
    f = jnp.dot(d_ref[...], y_ref[...], preferred_element_type=f32)
    for g in range(FNET_GROUPS):
        sl = slice(g * FNET_CH, (g + 1) * FNET_CH)
        o = jnp.dot(f[:, sl].astype(bf16), w_ref[g], preferred_element_type=f32)
        o_ref[:, sl] = o.astype(o_ref.dtype)


def _dft_tables(seq):
    c = np.arange(FNET_CH, dtype=np.int64)
    ang_c = (2.0 * np.pi / FNET_CH) * ((c[:, None] * c[None, :]) % FNET_CH)
    cs = np.concatenate([np.cos(ang_c), np.sin(ang_c)], axis=1)
    n = np.arange(seq, dtype=np.int64)
    ang_n = (2.0 * np.pi / seq) * ((n[:, None] * n[None, :]) % seq)
    norm = 1.0 / math.sqrt(seq * FNET_CH)
    d = np.concatenate([np.cos(ang_n) * norm, -np.sin(ang_n) * norm], axis=1)
    return jnp.asarray(cs, dtype=bf16), jnp.asarray(d, dtype=bf16)


def fourier_mix(proj, col_blk, fnet_w, seq):
    t_tokens = proj.shape[0]
    n_b = t_tokens // seq
    cs, d = _dft_tables(seq)
    tn = min(seq, 512)
    ns = seq // tn
    y = pl.pallas_call(
        _dft_cols_kernel,
        grid=(n_b, ns),
        in_specs=[
            pl.BlockSpec((tn, FNET_WIDTH), lambda b, i: (b * ns + i, col_blk)),
            pl.BlockSpec((FNET_CH, 2 * FNET_CH), lambda b, i: (0, 0)),
        ],
        out_specs=pl.BlockSpec((2, tn, FNET_WIDTH), lambda b, i: (0, i, b)),
        out_shape=jax.ShapeDtypeStruct((2, seq, n_b * FNET_WIDTH), bf16),
        compiler_params=_params("arbitrary", "arbitrary"),
        name="dft_cols",
    )(proj, cs)
    y2 = y.reshape(2 * seq, n_b * FNET_WIDTH)
    return pl.pallas_call(
        _dft_rows_kernel,
        grid=(n_b, ns),
        in_specs=[
            pl.BlockSpec((tn, 2 * seq), lambda b, i: (i, 0)),
            pl.BlockSpec((2 * seq, FNET_WIDTH), lambda b, i: (0, b)),
            pl.BlockSpec((FNET_GROUPS, FNET_CH, FNET_CH), lambda b, i: (0, 0, 0)),
        ],
        out_specs=pl.BlockSpec((tn, FNET_WIDTH), lambda b, i: (b * ns + i, 0)),
        out_shape=jax.ShapeDtypeStruct((t_tokens, FNET_WIDTH), bf16),
        compiler_params=_params("arbitrary", "arbitrary"),
        name="dft_rows",
    )(d, y2, fnet_w.astype(bf16))


def _pool_kernel(seq, x_ref, w_ref, sc_ref, o_ref):
    t = lax.broadcasted_iota(jnp.int32, (seq, 1), 0)

    def prev(a, k):
        return jnp.where(t >= k, pltpu.roll(a, k, 0), 0.0)

    def nxt(a, k):
        return jnp.where(t < seq - k, pltpu.roll(a, seq - k, 0), 0.0)

    for g, win in enumerate(POOL_WINDOWS):
        sl = slice(g * FNET_CH, (g + 1) * FNET_CH)
        x = x_ref[:, sl].astype(f32)
        half = win // 2
        back, fwd, k = x, x, 1
        while k < half:
            back = back + prev(back, k)
            fwd = fwd + nxt(fwd, k)
            k *= 2
        total = prev(back, 1) + fwd
        cnt = (jnp.minimum(t + half, seq) - jnp.maximum(t - half, 0)).astype(f32)
        pooled = (total / cnt - x).astype(bf16)
        o = jnp.dot(pooled, w_ref[g], preferred_element_type=f32) * sc_ref[:, sl]
        o_ref[:, sl] = o.astype(o_ref.dtype)


def pool_mix(proj, col_blk, pool_w, pool_scale, seq):
    t_tokens = proj.shape[0]
    return pl.pallas_call(
        functools.partial(_pool_kernel, seq),
        grid=(t_tokens // seq,),
        in_specs=[
            pl.BlockSpec((seq, POOL_WIDTH), lambda b: (b, col_blk)),
            pl.BlockSpec((len(POOL_WINDOWS), FNET_CH, FNET_CH), lambda b: (0, 0, 0)),
            pl.BlockSpec((1, POOL_WIDTH), lambda b: (0, 0)),
        ],
        out_specs=pl.BlockSpec((seq, POOL_WIDTH), lambda b: (b, 0)),
        out_shape=jax.ShapeDtypeStruct((t_tokens, POOL_WIDTH), bf16),
        compiler_params=_params("arbitrary"),
        name="pool_mix",
    )(proj, pool_w.astype(bf16), pool_scale.reshape(1, POOL_WIDTH))


def _out_proj_kernel(x_ref, gate_ref, a_ref, b_ref, wa_ref, wb_ref, o_ref):
    y = (jnp.dot(a_ref[...], wa_ref[...], preferred_element_type=f32)
         + jnp.dot(b_ref[...], wb_ref[...], preferred_element_type=f32))
    o_ref[...] = x_ref[...] + gate_ref[...] * y


def out_proj(x, mod, a, b, wa, wb, seq, base_row, per_seq):
    t_tokens = x.shape[0]
    tm = 512
    return pl.pallas_call(
        _out_proj_kernel,
        grid=(t_tokens // tm, 1),
        in_specs=[
            pl.BlockSpec((tm, D_MODEL), lambda m, n: (m, 0)),
            _mod_spec(2, tm, seq, base_row, per_seq),
            pl.BlockSpec((tm, a.shape[1]), lambda m, n: (m, 0)),
            pl.BlockSpec((tm, b.shape[1]), lambda m, n: (m, 0)),
            pl.BlockSpec(wa.shape, lambda m, n: (0, 0)),
            pl.BlockSpec(wb.shape, lambda m, n: (0, 0)),
        ],
        out_specs=pl.BlockSpec((tm, D_MODEL), lambda m, n: (m, 0)),
        out_shape=jax.ShapeDtypeStruct((t_tokens, D_MODEL), f32),
        compiler_params=_params("arbitrary", "arbitrary"),
        name="out_proj",
    )(x, mod, a, b, wa, wb)


def _ffn_kernel(tm, seq, nj, final, *refs):
    it = iter(refs)
    (x_hbm, xp_ref, xn_ref, shift_ref, scale_ref, gate_ref, g_ref,
     wv_ref, wg_ref, cwv_ref, cwg_ref, cbv_ref, cbg_ref, wd_ref) = (next(it) for _ in range(14))
    gf_ref = next(it) if final else None
    o_hbm, h_ref, acc_ref = next(it), next(it), next(it)
    u_refs = [(next(it), next(it)), (next(it), next(it))]
    x_buf, x_sem, o_sem = next(it), next(it), next(it)
    i = pl.program_id(0)
    j = pl.program_id(1)
    n_tiles = pl.num_programs(0)
    rows = tm + 2 * HALO

    def x_copy(tile, slot):
        return pltpu.make_async_copy(x_hbm.at[pl.ds(tile * tm, tm), :], x_buf.at[slot], x_sem.at[slot])

    def o_copy(tile):
        return pltpu.make_async_copy(acc_ref, o_hbm.at[pl.ds(tile * tm, tm), :], o_sem.at[0])

    slot = i % 2
    x_ref = x_buf.at[slot]

    @pl.when((j == 0) & (i == 0))
    def _():
        x_copy(0, 0).start()

    @pl.when(j == 0)
    def _():
        x_copy(i, slot).wait()

    @pl.when((j == 1) & (i + 1 < n_tiles))
    def _():
        x_copy(i + 1, 1 - slot).start()

    @pl.when((j == 1) & (i > 0))
    def _():
        o_copy(i - 1).wait()

    @pl.when(j == 1)
    def _():
        acc_ref[...] = jnp.zeros_like(acc_ref)

    def prologue_and_first_up():
        g, sh, sc = g_ref[...], shift_ref[...], scale_ref[...]
        uv_ref, ug_ref = u_refs[0]

        def emit(rs, h):
            h_ref[rs, :] = h
            uv_ref[rs, :] = jnp.dot(h, wv_ref[...], preferred_element_type=f32)
            ug_ref[rs, :] = jnp.dot(h, wg_ref[...], preferred_element_type=f32)

        for r0 in range(0, tm - IN_ROWS, IN_ROWS):
            rs = slice(r0, r0 + IN_ROWS)
            emit(rs, _rms_modulate(x_ref[rs, :], g, sh, sc).astype(bf16))
        next_ok = jnp.where((((i + 1) * tm) & (seq - 1)) != 0, 1.0, 0.0)
        prev_ok = jnp.where(((i * tm) & (seq - 1)) != 0, 1.0, 0.0)
        h_last = _rms_modulate(x_ref[tm - IN_ROWS:tm, :], g, sh, sc)
        h_next = _rms_modulate(xn_ref[...], g, sh, sc) * next_ok
        h_prev = _rms_modulate(xp_ref[...], g, sh, sc) * prev_ok
        emit(slice(tm - IN_ROWS, rows), jnp.concatenate([h_last, h_next, h_prev], axis=0).astype(bf16))

    def zero_row(a, r):
        r0 = (r // HALO) * HALO
        row8 = lax.broadcasted_iota(jnp.int32, (HALO, 1), 0)
        fixed = jnp.where(row8 == r - r0, 0.0, a[r0:r0 + HALO])
        return jnp.concatenate([a[:r0], fixed, a[r0 + HALO:]], axis=0)

    def window(u_ref, r0):
        lo, hi = r0 - HALO, r0 + FFN_ROWS + HALO
        if lo < 0:
            return jnp.concatenate([u_ref[rows + lo:rows, :], u_ref[0:hi, :]], axis=0)
        return u_ref[lo:hi, :]

    def conv(u_ref, r0, cw_ref, cb_ref):
        u = window(u_ref, r0)
        n = FFN_ROWS + 2 * HALO
        up = pltpu.roll(u, 1, 0)
        un = pltpu.roll(u, n - 1, 0)
        for edge in range(seq, tm, seq):
            if r0 <= edge < r0 + FFN_ROWS:
                up = zero_row(up, edge - r0 + HALO)
            if r0 <= edge - 1 < r0 + FFN_ROWS:
                un = zero_row(un, edge - 1 - r0 + HALO)
        keep = slice(HALO, HALO + FFN_ROWS)
        return (up[keep] * cw_ref[0:1, :] + u[keep] * cw_ref[1:2, :] + un[keep] * cw_ref[2:3, :]
                + cb_ref[...])

    def gated(slot):
        uv_ref, ug_ref = u_refs[slot]
        chunks = []
        for r0 in range(0, tm, FFN_ROWS):
            val = conv(uv_ref, r0, cwv_ref, cbv_ref)
            gate = conv(ug_ref, r0, cwg_ref, cbg_ref)
            chunks.append((gate * jax.nn.sigmoid(gate) * val).astype(bf16))
        return chunks

    def zero_after(chunks):
        parts = []
        for c in chunks:
            bits = pltpu.bitcast(c, jnp.uint32)
            parts += [bits[k:k + HALO] for k in range(0, bits.shape[0], HALO)]
        while len(parts) > 1:
            parts = [a | b for a, b in zip(parts[0::2], parts[1::2])] + parts[len(parts) & ~1:]
        return (parts[0] >> 16) >> 16

    def after(a, zero, row0=0):
        tf = zero.shape[1]
        mid = pltpu.bitcast(pltpu.bitcast(a[row0:row0 + 2 * HALO, 0:tf], jnp.uint32) | zero, bf16)
        if tf < a.shape[1]:
            mid = jnp.concatenate([mid, a[row0:row0 + 2 * HALO, tf:]], axis=1)
        parts = ([a[:row0]] if row0 else []) + [mid, a[row0 + 2 * HALO:]]
        return jnp.concatenate(parts, axis=0)

    def quarter(chunks, k):
        n = len(chunks) // 4
        return chunks[k * n:(k + 1) * n]

    def up_project(slot, chunks):
        h = h_ref[...]
        wv, wg = wv_ref[...], wg_ref[...]
        uv_ref, ug_ref = u_refs[slot]
        k_mid = wv.shape[0] // 2
        wv = after(wv, zero_after(quarter(chunks, 0)), k_mid)
        wg = after(wg, zero_after(quarter(chunks, 2)), k_mid)
        uv_ref[...] = jnp.dot(h, wv, preferred_element_type=f32)
        h = after(h, zero_after(quarter(chunks, 1)))
        ug_ref[...] = jnp.dot(h, wg, preferred_element_type=f32)

    def down_project(chunks, anchored):
        if anchored:
            chunks = [after(chunks[0], zero_after(quarter(chunks, 3)))] + chunks[1:]
        act = jnp.concatenate(chunks, axis=0)
        acc_ref[...] += jnp.dot(act, wd_ref[...], preferred_element_type=f32)

    @pl.when(j == 0)
    def _():
        prologue_and_first_up()

    for p in range(2):
        @pl.when((j >= 1) & (j < nj) & (j % 2 == p))
        def _(p=p):
            chunks = gated(1 - p)
            up_project(p, chunks)
            down_project(chunks, True)

    @pl.when(j == nj)
    def _():
        down_project(gated((nj - 1) % 2), False)
        y = x_ref[...] + gate_ref[...] * acc_ref[...]
        if final:
            ms = jnp.mean(y * y, axis=-1, keepdims=True)
            y = y * lax.rsqrt(ms + EPS) * gf_ref[...]
        acc_ref[...] = y
        o_copy(i).start()

    @pl.when((j == nj) & (i == n_tiles - 1))
    def _():
        o_copy(i).wait()


def conv_ffn(x, mod, g, w_up, conv_w, conv_b, w_down, seq, base_row, per_seq, final_g=None):
    t_tokens = x.shape[0]
    tm, tf = FFN_TOKENS, W_TILE
    assert seq & (seq - 1) == 0 and (tm // FFN_ROWS) % 4 == 0
    nj = D_FF // tf
    hb = tm // HALO
    n_hblk = t_tokens // HALO
    conv_b2 = conv_b.reshape(1, 2 * D_FF)

    def blk(j):
        return jnp.clip(j, 0, nj - 1)

    assert nj >= 2
    in_specs = [
        pl.BlockSpec(memory_space=pl.ANY),
        pl.BlockSpec((HALO, D_MODEL), lambda i, j: (jnp.maximum(i * hb - 1, 0), 0)),
        pl.BlockSpec((HALO, D_MODEL), lambda i, j: (jnp.minimum((i + 1) * hb, n_hblk - 1), 0)),
        _mod_spec(3, tm, seq, base_row, per_seq),
        _mod_spec(4, tm, seq, base_row, per_seq),
        _mod_spec(5, tm, seq, base_row, per_seq),
        pl.BlockSpec((1, D_MODEL), lambda i, j: (0, 0)),
        pl.BlockSpec((D_MODEL, tf), lambda i, j: (0, blk(j))),
        pl.BlockSpec((D_MODEL, tf), lambda i, j: (0, nj + blk(j))),
        pl.BlockSpec((3, tf), lambda i, j: (0, blk(j - 1))),
        pl.BlockSpec((3, tf), lambda i, j: (0, nj + blk(j - 1))),
        pl.BlockSpec((1, tf), lambda i, j: (0, blk(j - 1))),
        pl.BlockSpec((1, tf), lambda i, j: (0, nj + blk(j - 1))),
        pl.BlockSpec((tf, D_MODEL), lambda i, j: (blk(j - 1), 0)),
    ]
    args = [x, x, x, mod, mod, mod, g.reshape(1, D_MODEL), w_up, w_up,
            conv_w, conv_w, conv_b2, conv_b2, w_down]
    if final_g is not None:
        in_specs.append(pl.BlockSpec((1, D_MODEL), lambda i, j: (0, 0)))
        args.append(final_g.reshape(1, D_MODEL))
    return pl.pallas_call(
        functools.partial(_ffn_kernel, tm, seq, nj, final_g is not None),
        grid=(t_tokens // tm, nj + 1),
        in_specs=in_specs,
        out_specs=pl.BlockSpec(memory_space=pl.ANY),
        out_shape=jax.ShapeDtypeStruct((t_tokens, D_MODEL), f32),
        scratch_shapes=[pltpu.VMEM((tm + 2 * HALO, D_MODEL), bf16),
                        pltpu.VMEM((tm, D_MODEL), f32)]
                       + [pltpu.VMEM((tm + 2 * HALO, tf), f32)] * 4
                       + [pltpu.VMEM((2, tm, D_MODEL), f32),
                          pltpu.SemaphoreType.DMA((2,)),
                          pltpu.SemaphoreType.DMA((1,))],
        compiler_params=_params("arbitrary", "arbitrary"),
        name="conv_ffn",
    )(*args)


def _rope_tables(seq):
    t = np.arange(seq)
    half = HEAD_DIM // 2
    inv_freq = np.power(ROPE_BASE, -np.arange(0, half, 2, dtype=np.float64) / half)
    ang_r = (t // GRID_W)[:, None] * inv_freq[None, :]
    ang_c = (t % GRID_W)[:, None] * inv_freq[None, :]
    cos = np.concatenate([np.cos(ang_r)] * 2 + [np.cos(ang_c)] * 2, axis=-1)
    sin = np.concatenate([-np.sin(ang_r), np.sin(ang_r), -np.sin(ang_c), np.sin(ang_c)], axis=-1)
    return jnp.asarray(cos, dtype=f32), jnp.asarray(sin, dtype=f32)


def kernel(x_prompt, x_sample, c, cache_na_k, cache_na_v, cache_gqa_k, cache_gqa_v, c_ctx, norm1_g, norm2_g, ada_w, ada_b, ev_w_in, ev_na_bias, ev_fnet_w, ev_w_out, od_w_in, od_pool_w, od_pool_scale, od_q_norm_g, od_k_norm_g, od_w_out, ffn_w_up, ffn_conv_w, ffn_conv_b, ffn_w_down, final_norm_g):
    n_ctx, seq_p, _ = x_prompt.shape
    n_lat, seq_s, _ = x_sample.shape
    past = cache_na_k.shape[2]
    assert n_lat + 1 <= MOD_ROWS
    xp = x_prompt.reshape(n_ctx * seq_p, D_MODEL)
    xs = x_sample.reshape(n_lat * seq_s, D_MODEL)

    cond = jnp.zeros((MOD_ROWS, D_MODEL), f32).at[0].set(c_ctx).at[1:1 + n_lat].set(c)
    mods = ada_modulation(cond, ada_w, ada_b).reshape(DEPTH, MOD_ROWS * 6, 1, D_MODEL)
    rope = _rope_tables(seq_s)

    ctx = dict(seq=seq_p, base_row=0, per_seq=0)
    lat = dict(seq=seq_s, base_row=1, per_seq=1)
    new_k, new_v = {}, {}
    for i in range(DEPTH):
        j = i // 2
        mod = mods[i]
        w_up = ffn_w_up[i].astype(bf16)
        w_down = ffn_w_down[i].astype(bf16)
        if i % 2 == 0:
            w_in = ev_w_in[j].astype(bf16)
            w_attn = ev_w_out[j, :NA_WIDTH].astype(bf16)
            w_mix = ev_w_out[j, NA_WIDTH:].astype(bf16)
            mix_blk = 3 * NA_WIDTH // FNET_WIDTH
            proj_p = in_proj(xp, mod, norm1_g[i], w_in, f32, NA_WIDTH, **ctx)
            new_k[i] = proj_p[:, NA_WIDTH:2 * NA_WIDTH]
            new_v[i] = proj_p[:, 2 * NA_WIDTH:3 * NA_WIDTH]
            attn_p = ctx_attention(proj_p, seq_p, NA_HEADS, NA_HEADS)
            mix_p = fourier_mix(proj_p, mix_blk, ev_fnet_w[j], seq_p)
            proj_s = in_proj(xs, mod, norm1_g[i], w_in, bf16, NA_WIDTH, **lat)
            attn_s = na_attention(proj_s,
                                  cache_na_k[:, j].reshape(n_lat * past, NA_WIDTH),
                                  cache_na_v[:, j].reshape(n_lat * past, NA_WIDTH),
                                  ev_na_bias[j], seq_s, past)
            mix_s = fourier_mix(proj_s, mix_blk, ev_fnet_w[j], seq_s)
        else:
            w_in = od_w_in[j].astype(bf16)
            rot = dict(w_rot=POOL_WIDTH // W_TILE)
            w_attn = od_w_out[j, POOL_WIDTH:].astype(bf16)
            w_mix = od_w_out[j, :POOL_WIDTH].astype(bf16)
            norm = (od_q_norm_g[j], od_k_norm_g[j], GQA_KV_WIDTH)
            k_lo = GQA_Q_WIDTH
            v_lo = GQA_Q_WIDTH + GQA_KV_WIDTH
            mix_blk = (GQA_Q_WIDTH + 2 * GQA_KV_WIDTH) // POOL_WIDTH
            proj_p = in_proj(xp, mod, norm1_g[i], w_in, f32, GQA_Q_WIDTH, norm=norm, **rot, **ctx)
            new_k[i] = proj_p[:, k_lo:k_lo + GQA_KV_WIDTH]
            new_v[i] = proj_p[:, v_lo:v_lo + GQA_KV_WIDTH]
            attn_p = ctx_attention(proj_p, seq_p, GQA_Q_HEADS, GQA_KV_HEADS)
            mix_p = pool_mix(proj_p, mix_blk, od_pool_w[j], od_pool_scale[j], seq_p)
            proj_s = in_proj(xs, mod, norm1_g[i], w_in, bf16, GQA_Q_WIDTH, norm=norm, rope=rope, **rot, **lat)
            attn_s = gqa_attention(proj_s,
                                   cache_gqa_k[:, j].reshape(n_lat * past, GQA_KV_WIDTH),
                                   cache_gqa_v[:, j].reshape(n_lat * past, GQA_KV_WIDTH),
                                   seq_s, past)
            mix_s = pool_mix(proj_s, mix_blk, od_pool_w[j], od_pool_scale[j], seq_s)
        xp = out_proj(xp, mod, attn_p, mix_p, w_attn, w_mix, **ctx)
        xs = out_proj(xs, mod, attn_s, mix_s, w_attn, w_mix, **lat)
        final_g = final_norm_g if i == DEPTH - 1 else None
        xp = conv_ffn(xp, mod, norm2_g[i], w_up, ffn_conv_w[i], ffn_conv_b[i], w_down,
                      final_g=final_g, **ctx)
        xs = conv_ffn(xs, mod, norm2_g[i], w_up, ffn_conv_w[i], ffn_conv_b[i], w_down,
                      final_g=final_g, **lat)

    y_prompt = xp.reshape(n_ctx, seq_p, D_MODEL)
    y_sample = xs.reshape(n_lat, seq_s, D_MODEL)
    even = [i for i in range(DEPTH) if i % 2 == 0]
    odd = [i for i in range(DEPTH) if i % 2 == 1]

    def stack(parts, layers, heads):
        return jnp.stack([parts[i].reshape(n_ctx, seq_p, heads, HEAD_DIM) for i in layers], axis=1)

    return (y_prompt, y_sample,
            stack(new_k, even, NA_HEADS), stack(new_v, even, NA_HEADS),
            stack(new_k, odd, GQA_KV_HEADS), stack(new_v, odd, GQA_KV_HEADS))
```

```python
import functools
import math

import jax
import jax.numpy as jnp
import numpy as np
from jax import lax
from jax.experimental import pallas as pl
from jax.experimental.pallas import tpu as pltpu

f32 = jnp.float32
bf16 = jnp.bfloat16

D_MODEL = 2048
DEPTH = 2
GRID_W = 64
HEAD_DIM = 128
NA_HEADS = 12
NA_KR = 8
NA_KC = 16
FNET_GROUPS = 4
FNET_CH = 128
POOL_WINDOWS = (2, 4, 8, 16)
GQA_Q_HEADS = 12
GQA_KV_HEADS = 4
D_FF = 5632
ROPE_BASE = 10000.0
EPS = 1e-6
NEG_INF = -1e30
NA_WIDTH = NA_HEADS * HEAD_DIM
FNET_WIDTH = FNET_GROUPS * FNET_CH
POOL_WIDTH = len(POOL_WINDOWS) * FNET_CH
GQA_Q_WIDTH = GQA_Q_HEADS * HEAD_DIM
GQA_KV_WIDTH = GQA_KV_HEADS * HEAD_DIM
LOG2_E = math.log2(math.e)
ATTN_SCALE = HEAD_DIM ** -0.5 * LOG2_E

MOD_ROWS = 16
VMEM_LIMIT = 60 * 1024 * 1024
HALO = 8
NA_QROWS = 4
NA_WROWS = 12
NA_INVALID = 2 * NA_KR - 1
GQA_QUERIES = 512
NA_HEADS_PER_STEP = 6
FFN_TOKENS = 1024
FFN_ROWS = 128
W_TILE = 512
IN_TOKENS = 1024
IN_WIDE_TILE = 1024
IN_ROWS = 256


def _params(*sem, flags=None):
    return pltpu.CompilerParams(dimension_semantics=sem, vmem_limit_bytes=VMEM_LIMIT, flags=flags)


def _rms_modulate(x, g, shift, scale):
    ms = jnp.mean(x * x, axis=-1, keepdims=True)
    return (x * lax.rsqrt(ms + EPS)) * (g * (1.0 + scale)) + shift


def _mod_spec(k, tm, seq, base_row, per_seq):
    def index(m, n):
        return ((base_row + per_seq * ((m * tm) // seq)) * 6 + k, 0, 0)
    return pl.BlockSpec((None, 1, D_MODEL), index)


def _ada_kernel(c_ref, w_ref, b_ref, o_ref):
    c = c_ref[...]
    s = (c * jax.nn.sigmoid(c)).astype(bf16)
    o_ref[...] = jnp.dot(s, w_ref[...].astype(bf16), preferred_element_type=f32) + b_ref[...]


def ada_modulation(cond, ada_w, ada_b):
    tn = 1024
    n_out = 6 * D_MODEL
    return pl.pallas_call(
        _ada_kernel,
        grid=(DEPTH, n_out // tn),
        in_specs=[
            pl.BlockSpec((MOD_ROWS, D_MODEL), lambda i, n: (0, 0)),
            pl.BlockSpec((None, D_MODEL, tn), lambda i, n: (i, 0, n)),
            pl.BlockSpec((None, 1, tn), lambda i, n: (i, 0, n)),
        ],
        out_specs=pl.BlockSpec((None, MOD_ROWS, tn), lambda i, n: (i, 0, n)),
        out_shape=jax.ShapeDtypeStruct((DEPTH, MOD_ROWS, n_out), f32),
        compiler_params=_params("arbitrary", "arbitrary"),
        name="ada_modulation",
    )(cond, ada_w, ada_b.reshape(DEPTH, 1, n_out))


def _rope(a, cos, sin):
    lane = lax.broadcasted_iota(jnp.int32, a.shape, 1)
    quarter = HEAD_DIM // 4
    partner = jnp.where((lane & (2 * quarter - 1)) < quarter,
                        pltpu.roll(a, HEAD_DIM - quarter, 1),
                        pltpu.roll(a, quarter, 1))
    return a * cos + partner * sin


def _in_proj_kernel(tn, q_blocks, k_blocks, has_norm, has_rope, *refs):
    it = iter(refs)
    x_hbm, shift_ref, scale_ref, g_ref, w_ref = (next(it) for _ in range(5))
    qg_ref = kg_ref = cos_ref = sin_ref = qs_ref = None
    if has_norm:
        qg_ref, kg_ref, avg_ref = next(it), next(it), next(it)
    else:
        qs_ref = next(it)
    if has_rope:
        cos_ref, sin_ref = next(it), next(it)
    o_ref, h_ref, x_buf, x_sem = next(it), next(it), next(it), next(it)
    m = pl.program_id(0)
    n = pl.program_id(1)
    tm = h_ref.shape[0]
    everything = slice(0, tm)

    def x_copy(tile, slot):
        return pltpu.make_async_copy(x_hbm.at[pl.ds(tile * tm, tm), :], x_buf.at[slot], x_sem.at[slot])

    slot = m % 2
    x_ref = x_buf.at[slot]

    @pl.when((n == 0) & (m == 0))
    def _():
        x_copy(0, 0).start()

    @pl.when(n == 0)
    def _():
        x_copy(m, slot).wait()

    @pl.when((n == 1) & (m + 1 < pl.num_programs(0)))
    def _():
        x_copy(m + 1, 1 - slot).start()

    def project(rs, h=None):
        h = h_ref[rs, :] if h is None else h
        return jnp.dot(h, w_ref[...], preferred_element_type=f32)

    def store_scaled(rs, acc, scale):
        o_ref[rs, :] = (acc if scale is None else acc * scale).astype(o_ref.dtype)

    def store_normed(rs, acc, gn):
        ms = jnp.dot((acc * acc).astype(bf16), avg_ref[...], preferred_element_type=f32)
        inv = lax.rsqrt(ms + EPS)
        for hh in range(tn // HEAD_DIM):
            sl = slice(hh * HEAD_DIM, (hh + 1) * HEAD_DIM)
            a = acc[:, sl] * inv[:, sl] * gn
            if has_rope:
                a = _rope(a, cos_ref[rs, :], sin_ref[rs, :])
            o_ref[rs, sl] = a.astype(o_ref.dtype)

    @pl.when(n == 0)
    def _():
        g, sh, sc = g_ref[...], shift_ref[...], scale_ref[...]
        for r0 in range(0, tm, IN_ROWS):
            rs = slice(r0, r0 + IN_ROWS)
            h = _rms_modulate(x_ref[rs, :], g, sh, sc).astype(bf16)
            h_ref[rs, :] = h
            if has_norm:
                store_normed(rs, project(rs, h), qg_ref[...] * ATTN_SCALE)
            else:
                store_scaled(rs, project(rs, h), qs_ref[...])

    if not has_norm:
        @pl.when(n > 0)
        def _():
            store_scaled(everything, project(everything), qs_ref[...])
        return

    is_q = n < q_blocks
    q_scale = jnp.where(is_q, ATTN_SCALE, 1.0)

    is_qk = n < q_blocks + k_blocks

    @pl.when((n > 0) & is_qk)
    def _():
        gn = jnp.where(is_q, qg_ref[...], kg_ref[...]) * q_scale
        store_normed(everything, project(everything), gn)

    @pl.when(jnp.logical_not(is_qk))
    def _():
        store_scaled(everything, project(everything), None)


def in_proj(x, mod, g, w, out_dtype, q_width, seq, base_row, per_seq, norm=None, rope=None, w_rot=0):
    t_tokens = x.shape[0]
    tm, tn = IN_TOKENS, (W_TILE if norm is not None else IN_WIDE_TILE)
    n_total = w.shape[1]
    assert n_total % tn == 0 and n_total // tn >= 2
    in_specs = [
        pl.BlockSpec(memory_space=pl.ANY),
        _mod_spec(0, tm, seq, base_row, per_seq),
        _mod_spec(1, tm, seq, base_row, per_seq),
        pl.BlockSpec((1, D_MODEL), lambda m, n: (0, 0)),
        pl.BlockSpec((D_MODEL, tn), lambda m, n: (0, (n + w_rot) % (n_total // tn))),
    ]
    args = [x, mod, mod, g.reshape(1, D_MODEL), w]
    k_width = 0
    if norm is not None:
        k_width = norm[2]
        in_specs += [pl.BlockSpec((1, HEAD_DIM), lambda m, n: (0, 0))] * 2
        in_specs += [pl.BlockSpec((tn, tn), lambda m, n: (0, 0))]
        head_of = np.arange(tn) // HEAD_DIM
        avg = (head_of[:, None] == head_of[None, :]) / HEAD_DIM
        args += [norm[0].reshape(1, HEAD_DIM), norm[1].reshape(1, HEAD_DIM), jnp.asarray(avg, dtype=bf16)]
    else:
        in_specs += [pl.BlockSpec((1, tn), lambda m, n: (0, n))]
        args += [jnp.asarray(np.where(np.arange(n_total) < q_width, ATTN_SCALE, 1.0)[None, :], dtype=f32)]
    if rope is not None:
        spb = seq // tm
        in_specs += [pl.BlockSpec((tm, HEAD_DIM), lambda m, n: (m % spb, 0))] * 2
        args += [rope[0], rope[1]]
    kern = functools.partial(_in_proj_kernel, tn, q_width // tn, k_width // tn,
                             norm is not None, rope is not None)
    return pl.pallas_call(
        kern,
        grid=(t_tokens // tm, n_total // tn),
        in_specs=in_specs,
        out_specs=pl.BlockSpec((tm, tn), lambda m, n: (m, n)),
        out_shape=jax.ShapeDtypeStruct((t_tokens, n_total), out_dtype),
        scratch_shapes=[pltpu.VMEM((tm, D_MODEL), bf16),
                        pltpu.VMEM((2, tm, D_MODEL), f32),
                        pltpu.SemaphoreType.DMA((2,))],
        compiler_params=_params("arbitrary", "arbitrary"),
        name="in_proj",
    )(*args)


def _with_ones(v):
    return jnp.concatenate([v, jnp.ones_like(v)], axis=1)


def _softmax_pv(parts):
    m = functools.reduce(jnp.maximum, [jnp.max(s, axis=-1, keepdims=True) for s, _ in parts])
    acc = sum(jnp.dot(jnp.exp2(s - m).astype(bf16), v1, preferred_element_type=f32) for s, v1 in parts)
    return acc[:, :HEAD_DIM] / acc[:, HEAD_DIM:]


def _ctx_attn_kernel(n_q, n_kv, q_ref, k_ref, v_ref, o_ref):
    group = n_q // n_kv
    values = [_with_ones(v_ref[:, kv * HEAD_DIM:(kv + 1) * HEAD_DIM].astype(bf16)) for kv in range(n_kv)]
    keys = [k_ref[:, kv * HEAD_DIM:(kv + 1) * HEAD_DIM].astype(bf16) for kv in range(n_kv)]
    dn = (((1,), (1,)), ((), ()))
    scores = [lax.dot_general(q_ref[:, h * HEAD_DIM:(h + 1) * HEAD_DIM].astype(bf16), keys[h // group],
                              dn, preferred_element_type=f32) for h in range(n_q)]
    for h, s in enumerate(scores):
        o = _softmax_pv([(s, values[h // group])])
        o_ref[:, h * HEAD_DIM:(h + 1) * HEAD_DIM] = o.astype(o_ref.dtype)


def ctx_attention(proj, seq, n_q, n_kv):
    t_tokens = proj.shape[0]
    qw, kw = n_q * HEAD_DIM, n_kv * HEAD_DIM
    assert qw % kw == 0
    k_blk = qw // kw
    return pl.pallas_call(
        functools.partial(_ctx_attn_kernel, n_q, n_kv),
        grid=(t_tokens // seq,),
        in_specs=[
            pl.BlockSpec((seq, qw), lambda b: (b, 0)),
            pl.BlockSpec((seq, kw), lambda b: (b, k_blk)),
            pl.BlockSpec((seq, kw), lambda b: (b, k_blk + 1)),
        ],
        out_specs=pl.BlockSpec((seq, qw), lambda b: (b, 0)),
        out_shape=jax.ShapeDtypeStruct((t_tokens, qw), bf16),
        compiler_params=_params("arbitrary"),
        name="ctx_attention",
    )(proj, proj, proj)


def _gqa_attn_kernel(group, q_ref, ck_ref, cv_ref, k_ref, v_ref, o_ref):
    ck = ck_ref[...].astype(bf16)
    cv = _with_ones(cv_ref[...].astype(bf16))
    k = k_ref[...]
    v = _with_ones(v_ref[...])
    dn = (((1,), (1,)), ((), ()))
    scores = []
    for gi in range(group):
        q = q_ref[:, gi * HEAD_DIM:(gi + 1) * HEAD_DIM]
        scores.append((lax.dot_general(q, ck, dn, preferred_element_type=f32),
                       lax.dot_general(q, k, dn, preferred_element_type=f32)))
    for gi, (s_c, s_l) in enumerate(scores):
        o = _softmax_pv([(s_c, cv), (s_l, v)])
        o_ref[:, gi * HEAD_DIM:(gi + 1) * HEAD_DIM] = o.astype(o_ref.dtype)


def gqa_attention(proj, cache_k, cache_v, seq, past):
    t_tokens = proj.shape[0]
    n_b = t_tokens // seq
    group = GQA_Q_HEADS // GQA_KV_HEADS
    tq = GQA_QUERIES
    nq = seq // tq
    return pl.pallas_call(
        functools.partial(_gqa_attn_kernel, group),
        grid=(n_b, GQA_KV_HEADS, nq),
        in_specs=[
            pl.BlockSpec((tq, group * HEAD_DIM), lambda b, h, i: (b * nq + i, h)),
            pl.BlockSpec((past, HEAD_DIM), lambda b, h, i: (b, h)),
            pl.BlockSpec((past, HEAD_DIM), lambda b, h, i: (b, h)),
            pl.BlockSpec((seq, HEAD_DIM), lambda b, h, i: (b, GQA_Q_HEADS + h)),
            pl.BlockSpec((seq, HEAD_DIM), lambda b, h, i: (b, GQA_Q_HEADS + GQA_KV_HEADS + h)),
        ],
        out_specs=pl.BlockSpec((tq, group * HEAD_DIM), lambda b, h, i: (b * nq + i, h)),
        out_shape=jax.ShapeDtypeStruct((t_tokens, GQA_Q_WIDTH), bf16),
        compiler_params=_params("arbitrary", "arbitrary", "arbitrary"),
        name="gqa_attention",
    )(proj, cache_k, cache_v, proj, proj)


def _na_bias_tiles(rel_bias):
    qc = np.arange(GRID_W)[:, None]
    kc = np.arange(GRID_W)[None, :]
    q_start = np.clip(qc - NA_KC // 2, 0, GRID_W - NA_KC)
    valid = (kc >= q_start) & (kc < q_start + NA_KC)
    dc = np.clip(kc - qc, -(NA_KC - 1), NA_KC - 1) + NA_KC - 1
    pick = (np.arange(2 * NA_KC - 1)[:, None, None] == dc[None]) & valid[None]
    tiles = jnp.einsum("hdc,cqk->hdqk", rel_bias.astype(f32) * LOG2_E, jnp.asarray(pick, dtype=f32),
                       precision=lax.Precision.HIGHEST)
    tiles = jnp.where(jnp.asarray(valid)[None, None], tiles, NEG_INF)
    masked = jnp.full((NA_HEADS, 1, GRID_W, GRID_W), NEG_INF, f32)
    tiles = jnp.concatenate([tiles, masked], axis=1)
    zeros = jnp.zeros_like(tiles)
    left = jnp.concatenate([tiles, zeros], axis=-1)
    right = jnp.concatenate([zeros, tiles], axis=-1)
    return left, right


def _na_attn_kernel(n_heads, n_rows, q_ref, k_ref, v_ref, ck_ref, cv_ref, bl_ref, br_ref, o_ref):
    rb = pl.program_id(2)
    start = jnp.clip(rb * NA_QROWS - NA_KR // 2, 0, n_rows - NA_WROWS)
    tok0 = pl.multiple_of(start * GRID_W, GRID_W)
    dn = (((1,), (1,)), ((), ()))

    def tile_index(i, kr):
        qr = rb * NA_QROWS + i
        r0 = jnp.clip(qr - NA_KR // 2, 0, n_rows - NA_KR)
        return jnp.where((kr >= r0) & (kr < r0 + NA_KR), kr - qr + NA_KR - 1, NA_INVALID)

    index = [[(tile_index(i, start + 2 * jp), tile_index(i, start + 2 * jp + 1))
              for jp in range(NA_WROWS // 2)] for i in range(NA_QROWS)]
    scores = []
    for hh in range(n_heads):
        hs = slice(hh * HEAD_DIM, (hh + 1) * HEAD_DIM)
        q = q_ref[:, hs]
        bias = jnp.concatenate(
            [jnp.concatenate([bl_ref[hh, ia] + br_ref[hh, ib] for ia, ib in row], axis=1)
             for row in index], axis=0)
        s_l = lax.dot_general(q, k_ref[pl.ds(tok0, NA_WROWS * GRID_W), hs], dn,
                              preferred_element_type=f32) + bias
        s_c = lax.dot_general(q, ck_ref[:, hs].astype(bf16), dn, preferred_element_type=f32)
        scores.append((s_c, s_l))
    for hh, (s_c, s_l) in enumerate(scores):
        hs = slice(hh * HEAD_DIM, (hh + 1) * HEAD_DIM)
        vw = v_ref[pl.ds(tok0, NA_WROWS * GRID_W), hs]
        o = _softmax_pv([(s_c, _with_ones(cv_ref[:, hs].astype(bf16))), (s_l, _with_ones(vw))])
        o_ref[:, hs] = o.astype(o_ref.dtype)


def na_attention(proj, cache_k, cache_v, rel_bias, seq, past):
    t_tokens = proj.shape[0]
    n_b = t_tokens // seq
    n_rows = seq // GRID_W
    n_rb = n_rows // NA_QROWS
    tq = NA_QROWS * GRID_W
    hb = NA_HEADS_PER_STEP
    n_hg = NA_HEADS // hb
    wide = hb * HEAD_DIM
    bias_l, bias_r = _na_bias_tiles(rel_bias)
    bias_spec = pl.BlockSpec((hb, 2 * NA_KR, GRID_W, 2 * GRID_W), lambda b, h, r: (h, 0, 0, 0))
    return pl.pallas_call(
        functools.partial(_na_attn_kernel, hb, n_rows),
        grid=(n_b, n_hg, n_rb),
        in_specs=[
            pl.BlockSpec((tq, wide), lambda b, h, r: (b * n_rb + r, h)),
            pl.BlockSpec((seq, wide), lambda b, h, r: (b, n_hg + h)),
            pl.BlockSpec((seq, wide), lambda b, h, r: (b, 2 * n_hg + h)),
            pl.BlockSpec((past, wide), lambda b, h, r: (b, h)),
            pl.BlockSpec((past, wide), lambda b, h, r: (b, h)),
            bias_spec,
            bias_spec,
        ],
        out_specs=pl.BlockSpec((tq, wide), lambda b, h, r: (b * n_rb + r, h)),
        out_shape=jax.ShapeDtypeStruct((t_tokens, NA_WIDTH), bf16),
        compiler_params=_params("arbitrary", "arbitrary", "arbitrary"),
        name="na_attention",
    )(proj, proj, proj, cache_k, cache_v, bias_l, bias_r)


def _dft_cols_kernel(x_ref, cs_ref, y_ref):
    for g in range(FNET_GROUPS):
        sl = slice(g * FNET_CH, (g + 1) * FNET_CH)
        y = jnp.dot(x_ref[:, sl].astype(bf16), cs_ref[...], preferred_element_type=f32)
        y_ref[0, :, sl] = y[:, :FNET_CH].astype(bf16)
        y_ref[1, :, sl] = y[:, FNET_CH:].astype(bf16)


def _dft_rows_kernel(d_ref, y_ref, w_ref, o_ref):
    f = jnp.dot(d_ref[...], y_ref[...], preferred_element_type=f32)
    for g in range(FNET_GROUPS):
        sl = slice(g * FNET_CH, (g + 1) * FNET_CH)
        o = jnp.dot(f[:, sl].astype(bf16), w_ref[g], preferred_element_type=f32)
        o_ref[:, sl] = o.astype(o_ref.dtype)


def _dft_tables(seq):
    c = np.arange(FNET_CH, dtype=np.int64)
    ang_c = (2.0 * np.pi / FNET_CH) * ((c[:, None] * c[None, :]) % FNET_CH)
    cs = np.concatenate([np.cos(ang_c), np.sin(ang_c)], axis=1)
    n = np.arange(seq, dtype=np.int64)
    ang_n = (2.0 * np.pi / seq) * ((n[:, None] * n[None, :]) % seq)
    norm = 1.0 / math.sqrt(seq * FNET_CH)
    d = np.concatenate([np.cos(ang_n) * norm, -np.sin(ang_n) * norm], axis=1)
    return jnp.asarray(cs, dtype=bf16), jnp.asarray(d, dtype=bf16)


def fourier_mix(proj, col_blk, fnet_w, seq):
    t_tokens = proj.shape[0]
    n_b = t_tokens // seq
    cs, d = _dft_tables(seq)
    tn = min(seq, 512)
    ns = seq // tn
    y = pl.pallas_call(
        _dft_cols_kernel,
        grid=(n_b, ns),
        in_specs=[
            pl.BlockSpec((tn, FNET_WIDTH), lambda b, i: (b * ns + i, col_blk)),
            pl.BlockSpec((FNET_CH, 2 * FNET_CH), lambda b, i: (0, 0)),
        ],
        out_specs=pl.BlockSpec((2, tn, FNET_WIDTH), lambda b, i: (0, i, b)),
        out_shape=jax.ShapeDtypeStruct((2, seq, n_b * FNET_WIDTH), bf16),
        compiler_params=_params("arbitrary", "arbitrary"),
        name="dft_cols",
    )(proj, cs)
    y2 = y.reshape(2 * seq, n_b * FNET_WIDTH)
    return pl.pallas_call(
        _dft_rows_kernel,
        grid=(n_b, ns),
        in_specs=[
            pl.BlockSpec((tn, 2 * seq), lambda b, i: (i, 0)),
            pl.BlockSpec((2 * seq, FNET_WIDTH), lambda b, i: (0, b)),
            pl.BlockSpec((FNET_GROUPS, FNET_CH, FNET_CH), lambda b, i: (0, 0, 0)),
        ],
        out_specs=pl.BlockSpec((tn, FNET_WIDTH), lambda b, i: (b * ns + i, 0)),
        out_shape=jax.ShapeDtypeStruct((t_tokens, FNET_WIDTH), bf16),
        compiler_params=_params("arbitrary", "arbitrary"),
        name="dft_rows",
    )(d, y2, fnet_w.astype(bf16))


def _pool_kernel(seq, x_ref, w_ref, sc_ref, o_ref):
    t = lax.broadcasted_iota(jnp.int32, (seq, 1), 0)

    def prev(a, k):
        return jnp.where(t >= k, pltpu.roll(a, k, 0), 0.0)

    def nxt(a, k):
        return jnp.where(t < seq - k, pltpu.roll(a, seq - k, 0), 0.0)

    for g, win in enumerate(POOL_WINDOWS):
        sl = slice(g * FNET_CH, (g + 1) * FNET_CH)
        x = x_ref[:, sl].astype(f32)
        half = win // 2
        back, fwd, k = x, x, 1
        while k < half:
            back = back + prev(back, k)
            fwd = fwd + nxt(fwd, k)
            k *= 2
        total = prev(back, 1) + fwd
        cnt = (jnp.minimum(t + half, seq) - jnp.maximum(t - half, 0)).astype(f32)
        pooled = (total / cnt - x).astype(bf16)
        o = jnp.dot(pooled, w_ref[g], preferred_element_type=f32) * sc_ref[:, sl]
        o_ref[:, sl] = o.astype(o_ref.dtype)


def pool_mix(proj, col_blk, pool_w, pool_scale, seq):
    t_tokens = proj.shape[0]
    return pl.pallas_call(
        functools.partial(_pool_kernel, seq),
        grid=(t_tokens // seq,),
        in_specs=[
            pl.BlockSpec((seq, POOL_WIDTH), lambda b: (b, col_blk)),
            pl.BlockSpec((len(POOL_WINDOWS), FNET_CH, FNET_CH), lambda b: (0, 0, 0)),
            pl.BlockSpec((1, POOL_WIDTH), lambda b: (0, 0)),
        ],
        out_specs=pl.BlockSpec((seq, POOL_WIDTH), lambda b: (b, 0)),
        out_shape=jax.ShapeDtypeStruct((t_tokens, POOL_WIDTH), bf16),
        compiler_params=_params("arbitrary"),
        name="pool_mix",
    )(proj, pool_w.astype(bf16), pool_scale.reshape(1, POOL_WIDTH))


def _out_proj_kernel(x_ref, gate_ref, a_ref, b_ref, wa_ref, wb_ref, o_ref):
    y = (jnp.dot(a_ref[...], wa_ref[...], preferred_element_type=f32)
         + jnp.dot(b_ref[...], wb_ref[...], preferred_element_type=f32))
    o_ref[...] = x_ref[...] + gate_ref[...] * y


def out_proj(x, mod, a, b, wa, wb, seq, base_row, per_seq):
    t_tokens = x.shape[0]
    tm = 512
    return pl.pallas_call(
        _out_proj_kernel,
        grid=(t_tokens // tm, 1),
        in_specs=[
            pl.BlockSpec((tm, D_MODEL), lambda m, n: (m, 0)),
            _mod_spec(2, tm, seq, base_row, per_seq),
            pl.BlockSpec((tm, a.shape[1]), lambda m, n: (m, 0)),
            pl.BlockSpec((tm, b.shape[1]), lambda m, n: (m, 0)),
            pl.BlockSpec(wa.shape, lambda m, n: (0, 0)),
            pl.BlockSpec(wb.shape, lambda m, n: (0, 0)),
        ],
        out_specs=pl.BlockSpec((tm, D_MODEL), lambda m, n: (m, 0)),
        out_shape=jax.ShapeDtypeStruct((t_tokens, D_MODEL), f32),
        compiler_params=_params("arbitrary", "arbitrary"),
        name="out_proj",
    )(x, mod, a, b, wa, wb)


def _ffn_kernel(tm, seq, nj, final, *refs):
    it = iter(refs)
    (x_hbm, xp_ref, xn_ref, shift_ref, scale_ref, gate_ref, g_ref,
     wv_ref, wg_ref, cwv_ref, cwg_ref, cbv_ref, cbg_ref, wd_ref) = (next(it) for _ in range(14))
    gf_ref = next(it) if final else None
    o_hbm, h_ref, acc_ref = next(it), next(it), next(it)
    u_refs = [(next(it), next(it)), (next(it), next(it))]
    x_buf, x_sem, o_sem = next(it), next(it), next(it)
    i = pl.program_id(0)
    j = pl.program_id(1)
    n_tiles = pl.num_programs(0)
    rows = tm + 2 * HALO

    def x_copy(tile, slot):
        return pltpu.make_async_copy(x_hbm.at[pl.ds(tile * tm, tm), :], x_buf.at[slot], x_sem.at[slot])

    def o_copy(tile):
        return pltpu.make_async_copy(acc_ref, o_hbm.at[pl.ds(tile * tm, tm), :], o_sem.at[0])

    slot = i % 2
    x_ref = x_buf.at[slot]

    @pl.when((j == 0) & (i == 0))
    def _():
        x_copy(0, 0).start()

    @pl.when(j == 0)
    def _():
        x_copy(i, slot).wait()

    @pl.when((j == 1) & (i + 1 < n_tiles))
    def _():
        x_copy(i + 1, 1 - slot).start()

    def prologue_and_first_up():
        g, sh, sc = g_ref[...], shift_ref[...], scale_ref[...]
        uv_ref, ug_ref = u_refs[0]

        def emit(rs, h):
            h_ref[rs, :] = h
            uv_ref[rs, :] = jnp.dot(h, wv_ref[...], preferred_element_type=f32)
            ug_ref[rs, :] = jnp.dot(h, wg_ref[...], preferred_element_type=f32)

        for r0 in range(0, tm - IN_ROWS, IN_ROWS):
            rs = slice(r0, r0 + IN_ROWS)
            emit(rs, _rms_modulate(x_ref[rs, :], g, sh, sc).astype(bf16))
        next_ok = jnp.where((((i + 1) * tm) & (seq - 1)) != 0, 1.0, 0.0)
        prev_ok = jnp.where(((i * tm) & (seq - 1)) != 0, 1.0, 0.0)
        h_last = _rms_modulate(x_ref[tm - IN_ROWS:tm, :], g, sh, sc)
        h_next = _rms_modulate(xn_ref[...], g, sh, sc) * next_ok
        h_prev = _rms_modulate(xp_ref[...], g, sh, sc) * prev_ok
        emit(slice(tm - IN_ROWS, rows), jnp.concatenate([h_last, h_next, h_prev], axis=0).astype(bf16))

    def zero_row(a, r):
        r0 = (r // HALO) * HALO
        row8 = lax.broadcasted_iota(jnp.int32, (HALO, 1), 0)
        fixed = jnp.where(row8 == r - r0, 0.0, a[r0:r0 + HALO])
        return jnp.concatenate([a[:r0], fixed, a[r0 + HALO:]], axis=0)

    def window(u_ref, r0):
        lo, hi = r0 - HALO, r0 + FFN_ROWS + HALO
        if lo < 0:
            return jnp.concatenate([u_ref[rows + lo:rows, :], u_ref[0:hi, :]], axis=0)
        return u_ref[lo:hi, :]

    def conv(u_ref, r0, cw_ref, cb_ref):
        u = window(u_ref, r0)
        n = FFN_ROWS + 2 * HALO
        up = pltpu.roll(u, 1, 0)
        un = pltpu.roll(u, n - 1, 0)
        for edge in range(seq, tm, seq):
            if r0 <= edge < r0 + FFN_ROWS:
                up = zero_row(up, edge - r0 + HALO)
            if r0 <= edge - 1 < r0 + FFN_ROWS:
                un = zero_row(un, edge - 1 - r0 + HALO)
        keep = slice(HALO, HALO + FFN_ROWS)
        return (up[keep] * cw_ref[0:1, :] + u[keep] * cw_ref[1:2, :] + un[keep] * cw_ref[2:3, :]
                + cb_ref[...])

    def gated(slot):
        uv_ref, ug_ref = u_refs[slot]
        chunks = []
        for r0 in range(0, tm, FFN_ROWS):
            val = conv(uv_ref, r0, cwv_ref, cbv_ref)
            gate = conv(ug_ref, r0, cwg_ref, cbg_ref)
            chunks.append((gate * jax.nn.sigmoid(gate) * val).astype(bf16))
        return chunks

    def zero_after(chunks):
        parts = []
        for c in chunks:
            bits = pltpu.bitcast(c, jnp.uint32)
            parts += [bits[k:k + HALO] for k in range(0, bits.shape[0], HALO)]
        while len(parts) > 1:
            parts = [a | b for a, b in zip(parts[0::2], parts[1::2])] + parts[len(parts) & ~1:]
        return (parts[0] >> 16) >> 16

    def after(a, zero, row0=0):
        tf = zero.shape[1]
        mid = pltpu.bitcast(pltpu.bitcast(a[row0:row0 + 2 * HALO, 0:tf], jnp.uint32) | zero, bf16)
        if tf < a.shape[1]:
            mid = jnp.concatenate([mid, a[row0:row0 + 2 * HALO, tf:]], axis=1)
        parts = ([a[:row0]] if row0 else []) + [mid, a[row0 + 2 * HALO:]]
        return jnp.concatenate(parts, axis=0)

    def quarter(chunks, k):
        n = len(chunks) // 4
        return chunks[k * n:(k + 1) * n]

    def up_project(slot, chunks):
        h = h_ref[...]
        wv, wg = wv_ref[...], wg_ref[...]
        uv_ref, ug_ref = u_refs[slot]
        k_mid = wv.shape[0] // 2
        wv = after(wv, zero_after(quarter(chunks, 0)), k_mid)
        wg = after(wg, zero_after(quarter(chunks, 2)), k_mid)
        uv_ref[...] = jnp.dot(h, wv, preferred_element_type=f32)
        h = after(h, zero_after(quarter(chunks, 1)))
        ug_ref[...] = jnp.dot(h, wg, preferred_element_type=f32)

    def down_project(chunks, anchored, first=False):
        if anchored:
            chunks = [after(chunks[0], zero_after(quarter(chunks, 3)))] + chunks[1:]
        act = jnp.concatenate(chunks, axis=0)
        y = jnp.dot(act, wd_ref[...], preferred_element_type=f32)
        if first:
            acc_ref[...] = y
        else:
            acc_ref[...] += y

    @pl.when(j == 0)
    def _():
        prologue_and_first_up()

    for has_prev in (False, True):
        @pl.when((j == 1) & ((i > 0) == has_prev))
        def _(has_prev=has_prev):
            chunks = gated(0)
            up_project(1, chunks)
            if has_prev:
                o_copy(i - 1).wait()
            down_project(chunks, True, first=True)

    for p in range(2):
        @pl.when((j >= 2) & (j < nj) & (j % 2 == p))
        def _(p=p):
            chunks = gated(1 - p)
            up_project(p, chunks)
            down_project(chunks, True)

    @pl.when(j == nj)
    def _():
        down_project(gated((nj - 1) % 2), False)
        y = x_ref[...] + gate_ref[...] * acc_ref[...]
        if final:
            ms = jnp.mean(y * y, axis=-1, keepdims=True)
            y = y * lax.rsqrt(ms + EPS) * gf_ref[...]
        acc_ref[...] = y
        o_copy(i).start()

    @pl.when((j == nj) & (i == n_tiles - 1))
    def _():
        o_copy(i).wait()


def conv_ffn(x, mod, g, w_up, conv_w, conv_b, w_down, seq, base_row, per_seq, final_g=None):
    t_tokens = x.shape[0]
    tm, tf = FFN_TOKENS, W_TILE
    assert seq & (seq - 1) == 0 and (tm // FFN_ROWS) % 4 == 0
    nj = D_FF // tf
    hb = tm // HALO
    n_hblk = t_tokens // HALO
    conv_b2 = conv_b.reshape(1, 2 * D_FF)

    def blk(j):
        return jnp.clip(j, 0, nj - 1)

    assert nj >= 2
    in_specs = [
        pl.BlockSpec(memory_space=pl.ANY),
        pl.BlockSpec((HALO, D_MODEL), lambda i, j: (jnp.maximum(i * hb - 1, 0), 0)),
        pl.BlockSpec((HALO, D_MODEL), lambda i, j: (jnp.minimum((i + 1) * hb, n_hblk - 1), 0)),
        _mod_spec(3, tm, seq, base_row, per_seq),
        _mod_spec(4, tm, seq, base_row, per_seq),
        _mod_spec(5, tm, seq, base_row, per_seq),
        pl.BlockSpec((1, D_MODEL), lambda i, j: (0, 0)),
        pl.BlockSpec((D_MODEL, tf), lambda i, j: (0, blk(j))),
        pl.BlockSpec((D_MODEL, tf), lambda i, j: (0, nj + blk(j))),
        pl.BlockSpec((3, tf), lambda i, j: (0, blk(j - 1))),
        pl.BlockSpec((3, tf), lambda i, j: (0, nj + blk(j - 1))),
        pl.BlockSpec((1, tf), lambda i, j: (0, blk(j - 1))),
        pl.BlockSpec((1, tf), lambda i, j: (0, nj + blk(j - 1))),
        pl.BlockSpec((tf, D_MODEL), lambda i, j: (blk(j - 1), 0)),
    ]
    args = [x, x, x, mod, mod, mod, g.reshape(1, D_MODEL), w_up, w_up,
            conv_w, conv_w, conv_b2, conv_b2, w_down]
    if final_g is not None:
        in_specs.append(pl.BlockSpec((1, D_MODEL), lambda i, j: (0, 0)))
        args.append(final_g.reshape(1, D_MODEL))
    return pl.pallas_call(
        functools.partial(_ffn_kernel, tm, seq, nj, final_g is not None),
        grid=(t_tokens // tm, nj + 1),
        in_specs=in_specs,
        out_specs=pl.BlockSpec(memory_space=pl.ANY),
        out_shape=jax.ShapeDtypeStruct((t_tokens, D_MODEL), f32),
        scratch_shapes=[pltpu.VMEM((tm + 2 * HALO, D_MODEL), bf16),
                        pltpu.VMEM((tm, D_MODEL), f32)]
                       + [pltpu.VMEM((tm + 2 * HALO, tf), f32)] * 4
                       + [pltpu.VMEM((2, tm, D_MODEL), f32),
                          pltpu.SemaphoreType.DMA((2,)),
                          pltpu.SemaphoreType.DMA((1,))],
        compiler_params=_params("arbitrary", "arbitrary"),
        name="conv_ffn",
    )(*args)


def _rope_tables(seq):
    t = np.arange(seq)
    half = HEAD_DIM // 2
    inv_freq = np.power(ROPE_BASE, -np.arange(0, half, 2, dtype=np.float64) / half)
    ang_r = (t // GRID_W)[:, None] * inv_freq[None, :]
    ang_c = (t % GRID_W)[:, None] * inv_freq[None, :]
    cos = np.concatenate([np.cos(ang_r)] * 2 + [np.cos(ang_c)] * 2, axis=-1)
    sin = np.concatenate([-np.sin(ang_r), np.sin(ang_r), -np.sin(ang_c), np.sin(ang_c)], axis=-1)
    return jnp.asarray(cos, dtype=f32), jnp.asarray(sin, dtype=f32)


def kernel(x_prompt, x_sample, c, cache_na_k, cache_na_v, cache_gqa_k, cache_gqa_v, c_ctx, norm1_g, norm2_g, ada_w, ada_b, ev_w_in, ev_na_bias, ev_fnet_w, ev_w_out, od_w_in, od_pool_w, od_pool_scale, od_q_norm_g, od_k_norm_g, od_w_out, ffn_w_up, ffn_conv_w, ffn_conv_b, ffn_w_down, final_norm_g):
    n_ctx, seq_p, _ = x_prompt.shape
    n_lat, seq_s, _ = x_sample.shape
    past = cache_na_k.shape[2]
    assert n_lat + 1 <= MOD_ROWS
    xp = x_prompt.reshape(n_ctx * seq_p, D_MODEL)
    xs = x_sample.reshape(n_lat * seq_s, D_MODEL)

    cond = jnp.zeros((MOD_ROWS, D_MODEL), f32).at[0].set(c_ctx).at[1:1 + n_lat].set(c)
    mods = ada_modulation(cond, ada_w, ada_b).reshape(DEPTH, MOD_ROWS * 6, 1, D_MODEL)
    rope = _rope_tables(seq_s)

    ctx = dict(seq=seq_p, base_row=0, per_seq=0)
    lat = dict(seq=seq_s, base_row=1, per_seq=1)
    new_k, new_v = {}, {}
    for i in range(DEPTH):
        j = i // 2
        mod = mods[i]
        w_up = ffn_w_up[i].astype(bf16)
        w_down = ffn_w_down[i].astype(bf16)
        if i % 2 == 0:
            w_in = ev_w_in[j].astype(bf16)
            w_attn = ev_w_out[j, :NA_WIDTH].astype(bf16)
            w_mix = ev_w_out[j, NA_WIDTH:].astype(bf16)
            mix_blk = 3 * NA_WIDTH // FNET_WIDTH
            proj_p = in_proj(xp, mod, norm1_g[i], w_in, f32, NA_WIDTH, **ctx)
            new_k[i] = proj_p[:, NA_WIDTH:2 * NA_WIDTH]
            new_v[i] = proj_p[:, 2 * NA_WIDTH:3 * NA_WIDTH]
            attn_p = ctx_attention(proj_p, seq_p, NA_HEADS, NA_HEADS)
            mix_p = fourier_mix(proj_p, mix_blk, ev_fnet_w[j], seq_p)
            proj_s = in_proj(xs, mod, norm1_g[i], w_in, bf16, NA_WIDTH, **lat)
            attn_s = na_attention(proj_s,
                                  cache_na_k[:, j].reshape(n_lat * past, NA_WIDTH),
                                  cache_na_v[:, j].reshape(n_lat * past, NA_WIDTH),
                                  ev_na_bias[j], seq_s, past)
            mix_s = fourier_mix(proj_s, mix_blk, ev_fnet_w[j], seq_s)
        else:
            w_in = od_w_in[j].astype(bf16)
            rot = dict(w_rot=POOL_WIDTH // W_TILE)
            w_attn = od_w_out[j, POOL_WIDTH:].astype(bf16)
            w_mix = od_w_out[j, :POOL_WIDTH].astype(bf16)
            norm = (od_q_norm_g[j], od_k_norm_g[j], GQA_KV_WIDTH)
            k_lo = GQA_Q_WIDTH
            v_lo = GQA_Q_WIDTH + GQA_KV_WIDTH
            mix_blk = (GQA_Q_WIDTH + 2 * GQA_KV_WIDTH) // POOL_WIDTH
            proj_p = in_proj(xp, mod, norm1_g[i], w_in, f32, GQA_Q_WIDTH, norm=norm, **rot, **ctx)
            new_k[i] = proj_p[:, k_lo:k_lo + GQA_KV_WIDTH]
            new_v[i] = proj_p[:, v_lo:v_lo + GQA_KV_WIDTH]
            attn_p = ctx_attention(proj_p, seq_p, GQA_Q_HEADS, GQA_KV_HEADS)
            mix_p = pool_mix(proj_p, mix_blk, od_pool_w[j], od_pool_scale[j], seq_p)
            proj_s = in_proj(xs, mod, norm1_g[i], w_in, bf16, GQA_Q_WIDTH, norm=norm, rope=rope, **rot, **lat)
            attn_s = gqa_attention(proj_s,
                                   cache_gqa_k[:, j].reshape(n_lat * past, GQA_KV_WIDTH),
                                   cache_gqa_v[:, j].reshape(n_lat * past, GQA_KV_WIDTH),
                                   seq_s, past)
            mix_s = pool_mix(proj_s, mix_blk, od_pool_w[j], od_pool_scale[j], seq_s)
        xp = out_proj(xp, mod, attn_p, mix_p, w_attn, w_mix, **ctx)
        xs = out_proj(xs, mod, attn_s, mix_s, w_attn, w_mix, **lat)
        final_g = final_norm_g if i == DEPTH - 1 else None
        xp = conv_ffn(xp, mod, norm2_g[i], w_up, ffn_conv_w[i], ffn_conv_b[i], w_down,
                      final_g=final_g, **ctx)
        xs = conv_ffn(xs, mod, norm2_g[i], w_up, ffn_conv_w[i], ffn_conv_b[i], w_down,
                      final_g=final_g, **lat)

    y_prompt = xp.reshape(n_ctx, seq_p, D_MODEL)
    y_sample = xs.reshape(n_lat, seq_s, D_MODEL)
    even = [i for i in range(DEPTH) if i % 2 == 0]
    odd = [i for i in range(DEPTH) if i % 2 == 1]

    def stack(parts, layers, heads):
        return jnp.stack([parts[i].reshape(n_ctx, seq_p, heads, HEAD_DIM) for i in layers], axis=1)

    return (y_prompt, y_sample,
            stack(new_k, even, NA_HEADS), stack(new_v, even, NA_HEADS),
            stack(new_k, odd, GQA_KV_HEADS), stack(new_v, odd, GQA_KV_HEADS))
```

```python
import functools
import math

import jax
import jax.numpy as jnp
import numpy as np
from jax import lax
from jax.experimental import pallas as pl
from jax.experimental.pallas import tpu as pltpu

f32 = jnp.float32
bf16 = jnp.bfloat16

D_MODEL = 2048
DEPTH = 2
GRID_W = 64
HEAD_DIM = 128
NA_HEADS = 12
NA_KR = 8
NA_KC = 16
FNET_GROUPS = 4
FNET_CH = 128
POOL_WINDOWS = (2, 4, 8, 16)
GQA_Q_HEADS = 12
GQA_KV_HEADS = 4
D_FF = 5632
ROPE_BASE = 10000.0
EPS = 1e-6
NEG_INF = -1e30
NA_WIDTH = NA_HEADS * HEAD_DIM
FNET_WIDTH = FNET_GROUPS * FNET_CH
POOL_WIDTH = len(POOL_WINDOWS) * FNET_CH
GQA_Q_WIDTH = GQA_Q_HEADS * HEAD_DIM
GQA_KV_WIDTH = GQA_KV_HEADS * HEAD_DIM
LOG2_E = math.log2(math.e)
ATTN_SCALE = HEAD_DIM ** -0.5 * LOG2_E

MOD_ROWS = 16
VMEM_LIMIT = 60 * 1024 * 1024
HALO = 8
NA_QROWS = 4
NA_WROWS = 12
NA_INVALID = 2 * NA_KR - 1
GQA_QUERIES = 512
NA_HEADS_PER_STEP = 6
FFN_TOKENS = 1024
FFN_ROWS = 128
W_TILE = 512
IN_TOKENS = 1024
X_PARTS = 4
IN_WIDE_TILE = 1024
IN_ROWS = 256


def _params(*sem, flags=None):
    return pltpu.CompilerParams(dimension_semantics=sem, vmem_limit_bytes=VMEM_LIMIT, flags=flags)


def _rms_modulate(x, g, shift, scale):
    ms = jnp.mean(x * x, axis=-1, keepdims=True)
    return (x * lax.rsqrt(ms + EPS)) * (g * (1.0 + scale)) + shift


def _prefetch_token_tiles(x_hbm, x_buf, x_sem, tile, step, n_tiles, first_step):
    tm = x_buf.shape[1]
    rows = tm // X_PARTS
    slot = tile % 2

    def part(t, s, p):
        return pltpu.make_async_copy(x_hbm.at[pl.ds(t * tm + p * rows, rows), :],
                                     x_buf.at[s, pl.ds(p * rows, rows), :], x_sem.at[s])

    @pl.when((step == 0) & (tile == 0))
    def _():
        for p in range(X_PARTS):
            part(0, 0, p).start()

    @pl.when(step == 0)
    def _():
        for p in range(X_PARTS):
            part(tile, slot, p).wait()

    for p in range(X_PARTS):
        @pl.when((step == first_step + p) & (tile + 1 < n_tiles))
        def _(p=p):
            part(tile + 1, 1 - slot, p).start()

    return x_buf.at[slot]


def _mod_spec(k, tm, seq, base_row, per_seq):
    def index(m, n):
        return ((base_row + per_seq * ((m * tm) // seq)) * 6 + k, 0, 0)
    return pl.BlockSpec((None, 1, D_MODEL), index)


def _ada_kernel(c_ref, w_ref, b_ref, o_ref):
    c = c_ref[...]
    s = (c * jax.nn.sigmoid(c)).astype(bf16)
    o_ref[...] = jnp.dot(s, w_ref[...].astype(bf16), preferred_element_type=f32) + b_ref[...]


def ada_modulation(cond, ada_w, ada_b):
    tn = 1024
    n_out = 6 * D_MODEL
    return pl.pallas_call(
        _ada_kernel,
        grid=(DEPTH, n_out // tn),
        in_specs=[
            pl.BlockSpec((MOD_ROWS, D_MODEL), lambda i, n: (0, 0)),
            pl.BlockSpec((None, D_MODEL, tn), lambda i, n: (i, 0, n)),
            pl.BlockSpec((None, 1, tn), lambda i, n: (i, 0, n)),
        ],
        out_specs=pl.BlockSpec((None, MOD_ROWS, tn), lambda i, n: (i, 0, n)),
        out_shape=jax.ShapeDtypeStruct((DEPTH, MOD_ROWS, n_out), f32),
        compiler_params=_params("arbitrary", "arbitrary"),
        name="ada_modulation",
    )(cond, ada_w, ada_b.reshape(DEPTH, 1, n_out))


def _rope(a, cos, sin):
    lane = lax.broadcasted_iota(jnp.int32, a.shape, 1)
    quarter = HEAD_DIM // 4
    partner = jnp.where((lane & (2 * quarter - 1)) < quarter,
                        pltpu.roll(a, HEAD_DIM - quarter, 1),
                        pltpu.roll(a, quarter, 1))
    return a * cos + partner * sin


def _in_proj_kernel(tn, q_blocks, k_blocks, has_norm, has_rope, *refs):
    it = iter(refs)
    x_hbm, shift_ref, scale_ref, g_ref, w_ref = (next(it) for _ in range(5))
    qg_ref = kg_ref = cos_ref = sin_ref = qs_ref = None
    if has_norm:
        qg_ref, kg_ref, avg_ref = next(it), next(it), next(it)
    else:
        qs_ref = next(it)
    if has_rope:
        cos_ref, sin_ref = next(it), next(it)
    o_ref, h_ref, x_buf, x_sem = next(it), next(it), next(it), next(it)
    m = pl.program_id(0)
    n = pl.program_id(1)
    tm = h_ref.shape[0]
    everything = slice(0, tm)

    x_ref = _prefetch_token_tiles(x_hbm, x_buf, x_sem, m, n, pl.num_programs(0), first_step=1)

    def project(rs, h=None):
        h = h_ref[rs, :] if h is None else h
        return jnp.dot(h, w_ref[...], preferred_element_type=f32)

    def store_scaled(rs, acc, scale):
        o_ref[rs, :] = (acc if scale is None else acc * scale).astype(o_ref.dtype)

    def store_normed(rs, acc, gn):
        ms = jnp.dot((acc * acc).astype(bf16), avg_ref[...], preferred_element_type=f32)
        inv = lax.rsqrt(ms + EPS)
        for hh in range(tn // HEAD_DIM):
            sl = slice(hh * HEAD_DIM, (hh + 1) * HEAD_DIM)
            a = acc[:, sl] * inv[:, sl] * gn
            if has_rope:
                a = _rope(a, cos_ref[rs, :], sin_ref[rs, :])
            o_ref[rs, sl] = a.astype(o_ref.dtype)

    @pl.when(n == 0)
    def _():
        g, sh, sc = g_ref[...], shift_ref[...], scale_ref[...]
        for r0 in range(0, tm, IN_ROWS):
            rs = slice(r0, r0 + IN_ROWS)
            h = _rms_modulate(x_ref[rs, :], g, sh, sc).astype(bf16)
            h_ref[rs, :] = h
            if has_norm:
                store_normed(rs, project(rs, h), qg_ref[...] * ATTN_SCALE)
            else:
                store_scaled(rs, project(rs, h), qs_ref[...])

    if not has_norm:
        @pl.when(n > 0)
        def _():
            store_scaled(everything, project(everything), qs_ref[...])
        return

    is_q = n < q_blocks
    q_scale = jnp.where(is_q, ATTN_SCALE, 1.0)

    is_qk = n < q_blocks + k_blocks

    @pl.when((n > 0) & is_qk)
    def _():
        gn = jnp.where(is_q, qg_ref[...], kg_ref[...]) * q_scale
        store_normed(everything, project(everything), gn)

    @pl.when(jnp.logical_not(is_qk))
    def _():
        store_scaled(everything, project(everything), None)


def in_proj(x, mod, g, w, out_dtype, q_width, seq, base_row, per_seq, norm=None, rope=None, w_rot=0):
    t_tokens = x.shape[0]
    tm, tn = IN_TOKENS, (W_TILE if norm is not None else IN_WIDE_TILE)
    n_total = w.shape[1]
    assert n_total % tn == 0 and n_total // tn > X_PARTS
    in_specs = [
        pl.BlockSpec(memory_space=pl.ANY),
        _mod_spec(0, tm, seq, base_row, per_seq),
        _mod_spec(1, tm, seq, base_row, per_seq),
        pl.BlockSpec((1, D_MODEL), lambda m, n: (0, 0)),
        pl.BlockSpec((D_MODEL, tn), lambda m, n: (0, (n + w_rot) % (n_total // tn))),
    ]
    args = [x, mod, mod, g.reshape(1, D_MODEL), w]
    k_width = 0
    if norm is not None:
        k_width = norm[2]
        in_specs += [pl.BlockSpec((1, HEAD_DIM), lambda m, n: (0, 0))] * 2
        in_specs += [pl.BlockSpec((tn, tn), lambda m, n: (0, 0))]
        head_of = np.arange(tn) // HEAD_DIM
        avg = (head_of[:, None] == head_of[None, :]) / HEAD_DIM
        args += [norm[0].reshape(1, HEAD_DIM), norm[1].reshape(1, HEAD_DIM), jnp.asarray(avg, dtype=bf16)]
    else:
        in_specs += [pl.BlockSpec((1, tn), lambda m, n: (0, n))]
        args += [jnp.asarray(np.where(np.arange(n_total) < q_width, ATTN_SCALE, 1.0)[None, :], dtype=f32)]
    if rope is not None:
        spb = seq // tm
        in_specs += [pl.BlockSpec((tm, HEAD_DIM), lambda m, n: (m % spb, 0))] * 2
        args += [rope[0], rope[1]]
    kern = functools.partial(_in_proj_kernel, tn, q_width // tn, k_width // tn,
                             norm is not None, rope is not None)
    return pl.pallas_call(
        kern,
        grid=(t_tokens // tm, n_total // tn),
        in_specs=in_specs,
        out_specs=pl.BlockSpec((tm, tn), lambda m, n: (m, n)),
        out_shape=jax.ShapeDtypeStruct((t_tokens, n_total), out_dtype),
        scratch_shapes=[pltpu.VMEM((tm, D_MODEL), bf16),
                        pltpu.VMEM((2, tm, D_MODEL), f32),
                        pltpu.SemaphoreType.DMA((2,))],
        compiler_params=_params("arbitrary", "arbitrary"),
        name="in_proj",
    )(*args)


def _with_ones(v):
    return jnp.concatenate([v, jnp.ones_like(v)], axis=1)


def _softmax_pv(parts):
    m = functools.reduce(jnp.maximum, [jnp.max(s, axis=-1, keepdims=True) for s, _ in parts])
    acc = sum(jnp.dot(jnp.exp2(s - m).astype(bf16), v1, preferred_element_type=f32) for s, v1 in parts)
    return acc[:, :HEAD_DIM] / acc[:, HEAD_DIM:]


def _ctx_attn_kernel(n_q, n_kv, q_ref, k_ref, v_ref, o_ref):
    group = n_q // n_kv
    values = [_with_ones(v_ref[:, kv * HEAD_DIM:(kv + 1) * HEAD_DIM].astype(bf16)) for kv in range(n_kv)]
    keys = [k_ref[:, kv * HEAD_DIM:(kv + 1) * HEAD_DIM].astype(bf16) for kv in range(n_kv)]
    dn = (((1,), (1,)), ((), ()))
    scores = [lax.dot_general(q_ref[:, h * HEAD_DIM:(h + 1) * HEAD_DIM].astype(bf16), keys[h // group],
                              dn, preferred_element_type=f32) for h in range(n_q)]
    for h, s in enumerate(scores):
        o = _softmax_pv([(s, values[h // group])])
        o_ref[:, h * HEAD_DIM:(h + 1) * HEAD_DIM] = o.astype(o_ref.dtype)


def ctx_attention(proj, seq, n_q, n_kv):
    t_tokens = proj.shape[0]
    qw, kw = n_q * HEAD_DIM, n_kv * HEAD_DIM
    assert qw % kw == 0
    k_blk = qw // kw
    return pl.pallas_call(
        functools.partial(_ctx_attn_kernel, n_q, n_kv),
        grid=(t_tokens // seq,),
        in_specs=[
            pl.BlockSpec((seq, qw), lambda b: (b, 0)),
            pl.BlockSpec((seq, kw), lambda b: (b, k_blk)),
            pl.BlockSpec((seq, kw), lambda b: (b, k_blk + 1)),
        ],
        out_specs=pl.BlockSpec((seq, qw), lambda b: (b, 0)),
        out_shape=jax.ShapeDtypeStruct((t_tokens, qw), bf16),
        compiler_params=_params("arbitrary"),
        name="ctx_attention",
    )(proj, proj, proj)


def _gqa_attn_kernel(group, q_ref, ck_ref, cv_ref, k_ref, v_ref, o_ref):
    ck = ck_ref[...].astype(bf16)
    cv = _with_ones(cv_ref[...].astype(bf16))
    k = k_ref[...]
    v = _with_ones(v_ref[...])
    dn = (((1,), (1,)), ((), ()))
    scores = []
    for gi in range(group):
        q = q_ref[:, gi * HEAD_DIM:(gi + 1) * HEAD_DIM]
        scores.append((lax.dot_general(q, ck, dn, preferred_element_type=f32),
                       lax.dot_general(q, k, dn, preferred_element_type=f32)))
    for gi, (s_c, s_l) in enumerate(scores):
        o = _softmax_pv([(s_c, cv), (s_l, v)])
        o_ref[:, gi * HEAD_DIM:(gi + 1) * HEAD_DIM] = o.astype(o_ref.dtype)


def gqa_attention(proj, cache_k, cache_v, seq, past):
    t_tokens = proj.shape[0]
    n_b = t_tokens // seq
    group = GQA_Q_HEADS // GQA_KV_HEADS
    tq = GQA_QUERIES
    nq = seq // tq
    return pl.pallas_call(
        functools.partial(_gqa_attn_kernel, group),
        grid=(n_b, GQA_KV_HEADS, nq),
        in_specs=[
            pl.BlockSpec((tq, group * HEAD_DIM), lambda b, h, i: (b * nq + i, h)),
            pl.BlockSpec((past, HEAD_DIM), lambda b, h, i: (b, h)),
            pl.BlockSpec((past, HEAD_DIM), lambda b, h, i: (b, h)),
            pl.BlockSpec((seq, HEAD_DIM), lambda b, h, i: (b, GQA_Q_HEADS + h)),
            pl.BlockSpec((seq, HEAD_DIM), lambda b, h, i: (b, GQA_Q_HEADS + GQA_KV_HEADS + h)),
        ],
        out_specs=pl.BlockSpec((tq, group * HEAD_DIM), lambda b, h, i: (b * nq + i, h)),
        out_shape=jax.ShapeDtypeStruct((t_tokens, GQA_Q_WIDTH), bf16),
        compiler_params=_params("arbitrary", "arbitrary", "arbitrary"),
        name="gqa_attention",
    )(proj, cache_k, cache_v, proj, proj)


def _na_bias_tiles(rel_bias):
    qc = np.arange(GRID_W)[:, None]
    kc = np.arange(GRID_W)[None, :]
    q_start = np.clip(qc - NA_KC // 2, 0, GRID_W - NA_KC)
    valid = (kc >= q_start) & (kc < q_start + NA_KC)
    dc = np.clip(kc - qc, -(NA_KC - 1), NA_KC - 1) + NA_KC - 1
    pick = (np.arange(2 * NA_KC - 1)[:, None, None] == dc[None]) & valid[None]
    tiles = jnp.einsum("hdc,cqk->hdqk", rel_bias.astype(f32) * LOG2_E, jnp.asarray(pick, dtype=f32),
                       precision=lax.Precision.HIGHEST)
    tiles = jnp.where(jnp.asarray(valid)[None, None], tiles, NEG_INF)
    masked = jnp.full((NA_HEADS, 1, GRID_W, GRID_W), NEG_INF, f32)
    tiles = jnp.concatenate([tiles, masked], axis=1)
    zeros = jnp.zeros_like(tiles)
    left = jnp.concatenate([tiles, zeros], axis=-1)
    right = jnp.concatenate([zeros, tiles], axis=-1)
    return left, right


def _na_attn_kernel(n_heads, n_rows, q_ref, k_ref, v_ref, ck_ref, cv_ref, bl_ref, br_ref, o_ref):
    rb = pl.program_id(2)
    start = jnp.clip(rb * NA_QROWS - NA_KR // 2, 0, n_rows - NA_WROWS)
    tok0 = pl.multiple_of(start * GRID_W, GRID_W)
    dn = (((1,), (1,)), ((), ()))

    def tile_index(i, kr):
        qr = rb * NA_QROWS + i
        r0 = jnp.clip(qr - NA_KR // 2, 0, n_rows - NA_KR)
        return jnp.where((kr >= r0) & (kr < r0 + NA_KR), kr - qr + NA_KR - 1, NA_INVALID)

    index = [[(tile_index(i, start + 2 * jp), tile_index(i, start + 2 * jp + 1))
              for jp in range(NA_WROWS // 2)] for i in range(NA_QROWS)]
    scores = []
    for hh in range(n_heads):
        hs = slice(hh * HEAD_DIM, (hh + 1) * HEAD_DIM)
        q = q_ref[:, hs]
        bias = jnp.concatenate(
            [jnp.concatenate([bl_ref[hh, ia] + br_ref[hh, ib] for ia, ib in row], axis=1)
             for row in index], axis=0)
        s_l = lax.dot_general(q, k_ref[pl.ds(tok0, NA_WROWS * GRID_W), hs], dn,
                              preferred_element_type=f32) + bias
        s_c = lax.dot_general(q, ck_ref[:, hs].astype(bf16), dn, preferred_element_type=f32)
        scores.append((s_c, s_l))
    for hh, (s_c, s_l) in enumerate(scores):
        hs = slice(hh * HEAD_DIM, (hh + 1) * HEAD_DIM)
        vw = v_ref[pl.ds(tok0, NA_WROWS * GRID_W), hs]
        o = _softmax_pv([(s_c, _with_ones(cv_ref[:, hs].astype(bf16))), (s_l, _with_ones(vw))])
        o_ref[:, hs] = o.astype(o_ref.dtype)


def na_attention(proj, cache_k, cache_v, rel_bias, seq, past):
    t_tokens = proj.shape[0]
    n_b = t_tokens // seq
    n_rows = seq // GRID_W
    n_rb = n_rows // NA_QROWS
    tq = NA_QROWS * GRID_W
    hb = NA_HEADS_PER_STEP
    n_hg = NA_HEADS // hb
    wide = hb * HEAD_DIM
    bias_l, bias_r = _na_bias_tiles(rel_bias)
    bias_spec = pl.BlockSpec((hb, 2 * NA_KR, GRID_W, 2 * GRID_W), lambda b, h, r: (h, 0, 0, 0))
    return pl.pallas_call(
        functools.partial(_na_attn_kernel, hb, n_rows),
        grid=(n_b, n_hg, n_rb),
        in_specs=[
            pl.BlockSpec((tq, wide), lambda b, h, r: (b * n_rb + r, h)),
            pl.BlockSpec((seq, wide), lambda b, h, r: (b, n_hg + h)),
            pl.BlockSpec((seq, wide), lambda b, h, r: (b, 2 * n_hg + h)),
            pl.BlockSpec((past, wide), lambda b, h, r: (b, h)),
            pl.BlockSpec((past, wide), lambda b, h, r: (b, h)),
            bias_spec,
            bias_spec,
        ],
        out_specs=pl.BlockSpec((tq, wide), lambda b, h, r: (b * n_rb + r, h)),
        out_shape=jax.ShapeDtypeStruct((t_tokens, NA_WIDTH), bf16),
        compiler_params=_params("arbitrary", "arbitrary", "arbitrary"),
        name="na_attention",
    )(proj, proj, proj, cache_k, cache_v, bias_l, bias_r)


def _dft_cols_kernel(x_ref, cs_ref, y_ref):
    for g in range(FNET_GROUPS):
        sl = slice(g * FNET_CH, (g + 1) * FNET_CH)
        y = jnp.dot(x_ref[:, sl].astype(bf16), cs_ref[...], preferred_element_type=f32)
        y_ref[0, :, sl] = y[:, :FNET_CH].astype(bf16)
        y_ref[1, :, sl] = y[:, FNET_CH:].astype(bf16)


def _dft_rows_kernel(d_ref, y_ref, w_ref, o_ref):
    f = jnp.dot(d_ref[...], y_ref[...], preferred_element_type=f32)
    for g in range(FNET_GROUPS):
        sl = slice(g * FNET_CH, (g + 1) * FNET_CH)
        o = jnp.dot(f[:, sl].astype(bf16), w_ref[g], preferred_element_type=f32)
        o_ref[:, sl] = o.astype(o_ref.dtype)


def _dft_tables(seq):
    c = np.arange(FNET_CH, dtype=np.int64)
    ang_c = (2.0 * np.pi / FNET_CH) * ((c[:, None] * c[None, :]) % FNET_CH)
    cs = np.concatenate([np.cos(ang_c), np.sin(ang_c)], axis=1)
    n = np.arange(seq, dtype=np.int64)
    ang_n = (2.0 * np.pi / seq) * ((n[:, None] * n[None, :]) % seq)
    norm = 1.0 / math.sqrt(seq * FNET_CH)
    d = np.concatenate([np.cos(ang_n) * norm, -np.sin(ang_n) * norm], axis=1)
    return jnp.asarray(cs, dtype=bf16), jnp.asarray(d, dtype=bf16)


def fourier_mix(proj, col_blk, fnet_w, seq):
    t_tokens = proj.shape[0]
    n_b = t_tokens // seq
    cs, d = _dft_tables(seq)
    tn = min(seq, 512)
    ns = seq // tn
    y = pl.pallas_call(
        _dft_cols_kernel,
        grid=(n_b, ns),
        in_specs=[
            pl.BlockSpec((tn, FNET_WIDTH), lambda b, i: (b * ns + i, col_blk)),
            pl.BlockSpec((FNET_CH, 2 * FNET_CH), lambda b, i: (0, 0)),
        ],
        out_specs=pl.BlockSpec((2, tn, FNET_WIDTH), lambda b, i: (0, i, b)),
        out_shape=jax.ShapeDtypeStruct((2, seq, n_b * FNET_WIDTH), bf16),
        compiler_params=_params("arbitrary", "arbitrary"),
        name="dft_cols",
    )(proj, cs)
    y2 = y.reshape(2 * seq, n_b * FNET_WIDTH)
    return pl.pallas_call(
        _dft_rows_kernel,
        grid=(n_b, ns),
        in_specs=[
            pl.BlockSpec((tn, 2 * seq), lambda b, i: (i, 0)),
            pl.BlockSpec((2 * seq, FNET_WIDTH), lambda b, i: (0, b)),
            pl.BlockSpec((FNET_GROUPS, FNET_CH, FNET_CH), lambda b, i: (0, 0, 0)),
        ],
        out_specs=pl.BlockSpec((tn, FNET_WIDTH), lambda b, i: (b * ns + i, 0)),
        out_shape=jax.ShapeDtypeStruct((t_tokens, FNET_WIDTH), bf16),
        compiler_params=_params("arbitrary", "arbitrary"),
        name="dft_rows",
    )(d, y2, fnet_w.astype(bf16))


def _pool_kernel(seq, x_ref, w_ref, sc_ref, o_ref):
    t = lax.broadcasted_iota(jnp.int32, (seq, 1), 0)

    def prev(a, k):
        return jnp.where(t >= k, pltpu.roll(a, k, 0), 0.0)

    def nxt(a, k):
        return jnp.where(t < seq - k, pltpu.roll(a, seq - k, 0), 0.0)

    for g, win in enumerate(POOL_WINDOWS):
        sl = slice(g * FNET_CH, (g + 1) * FNET_CH)
        x = x_ref[:, sl].astype(f32)
        half = win // 2
        back, fwd, k = x, x, 1
        while k < half:
            back = back + prev(back, k)
            fwd = fwd + nxt(fwd, k)
            k *= 2
        total = prev(back, 1) + fwd
        cnt = (jnp.minimum(t + half, seq) - jnp.maximum(t - half, 0)).astype(f32)
        pooled = (total / cnt - x).astype(bf16)
        o = jnp.dot(pooled, w_ref[g], preferred_element_type=f32) * sc_ref[:, sl]
        o_ref[:, sl] = o.astype(o_ref.dtype)


def pool_mix(proj, col_blk, pool_w, pool_scale, seq):
    t_tokens = proj.shape[0]
    return pl.pallas_call(
        functools.partial(_pool_kernel, seq),
        grid=(t_tokens // seq,),
        in_specs=[
            pl.BlockSpec((seq, POOL_WIDTH), lambda b: (b, col_blk)),
            pl.BlockSpec((len(POOL_WINDOWS), FNET_CH, FNET_CH), lambda b: (0, 0, 0)),
            pl.BlockSpec((1, POOL_WIDTH), lambda b: (0, 0)),
        ],
        out_specs=pl.BlockSpec((seq, POOL_WIDTH), lambda b: (b, 0)),
        out_shape=jax.ShapeDtypeStruct((t_tokens, POOL_WIDTH), bf16),
        compiler_params=_params("arbitrary"),
        name="pool_mix",
    )(proj, pool_w.astype(bf16), pool_scale.reshape(1, POOL_WIDTH))


def _out_proj_kernel(x_ref, gate_ref, a_ref, b_ref, wa_ref, wb_ref, o_ref):
    y = (jnp.dot(a_ref[...], wa_ref[...], preferred_element_type=f32)
         + jnp.dot(b_ref[...], wb_ref[...], preferred_element_type=f32))
    o_ref[...] = x_ref[...] + gate_ref[...] * y


def out_proj(x, mod, a, b, wa, wb, seq, base_row, per_seq):
    t_tokens = x.shape[0]
    tm = 512
    return pl.pallas_call(
        _out_proj_kernel,
        grid=(t_tokens // tm, 1),
        in_specs=[
            pl.BlockSpec((tm, D_MODEL), lambda m, n: (m, 0)),
            _mod_spec(2, tm, seq, base_row, per_seq),
            pl.BlockSpec((tm, a.shape[1]), lambda m, n: (m, 0)),
            pl.BlockSpec((tm, b.shape[1]), lambda m, n: (m, 0)),
            pl.BlockSpec(wa.shape, lambda m, n: (0, 0)),
            pl.BlockSpec(wb.shape, lambda m, n: (0, 0)),
        ],
        out_specs=pl.BlockSpec((tm, D_MODEL), lambda m, n: (m, 0)),
        out_shape=jax.ShapeDtypeStruct((t_tokens, D_MODEL), f32),
        compiler_params=_params("arbitrary", "arbitrary"),
        name="out_proj",
    )(x, mod, a, b, wa, wb)


def _ffn_kernel(tm, seq, nj, final, *refs):
    it = iter(refs)
    (x_hbm, xp_ref, xn_ref, shift_ref, scale_ref, gate_ref, g_ref,
     wv_ref, wg_ref, cwv_ref, cwg_ref, cbv_ref, cbg_ref, wd_ref) = (next(it) for _ in range(14))
    gf_ref = next(it) if final else None
    o_hbm, h_ref, acc_ref = next(it), next(it), next(it)
    u_refs = [(next(it), next(it)), (next(it), next(it))]
    x_buf, x_sem, o_sem = next(it), next(it), next(it)
    i = pl.program_id(0)
    j = pl.program_id(1)
    n_tiles = pl.num_programs(0)
    rows = tm + 2 * HALO

    x_ref = _prefetch_token_tiles(x_hbm, x_buf, x_sem, i, j, n_tiles, first_step=3)

    def o_copy(tile):
        return pltpu.make_async_copy(acc_ref, o_hbm.at[pl.ds(tile * tm, tm), :], o_sem.at[0])

    @pl.when((j == 1) & (i > 0))
    def _():
        o_copy(i - 1).wait()

    @pl.when(j == 1)
    def _():
        acc_ref[...] = jnp.zeros_like(acc_ref)

    def prologue_and_first_up():
        g, sh, sc = g_ref[...], shift_ref[...], scale_ref[...]
        uv_ref, ug_ref = u_refs[0]

        def emit(rs, h):
            h_ref[rs, :] = h
            uv_ref[rs, :] = jnp.dot(h, wv_ref[...], preferred_element_type=f32)
            ug_ref[rs, :] = jnp.dot(h, wg_ref[...], preferred_element_type=f32)

        for r0 in range(0, tm - IN_ROWS, IN_ROWS):
            rs = slice(r0, r0 + IN_ROWS)
            emit(rs, _rms_modulate(x_ref[rs, :], g, sh, sc).astype(bf16))
        next_ok = jnp.where((((i + 1) * tm) & (seq - 1)) != 0, 1.0, 0.0)
        prev_ok = jnp.where(((i * tm) & (seq - 1)) != 0, 1.0, 0.0)
        h_last = _rms_modulate(x_ref[tm - IN_ROWS:tm, :], g, sh, sc)
        h_next = _rms_modulate(xn_ref[...], g, sh, sc) * next_ok
        h_prev = _rms_modulate(xp_ref[...], g, sh, sc) * prev_ok
        emit(slice(tm - IN_ROWS, rows), jnp.concatenate([h_last, h_next, h_prev], axis=0).astype(bf16))

    def zero_row(a, r):
        r0 = (r // HALO) * HALO
        row8 = lax.broadcasted_iota(jnp.int32, (HALO, 1), 0)
        fixed = jnp.where(row8 == r - r0, 0.0, a[r0:r0 + HALO])
        return jnp.concatenate([a[:r0], fixed, a[r0 + HALO:]], axis=0)

    def window(u_ref, r0):
        lo, hi = r0 - HALO, r0 + FFN_ROWS + HALO
        if lo < 0:
            return jnp.concatenate([u_ref[rows + lo:rows, :], u_ref[0:hi, :]], axis=0)
        return u_ref[lo:hi, :]

    def conv(u_ref, r0, cw_ref, cb_ref):
        u = window(u_ref, r0)
        n = FFN_ROWS + 2 * HALO
        up = pltpu.roll(u, 1, 0)
        un = pltpu.roll(u, n - 1, 0)
        for edge in range(seq, tm, seq):
            if r0 <= edge < r0 + FFN_ROWS:
                up = zero_row(up, edge - r0 + HALO)
            if r0 <= edge - 1 < r0 + FFN_ROWS:
                un = zero_row(un, edge - 1 - r0 + HALO)
        keep = slice(HALO, HALO + FFN_ROWS)
        return (up[keep] * cw_ref[0:1, :] + u[keep] * cw_ref[1:2, :] + un[keep] * cw_ref[2:3, :]
                + cb_ref[...])

    def gated(slot):
        uv_ref, ug_ref = u_refs[slot]
        chunks = []
        for r0 in range(0, tm, FFN_ROWS):
            val = conv(uv_ref, r0, cwv_ref, cbv_ref)
            gate = conv(ug_ref, r0, cwg_ref, cbg_ref)
            chunks.append((gate * jax.nn.sigmoid(gate) * val).astype(bf16))
        return chunks

    def zero_after(chunks):
        parts = []
        for c in chunks:
            bits = pltpu.bitcast(c, jnp.uint32)
            parts += [bits[k:k + HALO] for k in range(0, bits.shape[0], HALO)]
        while len(parts) > 1:
            parts = [a | b for a, b in zip(parts[0::2], parts[1::2])] + parts[len(parts) & ~1:]
        return (parts[0] >> 16) >> 16

    def after(a, zero, row0=0):
        tf = zero.shape[1]
        mid = pltpu.bitcast(pltpu.bitcast(a[row0:row0 + 2 * HALO, 0:tf], jnp.uint32) | zero, bf16)
        if tf < a.shape[1]:
            mid = jnp.concatenate([mid, a[row0:row0 + 2 * HALO, tf:]], axis=1)
        parts = ([a[:row0]] if row0 else []) + [mid, a[row0 + 2 * HALO:]]
        return jnp.concatenate(parts, axis=0)

    def quarter(chunks, k):
        n = len(chunks) // 4
        return chunks[k * n:(k + 1) * n]

    def up_project(slot, chunks):
        h = h_ref[...]
        wv, wg = wv_ref[...], wg_ref[...]
        uv_ref, ug_ref = u_refs[slot]
        k_mid = wv.shape[0] // 2
        wv = after(wv, zero_after(quarter(chunks, 0)), k_mid)
        wg = after(wg, zero_after(quarter(chunks, 2)), k_mid)
        uv_ref[...] = jnp.dot(h, wv, preferred_element_type=f32)
        h = after(h, zero_after(quarter(chunks, 1)))
        ug_ref[...] = jnp.dot(h, wg, preferred_element_type=f32)

    def down_project(chunks, anchored):
        if anchored:
            chunks = [after(chunks[0], zero_after(quarter(chunks, 3)))] + chunks[1:]
        act = jnp.concatenate(chunks, axis=0)
        acc_ref[...] += jnp.dot(act, wd_ref[...], preferred_element_type=f32)

    @pl.when(j == 0)
    def _():
        prologue_and_first_up()

    for p in range(2):
        @pl.when((j >= 1) & (j < nj) & (j % 2 == p))
        def _(p=p):
            chunks = gated(1 - p)
            up_project(p, chunks)
            down_project(chunks, True)

    @pl.when(j == nj)
    def _():
        down_project(gated((nj - 1) % 2), False)
        y = x_ref[...] + gate_ref[...] * acc_ref[...]
        if final:
            ms = jnp.mean(y * y, axis=-1, keepdims=True)
            y = y * lax.rsqrt(ms + EPS) * gf_ref[...]
        acc_ref[...] = y
        o_copy(i).start()

    @pl.when((j == nj) & (i == n_tiles - 1))
    def _():
        o_copy(i).wait()


def conv_ffn(x, mod, g, w_up, conv_w, conv_b, w_down, seq, base_row, per_seq, final_g=None):
    t_tokens = x.shape[0]
    tm, tf = FFN_TOKENS, W_TILE
    assert seq & (seq - 1) == 0 and (tm // FFN_ROWS) % 4 == 0
    nj = D_FF // tf
    hb = tm // HALO
    n_hblk = t_tokens // HALO
    conv_b2 = conv_b.reshape(1, 2 * D_FF)

    def blk(j):
        return jnp.clip(j, 0, nj - 1)

    assert nj >= 3 + X_PARTS
    in_specs = [
        pl.BlockSpec(memory_space=pl.ANY),
        pl.BlockSpec((HALO, D_MODEL), lambda i, j: (jnp.maximum(i * hb - 1, 0), 0)),
        pl.BlockSpec((HALO, D_MODEL), lambda i, j: (jnp.minimum((i + 1) * hb, n_hblk - 1), 0)),
        _mod_spec(3, tm, seq, base_row, per_seq),
        _mod_spec(4, tm, seq, base_row, per_seq),
        _mod_spec(5, tm, seq, base_row, per_seq),
        pl.BlockSpec((1, D_MODEL), lambda i, j: (0, 0)),
        pl.BlockSpec((D_MODEL, tf), lambda i, j: (0, blk(j))),
        pl.BlockSpec((D_MODEL, tf), lambda i, j: (0, nj + blk(j))),
        pl.BlockSpec((3, tf), lambda i, j: (0, blk(j - 1))),
        pl.BlockSpec((3, tf), lambda i, j: (0, nj + blk(j - 1))),
        pl.BlockSpec((1, tf), lambda i, j: (0, blk(j - 1))),
        pl.BlockSpec((1, tf), lambda i, j: (0, nj + blk(j - 1))),
        pl.BlockSpec((tf, D_MODEL), lambda i, j: (blk(j - 1), 0)),
    ]
    args = [x, x, x, mod, mod, mod, g.reshape(1, D_MODEL), w_up, w_up,
            conv_w, conv_w, conv_b2, conv_b2, w_down]
    if final_g is not None:
        in_specs.append(pl.BlockSpec((1, D_MODEL), lambda i, j: (0, 0)))
        args.append(final_g.reshape(1, D_MODEL))
    return pl.pallas_call(
        functools.partial(_ffn_kernel, tm, seq, nj, final_g is not None),
        grid=(t_tokens // tm, nj + 1),
        in_specs=in_specs,
        out_specs=pl.BlockSpec(memory_space=pl.ANY),
        out_shape=jax.ShapeDtypeStruct((t_tokens, D_MODEL), f32),
        scratch_shapes=[pltpu.VMEM((tm + 2 * HALO, D_MODEL), bf16),
                        pltpu.VMEM((tm, D_MODEL), f32)]
                       + [pltpu.VMEM((tm + 2 * HALO, tf), f32)] * 4
                       + [pltpu.VMEM((2, tm, D_MODEL), f32),
                          pltpu.SemaphoreType.DMA((2,)),
                          pltpu.SemaphoreType.DMA((1,))],
        compiler_params=_params("arbitrary", "arbitrary"),
        name="conv_ffn",
    )(*args)


def _rope_tables(seq):
    t = np.arange(seq)
    half = HEAD_DIM // 2
    inv_freq = np.power(ROPE_BASE, -np.arange(0, half, 2, dtype=np.float64) / half)
    ang_r = (t // GRID_W)[:, None] * inv_freq[None, :]
    ang_c = (t % GRID_W)[:, None] * inv_freq[None, :]
    cos = np.concatenate([np.cos(ang_r)] * 2 + [np.cos(ang_c)] * 2, axis=-1)
    sin = np.concatenate([-np.sin(ang_r), np.sin(ang_r), -np.sin(ang_c), np.sin(ang_c)], axis=-1)
    return jnp.asarray(cos, dtype=f32), jnp.asarray(sin, dtype=f32)


def kernel(x_prompt, x_sample, c, cache_na_k, cache_na_v, cache_gqa_k, cache_gqa_v, c_ctx, norm1_g, norm2_g, ada_w, ada_b, ev_w_in, ev_na_bias, ev_fnet_w, ev_w_out, od_w_in, od_pool_w, od_pool_scale, od_q_norm_g, od_k_norm_g, od_w_out, ffn_w_up, ffn_conv_w, ffn_conv_b, ffn_w_down, final_norm_g):
    n_ctx, seq_p, _ = x_prompt.shape
    n_lat, seq_s, _ = x_sample.shape
    past = cache_na_k.shape[2]
    assert n_lat + 1 <= MOD_ROWS
    xp = x_prompt.reshape(n_ctx * seq_p, D_MODEL)
    xs = x_sample.reshape(n_lat * seq_s, D_MODEL)

    cond = jnp.zeros((MOD_ROWS, D_MODEL), f32).at[0].set(c_ctx).at[1:1 + n_lat].set(c)
    mods = ada_modulation(cond, ada_w, ada_b).reshape(DEPTH, MOD_ROWS * 6, 1, D_MODEL)
    rope = _rope_tables(seq_s)

    ctx = dict(seq=seq_p, base_row=0, per_seq=0)
    lat = dict(seq=seq_s, base_row=1, per_seq=1)
    new_k, new_v = {}, {}
    for i in range(DEPTH):
        j = i // 2
        mod = mods[i]
        w_up = ffn_w_up[i].astype(bf16)
        w_down = ffn_w_down[i].astype(bf16)
        if i % 2 == 0:
            w_in = ev_w_in[j].astype(bf16)
            w_attn = ev_w_out[j, :NA_WIDTH].astype(bf16)
            w_mix = ev_w_out[j, NA_WIDTH:].astype(bf16)
            mix_blk = 3 * NA_WIDTH // FNET_WIDTH
            proj_p = in_proj(xp, mod, norm1_g[i], w_in, f32, NA_WIDTH, **ctx)
            new_k[i] = proj_p[:, NA_WIDTH:2 * NA_WIDTH]
            new_v[i] = proj_p[:, 2 * NA_WIDTH:3 * NA_WIDTH]
            attn_p = ctx_attention(proj_p, seq_p, NA_HEADS, NA_HEADS)
            mix_p = fourier_mix(proj_p, mix_blk, ev_fnet_w[j], seq_p)
            proj_s = in_proj(xs, mod, norm1_g[i], w_in, bf16, NA_WIDTH, **lat)
            attn_s = na_attention(proj_s,
                                  cache_na_k[:, j].reshape(n_lat * past, NA_WIDTH),
                                  cache_na_v[:, j].reshape(n_lat * past, NA_WIDTH),
                                  ev_na_bias[j], seq_s, past)
            mix_s = fourier_mix(proj_s, mix_blk, ev_fnet_w[j], seq_s)
        else:
            w_in = od_w_in[j].astype(bf16)
            rot = dict(w_rot=POOL_WIDTH // W_TILE)
            w_attn = od_w_out[j, POOL_WIDTH:].astype(bf16)
            w_mix = od_w_out[j, :POOL_WIDTH].astype(bf16)
            norm = (od_q_norm_g[j], od_k_norm_g[j], GQA_KV_WIDTH)
            k_lo = GQA_Q_WIDTH
            v_lo = GQA_Q_WIDTH + GQA_KV_WIDTH
            mix_blk = (GQA_Q_WIDTH + 2 * GQA_KV_WIDTH) // POOL_WIDTH
            proj_p = in_proj(xp, mod, norm1_g[i], w_in, f32, GQA_Q_WIDTH, norm=norm, **rot, **ctx)
            new_k[i] = proj_p[:, k_lo:k_lo + GQA_KV_WIDTH]
            new_v[i] = proj_p[:, v_lo:v_lo + GQA_KV_WIDTH]
            attn_p = ctx_attention(proj_p, seq_p, GQA_Q_HEADS, GQA_KV_HEADS)
            mix_p = pool_mix(proj_p, mix_blk, od_pool_w[j], od_pool_scale[j], seq_p)
            proj_s = in_proj(xs, mod, norm1_g[i], w_in, bf16, GQA_Q_WIDTH, norm=norm, rope=rope, **rot, **lat)
            attn_s = gqa_attention(proj_s,
                                   cache_gqa_k[:, j].reshape(n_lat * past, GQA_KV_WIDTH),
                                   cache_gqa_v[:, j].reshape(n_lat * past, GQA_KV_WIDTH),
                                   seq_s, past)
            mix_s = pool_mix(proj_s, mix_blk, od_pool_w[j], od_pool_scale[j], seq_s)
        xp = out_proj(xp, mod, attn_p, mix_p, w_attn, w_mix, **ctx)
        xs = out_proj(xs, mod, attn_s, mix_s, w_attn, w_mix, **lat)
        final_g = final_norm_g if i == DEPTH - 1 else None
        xp = conv_ffn(xp, mod, norm2_g[i], w_up, ffn_conv_w[i], ffn_conv_b[i], w_down,
                      final_g=final_g, **ctx)
        xs = conv_ffn(xs, mod, norm2_g[i], w_up, ffn_conv_w[i], ffn_conv_b[i], w_down,
                      final_g=final_g, **lat)

    y_prompt = xp.reshape(n_ctx, seq_p, D_MODEL)
    y_sample = xs.reshape(n_lat, seq_s, D_MODEL)
    even = [i for i in range(DEPTH) if i % 2 == 0]
    odd = [i for i in range(DEPTH) if i % 2 == 1]

    def stack(parts, layers, heads):
        return jnp.stack([parts[i].reshape(n_ctx, seq_p, heads, HEAD_DIM) for i in layers], axis=1)

    return (y_prompt, y_sample,
            stack(new_k, even, NA_HEADS), stack(new_v, even, NA_HEADS),
            stack(new_k, odd, GQA_KV_HEADS), stack(new_v, odd, GQA_KV_HEADS))
```

```python
import functools
import math

import jax
import jax.numpy as jnp
import numpy as np
from jax import lax
from jax.experimental import pallas as pl
from jax.experimental.pallas import tpu as pltpu

f32 = jnp.float32
bf16 = jnp.bfloat16

D_MODEL = 2048
DEPTH = 2
GRID_W = 64
HEAD_DIM = 128
NA_HEADS = 12
NA_KR = 8
NA_KC = 16
FNET_GROUPS = 4
FNET_CH = 128
POOL_WINDOWS = (2, 4, 8, 16)
GQA_Q_HEADS = 12
GQA_KV_HEADS = 4
D_FF = 5632
ROPE_BASE = 10000.0
EPS = 1e-6
NEG_INF = -1e30
NA_WIDTH = NA_HEADS * HEAD_DIM
FNET_WIDTH = FNET_GROUPS * FNET_CH
POOL_WIDTH = len(POOL_WINDOWS) * FNET_CH
GQA_Q_WIDTH = GQA_Q_HEADS * HEAD_DIM
GQA_KV_WIDTH = GQA_KV_HEADS * HEAD_DIM
LOG2_E = math.log2(math.e)
ATTN_SCALE = HEAD_DIM ** -0.5 * LOG2_E

MOD_ROWS = 16
VMEM_LIMIT = 60 * 1024 * 1024
HALO = 8
NA_QROWS = 4
NA_WROWS = 12
NA_INVALID = 2 * NA_KR - 1
GQA_QUERIES = 1024
NA_HEADS_PER_STEP = 6
FFN_TOKENS = 1024
FFN_ROWS = 128
W_TILE = 512
IN_TOKENS = 1024
IN_WIDE_TILE = 1024
IN_ROWS = 256


def _params(*sem, flags=None):
    return pltpu.CompilerParams(dimension_semantics=sem, vmem_limit_bytes=VMEM_LIMIT, flags=flags)


def _rms_modulate(x, g, shift, scale):
    ms = jnp.mean(x * x, axis=-1, keepdims=True)
    return (x * lax.rsqrt(ms + EPS)) * (g * (1.0 + scale)) + shift


def _mod_spec(k, tm, seq, base_row, per_seq):
    def index(m, n):
        return ((base_row + per_seq * ((m * tm) // seq)) * 6 + k, 0, 0)
    return pl.BlockSpec((None, 1, D_MODEL), index)


def _ada_kernel(c_ref, w_ref, b_ref, o_ref):
    c = c_ref[...]
    s = (c * jax.nn.sigmoid(c)).astype(bf16)
    o_ref[...] = jnp.dot(s, w_ref[...].astype(bf16), preferred_element_type=f32) + b_ref[...]


def ada_modulation(cond, ada_w, ada_b):
    tn = 1024
    n_out = 6 * D_MODEL
    return pl.pallas_call(
        _ada_kernel,
        grid=(DEPTH, n_out // tn),
        in_specs=[
            pl.BlockSpec((MOD_ROWS, D_MODEL), lambda i, n: (0, 0)),
            pl.BlockSpec((None, D_MODEL, tn), lambda i, n: (i, 0, n)),
            pl.BlockSpec((None, 1, tn), lambda i, n: (i, 0, n)),
        ],
        out_specs=pl.BlockSpec((None, MOD_ROWS, tn), lambda i, n: (i, 0, n)),
        out_shape=jax.ShapeDtypeStruct((DEPTH, MOD_ROWS, n_out), f32),
        compiler_params=_params("arbitrary", "arbitrary"),
        name="ada_modulation",
    )(cond, ada_w, ada_b.reshape(DEPTH, 1, n_out))


def _rope(a, cos, sin):
    lane = lax.broadcasted_iota(jnp.int32, a.shape, 1)
    quarter = HEAD_DIM // 4
    partner = jnp.where((lane & (2 * quarter - 1)) < quarter,
                        pltpu.roll(a, HEAD_DIM - quarter, 1),
                        pltpu.roll(a, quarter, 1))
    return a * cos + partner * sin


def _in_proj_kernel(tn, q_blocks, k_blocks, has_norm, has_rope, *refs):
    it = iter(refs)
    x_hbm, shift_ref, scale_ref, g_ref, w_ref = (next(it) for _ in range(5))
    qg_ref = kg_ref = cos_ref = sin_ref = qs_ref = None
    if has_norm:
        qg_ref, kg_ref, avg_ref = next(it), next(it), next(it)
    else:
        qs_ref = next(it)
    if has_rope:
        cos_ref, sin_ref = next(it), next(it)
    o_ref, h_ref, x_buf, x_sem = next(it), next(it), next(it), next(it)
    m = pl.program_id(0)
    n = pl.program_id(1)
    tm = h_ref.shape[0]
    everything = slice(0, tm)

    def x_copy(tile, slot):
        return pltpu.make_async_copy(x_hbm.at[pl.ds(tile * tm, tm), :], x_buf.at[slot], x_sem.at[slot])

    slot = m % 2
    x_ref = x_buf.at[slot]

    @pl.when((n == 0) & (m == 0))
    def _():
        x_copy(0, 0).start()

    @pl.when(n == 0)
    def _():
        x_copy(m, slot).wait()

    @pl.when((n == 1) & (m + 1 < pl.num_programs(0)))
    def _():
        x_copy(m + 1, 1 - slot).start()

    def project(rs, h=None):
        h = h_ref[rs, :] if h is None else h
        return jnp.dot(h, w_ref[...], preferred_element_type=f32)

    def store_scaled(rs, acc, scale):
        o_ref[rs, :] = (acc if scale is None else acc * scale).astype(o_ref.dtype)

    def store_normed(rs, acc, gn):
        ms = jnp.dot((acc * acc).astype(bf16), avg_ref[...], preferred_element_type=f32)
        inv = lax.rsqrt(ms + EPS)
        for hh in range(tn // HEAD_DIM):
            sl = slice(hh * HEAD_DIM, (hh + 1) * HEAD_DIM)
            a = acc[:, sl] * inv[:, sl] * gn
            if has_rope:
                a = _rope(a, cos_ref[rs, :], sin_ref[rs, :])
            o_ref[rs, sl] = a.astype(o_ref.dtype)

    @pl.when(n == 0)
    def _():
        g, sh, sc = g_ref[...], shift_ref[...], scale_ref[...]
        for r0 in range(0, tm, IN_ROWS):
            rs = slice(r0, r0 + IN_ROWS)
            h = _rms_modulate(x_ref[rs, :], g, sh, sc).astype(bf16)
            h_ref[rs, :] = h
            if has_norm:
                store_normed(rs, project(rs, h), qg_ref[...] * ATTN_SCALE)
            else:
                store_scaled(rs, project(rs, h), qs_ref[...])

    if not has_norm:
        @pl.when(n > 0)
        def _():
            store_scaled(everything, project(everything), qs_ref[...])
        return

    is_q = n < q_blocks
    q_scale = jnp.where(is_q, ATTN_SCALE, 1.0)

    is_qk = n < q_blocks + k_blocks

    @pl.when((n > 0) & is_qk)
    def _():
        gn = jnp.where(is_q, qg_ref[...], kg_ref[...]) * q_scale
        store_normed(everything, project(everything), gn)

    @pl.when(jnp.logical_not(is_qk))
    def _():
        store_scaled(everything, project(everything), None)


def in_proj(x, mod, g, w, out_dtype, q_width, seq, base_row, per_seq, norm=None, rope=None, w_rot=0):
    t_tokens = x.shape[0]
    tm, tn = IN_TOKENS, (W_TILE if norm is not None else IN_WIDE_TILE)
    n_total = w.shape[1]
    assert n_total % tn == 0 and n_total // tn >= 2
    in_specs = [
        pl.BlockSpec(memory_space=pl.ANY),
        _mod_spec(0, tm, seq, base_row, per_seq),
        _mod_spec(1, tm, seq, base_row, per_seq),
        pl.BlockSpec((1, D_MODEL), lambda m, n: (0, 0)),
        pl.BlockSpec((D_MODEL, tn), lambda m, n: (0, (n + w_rot) % (n_total // tn))),
    ]
    args = [x, mod, mod, g.reshape(1, D_MODEL), w]
    k_width = 0
    if norm is not None:
        k_width = norm[2]
        in_specs += [pl.BlockSpec((1, HEAD_DIM), lambda m, n: (0, 0))] * 2
        in_specs += [pl.BlockSpec((tn, tn), lambda m, n: (0, 0))]
        head_of = np.arange(tn) // HEAD_DIM
        avg = (head_of[:, None] == head_of[None, :]) / HEAD_DIM
        args += [norm[0].reshape(1, HEAD_DIM), norm[1].reshape(1, HEAD_DIM), jnp.asarray(avg, dtype=bf16)]
    else:
        in_specs += [pl.BlockSpec((1, tn), lambda m, n: (0, n))]
        args += [jnp.asarray(np.where(np.arange(n_total) < q_width, ATTN_SCALE, 1.0)[None, :], dtype=f32)]
    if rope is not None:
        spb = seq // tm
        in_specs += [pl.BlockSpec((tm, HEAD_DIM), lambda m, n: (m % spb, 0))] * 2
        args += [rope[0], rope[1]]
    kern = functools.partial(_in_proj_kernel, tn, q_width // tn, k_width // tn,
                             norm is not None, rope is not None)
    return pl.pallas_call(
        kern,
        grid=(t_tokens // tm, n_total // tn),
        in_specs=in_specs,
        out_specs=pl.BlockSpec((tm, tn), lambda m, n: (m, n)),
        out_shape=jax.ShapeDtypeStruct((t_tokens, n_total), out_dtype),
        scratch_shapes=[pltpu.VMEM((tm, D_MODEL), bf16),
                        pltpu.VMEM((2, tm, D_MODEL), f32),
                        pltpu.SemaphoreType.DMA((2,))],
        compiler_params=_params("arbitrary", "arbitrary"),
        name="in_proj",
    )(*args)


def _with_ones(v):
    return jnp.concatenate([v, jnp.ones_like(v)], axis=1)


def _softmax_pv(parts):
    m = functools.reduce(jnp.maximum, [jnp.max(s, axis=-1, keepdims=True) for s, _ in parts])
    acc = sum(jnp.dot(jnp.exp2(s - m).astype(bf16), v1, preferred_element_type=f32) for s, v1 in parts)
    return acc[:, :HEAD_DIM] / acc[:, HEAD_DIM:]


def _ctx_attn_kernel(n_q, n_kv, q_ref, k_ref, v_ref, o_ref):
    group = n_q // n_kv
    values = [_with_ones(v_ref[:, kv * HEAD_DIM:(kv + 1) * HEAD_DIM].astype(bf16)) for kv in range(n_kv)]
    keys = [k_ref[:, kv * HEAD_DIM:(kv + 1) * HEAD_DIM].astype(bf16) for kv in range(n_kv)]
    dn = (((1,), (1,)), ((), ()))
    scores = [lax.dot_general(q_ref[:, h * HEAD_DIM:(h + 1) * HEAD_DIM].astype(bf16), keys[h // group],
                              dn, preferred_element_type=f32) for h in range(n_q)]
    for h, s in enumerate(scores):
        o = _softmax_pv([(s, values[h // group])])
        o_ref[:, h * HEAD_DIM:(h + 1) * HEAD_DIM] = o.astype(o_ref.dtype)


def ctx_attention(proj, seq, n_q, n_kv):
    t_tokens = proj.shape[0]
    qw, kw = n_q * HEAD_DIM, n_kv * HEAD_DIM
    assert qw % kw == 0
    k_blk = qw // kw
    return pl.pallas_call(
        functools.partial(_ctx_attn_kernel, n_q, n_kv),
        grid=(t_tokens // seq,),
        in_specs=[
            pl.BlockSpec((seq, qw), lambda b: (b, 0)),
            pl.BlockSpec((seq, kw), lambda b: (b, k_blk)),
            pl.BlockSpec((seq, kw), lambda b: (b, k_blk + 1)),
        ],
        out_specs=pl.BlockSpec((seq, qw), lambda b: (b, 0)),
        out_shape=jax.ShapeDtypeStruct((t_tokens, qw), bf16),
        compiler_params=_params("arbitrary"),
        name="ctx_attention",
    )(proj, proj, proj)


def _gqa_attn_kernel(group, q_ref, ck_ref, cv_ref, k_ref, v_ref, o_ref):
    ck = ck_ref[...].astype(bf16)
    cv = _with_ones(cv_ref[...].astype(bf16))
    k = k_ref[...]
    v = _with_ones(v_ref[...])
    dn = (((1,), (1,)), ((), ()))
    scores = []
    for gi in range(group):
        q = q_ref[:, gi * HEAD_DIM:(gi + 1) * HEAD_DIM]
        scores.append((lax.dot_general(q, ck, dn, preferred_element_type=f32),
                       lax.dot_general(q, k, dn, preferred_element_type=f32)))
    for gi, (s_c, s_l) in enumerate(scores):
        o = _softmax_pv([(s_c, cv), (s_l, v)])
        o_ref[:, gi * HEAD_DIM:(gi + 1) * HEAD_DIM] = o.astype(o_ref.dtype)


def gqa_attention(proj, cache_k, cache_v, seq, past):
    t_tokens = proj.shape[0]
    n_b = t_tokens // seq
    group = GQA_Q_HEADS // GQA_KV_HEADS
    tq = GQA_QUERIES
    nq = seq // tq
    return pl.pallas_call(
        functools.partial(_gqa_attn_kernel, group),
        grid=(n_b, GQA_KV_HEADS, nq),
        in_specs=[
            pl.BlockSpec((tq, group * HEAD_DIM), lambda b, h, i: (b * nq + i, h)),
            pl.BlockSpec((past, HEAD_DIM), lambda b, h, i: (b, h)),
            pl.BlockSpec((past, HEAD_DIM), lambda b, h, i: (b, h)),
            pl.BlockSpec((seq, HEAD_DIM), lambda b, h, i: (b, GQA_Q_HEADS + h)),
            pl.BlockSpec((seq, HEAD_DIM), lambda b, h, i: (b, GQA_Q_HEADS + GQA_KV_HEADS + h)),
        ],
        out_specs=pl.BlockSpec((tq, group * HEAD_DIM), lambda b, h, i: (b * nq + i, h)),
        out_shape=jax.ShapeDtypeStruct((t_tokens, GQA_Q_WIDTH), bf16),
        compiler_params=_params("arbitrary", "arbitrary", "arbitrary"),
        name="gqa_attention",
    )(proj, cache_k, cache_v, proj, proj)


def _na_bias_tiles(rel_bias):
    qc = np.arange(GRID_W)[:, None]
    kc = np.arange(GRID_W)[None, :]
    q_start = np.clip(qc - NA_KC // 2, 0, GRID_W - NA_KC)
    valid = (kc >= q_start) & (kc < q_start + NA_KC)
    dc = np.clip(kc - qc, -(NA_KC - 1), NA_KC - 1) + NA_KC - 1
    pick = (np.arange(2 * NA_KC - 1)[:, None, None] == dc[None]) & valid[None]
    tiles = jnp.einsum("hdc,cqk->hdqk", rel_bias.astype(f32) * LOG2_E, jnp.asarray(pick, dtype=f32),
                       precision=lax.Precision.HIGHEST)
    tiles = jnp.where(jnp.asarray(valid)[None, None], tiles, NEG_INF)
    masked = jnp.full((NA_HEADS, 1, GRID_W, GRID_W), NEG_INF, f32)
    tiles = jnp.concatenate([tiles, masked], axis=1)
    zeros = jnp.zeros_like(tiles)
    left = jnp.concatenate([tiles, zeros], axis=-1)
    right = jnp.concatenate([zeros, tiles], axis=-1)
    return left, right


def _na_attn_kernel(n_heads, n_rows, q_ref, k_ref, v_ref, ck_ref, cv_ref, bl_ref, br_ref, o_ref):
    rb = pl.program_id(2)
    start = jnp.clip(rb * NA_QROWS - NA_KR // 2, 0, n_rows - NA_WROWS)
    tok0 = pl.multiple_of(start * GRID_W, GRID_W)
    dn = (((1,), (1,)), ((), ()))

    def tile_index(i, kr):
        qr = rb * NA_QROWS + i
        r0 = jnp.clip(qr - NA_KR // 2, 0, n_rows - NA_KR)
        return jnp.where((kr >= r0) & (kr < r0 + NA_KR), kr - qr + NA_KR - 1, NA_INVALID)

    index = [[(tile_index(i, start + 2 * jp), tile_index(i, start + 2 * jp + 1))
              for jp in range(NA_WROWS // 2)] for i in range(NA_QROWS)]
    scores = []
    for hh in range(n_heads):
        hs = slice(hh * HEAD_DIM, (hh + 1) * HEAD_DIM)
        q = q_ref[:, hs]
        bias = jnp.concatenate(
            [jnp.concatenate([bl_ref[hh, ia] + br_ref[hh, ib] for ia, ib in row], axis=1)
             for row in index], axis=0)
        s_l = lax.dot_general(q, k_ref[pl.ds(tok0, NA_WROWS * GRID_W), hs], dn,
                              preferred_element_type=f32) + bias
        s_c = lax.dot_general(q, ck_ref[:, hs].astype(bf16), dn, preferred_element_type=f32)
        scores.append((s_c, s_l))
    for hh, (s_c, s_l) in enumerate(scores):
        hs = slice(hh * HEAD_DIM, (hh + 1) * HEAD_DIM)
        vw = v_ref[pl.ds(tok0, NA_WROWS * GRID_W), hs]
        o = _softmax_pv([(s_c, _with_ones(cv_ref[:, hs].astype(bf16))), (s_l, _with_ones(vw))])
        o_ref[:, hs] = o.astype(o_ref.dtype)


def na_attention(proj, cache_k, cache_v, rel_bias, seq, past):
    t_tokens = proj.shape[0]
    n_b = t_tokens // seq
    n_rows = seq // GRID_W
    n_rb = n_rows // NA_QROWS
    tq = NA_QROWS * GRID_W
    hb = NA_HEADS_PER_STEP
    n_hg = NA_HEADS // hb
    wide = hb * HEAD_DIM
    bias_l, bias_r = _na_bias_tiles(rel_bias)
    bias_spec = pl.BlockSpec((hb, 2 * NA_KR, GRID_W, 2 * GRID_W), lambda b, h, r: (h, 0, 0, 0))
    return pl.pallas_call(
        functools.partial(_na_attn_kernel, hb, n_rows),
        grid=(n_b, n_hg, n_rb),
        in_specs=[
            pl.BlockSpec((tq, wide), lambda b, h, r: (b * n_rb + r, h)),
            pl.BlockSpec((seq, wide), lambda b, h, r: (b, n_hg + h)),
            pl.BlockSpec((seq, wide), lambda b, h, r: (b, 2 * n_hg + h)),
            pl.BlockSpec((past, wide), lambda b, h, r: (b, h)),
            pl.BlockSpec((past, wide), lambda b, h, r: (b, h)),
            bias_spec,
            bias_spec,
        ],
        out_specs=pl.BlockSpec((tq, wide), lambda b, h, r: (b * n_rb + r, h)),
        out_shape=jax.ShapeDtypeStruct((t_tokens, NA_WIDTH), bf16),
        compiler_params=_params("arbitrary", "arbitrary", "arbitrary"),
        name="na_attention",
    )(proj, proj, proj, cache_k, cache_v, bias_l, bias_r)


def _dft_cols_kernel(x_ref, cs_ref, y_ref):
    for g in range(FNET_GROUPS):
        sl = slice(g * FNET_CH, (g + 1) * FNET_CH)
        y = jnp.dot(x_ref[:, sl].astype(bf16), cs_ref[...], preferred_element_type=f32)
        y_ref[0, :, sl] = y[:, :FNET_CH].astype(bf16)
        y_ref[1, :, sl] = y[:, FNET_CH:].astype(bf16)


def _dft_rows_kernel(d_ref, y_ref, w_ref, o_ref):
    f = jnp.dot(d_ref[...], y_ref[...], preferred_element_type=f32)
    for g in range(FNET_GROUPS):
        sl = slice(g * FNET_CH, (g + 1) * FNET_CH)
        o = jnp.dot(f[:, sl].astype(bf16), w_ref[g], preferred_element_type=f32)
        o_ref[:, sl] = o.astype(o_ref.dtype)


def _dft_tables(seq):
    c = np.arange(FNET_CH, dtype=np.int64)
    ang_c = (2.0 * np.pi / FNET_CH) * ((c[:, None] * c[None, :]) % FNET_CH)
    cs = np.concatenate([np.cos(ang_c), np.sin(ang_c)], axis=1)
    n = np.arange(seq, dtype=np.int64)
    ang_n = (2.0 * np.pi / seq) * ((n[:, None] * n[None, :]) % seq)
    norm = 1.0 / math.sqrt(seq * FNET_CH)
    d = np.concatenate([np.cos(ang_n) * norm, -np.sin(ang_n) * norm], axis=1)
    return jnp.asarray(cs, dtype=bf16), jnp.asarray(d, dtype=bf16)


def fourier_mix(proj, col_blk, fnet_w, seq):
    t_tokens = proj.shape[0]
    n_b = t_tokens // seq
    cs, d = _dft_tables(seq)
    tn = min(seq, 512)
    ns = seq // tn
    y = pl.pallas_call(
        _dft_cols_kernel,
        grid=(n_b, ns),
        in_specs=[
            pl.BlockSpec((tn, FNET_WIDTH), lambda b, i: (b * ns + i, col_blk)),
            pl.BlockSpec((FNET_CH, 2 * FNET_CH), lambda b, i: (0, 0)),
        ],
        out_specs=pl.BlockSpec((2, tn, FNET_WIDTH), lambda b, i: (0, i, b)),
        out_shape=jax.ShapeDtypeStruct((2, seq, n_b * FNET_WIDTH), bf16),
        compiler_params=_params("arbitrary", "arbitrary"),
        name="dft_cols",
    )(proj, cs)
    y2 = y.reshape(2 * seq, n_b * FNET_WIDTH)
    return pl.pallas_call(
        _dft_rows_kernel,
        grid=(n_b, ns),
        in_specs=[
            pl.BlockSpec((tn, 2 * seq), lambda b, i: (i, 0)),
            pl.BlockSpec((2 * seq, FNET_WIDTH), lambda b, i: (0, b)),
            pl.BlockSpec((FNET_GROUPS, FNET_CH, FNET_CH), lambda b, i: (0, 0, 0)),
        ],
        out_specs=pl.BlockSpec((tn, FNET_WIDTH), lambda b, i: (b * ns + i, 0)),
        out_shape=jax.ShapeDtypeStruct((t_tokens, FNET_WIDTH), bf16),
        compiler_params=_params("arbitrary", "arbitrary"),
        name="dft_rows",
    )(d, y2, fnet_w.astype(bf16))


def _pool_kernel(seq, x_ref, w_ref, sc_ref, o_ref):
    t = lax.broadcasted_iota(jnp.int32, (seq, 1), 0)

    def prev(a, k):
        return jnp.where(t >= k, pltpu.roll(a, k, 0), 0.0)

    def nxt(a, k):
        return jnp.where(t < seq - k, pltpu.roll(a, seq - k, 0), 0.0)

    for g, win in enumerate(POOL_WINDOWS):
        sl = slice(g * FNET_CH, (g + 1) * FNET_CH)
        x = x_ref[:, sl].astype(f32)
        half = win // 2
        back, fwd, k = x, x, 1
        while k < half:
            back = back + prev(back, k)
            fwd = fwd + nxt(fwd, k)
            k *= 2
        total = prev(back, 1) + fwd
        cnt = (jnp.minimum(t + half, seq) - jnp.maximum(t - half, 0)).astype(f32)
        pooled = (total / cnt - x).astype(bf16)
        o = jnp.dot(pooled, w_ref[g], preferred_element_type=f32) * sc_ref[:, sl]
        o_ref[:, sl] = o.astype(o_ref.dtype)


def pool_mix(proj, col_blk, pool_w, pool_scale, seq):
    t_tokens = proj.shape[0]
    return pl.pallas_call(
        functools.partial(_pool_kernel, seq),
        grid=(t_tokens // seq,),
        in_specs=[
            pl.BlockSpec((seq, POOL_WIDTH), lambda b: (b, col_blk)),
            pl.BlockSpec((len(POOL_WINDOWS), FNET_CH, FNET_CH), lambda b: (0, 0, 0)),
            pl.BlockSpec((1, POOL_WIDTH), lambda b: (0, 0)),
        ],
        out_specs=pl.BlockSpec((seq, POOL_WIDTH), lambda b: (b, 0)),
        out_shape=jax.ShapeDtypeStruct((t_tokens, POOL_WIDTH), bf16),
        compiler_params=_params("arbitrary"),
        name="pool_mix",
    )(proj, pool_w.astype(bf16), pool_scale.reshape(1, POOL_WIDTH))


def _out_proj_kernel(x_ref, gate_ref, a_ref, b_ref, wa_ref, wb_ref, o_ref):
    y = (jnp.dot(a_ref[...], wa_ref[...], preferred_element_type=f32)
         + jnp.dot(b_ref[...], wb_ref[...], preferred_element_type=f32))
    o_ref[...] = x_ref[...] + gate_ref[...] * y


def out_proj(x, mod, a, b, wa, wb, seq, base_row, per_seq):
    t_tokens = x.shape[0]
    tm = 512
    return pl.pallas_call(
        _out_proj_kernel,
        grid=(t_tokens // tm, 1),
        in_specs=[
            pl.BlockSpec((tm, D_MODEL), lambda m, n: (m, 0)),
            _mod_spec(2, tm, seq, base_row, per_seq),
            pl.BlockSpec((tm, a.shape[1]), lambda m, n: (m, 0)),
            pl.BlockSpec((tm, b.shape[1]), lambda m, n: (m, 0)),
            pl.BlockSpec(wa.shape, lambda m, n: (0, 0)),
            pl.BlockSpec(wb.shape, lambda m, n: (0, 0)),
        ],
        out_specs=pl.BlockSpec((tm, D_MODEL), lambda m, n: (m, 0)),
        out_shape=jax.ShapeDtypeStruct((t_tokens, D_MODEL), f32),
        compiler_params=_params("arbitrary", "arbitrary"),
        name="out_proj",
    )(x, mod, a, b, wa, wb)


def _ffn_kernel(tm, seq, nj, final, *refs):
    it = iter(refs)
    (x_hbm, xp_ref, xn_ref, shift_ref, scale_ref, gate_ref, g_ref,
     wv_ref, wg_ref, cwv_ref, cwg_ref, cbv_ref, cbg_ref, wd_ref) = (next(it) for _ in range(14))
    gf_ref = next(it) if final else None
    o_hbm, h_ref, acc_ref = next(it), next(it), next(it)
    u_refs = [(next(it), next(it)), (next(it), next(it))]
    x_buf, x_sem, o_sem = next(it), next(it), next(it)
    i = pl.program_id(0)
    j = pl.program_id(1)
    n_tiles = pl.num_programs(0)
    rows = tm + 2 * HALO

    def x_copy(tile, slot):
        return pltpu.make_async_copy(x_hbm.at[pl.ds(tile * tm, tm), :], x_buf.at[slot], x_sem.at[slot])

    def o_copy(tile):
        return pltpu.make_async_copy(acc_ref, o_hbm.at[pl.ds(tile * tm, tm), :], o_sem.at[0])

    slot = i % 2
    x_ref = x_buf.at[slot]

    @pl.when((j == 0) & (i == 0))
    def _():
        x_copy(0, 0).start()

    @pl.when(j == 0)
    def _():
        x_copy(i, slot).wait()

    @pl.when((j == 1) & (i + 1 < n_tiles))
    def _():
        x_copy(i + 1, 1 - slot).start()

    @pl.when((j == 1) & (i > 0))
    def _():
        o_copy(i - 1).wait()

    @pl.when(j == 1)
    def _():
        acc_ref[...] = jnp.zeros_like(acc_ref)

    def prologue_and_first_up():
        g, sh, sc = g_ref[...], shift_ref[...], scale_ref[...]
        uv_ref, ug_ref = u_refs[0]

        def emit(rs, h):
            h_ref[rs, :] = h
            uv_ref[rs, :] = jnp.dot(h, wv_ref[...], preferred_element_type=f32)
            ug_ref[rs, :] = jnp.dot(h, wg_ref[...], preferred_element_type=f32)

        for r0 in range(0, tm - IN_ROWS, IN_ROWS):
            rs = slice(r0, r0 + IN_ROWS)
            emit(rs, _rms_modulate(x_ref[rs, :], g, sh, sc).astype(bf16))
        next_ok = jnp.where((((i + 1) * tm) & (seq - 1)) != 0, 1.0, 0.0)
        prev_ok = jnp.where(((i * tm) & (seq - 1)) != 0, 1.0, 0.0)
        h_last = _rms_modulate(x_ref[tm - IN_ROWS:tm, :], g, sh, sc)
        h_next = _rms_modulate(xn_ref[...], g, sh, sc) * next_ok
        h_prev = _rms_modulate(xp_ref[...], g, sh, sc) * prev_ok
        emit(slice(tm - IN_ROWS, rows), jnp.concatenate([h_last, h_next, h_prev], axis=0).astype(bf16))

    def zero_row(a, r):
        r0 = (r // HALO) * HALO
        row8 = lax.broadcasted_iota(jnp.int32, (HALO, 1), 0)
        fixed = jnp.where(row8 == r - r0, 0.0, a[r0:r0 + HALO])
        return jnp.concatenate([a[:r0], fixed, a[r0 + HALO:]], axis=0)

    def window(u_ref, r0):
        lo, hi = r0 - HALO, r0 + FFN_ROWS + HALO
        if lo < 0:
            return jnp.concatenate([u_ref[rows + lo:rows, :], u_ref[0:hi, :]], axis=0)
        return u_ref[lo:hi, :]

    def conv(u_ref, r0, cw_ref, cb_ref):
        u = window(u_ref, r0)
        n = FFN_ROWS + 2 * HALO
        up = pltpu.roll(u, 1, 0)
        un = pltpu.roll(u, n - 1, 0)
        for edge in range(seq, tm, seq):
            if r0 <= edge < r0 + FFN_ROWS:
                up = zero_row(up, edge - r0 + HALO)
            if r0 <= edge - 1 < r0 + FFN_ROWS:
                un = zero_row(un, edge - 1 - r0 + HALO)
        keep = slice(HALO, HALO + FFN_ROWS)
        return (up[keep] * cw_ref[0:1, :] + u[keep] * cw_ref[1:2, :] + un[keep] * cw_ref[2:3, :]
                + cb_ref[...])

    def gated(slot):
        uv_ref, ug_ref = u_refs[slot]
        chunks = []
        for r0 in range(0, tm, FFN_ROWS):
            val = conv(uv_ref, r0, cwv_ref, cbv_ref)
            gate = conv(ug_ref, r0, cwg_ref, cbg_ref)
            chunks.append((gate * jax.nn.sigmoid(gate) * val).astype(bf16))
        return chunks

    def zero_after(chunks):
        parts = []
        for c in chunks:
            bits = pltpu.bitcast(c, jnp.uint32)
            parts += [bits[k:k + HALO] for k in range(0, bits.shape[0], HALO)]
        while len(parts) > 1:
            parts = [a | b for a, b in zip(parts[0::2], parts[1::2])] + parts[len(parts) & ~1:]
        return (parts[0] >> 16) >> 16

    def after(a, zero, row0=0):
        tf = zero.shape[1]
        mid = pltpu.bitcast(pltpu.bitcast(a[row0:row0 + 2 * HALO, 0:tf], jnp.uint32) | zero, bf16)
        if tf < a.shape[1]:
            mid = jnp.concatenate([mid, a[row0:row0 + 2 * HALO, tf:]], axis=1)
        parts = ([a[:row0]] if row0 else []) + [mid, a[row0 + 2 * HALO:]]
        return jnp.concatenate(parts, axis=0)

    def quarter(chunks, k):
        n = len(chunks) // 4
        return chunks[k * n:(k + 1) * n]

    def up_project(slot, chunks):
        h = h_ref[...]
        wv, wg = wv_ref[...], wg_ref[...]
        uv_ref, ug_ref = u_refs[slot]
        k_mid = wv.shape[0] // 2
        wv = after(wv, zero_after(quarter(chunks, 0)), k_mid)
        wg = after(wg, zero_after(quarter(chunks, 2)), k_mid)
        uv_ref[...] = jnp.dot(h, wv, preferred_element_type=f32)
        h = after(h, zero_after(quarter(chunks, 1)))
        ug_ref[...] = jnp.dot(h, wg, preferred_element_type=f32)

    def down_project(chunks, anchored):
        if anchored:
            chunks = [after(chunks[0], zero_after(quarter(chunks, 3)))] + chunks[1:]
        act = jnp.concatenate(chunks, axis=0)
        acc_ref[...] += jnp.dot(act, wd_ref[...], preferred_element_type=f32)

    @pl.when(j == 0)
    def _():
        prologue_and_first_up()

    for p in range(2):
        @pl.when((j >= 1) & (j < nj) & (j % 2 == p))
        def _(p=p):
            chunks = gated(1 - p)
            up_project(p, chunks)
            down_project(chunks, True)

    @pl.when(j == nj)
    def _():
        down_project(gated((nj - 1) % 2), False)
        y = x_ref[...] + gate_ref[...] * acc_ref[...]
        if final:
            ms = jnp.mean(y * y, axis=-1, keepdims=True)
            y = y * lax.rsqrt(ms + EPS) * gf_ref[...]
        acc_ref[...] = y
        o_copy(i).start()

    @pl.when((j == nj) & (i == n_tiles - 1))
    def _():
        o_copy(i).wait()


def conv_ffn(x, mod, g, w_up, conv_w, conv_b, w_down, seq, base_row, per_seq, final_g=None):
    t_tokens = x.shape[0]
    tm, tf = FFN_TOKENS, W_TILE
    assert seq & (seq - 1) == 0 and (tm // FFN_ROWS) % 4 == 0
    nj = D_FF // tf
    hb = tm // HALO
    n_hblk = t_tokens // HALO
    conv_b2 = conv_b.reshape(1, 2 * D_FF)

    def blk(j):
        return jnp.clip(j, 0, nj - 1)

    assert nj >= 2
    in_specs = [
        pl.BlockSpec(memory_space=pl.ANY),
        pl.BlockSpec((HALO, D_MODEL), lambda i, j: (jnp.maximum(i * hb - 1, 0), 0)),
        pl.BlockSpec((HALO, D_MODEL), lambda i, j: (jnp.minimum((i + 1) * hb, n_hblk - 1), 0)),
        _mod_spec(3, tm, seq, base_row, per_seq),
        _mod_spec(4, tm, seq, base_row, per_seq),
        _mod_spec(5, tm, seq, base_row, per_seq),
        pl.BlockSpec((1, D_MODEL), lambda i, j: (0, 0)),
        pl.BlockSpec((D_MODEL, tf), lambda i, j: (0, blk(j))),
        pl.BlockSpec((D_MODEL, tf), lambda i, j: (0, nj + blk(j))),
        pl.BlockSpec((3, tf), lambda i, j: (0, blk(j - 1))),
        pl.BlockSpec((3, tf), lambda i, j: (0, nj + blk(j - 1))),
        pl.BlockSpec((1, tf), lambda i, j: (0, blk(j - 1))),
        pl.BlockSpec((1, tf), lambda i, j: (0, nj + blk(j - 1))),
        pl.BlockSpec((tf, D_MODEL), lambda i, j: (blk(j - 1), 0)),
    ]
    args = [x, x, x, mod, mod, mod, g.reshape(1, D_MODEL), w_up, w_up,
            conv_w, conv_w, conv_b2, conv_b2, w_down]
    if final_g is not None:
        in_specs.append(pl.BlockSpec((1, D_MODEL), lambda i, j: (0, 0)))
        args.append(final_g.reshape(1, D_MODEL))
    return pl.pallas_call(
        functools.partial(_ffn_kernel, tm, seq, nj, final_g is not None),
        grid=(t_tokens // tm, nj + 1),
        in_specs=in_specs,
        out_specs=pl.BlockSpec(memory_space=pl.ANY),
        out_shape=jax.ShapeDtypeStruct((t_tokens, D_MODEL), f32),
        scratch_shapes=[pltpu.VMEM((tm + 2 * HALO, D_MODEL), bf16),
                        pltpu.VMEM((tm, D_MODEL), f32)]
                       + [pltpu.VMEM((tm + 2 * HALO, tf), f32)] * 4
                       + [pltpu.VMEM((2, tm, D_MODEL), f32),
                          pltpu.SemaphoreType.DMA((2,)),
                          pltpu.SemaphoreType.DMA((1,))],
        compiler_params=_params("arbitrary", "arbitrary"),
        name="conv_ffn",
    )(*args)


def _rope_tables(seq):
    t = np.arange(seq)
    half = HEAD_DIM // 2
    inv_freq = np.power(ROPE_BASE, -np.arange(0, half, 2, dtype=np.float64) / half)
    ang_r = (t // GRID_W)[:, None] * inv_freq[None, :]
    ang_c = (t % GRID_W)[:, None] * inv_freq[None, :]
    cos = np.concatenate([np.cos(ang_r)] * 2 + [np.cos(ang_c)] * 2, axis=-1)
    sin = np.concatenate([-np.sin(ang_r), np.sin(ang_r), -np.sin(ang_c), np.sin(ang_c)], axis=-1)
    return jnp.asarray(cos, dtype=f32), jnp.asarray(sin, dtype=f32)


def kernel(x_prompt, x_sample, c, cache_na_k, cache_na_v, cache_gqa_k, cache_gqa_v, c_ctx, norm1_g, norm2_g, ada_w, ada_b, ev_w_in, ev_na_bias, ev_fnet_w, ev_w_out, od_w_in, od_pool_w, od_pool_scale, od_q_norm_g, od_k_norm_g, od_w_out, ffn_w_up, ffn_conv_w, ffn_conv_b, ffn_w_down, final_norm_g):
    n_ctx, seq_p, _ = x_prompt.shape
    n_lat, seq_s, _ = x_sample.shape
    past = cache_na_k.shape[2]
    assert n_lat + 1 <= MOD_ROWS
    xp = x_prompt.reshape(n_ctx * seq_p, D_MODEL)
    xs = x_sample.reshape(n_lat * seq_s, D_MODEL)

    cond = jnp.zeros((MOD_ROWS, D_MODEL), f32).at[0].set(c_ctx).at[1:1 + n_lat].set(c)
    mods = ada_modulation(cond, ada_w, ada_b).reshape(DEPTH, MOD_ROWS * 6, 1, D_MODEL)
    rope = _rope_tables(seq_s)

    ctx = dict(seq=seq_p, base_row=0, per_seq=0)
    lat = dict(seq=seq_s, base_row=1, per_seq=1)
    new_k, new_v = {}, {}
    for i in range(DEPTH):
        j = i // 2
        mod = mods[i]
        w_up = ffn_w_up[i].astype(bf16)
        w_down = ffn_w_down[i].astype(bf16)
        if i % 2 == 0:
            w_in = ev_w_in[j].astype(bf16)
            w_attn = ev_w_out[j, :NA_WIDTH].astype(bf16)
            w_mix = ev_w_out[j, NA_WIDTH:].astype(bf16)
            mix_blk = 3 * NA_WIDTH // FNET_WIDTH
            proj_p = in_proj(xp, mod, norm1_g[i], w_in, f32, NA_WIDTH, **ctx)
            new_k[i] = proj_p[:, NA_WIDTH:2 * NA_WIDTH]
            new_v[i] = proj_p[:, 2 * NA_WIDTH:3 * NA_WIDTH]
            attn_p = ctx_attention(proj_p, seq_p, NA_HEADS, NA_HEADS)
            mix_p = fourier_mix(proj_p, mix_blk, ev_fnet_w[j], seq_p)
            proj_s = in_proj(xs, mod, norm1_g[i], w_in, bf16, NA_WIDTH, **lat)
            attn_s = na_attention(proj_s,
                                  cache_na_k[:, j].reshape(n_lat * past, NA_WIDTH),
                                  cache_na_v[:, j].reshape(n_lat * past, NA_WIDTH),
                                  ev_na_bias[j], seq_s, past)
            mix_s = fourier_mix(proj_s, mix_blk, ev_fnet_w[j], seq_s)
        else:
            w_in = od_w_in[j].astype(bf16)
            rot = dict(w_rot=POOL_WIDTH // W_TILE)
            w_attn = od_w_out[j, POOL_WIDTH:].astype(bf16)
            w_mix = od_w_out[j, :POOL_WIDTH].astype(bf16)
            norm = (od_q_norm_g[j], od_k_norm_g[j], GQA_KV_WIDTH)
            k_lo = GQA_Q_WIDTH
            v_lo = GQA_Q_WIDTH + GQA_KV_WIDTH
            mix_blk = (GQA_Q_WIDTH + 2 * GQA_KV_WIDTH) // POOL_WIDTH
            proj_p = in_proj(xp, mod, norm1_g[i], w_in, f32, GQA_Q_WIDTH, norm=norm, **rot, **ctx)
            new_k[i] = proj_p[:, k_lo:k_lo + GQA_KV_WIDTH]
            new_v[i] = proj_p[:, v_lo:v_lo + GQA_KV_WIDTH]
            attn_p = ctx_attention(proj_p, seq_p, GQA_Q_HEADS, GQA_KV_HEADS)
            mix_p = pool_mix(proj_p, mix_blk, od_pool_w[j], od_pool_scale[j], seq_p)
            proj_s = in_proj(xs, mod, norm1_g[i], w_in, bf16, GQA_Q_WIDTH, norm=norm, rope=rope, **rot, **lat)
            attn_s = gqa_attention(proj_s,
                                   cache_gqa_k[:, j].reshape(n_lat * past, GQA_KV_WIDTH),
                                   cache_gqa_v[:, j].reshape(n_lat * past, GQA_KV_WIDTH),
                                   seq_s, past)
            mix_s = pool_mix(proj_s, mix_blk, od_pool_w[j], od_pool_scale[j], seq_s)
        xp = out_proj(xp, mod, attn_p, mix_p, w_attn, w_mix, **ctx)
        xs = out_proj(xs, mod, attn_s, mix_s, w_attn, w_mix, **lat)
        final_g = final_norm_g if i == DEPTH - 1 else None
        xp = conv_ffn(xp, mod, norm2_g[i], w_up, ffn_conv_w[i], ffn_conv_b[i], w_down,
                      final_g=final_g, **ctx)
        xs = conv_ffn(xs, mod, norm2_g[i], w_up, ffn_conv_w[i], ffn_conv_b[i], w_down,
                      final_g=final_g, **lat)

    y_prompt = xp.reshape(n_ctx, seq_p, D_MODEL)
    y_sample = xs.reshape(n_lat, seq_s, D_MODEL)
    even = [i for i in range(DEPTH) if i % 2 == 0]
    odd = [i for i in range(DEPTH) if i % 2 == 1]

    def stack(parts, layers, heads):
        return jnp.stack([parts[i].reshape(n_ctx, seq_p, heads, HEAD_DIM) for i in layers], axis=1)

    return (y_prompt, y_sample,
            stack(new_k, even, NA_HEADS), stack(new_v, even, NA_HEADS),
            stack(new_k, odd, GQA_KV_HEADS), stack(new_v, odd, GQA_KV_HEADS))
```

```python
import functools
import math

import jax
import jax.numpy as jnp
import numpy as np
from jax import lax
from jax.experimental import pallas as pl
from jax.experimental.pallas import tpu as pltpu

f32 = jnp.float32
bf16 = jnp.bfloat16

D_MODEL = 2048
DEPTH = 2
GRID_W = 64
HEAD_DIM = 128
NA_HEADS = 12
NA_KR = 8
NA_KC = 16
FNET_GROUPS = 4
FNET_CH = 128
POOL_WINDOWS = (2, 4, 8, 16)
GQA_Q_HEADS = 12
GQA_KV_HEADS = 4
D_FF = 5632
ROPE_BASE = 10000.0
EPS = 1e-6
NEG_INF = -1e30
NA_WIDTH = NA_HEADS * HEAD_DIM
FNET_WIDTH = FNET_GROUPS * FNET_CH
POOL_WIDTH = len(POOL_WINDOWS) * FNET_CH
GQA_Q_WIDTH = GQA_Q_HEADS * HEAD_DIM
GQA_KV_WIDTH = GQA_KV_HEADS * HEAD_DIM
LOG2_E = math.log2(math.e)
ATTN_SCALE = HEAD_DIM ** -0.5 * LOG2_E

MOD_ROWS = 16
VMEM_LIMIT = 60 * 1024 * 1024
HALO = 8
NA_QROWS = 4
NA_WROWS = 12
NA_INVALID = 2 * NA_KR - 1
GQA_QUERIES = 512
NA_HEADS_PER_STEP = 6
FFN_TOKENS = 1024
FFN_ROWS = 128
W_TILE = 512
IN_TOKENS = 1024
IN_WIDE_TILE = 1024
IN_ROWS = 256


def _params(*sem, flags=None):
    return pltpu.CompilerParams(dimension_semantics=sem, vmem_limit_bytes=VMEM_LIMIT, flags=flags)


def _rms_modulate(x, g, shift, scale):
    ms = jnp.mean(x * x, axis=-1, keepdims=True)
    return (x * lax.rsqrt(ms + EPS)) * (g * (1.0 + scale)) + shift


def _mod_spec(k, tm, seq, base_row, per_seq):
    def index(m, n):
        return ((base_row + per_seq * ((m * tm) // seq)) * 6 + k, 0, 0)
    return pl.BlockSpec((None, 1, D_MODEL), index)


def _ada_kernel(c_ref, w_ref, b_ref, o_ref):
    c = c_ref[...]
    s = (c * jax.nn.sigmoid(c)).astype(bf16)
    o_ref[...] = jnp.dot(s, w_ref[...].astype(bf16), preferred_element_type=f32) + b_ref[...]


def ada_modulation(cond, ada_w, ada_b):
    tn = 1024
    n_out = 6 * D_MODEL
    return pl.pallas_call(
        _ada_kernel,
        grid=(DEPTH, n_out // tn),
        in_specs=[
            pl.BlockSpec((MOD_ROWS, D_MODEL), lambda i, n: (0, 0)),
            pl.BlockSpec((None, D_MODEL, tn), lambda i, n: (i, 0, n)),
            pl.BlockSpec((None, 1, tn), lambda i, n: (i, 0, n)),
        ],
        out_specs=pl.BlockSpec((None, MOD_ROWS, tn), lambda i, n: (i, 0, n)),
        out_shape=jax.ShapeDtypeStruct((DEPTH, MOD_ROWS, n_out), f32),
        compiler_params=_params("arbitrary", "arbitrary"),
        name="ada_modulation",
    )(cond, ada_w, ada_b.reshape(DEPTH, 1, n_out))


def _rope(a, cos, sin):
    lane = lax.broadcasted_iota(jnp.int32, a.shape, 1)
    quarter = HEAD_DIM // 4
    partner = jnp.where((lane & (2 * quarter - 1)) < quarter,
                        pltpu.roll(a, HEAD_DIM - quarter, 1),
                        pltpu.roll(a, quarter, 1))
    return a * cos + partner * sin


def _in_proj_kernel(tn, q_blocks, k_blocks, has_norm, has_rope, *refs):
    it = iter(refs)
    x_hbm, shift_ref, scale_ref, g_ref, w_ref = (next(it) for _ in range(5))
    qg_ref = kg_ref = cos_ref = sin_ref = qs_ref = None
    if has_norm:
        qg_ref, kg_ref, avg_ref = next(it), next(it), next(it)
    else:
        qs_ref = next(it)
    if has_rope:
        cos_ref, sin_ref = next(it), next(it)
    o_ref, h_ref, x_buf, x_sem = next(it), next(it), next(it), next(it)
    m = pl.program_id(0)
    n = pl.program_id(1)
    tm = h_ref.shape[0]
    everything = slice(0, tm)

    def x_copy(tile, slot):
        return pltpu.make_async_copy(x_hbm.at[pl.ds(tile * tm, tm), :], x_buf.at[slot], x_sem.at[slot])

    slot = m % 2
    x_ref = x_buf.at[slot]

    @pl.when((n == 0) & (m == 0))
    def _():
        x_copy(0, 0).start()

    @pl.when(n == 0)
    def _():
        x_copy(m, slot).wait()

    @pl.when((n == 1) & (m + 1 < pl.num_programs(0)))
    def _():
        x_copy(m + 1, 1 - slot).start()

    def project(rs, h=None):
        h = h_ref[rs, :] if h is None else h
        return jnp.dot(h, w_ref[...], preferred_element_type=f32)

    def store_scaled(rs, acc, scale):
        o_ref[rs, :] = (acc if scale is None else acc * scale).astype(o_ref.dtype)

    def store_normed(rs, acc, gn):
        ms = jnp.dot((acc * acc).astype(bf16), avg_ref[...], preferred_element_type=f32)
        inv = lax.rsqrt(ms + EPS)
        for hh in range(tn // HEAD_DIM):
            sl = slice(hh * HEAD_DIM, (hh + 1) * HEAD_DIM)
            a = acc[:, sl] * inv[:, sl] * gn
            if has_rope:
                a = _rope(a, cos_ref[rs, :], sin_ref[rs, :])
            o_ref[rs, sl] = a.astype(o_ref.dtype)

    @pl.when(n == 0)
    def _():
        g, sh, sc = g_ref[...], shift_ref[...], scale_ref[...]
        for r0 in range(0, tm, IN_ROWS):
            rs = slice(r0, r0 + IN_ROWS)
            h = _rms_modulate(x_ref[rs, :], g, sh, sc).astype(bf16)
            h_ref[rs, :] = h
            if has_norm:
                store_normed(rs, project(rs, h), qg_ref[...] * ATTN_SCALE)
            else:
                store_scaled(rs, project(rs, h), qs_ref[...])

    if not has_norm:
        @pl.when(n > 0)
        def _():
            store_scaled(everything, project(everything), qs_ref[...])
        return

    is_q = n < q_blocks
    q_scale = jnp.where(is_q, ATTN_SCALE, 1.0)

    is_qk = n < q_blocks + k_blocks

    @pl.when((n > 0) & is_qk)
    def _():
        gn = jnp.where(is_q, qg_ref[...], kg_ref[...]) * q_scale
        store_normed(everything, project(everything), gn)

    @pl.when(jnp.logical_not(is_qk))
    def _():
        store_scaled(everything, project(everything), None)


def in_proj(x, mod, g, w, out_dtype, q_width, seq, base_row, per_seq, norm=None, rope=None, w_rot=0):
    t_tokens = x.shape[0]
    tm, tn = IN_TOKENS, (W_TILE if norm is not None else IN_WIDE_TILE)
    n_total = w.shape[1]
    assert n_total % tn == 0 and n_total // tn >= 2
    in_specs = [
        pl.BlockSpec(memory_space=pl.ANY),
        _mod_spec(0, tm, seq, base_row, per_seq),
        _mod_spec(1, tm, seq, base_row, per_seq),
        pl.BlockSpec((1, D_MODEL), lambda m, n: (0, 0)),
        pl.BlockSpec((D_MODEL, tn), lambda m, n: (0, (n + w_rot) % (n_total // tn))),
    ]
    args = [x, mod, mod, g.reshape(1, D_MODEL), w]
    k_width = 0
    if norm is not None:
        k_width = norm[2]
        in_specs += [pl.BlockSpec((1, HEAD_DIM), lambda m, n: (0, 0))] * 2
        in_specs += [pl.BlockSpec((tn, tn), lambda m, n: (0, 0))]
        head_of = np.arange(tn) // HEAD_DIM
        avg = (head_of[:, None] == head_of[None, :]) / HEAD_DIM
        args += [norm[0].reshape(1, HEAD_DIM), norm[1].reshape(1, HEAD_DIM), jnp.asarray(avg, dtype=bf16)]
    else:
        in_specs += [pl.BlockSpec((1, tn), lambda m, n: (0, n))]
        args += [jnp.asarray(np.where(np.arange(n_total) < q_width, ATTN_SCALE, 1.0)[None, :], dtype=f32)]
    if rope is not None:
        spb = seq // tm
        in_specs += [pl.BlockSpec((tm, HEAD_DIM), lambda m, n: (m % spb, 0))] * 2
        args += [rope[0], rope[1]]
    kern = functools.partial(_in_proj_kernel, tn, q_width // tn, k_width // tn,
                             norm is not None, rope is not None)
    return pl.pallas_call(
        kern,
        grid=(t_tokens // tm, n_total // tn),
        in_specs=in_specs,
        out_specs=pl.BlockSpec((tm, tn), lambda m, n: (m, n)),
        out_shape=jax.ShapeDtypeStruct((t_tokens, n_total), out_dtype),
        scratch_shapes=[pltpu.VMEM((tm, D_MODEL), bf16),
                        pltpu.VMEM((2, tm, D_MODEL), f32),
                        pltpu.SemaphoreType.DMA((2,))],
        compiler_params=_params("arbitrary", "arbitrary"),
        name="in_proj",
    )(*args)


def _with_ones(v):
    return jnp.concatenate([v, jnp.ones_like(v)], axis=1)


def _softmax_pv(parts):
    m = functools.reduce(jnp.maximum, [jnp.max(s, axis=-1, keepdims=True) for s, _ in parts])
    acc = sum(jnp.dot(jnp.exp2(s - m).astype(bf16), v1, preferred_element_type=f32) for s, v1 in parts)
    return acc[:, :HEAD_DIM] / acc[:, HEAD_DIM:]


def _ctx_attn_kernel(n_q, n_kv, q_ref, k_ref, v_ref, o_ref):
    group = n_q // n_kv
    values = [_with_ones(v_ref[:, kv * HEAD_DIM:(kv + 1) * HEAD_DIM].astype(bf16)) for kv in range(n_kv)]
    keys = [k_ref[:, kv * HEAD_DIM:(kv + 1) * HEAD_DIM].astype(bf16) for kv in range(n_kv)]
    dn = (((1,), (1,)), ((), ()))
    scores = [lax.dot_general(q_ref[:, h * HEAD_DIM:(h + 1) * HEAD_DIM].astype(bf16), keys[h // group],
                              dn, preferred_element_type=f32) for h in range(n_q)]
    for h, s in enumerate(scores):
        o = _softmax_pv([(s, values[h // group])])
        o_ref[:, h * HEAD_DIM:(h + 1) * HEAD_DIM] = o.astype(o_ref.dtype)


def ctx_attention(proj, seq, n_q, n_kv):
    t_tokens = proj.shape[0]
    qw, kw = n_q * HEAD_DIM, n_kv * HEAD_DIM
    assert qw % kw == 0
    k_blk = qw // kw
    return pl.pallas_call(
        functools.partial(_ctx_attn_kernel, n_q, n_kv),
        grid=(t_tokens // seq,),
        in_specs=[
            pl.BlockSpec((seq, qw), lambda b: (b, 0)),
            pl.BlockSpec((seq, kw), lambda b: (b, k_blk)),
            pl.BlockSpec((seq, kw), lambda b: (b, k_blk + 1)),
        ],
        out_specs=pl.BlockSpec((seq, qw), lambda b: (b, 0)),
        out_shape=jax.ShapeDtypeStruct((t_tokens, qw), bf16),
        compiler_params=_params("arbitrary"),
        name="ctx_attention",
    )(proj, proj, proj)


def _gqa_attn_kernel(group, q_ref, ck_ref, cv_ref, k_ref, v_ref, o_ref):
    ck = ck_ref[...].astype(bf16)
    cv = _with_ones(cv_ref[...].astype(bf16))
    k = k_ref[...]
    v = _with_ones(v_ref[...])
    dn = (((1,), (1,)), ((), ()))
    scores = []
    for gi in range(group):
        q = q_ref[:, gi * HEAD_DIM:(gi + 1) * HEAD_DIM]
        scores.append((lax.dot_general(q, ck, dn, preferred_element_type=f32),
                       lax.dot_general(q, k, dn, preferred_element_type=f32)))
    for gi, (s_c, s_l) in enumerate(scores):
        o = _softmax_pv([(s_c, cv), (s_l, v)])
        o_ref[:, gi * HEAD_DIM:(gi + 1) * HEAD_DIM] = o.astype(o_ref.dtype)


def gqa_attention(proj, cache_k, cache_v, seq, past):
    t_tokens = proj.shape[0]
    n_b = t_tokens // seq
    group = GQA_Q_HEADS // GQA_KV_HEADS
    tq = GQA_QUERIES
    nq = seq // tq
    return pl.pallas_call(
        functools.partial(_gqa_attn_kernel, group),
        grid=(n_b, GQA_KV_HEADS, nq),
        in_specs=[
            pl.BlockSpec((tq, group * HEAD_DIM), lambda b, h, i: (b * nq + i, h)),
            pl.BlockSpec((past, HEAD_DIM), lambda b, h, i: (b, h)),
            pl.BlockSpec((past, HEAD_DIM), lambda b, h, i: (b, h)),
            pl.BlockSpec((seq, HEAD_DIM), lambda b, h, i: (b, GQA_Q_HEADS + h)),
            pl.BlockSpec((seq, HEAD_DIM), lambda b, h, i: (b, GQA_Q_HEADS + GQA_KV_HEADS + h)),
        ],
        out_specs=pl.BlockSpec((tq, group * HEAD_DIM), lambda b, h, i: (b * nq + i, h)),
        out_shape=jax.ShapeDtypeStruct((t_tokens, GQA_Q_WIDTH), bf16),
        compiler_params=_params("arbitrary", "arbitrary", "arbitrary"),
        name="gqa_attention",
    )(proj, cache_k, cache_v, proj, proj)


def _na_bias_tiles(rel_bias):
    qc = np.arange(GRID_W)[:, None]
    kc = np.arange(GRID_W)[None, :]
    q_start = np.clip(qc - NA_KC // 2, 0, GRID_W - NA_KC)
    valid = (kc >= q_start) & (kc < q_start + NA_KC)
    dc = np.clip(kc - qc, -(NA_KC - 1), NA_KC - 1) + NA_KC - 1
    pick = (np.arange(2 * NA_KC - 1)[:, None, None] == dc[None]) & valid[None]
    tiles = jnp.einsum("hdc,cqk->hdqk", rel_bias.astype(f32) * LOG2_E, jnp.asarray(pick, dtype=f32),
                       precision=lax.Precision.HIGHEST)
    tiles = jnp.where(jnp.asarray(valid)[None, None], tiles, NEG_INF)
    masked = jnp.full((NA_HEADS, 1, GRID_W, GRID_W), NEG_INF, f32)
    tiles = jnp.concatenate([tiles, masked], axis=1)
    zeros = jnp.zeros_like(tiles)
    left = jnp.concatenate([tiles, zeros], axis=-1)
    right = jnp.concatenate([zeros, tiles], axis=-1)
    return left, right


def _na_attn_kernel(n_heads, n_rows, q_ref, k_ref, v_ref, ck_ref, cv_ref, bl_ref, br_ref, o_ref):
    rb = pl.program_id(2)
    start = jnp.clip(rb * NA_QROWS - NA_KR // 2, 0, n_rows - NA_WROWS)
    tok0 = pl.multiple_of(start * GRID_W, GRID_W)
    dn = (((1,), (1,)), ((), ()))

    def tile_index(i, kr):
        qr = rb * NA_QROWS + i
        r0 = jnp.clip(qr - NA_KR // 2, 0, n_rows - NA_KR)
        return jnp.where((kr >= r0) & (kr < r0 + NA_KR), kr - qr + NA_KR - 1, NA_INVALID)

    index = [[(tile_index(i, start + 2 * jp), tile_index(i, start + 2 * jp + 1))
              for jp in range(NA_WROWS // 2)] for i in range(NA_QROWS)]
    scores = []
    for hh in range(n_heads):
        hs = slice(hh * HEAD_DIM, (hh + 1) * HEAD_DIM)
        q = q_ref[:, hs]
        bias = jnp.concatenate(
            [jnp.concatenate([bl_ref[hh, ia] + br_ref[hh, ib] for ia, ib in row], axis=1)
             for row in index], axis=0)
        s_l = lax.dot_general(q, k_ref[pl.ds(tok0, NA_WROWS * GRID_W), hs], dn,
                              preferred_element_type=f32) + bias
        s_c = lax.dot_general(q, ck_ref[:, hs].astype(bf16), dn, preferred_element_type=f32)
        scores.append((s_c, s_l))
    for hh, (s_c, s_l) in enumerate(scores):
        hs = slice(hh * HEAD_DIM, (hh + 1) * HEAD_DIM)
        vw = v_ref[pl.ds(tok0, NA_WROWS * GRID_W), hs]
        o = _softmax_pv([(s_c, _with_ones(cv_ref[:, hs].astype(bf16))), (s_l, _with_ones(vw))])
        o_ref[:, hs] = o.astype(o_ref.dtype)


def na_attention(proj, cache_k, cache_v, rel_bias, seq, past):
    t_tokens = proj.shape[0]
    n_b = t_tokens // seq
    n_rows = seq // GRID_W
    n_rb = n_rows // NA_QROWS
    tq = NA_QROWS * GRID_W
    hb = NA_HEADS_PER_STEP
    n_hg = NA_HEADS // hb
    wide = hb * HEAD_DIM
    bias_l, bias_r = _na_bias_tiles(rel_bias)
    bias_spec = pl.BlockSpec((hb, 2 * NA_KR, GRID_W, 2 * GRID_W), lambda b, h, r: (h, 0, 0, 0))
    return pl.pallas_call(
        functools.partial(_na_attn_kernel, hb, n_rows),
        grid=(n_b, n_hg, n_rb),
        in_specs=[
            pl.BlockSpec((tq, wide), lambda b, h, r: (b * n_rb + r, h)),
            pl.BlockSpec((seq, wide), lambda b, h, r: (b, n_hg + h)),
            pl.BlockSpec((seq, wide), lambda b, h, r: (b, 2 * n_hg + h)),
            pl.BlockSpec((past, wide), lambda b, h, r: (b, h)),
            pl.BlockSpec((past, wide), lambda b, h, r: (b, h)),
            bias_spec,
            bias_spec,
        ],
        out_specs=pl.BlockSpec((tq, wide), lambda b, h, r: (b * n_rb + r, h)),
        out_shape=jax.ShapeDtypeStruct((t_tokens, NA_WIDTH), bf16),
        compiler_params=_params("arbitrary", "arbitrary", "arbitrary"),
        name="na_attention",
    )(proj, proj, proj, cache_k, cache_v, bias_l, bias_r)


def _dft_cols_kernel(x_ref, cs_ref, y_ref):
    for g in range(FNET_GROUPS):
        sl = slice(g * FNET_CH, (g + 1) * FNET_CH)
        y = jnp.dot(x_ref[:, sl].astype(bf16), cs_ref[...], preferred_element_type=f32)
        y_ref[0, :, sl] = y[:, :FNET_CH].astype(bf16)
        y_ref[1, :, sl] = y[:, FNET_CH:].astype(bf16)


def _dft_rows_kernel(d_ref, y_ref, w_ref, o_ref):
    f = jnp.dot(d_ref[...], y_ref[...], preferred_element_type=f32)
    for g in range(FNET_GROUPS):
        sl = slice(g * FNET_CH, (g + 1) * FNET_CH)
        o = jnp.dot(f[:, sl].astype(bf16), w_ref[g], preferred_element_type=f32)
        o_ref[:, sl] = o.astype(o_ref.dtype)


def _dft_tables(seq):
    c = np.arange(FNET_CH, dtype=np.int64)
    ang_c = (2.0 * np.pi / FNET_CH) * ((c[:, None] * c[None, :]) % FNET_CH)
    cs = np.concatenate([np.cos(ang_c), np.sin(ang_c)], axis=1)
    n = np.arange(seq, dtype=np.int64)
    ang_n = (2.0 * np.pi / seq) * ((n[:, None] * n[None, :]) % seq)
    norm = 1.0 / math.sqrt(seq * FNET_CH)
    d = np.concatenate([np.cos(ang_n) * norm, -np.sin(ang_n) * norm], axis=1)
    return jnp.asarray(cs, dtype=bf16), jnp.asarray(d, dtype=bf16)


def fourier_mix(proj, col_blk, fnet_w, seq):
    t_tokens = proj.shape[0]
    n_b = t_tokens // seq
    cs, d = _dft_tables(seq)
    tn = min(seq, 512)
    ns = seq // tn
    y = pl.pallas_call(
        _dft_cols_kernel,
        grid=(n_b, ns),
        in_specs=[
            pl.BlockSpec((tn, FNET_WIDTH), lambda b, i: (b * ns + i, col_blk)),
            pl.BlockSpec((FNET_CH, 2 * FNET_CH), lambda b, i: (0, 0)),
        ],
        out_specs=pl.BlockSpec((2, tn, FNET_WIDTH), lambda b, i: (0, i, b)),
        out_shape=jax.ShapeDtypeStruct((2, seq, n_b * FNET_WIDTH), bf16),
        compiler_params=_params("arbitrary", "arbitrary"),
        name="dft_cols",
    )(proj, cs)
    y2 = y.reshape(2 * seq, n_b * FNET_WIDTH)
    return pl.pallas_call(
        _dft_rows_kernel,
        grid=(n_b, ns),
        in_specs=[
            pl.BlockSpec((tn, 2 * seq), lambda b, i: (i, 0)),
            pl.BlockSpec((2 * seq, FNET_WIDTH), lambda b, i: (0, b)),
            pl.BlockSpec((FNET_GROUPS, FNET_CH, FNET_CH), lambda b, i: (0, 0, 0)),
        ],
        out_specs=pl.BlockSpec((tn, FNET_WIDTH), lambda b, i: (b * ns + i, 0)),
        out_shape=jax.ShapeDtypeStruct((t_tokens, FNET_WIDTH), bf16),
        compiler_params=_params("arbitrary", "arbitrary"),
        name="dft_rows",
    )(d, y2, fnet_w.astype(bf16))


def _pool_kernel(seq, x_ref, w_ref, sc_ref, o_ref):
    t = lax.broadcasted_iota(jnp.int32, (seq, 1), 0)

    def prev(a, k):
        return jnp.where(t >= k, pltpu.roll(a, k, 0), 0.0)

    def nxt(a, k):
        return jnp.where(t < seq - k, pltpu.roll(a, seq - k, 0), 0.0)

    for g, win in enumerate(POOL_WINDOWS):
        sl = slice(g * FNET_CH, (g + 1) * FNET_CH)
        x = x_ref[:, sl].astype(f32)
        half = win // 2
        back, fwd, k = x, x, 1
        while k < half:
            back = back + prev(back, k)
            fwd = fwd + nxt(fwd, k)
            k *= 2
        total = prev(back, 1) + fwd
        cnt = (jnp.minimum(t + half, seq) - jnp.maximum(t - half, 0)).astype(f32)
        pooled = (total / cnt - x).astype(bf16)
        o = jnp.dot(pooled, w_ref[g], preferred_element_type=f32) * sc_ref[:, sl]
        o_ref[:, sl] = o.astype(o_ref.dtype)


def pool_mix(proj, col_blk, pool_w, pool_scale, seq):
    t_tokens = proj.shape[0]
    return pl.pallas_call(
        functools.partial(_pool_kernel, seq),
        grid=(t_tokens // seq,),
        in_specs=[
            pl.BlockSpec((seq, POOL_WIDTH), lambda b: (b, col_blk)),
            pl.BlockSpec((len(POOL_WINDOWS), FNET_CH, FNET_CH), lambda b: (0, 0, 0)),
            pl.BlockSpec((1, POOL_WIDTH), lambda b: (0, 0)),
        ],
        out_specs=pl.BlockSpec((seq, POOL_WIDTH), lambda b: (b, 0)),
        out_shape=jax.ShapeDtypeStruct((t_tokens, POOL_WIDTH), bf16),
        compiler_params=_params("arbitrary"),
        name="pool_mix",
    )(proj, pool_w.astype(bf16), pool_scale.reshape(1, POOL_WIDTH))


def _out_proj_kernel(x_ref, gate_ref, a_ref, b_ref, wa_ref, wb_ref, o_ref):
    y = (jnp.dot(a_ref[...], wa_ref[...], preferred_element_type=f32)
         + jnp.dot(b_ref[...], wb_ref[...], preferred_element_type=f32))
    o_ref[...] = x_ref[...] + gate_ref[...] * y


def out_proj(x, mod, a, b, wa, wb, seq, base_row, per_seq):
    t_tokens = x.shape[0]
    tm = 512
    return pl.pallas_call(
        _out_proj_kernel,
        grid=(t_tokens // tm, 1),
        in_specs=[
            pl.BlockSpec((tm, D_MODEL), lambda m, n: (m, 0)),
            _mod_spec(2, tm, seq, base_row, per_seq),
            pl.BlockSpec((tm, a.shape[1]), lambda m, n: (m, 0)),
            pl.BlockSpec((tm, b.shape[1]), lambda m, n: (m, 0)),
            pl.BlockSpec(wa.shape, lambda m, n: (0, 0)),
            pl.BlockSpec(wb.shape, lambda m, n: (0, 0)),
        ],
        out_specs=pl.BlockSpec((tm, D_MODEL), lambda m, n: (m, 0)),
        out_shape=jax.ShapeDtypeStruct((t_tokens, D_MODEL), f32),
        compiler_params=_params("arbitrary", "arbitrary"),
        name="out_proj",
    )(x, mod, a, b, wa, wb)


def _ffn_kernel(tm, seq, nj, final, *refs):
    it = iter(refs)
    (x_hbm, xp_ref, xn_ref, shift_ref, scale_ref, gate_ref, g_ref,
     wv_ref, wg_ref, cwv_ref, cwg_ref, cbv_ref, cbg_ref, wd_ref) = (next(it) for _ in range(14))
    gf_ref = next(it) if final else None
    o_hbm, h_ref, acc_ref = next(it), next(it), next(it)
    u_refs = [(next(it), next(it)), (next(it), next(it))]
    x_buf, x_sem, o_sem = next(it), next(it), next(it)
    i = pl.program_id(0)
    j = pl.program_id(1)
    n_tiles = pl.num_programs(0)
    rows = tm + 2 * HALO

    def x_copy(tile, slot):
        return pltpu.make_async_copy(x_hbm.at[pl.ds(tile * tm, tm), :], x_buf.at[slot], x_sem.at[slot])

    def o_copy(tile):
        return pltpu.make_async_copy(acc_ref, o_hbm.at[pl.ds(tile * tm, tm), :], o_sem.at[0])

    slot = i % 2
    x_ref = x_buf.at[slot]

    @pl.when((j == 0) & (i == 0))
    def _():
        x_copy(0, 0).start()

    @pl.when(j == 0)
    def _():
        x_copy(i, slot).wait()

    @pl.when((j == 1) & (i + 1 < n_tiles))
    def _():
        x_copy(i + 1, 1 - slot).start()

    @pl.when((j == 1) & (i > 0))
    def _():
        o_copy(i - 1).wait()

    @pl.when(j == 1)
    def _():
        acc_ref[...] = jnp.zeros_like(acc_ref)

    def prologue_and_first_up():
        g, sh, sc = g_ref[...], shift_ref[...], scale_ref[...]
        uv_ref, ug_ref = u_refs[0]

        def emit(rs, h):
            h_ref[rs, :] = h
            uv_ref[rs, :] = jnp.dot(h, wv_ref[...], preferred_element_type=f32)
            ug_ref[rs, :] = jnp.dot(h, wg_ref[...], preferred_element_type=f32)

        for r0 in range(0, tm - IN_ROWS, IN_ROWS):
            rs = slice(r0, r0 + IN_ROWS)
            emit(rs, _rms_modulate(x_ref[rs, :], g, sh, sc).astype(bf16))
        next_ok = jnp.where((((i + 1) * tm) & (seq - 1)) != 0, 1.0, 0.0)
        prev_ok = jnp.where(((i * tm) & (seq - 1)) != 0, 1.0, 0.0)
        h_last = _rms_modulate(x_ref[tm - IN_ROWS:tm, :], g, sh, sc)
        h_next = _rms_modulate(xn_ref[...], g, sh, sc) * next_ok
        h_prev = _rms_modulate(xp_ref[...], g, sh, sc) * prev_ok
        emit(slice(tm - IN_ROWS, rows), jnp.concatenate([h_last, h_next, h_prev], axis=0).astype(bf16))

    def zero_row(a, r):
        r0 = (r // HALO) * HALO
        row8 = lax.broadcasted_iota(jnp.int32, (HALO, 1), 0)
        fixed = jnp.where(row8 == r - r0, 0.0, a[r0:r0 + HALO])
        return jnp.concatenate([a[:r0], fixed, a[r0 + HALO:]], axis=0)

    def window(u_ref, r0):
        lo, hi = r0 - HALO, r0 + FFN_ROWS + HALO
        if lo < 0:
            return jnp.concatenate([u_ref[rows + lo:rows, :], u_ref[0:hi, :]], axis=0)
        return u_ref[lo:hi, :]

    def conv(u_ref, r0, cw_ref, cb_ref):
        u = window(u_ref, r0)
        n = FFN_ROWS + 2 * HALO
        up = pltpu.roll(u, 1, 0)
        un = pltpu.roll(u, n - 1, 0)
        for edge in range(seq, tm, seq):
            if r0 <= edge < r0 + FFN_ROWS:
                up = zero_row(up, edge - r0 + HALO)
            if r0 <= edge - 1 < r0 + FFN_ROWS:
                un = zero_row(un, edge - 1 - r0 + HALO)
        keep = slice(HALO, HALO + FFN_ROWS)
        return (up[keep] * cw_ref[0:1, :] + u[keep] * cw_ref[1:2, :] + un[keep] * cw_ref[2:3, :]
                + cb_ref[...])

    def gated(slot):
        uv_ref, ug_ref = u_refs[slot]
        chunks = []
        for r0 in range(0, tm, FFN_ROWS):
            val = conv(uv_ref, r0, cwv_ref, cbv_ref)
            gate = conv(ug_ref, r0, cwg_ref, cbg_ref)
            half = 0.5 * gate
            chunks.append(((half + half * jnp.tanh(half)) * val).astype(bf16))
        return chunks

    def zero_after(chunks):
        parts = []
        for c in chunks:
            bits = pltpu.bitcast(c, jnp.uint32)
            parts += [bits[k:k + HALO] for k in range(0, bits.shape[0], HALO)]
        while len(parts) > 1:
            parts = [a | b for a, b in zip(parts[0::2], parts[1::2])] + parts[len(parts) & ~1:]
        return (parts[0] >> 16) >> 16

    def after(a, zero, row0=0):
        tf = zero.shape[1]
        mid = pltpu.bitcast(pltpu.bitcast(a[row0:row0 + 2 * HALO, 0:tf], jnp.uint32) | zero, bf16)
        if tf < a.shape[1]:
            mid = jnp.concatenate([mid, a[row0:row0 + 2 * HALO, tf:]], axis=1)
        parts = ([a[:row0]] if row0 else []) + [mid, a[row0 + 2 * HALO:]]
        return jnp.concatenate(parts, axis=0)

    def quarter(chunks, k):
        n = len(chunks) // 4
        return chunks[k * n:(k + 1) * n]

    def up_project(slot, chunks):
        h = h_ref[...]
        wv, wg = wv_ref[...], wg_ref[...]
        uv_ref, ug_ref = u_refs[slot]
        k_mid = wv.shape[0] // 2
        wv = after(wv, zero_after(quarter(chunks, 0)), k_mid)
        wg = after(wg, zero_after(quarter(chunks, 2)), k_mid)
        uv_ref[...] = jnp.dot(h, wv, preferred_element_type=f32)
        h = after(h, zero_after(quarter(chunks, 1)))
        ug_ref[...] = jnp.dot(h, wg, preferred_element_type=f32)

    def down_project(chunks, anchored):
        if anchored:
            chunks = [after(chunks[0], zero_after(quarter(chunks, 3)))] + chunks[1:]
        act = jnp.concatenate(chunks, axis=0)
        acc_ref[...] += jnp.dot(act, wd_ref[...], preferred_element_type=f32)

    @pl.when(j == 0)
    def _():
        prologue_and_first_up()

    for p in range(2):
        @pl.when((j >= 1) & (j < nj) & (j % 2 == p))
        def _(p=p):
            chunks = gated(1 - p)
            up_project(p, chunks)
            down_project(chunks, True)

    @pl.when(j == nj)
    def _():
        down_project(gated((nj - 1) % 2), False)
        y = x_ref[...] + gate_ref[...] * acc_ref[...]
        if final:
            ms = jnp.mean(y * y, axis=-1, keepdims=True)
            y = y * lax.rsqrt(ms + EPS) * gf_ref[...]
        acc_ref[...] = y
        o_copy(i).start()

    @pl.when((j == nj) & (i == n_tiles - 1))
    def _():
        o_copy(i).wait()


def conv_ffn(x, mod, g, w_up, conv_w, conv_b, w_down, seq, base_row, per_seq, final_g=None):
    t_tokens = x.shape[0]
    tm, tf = FFN_TOKENS, W_TILE
    assert seq & (seq - 1) == 0 and (tm // FFN_ROWS) % 4 == 0
    nj = D_FF // tf
    hb = tm // HALO
    n_hblk = t_tokens // HALO
    conv_b2 = conv_b.reshape(1, 2 * D_FF)

    def blk(j):
        return jnp.clip(j, 0, nj - 1)

    assert nj >= 2
    in_specs = [
        pl.BlockSpec(memory_space=pl.ANY),
        pl.BlockSpec((HALO, D_MODEL), lambda i, j: (jnp.maximum(i * hb - 1, 0), 0)),
        pl.BlockSpec((HALO, D_MODEL), lambda i, j: (jnp.minimum((i + 1) * hb, n_hblk - 1), 0)),
        _mod_spec(3, tm, seq, base_row, per_seq),
        _mod_spec(4, tm, seq, base_row, per_seq),
        _mod_spec(5, tm, seq, base_row, per_seq),
        pl.BlockSpec((1, D_MODEL), lambda i, j: (0, 0)),
        pl.BlockSpec((D_MODEL, tf), lambda i, j: (0, blk(j))),
        pl.BlockSpec((D_MODEL, tf), lambda i, j: (0, nj + blk(j))),
        pl.BlockSpec((3, tf), lambda i, j: (0, blk(j - 1))),
        pl.BlockSpec((3, tf), lambda i, j: (0, nj + blk(j - 1))),
        pl.BlockSpec((1, tf), lambda i, j: (0, blk(j - 1))),
        pl.BlockSpec((1, tf), lambda i, j: (0, nj + blk(j - 1))),
        pl.BlockSpec((tf, D_MODEL), lambda i, j: (blk(j - 1), 0)),
    ]
    args = [x, x, x, mod, mod, mod, g.reshape(1, D_MODEL), w_up, w_up,
            conv_w, conv_w, conv_b2, conv_b2, w_down]
    if final_g is not None:
        in_specs.append(pl.BlockSpec((1, D_MODEL), lambda i, j: (0, 0)))
        args.append(final_g.reshape(1, D_MODEL))
    return pl.pallas_call(
        functools.partial(_ffn_kernel, tm, seq, nj, final_g is not None),
        grid=(t_tokens // tm, nj + 1),
        in_specs=in_specs,
        out_specs=pl.BlockSpec(memory_space=pl.ANY),
        out_shape=jax.ShapeDtypeStruct((t_tokens, D_MODEL), f32),
        scratch_shapes=[pltpu.VMEM((tm + 2 * HALO, D_MODEL), bf16),
                        pltpu.VMEM((tm, D_MODEL), f32)]
                       + [pltpu.VMEM((tm + 2 * HALO, tf), f32)] * 4
                       + [pltpu.VMEM((2, tm, D_MODEL), f32),
                          pltpu.SemaphoreType.DMA((2,)),
                          pltpu.SemaphoreType.DMA((1,))],
        compiler_params=_params("arbitrary", "arbitrary"),
        name="conv_ffn",
    )(*args)


def _rope_tables(seq):
    t = np.arange(seq)
    half = HEAD_DIM // 2
    inv_freq = np.power(ROPE_BASE, -np.arange(0, half, 2, dtype=np.float64) / half)
    ang_r = (t // GRID_W)[:, None] * inv_freq[None, :]
    ang_c = (t % GRID_W)[:, None] * inv_freq[None, :]
    cos = np.concatenate([np.cos(ang_r)] * 2 + [np.cos(ang_c)] * 2, axis=-1)
    sin = np.concatenate([-np.sin(ang_r), np.sin(ang_r), -np.sin(ang_c), np.sin(ang_c)], axis=-1)
    return jnp.asarray(cos, dtype=f32), jnp.asarray(sin, dtype=f32)


def kernel(x_prompt, x_sample, c, cache_na_k, cache_na_v, cache_gqa_k, cache_gqa_v, c_ctx, norm1_g, norm2_g, ada_w, ada_b, ev_w_in, ev_na_bias, ev_fnet_w, ev_w_out, od_w_in, od_pool_w, od_pool_scale, od_q_norm_g, od_k_norm_g, od_w_out, ffn_w_up, ffn_conv_w, ffn_conv_b, ffn_w_down, final_norm_g):
    n_ctx, seq_p, _ = x_prompt.shape
    n_lat, seq_s, _ = x_sample.shape
    past = cache_na_k.shape[2]
    assert n_lat + 1 <= MOD_ROWS
    xp = x_prompt.reshape(n_ctx * seq_p, D_MODEL)
    xs = x_sample.reshape(n_lat * seq_s, D_MODEL)

    cond = jnp.zeros((MOD_ROWS, D_MODEL), f32).at[0].set(c_ctx).at[1:1 + n_lat].set(c)
    mods = ada_modulation(cond, ada_w, ada_b).reshape(DEPTH, MOD_ROWS * 6, 1, D_MODEL)
    rope = _rope_tables(seq_s)

    ctx = dict(seq=seq_p, base_row=0, per_seq=0)
    lat = dict(seq=seq_s, base_row=1, per_seq=1)
    new_k, new_v = {}, {}
    for i in range(DEPTH):
        j = i // 2
        mod = mods[i]
        w_up = ffn_w_up[i].astype(bf16)
        w_down = ffn_w_down[i].astype(bf16)
        if i % 2 == 0:
            w_in = ev_w_in[j].astype(bf16)
            w_attn = ev_w_out[j, :NA_WIDTH].astype(bf16)
            w_mix = ev_w_out[j, NA_WIDTH:].astype(bf16)
            mix_blk = 3 * NA_WIDTH // FNET_WIDTH
            proj_p = in_proj(xp, mod, norm1_g[i], w_in, f32, NA_WIDTH, **ctx)
            new_k[i] = proj_p[:, NA_WIDTH:2 * NA_WIDTH]
            new_v[i] = proj_p[:, 2 * NA_WIDTH:3 * NA_WIDTH]
            attn_p = ctx_attention(proj_p, seq_p, NA_HEADS, NA_HEADS)
            mix_p = fourier_mix(proj_p, mix_blk, ev_fnet_w[j], seq_p)
            proj_s = in_proj(xs, mod, norm1_g[i], w_in, bf16, NA_WIDTH, **lat)
            attn_s = na_attention(proj_s,
                                  cache_na_k[:, j].reshape(n_lat * past, NA_WIDTH),
                                  cache_na_v[:, j].reshape(n_lat * past, NA_WIDTH),
                                  ev_na_bias[j], seq_s, past)
            mix_s = fourier_mix(proj_s, mix_blk, ev_fnet_w[j], seq_s)
        else:
            w_in = od_w_in[j].astype(bf16)
            rot = dict(w_rot=POOL_WIDTH // W_TILE)
            w_attn = od_w_out[j, POOL_WIDTH:].astype(bf16)
            w_mix = od_w_out[j, :POOL_WIDTH].astype(bf16)
            norm = (od_q_norm_g[j], od_k_norm_g[j], GQA_KV_WIDTH)
            k_lo = GQA_Q_WIDTH
            v_lo = GQA_Q_WIDTH + GQA_KV_WIDTH
            mix_blk = (GQA_Q_WIDTH + 2 * GQA_KV_WIDTH) // POOL_WIDTH
            proj_p = in_proj(xp, mod, norm1_g[i], w_in, f32, GQA_Q_WIDTH, norm=norm, **rot, **ctx)
            new_k[i] = proj_p[:, k_lo:k_lo + GQA_KV_WIDTH]
            new_v[i] = proj_p[:, v_lo:v_lo + GQA_KV_WIDTH]
            attn_p = ctx_attention(proj_p, seq_p, GQA_Q_HEADS, GQA_KV_HEADS)
            mix_p = pool_mix(proj_p, mix_blk, od_pool_w[j], od_pool_scale[j], seq_p)
            proj_s = in_proj(xs, mod, norm1_g[i], w_in, bf16, GQA_Q_WIDTH, norm=norm, rope=rope, **rot, **lat)
            attn_s = gqa_attention(proj_s,
                                   cache_gqa_k[:, j].reshape(n_lat * past, GQA_KV_WIDTH),
                                   cache_gqa_v[:, j].reshape(n_lat * past, GQA_KV_WIDTH),
                                   seq_s, past)
            mix_s = pool_mix(proj_s, mix_blk, od_pool_w[j], od_pool_scale[j], seq_s)
        xp = out_proj(xp, mod, attn_p, mix_p, w_attn, w_mix, **ctx)
        xs = out_proj(xs, mod, attn_s, mix_s, w_attn, w_mix, **lat)
        final_g = final_norm_g if i == DEPTH - 1 else None
        xp = conv_ffn(xp, mod, norm2_g[i], w_up, ffn_conv_w[i], ffn_conv_b[i], w_down,
                      final_g=final_g, **ctx)
        xs = conv_ffn(xs, mod, norm2_g[i], w_up, ffn_conv_w[i], ffn_conv_b[i], w_down,
                      final_g=final_g, **lat)

    y_prompt = xp.reshape(n_ctx, seq_p, D_MODEL)
    y_sample = xs.reshape(n_lat, seq_s, D_MODEL)
    even = [i for i in range(DEPTH) if i % 2 == 0]
    odd = [i for i in range(DEPTH) if i % 2 == 1]

    def stack(parts, layers, heads):
        return jnp.stack([parts[i].reshape(n_ctx, seq_p, heads, HEAD_DIM) for i in layers], axis=1)

    return (y_prompt, y_sample,
            stack(new_k, even, NA_HEADS), stack(new_v, even, NA_HEADS),
            stack(new_k, odd, GQA_KV_HEADS), stack(new_v, odd, GQA_KV_HEADS))
```
